```python
import numpy as np
import jax
import jax.numpy as jnp
from jax import lax

D_MODEL = 2048
BATCH = 4
SEQ = 4096
DEPTH = 4

HEAD_DIM = 128
ROPE_DIM = HEAD_DIM // 4
ROPE_THETA = 500000.0
NORM_EPS = 1e-6
BAND_QBLK = 128

NSA_HEADS = 8
NSA_KV_HEADS = 2
NSA_GROUP = NSA_HEADS // NSA_KV_HEADS
CMP_BLOCK = 32
CMP_STRIDE = 16
SLC_BLOCK = 64
SLC_TOPN = 16
WIN_SIZE = 512
SLC_Q_CHUNK = 32
FORCE_BONUS = 1e4

RWKV_HEADS = 16
RWKV_HEAD = 64
RWKV_WIDTH = RWKV_HEADS * RWKV_HEAD
LORA_W = 96
LORA_A = 96
LORA_G = 256
LORA_V = 64
LNX_EPS = 64e-5

DIL_PATTERNS = ((128, 1), (512, 4), (2048, 16))
DIL_HPG = 4
DIL_HEADS = DIL_HPG * len(DIL_PATTERNS)
DIL_OUT = DIL_HPG * HEAD_DIM

N_BRANCHES = 3
FFN_DENSE = 5632
N_EXPERTS = 8
TOP_K = 2
FFN_EXPERT = 4096

NSA_Q_COLS = NSA_HEADS * HEAD_DIM
NSA_KV_COLS = 6 * NSA_KV_HEADS * HEAD_DIM
NSA_GATE_COLS = 3 * NSA_HEADS
RWKV_COLS = 3 * RWKV_WIDTH + LORA_W + LORA_A + LORA_G
DIL_COLS = 3 * DIL_HEADS * HEAD_DIM
MERGE_COLS = N_BRANCHES * D_MODEL
IN_SPLITS = (NSA_Q_COLS, NSA_KV_COLS, NSA_GATE_COLS, RWKV_COLS, DIL_COLS, MERGE_COLS)
IN_COLS = NSA_Q_COLS + NSA_KV_COLS + NSA_GATE_COLS + RWKV_COLS + DIL_COLS + MERGE_COLS
N_DENSE = (DEPTH + 1) // 2
N_MOE = DEPTH // 2

kernel_name = 'hybrid_nsa_rwkv7_dilated_moe_trunk'


def _offsets(sizes):
    return np.cumsum(np.asarray(sizes))[:-1].tolist()


def rms_norm(x, g):
    xf = x.astype(jnp.float32)
    y = xf * lax.rsqrt(jnp.mean(xf * xf, axis=-1, keepdims=True) + NORM_EPS)
    return (y * g.astype(jnp.float32)).astype(x.dtype)


def partial_rotary(t, positions):
    half = ROPE_DIM // 2
    inv_freq = ROPE_THETA ** (-jnp.arange(half, dtype=jnp.float32) / half)
    ang = positions.astype(jnp.float32)[:, :, None] * inv_freq
    cos = jnp.cos(ang)[:, :, None, :]
    sin = jnp.sin(ang)[:, :, None, :]
    t1 = t[..., :half].astype(jnp.float32)
    t2 = t[..., half:ROPE_DIM].astype(jnp.float32)
    rot = jnp.concatenate([t1 * cos - t2 * sin, t2 * cos + t1 * sin], axis=-1).astype(t.dtype)
    return jnp.concatenate([rot, t[..., ROPE_DIM:]], axis=-1)


def masked_softmax(s, mask):
    s = jnp.where(mask, s, -jnp.inf)
    m = jnp.max(s, axis=-1, keepdims=True)
    m = jnp.where(jnp.isfinite(m), m, 0.0)
    e = jnp.where(mask, jnp.exp(s - m), 0.0)
    den = jnp.sum(e, axis=-1, keepdims=True)
    return e / jnp.where(den > 0, den, 1.0)


def banded_attention(q, k, v, max_dist, qblk):
    b, L, hk, g, d = q.shape
    qb = min(qblk, L)
    nb = -(-L // qb)
    lp = nb * qb
    kb = qb + max_dist
    pad_back = lp - L
    qp = jnp.pad(q, ((0, 0), (0, pad_back), (0, 0), (0, 0), (0, 0)))
    kp = jnp.pad(k, ((0, 0), (max_dist, pad_back), (0, 0), (0, 0)))
    vp = jnp.pad(v, ((0, 0), (max_dist, pad_back), (0, 0), (0, 0)))
    scale = d ** -0.5

    def block(i):
        start = i * qb
        qi = lax.dynamic_slice_in_dim(qp, start, qb, axis=1)
        ki = lax.dynamic_slice_in_dim(kp, start, kb, axis=1)
        vi = lax.dynamic_slice_in_dim(vp, start, kb, axis=1)
        sc = jnp.einsum('bqhgd,bkhd->bhgqk', qi, ki).astype(jnp.float32) * scale
        qpos = start + jnp.arange(qb)
        kpos = start - max_dist + jnp.arange(kb)
        diff = qpos[:, None] - kpos[None, :]
        mask = (diff >= 0) & (diff <= max_dist) & (kpos >= 0)[None, :]
        sc = jnp.where(mask, sc, -jnp.inf)
        m = jnp.max(sc, axis=-1, keepdims=True)
        e = jnp.exp(sc - m)
        den = jnp.sum(e, axis=-1, keepdims=True)
        p = e / den
        lse = (m + jnp.log(den))[..., 0]
        o = jnp.einsum('bhgqk,bkhd->bqhgd', p.astype(vi.dtype), vi)
        return o, jnp.transpose(lse, (0, 3, 1, 2))

    o, lse = lax.map(block, jnp.arange(nb))
    o = jnp.moveaxis(o, 0, 1).reshape(b, lp, hk, g, d)[:, :L]
    lse = jnp.moveaxis(lse, 0, 1).reshape(b, lp, hk, g)[:, :L]
    return o, lse


def nsa_compress(t, pe, w1, w2):
    b, s, hk, d = t.shape
    ch = t.reshape(b, s // CMP_STRIDE, CMP_STRIDE, hk, d)
    blocks = jnp.concatenate([ch[:, :-1], ch[:, 1:]], axis=2) + pe[None, None, :, None, :]
    nc = blocks.shape[1]
    flat = blocks.transpose(0, 1, 3, 2, 4).reshape(b, nc, hk, CMP_BLOCK * d)
    return jax.nn.silu(flat @ w1) @ w2


def selected_block_attention(q, k, v, idx):
    b, s, hk, g, d = q.shape
    n_top = idx.shape[-1]
    ns = s // SLC_BLOCK
    nch = s // SLC_Q_CHUNK
    scale = d ** -0.5
    kb = k.reshape(b, ns, SLC_BLOCK, hk, d).transpose(0, 3, 1, 2, 4)
    vb = v.reshape(b, ns, SLC_BLOCK, hk, d).transpose(0, 3, 1, 2, 4)
    q_ch = q.reshape(b, nch, SLC_Q_CHUNK, hk, g, d).transpose(1, 0, 3, 4, 2, 5)
    i_ch = idx.reshape(b, hk, nch, SLC_Q_CHUNK, n_top).transpose(2, 0, 1, 3, 4)
    p_ch = jnp.arange(s).reshape(nch, SLC_Q_CHUNK)
    bi = jnp.arange(b)[:, None, None, None]
    hi = jnp.arange(hk)[None, :, None, None]

    def chunk(args):
        qi, ii, pi = args
        kg = kb[bi, hi, ii]
        vg = vb[bi, hi, ii]
        sc = jnp.einsum('bhgqd,bhqkld->bhgqkl', qi, kg).astype(jnp.float32) * scale
        kpos = ii[..., None] * SLC_BLOCK + jnp.arange(SLC_BLOCK)
        mask = (kpos <= pi[None, None, :, None, None])[:, :, None]
        sc = jnp.where(mask, sc, -jnp.inf).reshape(b, hk, g, SLC_Q_CHUNK, n_top * SLC_BLOCK)
        p = jax.nn.softmax(sc, axis=-1).reshape(b, hk, g, SLC_Q_CHUNK, n_top, SLC_BLOCK)
        return jnp.einsum('bhgqkl,bhqkld->bqhgd', p.astype(vg.dtype), vg)

    o = lax.map(chunk, (q_ch, i_ch, p_ch))
    return o.transpose(1, 0, 2, 3, 4, 5).reshape(b, s, hk, g, d)


def nsa_attention(q_cols, kv_cols, gate_cols, positions, cmp_pe, cmp_w1, cmp_w2):
    b, s, _ = q_cols.shape
    nc = s // CMP_STRIDE - 1
    ns = s // SLC_BLOCK
    n_top = min(SLC_TOPN, ns)
    scale = HEAD_DIM ** -0.5
    q = q_cols.reshape(b, s, NSA_HEADS, HEAD_DIM)
    q_grp = q.reshape(b, s, NSA_KV_HEADS, NSA_GROUP, HEAD_DIM)
    q_rot = partial_rotary(q, positions).reshape(b, s, NSA_KV_HEADS, NSA_GROUP, HEAD_DIM)
    k_cmp, v_cmp, k_slc, v_slc, k_win, v_win = [
        t.reshape(b, s, NSA_KV_HEADS, HEAD_DIM) for t in jnp.split(kv_cols, 6, axis=-1)]
    pos_idx = jnp.arange(s)

    kc = nsa_compress(k_cmp, cmp_pe[0], cmp_w1[0], cmp_w2[0])
    vc = nsa_compress(v_cmp, cmp_pe[1], cmp_w1[1], cmp_w2[1])
    sc = jnp.einsum('bshgd,bchd->bhgsc', q_grp, kc).astype(jnp.float32) * scale
    c_start = jnp.arange(nc) * CMP_STRIDE
    cmp_mask = (c_start + CMP_BLOCK - 1)[None, :] <= pos_idx[:, None]
    p_cmp = masked_softmax(sc, cmp_mask)
    o_cmp = jnp.einsum('bhgsc,bchd->bshgd', p_cmp.astype(vc.dtype), vc)

    j_start = jnp.arange(ns) * SLC_BLOCK
    cover = ((c_start[:, None] < j_start[None, :] + SLC_BLOCK)
             & (c_start[:, None] + CMP_BLOCK > j_start[None, :])).astype(jnp.float32)
    imp = jnp.einsum('bhgsc,cj->bhsj', p_cmp, cover)
    cur = (pos_idx // SLC_BLOCK)[:, None]
    j = jnp.arange(ns)[None, :]
    forced = (j == 0) | (j == cur) | (j == cur - 1)
    score = jnp.where(j <= cur, imp + FORCE_BONUS * forced.astype(jnp.float32), -jnp.inf)
    _, idx = lax.top_k(score, n_top)

    o_slc = selected_block_attention(q_rot, partial_rotary(k_slc, positions), v_slc, idx)
    o_win, _ = banded_attention(q_rot, partial_rotary(k_win, positions), v_win, WIN_SIZE - 1, BAND_QBLK)
    g = jax.nn.sigmoid(gate_cols.reshape(b, s, NSA_KV_HEADS, NSA_GROUP, 3))
    o = g[..., 0:1] * o_cmp + g[..., 1:2] * o_slc + g[..., 2:3] * o_win
    return o.reshape(b, s, NSA_Q_COLS)


def wkv7_scan(r, w, k, v, a, bb):
    b, _, h, n = r.shape

    def step(state, inp):
        r_t, w_t, k_t, v_t, a_t, b_t = inp
        sa = jnp.einsum('bhvk,bhk->bhv', state, a_t)
        state = (state * w_t[:, :, None, :] + sa[..., None] * b_t[:, :, None, :]
                 + v_t[..., None] * k_t[:, :, None, :])
        return state, jnp.einsum('bhvk,bhk->bhv', state, r_t)

    xs = tuple(jnp.moveaxis(t, 1, 0) for t in (r, w, k, v, a, bb))
    _, ys = lax.scan(step, jnp.zeros((b, h, n, n), jnp.float32), xs)
    return jnp.moveaxis(ys, 0, 1)


def rwkv7_time_mix(z, mu, vec, w_up, a_up, g_up, v_first, v_res):
    b, s, _ = z.shape
    z_prev = jnp.pad(z, ((0, 0), (1, 0), (0, 0)))[:, :-1]
    z = z + (z_prev - z) * mu
    r, k, v, zw, za, zg = jnp.split(
        z, _offsets((RWKV_WIDTH, RWKV_WIDTH, RWKV_WIDTH, LORA_W, LORA_A, LORA_G)), axis=-1)
    w0, a0, k_k, k_a, r_k, lnx_g, lnx_b = [vec[i] for i in range(7)]
    w = -jax.nn.softplus(-(w0 + jnp.tanh(zw) @ w_up)) - 0.5
    a = jax.nn.sigmoid(a0 + za @ a_up)
    g = jax.nn.sigmoid(zg) @ g_up
    if v_res is None:
        v_first = v
    else:
        v0, v_down, v_up = v_res
        v = v + (v_first - v) * jax.nn.sigmoid(v0 + (v @ v_down) @ v_up)

    def heads(t):
        return t.reshape(b, s, RWKV_HEADS, RWKV_HEAD).astype(jnp.float32)

    kk = heads(k * k_k)
    kk = kk / jnp.maximum(jnp.sqrt(jnp.sum(kk * kk, axis=-1, keepdims=True)), 1e-12)
    k = k * (1.0 + (a - 1.0) * k_a)
    rh, kh, vh, ah = heads(r), heads(k), heads(v), heads(a)
    decay = jnp.exp(-jnp.exp(heads(w)))
    y = wkv7_scan(rh, decay, kh, vh, -kk, kk * ah)
    mean = jnp.mean(y, axis=-1, keepdims=True)
    var = jnp.mean(jnp.square(y - mean), axis=-1, keepdims=True)
    y = ((y - mean) * lax.rsqrt(var + LNX_EPS)).reshape(b, s, RWKV_WIDTH) * lnx_g + lnx_b
    bonus = jnp.sum(rh * kh * r_k.reshape(RWKV_HEADS, RWKV_HEAD), axis=-1, keepdims=True) * vh
    y = (y + bonus.reshape(b, s, RWKV_WIDTH)) * g
    return y.astype(z.dtype), v_first


def _to_residue(t, dil):
    b, s = t.shape[:2]
    rest = t.shape[2:]
    return jnp.swapaxes(t.reshape((b, s // dil, dil) + rest), 1, 2).reshape((b * dil, s // dil) + rest)


def _from_residue(t, b, dil):
    length = t.shape[1]
    rest = t.shape[2:]
    return jnp.swapaxes(t.reshape((b, dil, length) + rest), 1, 2).reshape((b, length * dil) + rest)


def dilated_attention(q, k, v):
    b, s, _, d = q.shape
    outs, lses = [], []
    for gi, (win, dil) in enumerate(DIL_PATTERNS):
        lo = gi * DIL_HPG
        qg = _to_residue(q[:, :, lo:lo + DIL_HPG], dil)[:, :, :, None]
        kg = _to_residue(k[:, :, lo:lo + DIL_HPG], dil)
        vg = _to_residue(v[:, :, lo:lo + DIL_HPG], dil)
        o, lse = banded_attention(qg, kg, vg, win // dil, BAND_QBLK)
        outs.append(_from_residue(o[:, :, :, 0], b, dil))
        lses.append(_from_residue(lse[..., 0], b, dil))
    alpha = jax.nn.softmax(jnp.stack(lses, axis=-1), axis=-1)
    o = jnp.einsum('bshg,bshgd->bshd', alpha.astype(q.dtype), jnp.stack(outs, axis=-2))
    return o.reshape(b, s, DIL_OUT)


def swiglu(h, w_gu, w_d):
    gate, up = jnp.split(h @ w_gu, 2, axis=-1)
    return (jax.nn.silu(gate) * up) @ w_d


def moe_swiglu(h, router_w, router_b, moe_gu, moe_down):
    logits = (h @ router_w + router_b).astype(jnp.float32)
    top_v, top_i = lax.top_k(logits, TOP_K)
    wts = jax.nn.softmax(top_v, axis=-1)
    gate = jnp.sum(jax.nn.one_hot(top_i, N_EXPERTS, dtype=jnp.float32) * wts[..., None], axis=-2)
    y = jnp.zeros_like(h)
    for e in range(N_EXPERTS):
        y = y + gate[..., e:e + 1].astype(h.dtype) * swiglu(h, moe_gu[e], moe_down[e])
    return y


def hybrid_mixer(h, positions, w_in, cmp_pe, cmp_w1, cmp_w2, rwkv_mu, rwkv_vec, w_up, a_up, g_up,
                 v_first, v_res, w_br_a, w_br_b, w_br_c, w_out):
    b, s, _ = h.shape
    q_a, kv_a, gate_a, z_b, qkv_c, merge_logits = jnp.split(h @ w_in, _offsets(IN_SPLITS), axis=-1)
    y_a = nsa_attention(q_a, kv_a, gate_a, positions, cmp_pe, cmp_w1, cmp_w2)
    y_b, v_first = rwkv7_time_mix(z_b, rwkv_mu, rwkv_vec, w_up, a_up, g_up, v_first, v_res)
    q_c, k_c, v_c = [t.reshape(b, s, DIL_HEADS, HEAD_DIM) for t in jnp.split(qkv_c, 3, axis=-1)]
    y_c = dilated_attention(partial_rotary(q_c, positions), partial_rotary(k_c, positions), v_c)
    g_a, g_b, g_c = jnp.split(jax.nn.sigmoid(merge_logits), N_BRANCHES, axis=-1)
    merged = g_a * (y_a @ w_br_a) + g_b * (y_b @ w_br_b) + g_c * (y_c @ w_br_c)
    return merged @ w_out, v_first


def setup_inputs(seed: int = 0) -> dict:
    key = jax.random.key(seed)
    k = jax.random.split(key, 40)
    L = DEPTH
    D = D_MODEL
    W = RWKV_WIDTH

    def nrm(i, shape, scale):
        return jax.random.normal(k[i], shape, jnp.float32) * scale

    x = nrm(0, (BATCH, SEQ, D), 1.0)
    c = nrm(1, (BATCH, D), 1.0)
    positions = (jax.random.randint(k[2], (BATCH, 1), 0, 4096, dtype=jnp.int32)
                 + jnp.arange(SEQ, dtype=jnp.int32)[None, :]).astype(jnp.int32)
    ada_w = nrm(3, (L, D, 6 * D), 0.5 * D ** -0.5)
    ada_b = nrm(4, (L, 6 * D), 0.02)
    norm_g = 1.0 + nrm(5, (L, 4, D), 0.05)
    w_in = nrm(6, (L, D, IN_COLS), D ** -0.5)
    cmp_pe = nrm(7, (L, 2, CMP_BLOCK, HEAD_DIM), 0.1)
    cmp_w1 = nrm(8, (L, 2, CMP_BLOCK * HEAD_DIM, HEAD_DIM), (CMP_BLOCK * HEAD_DIM) ** -0.5)
    cmp_w2 = nrm(9, (L, 2, HEAD_DIM, HEAD_DIM), HEAD_DIM ** -0.5)
    rwkv_mu = jax.random.uniform(k[10], (L, RWKV_COLS), jnp.float32)
    rwkv_vec = jnp.stack([
        jax.random.uniform(k[11], (L, W), jnp.float32, -5.0, 1.0),
        nrm(12, (L, W), 0.5),
        0.85 + nrm(13, (L, W), 0.05),
        1.0 + nrm(14, (L, W), 0.05),
        nrm(15, (L, W), 0.1),
        1.0 + nrm(16, (L, W), 0.05),
        nrm(17, (L, W), 0.02),
    ], axis=1)
    w_up = nrm(18, (L, LORA_W, W), LORA_W ** -0.5)
    a_up = nrm(19, (L, LORA_A, W), LORA_A ** -0.5)
    g_up = nrm(20, (L, LORA_G, W), LORA_G ** -0.5)
    v_res0 = nrm(21, (L - 1, W), 0.5)
    v_res_down = nrm(22, (L - 1, W, LORA_V), W ** -0.5)
    v_res_up = nrm(23, (L - 1, LORA_V, W), LORA_V ** -0.5)
    w_br_a = nrm(24, (L, NSA_Q_COLS, D), NSA_Q_COLS ** -0.5)
    w_br_b = nrm(25, (L, W, D), W ** -0.5)
    w_br_c = nrm(26, (L, DIL_OUT, D), DIL_OUT ** -0.5)
    w_out = nrm(27, (L, D, D), D ** -0.5)
    ffn_gu = nrm(28, (N_DENSE, D, 2 * FFN_DENSE), D ** -0.5)
    ffn_down = nrm(29, (N_DENSE, FFN_DENSE, D), FFN_DENSE ** -0.5)
    router_w = nrm(30, (N_MOE, D, N_EXPERTS), D ** -0.5)
    router_b = nrm(31, (N_MOE, N_EXPERTS), 0.01)
    moe_gu = nrm(32, (N_MOE, N_EXPERTS, D, 2 * FFN_EXPERT), D ** -0.5)
    moe_down = nrm(33, (N_MOE, N_EXPERTS, FFN_EXPERT, D), FFN_EXPERT ** -0.5)
    return {'x': x, 'c': c, 'positions': positions, 'ada_w': ada_w, 'ada_b': ada_b, 'norm_g': norm_g,
            'w_in': w_in, 'cmp_pe': cmp_pe, 'cmp_w1': cmp_w1, 'cmp_w2': cmp_w2, 'rwkv_mu': rwkv_mu,
            'rwkv_vec': rwkv_vec, 'w_up': w_up, 'a_up': a_up, 'g_up': g_up, 'v_res0': v_res0,
            'v_res_down': v_res_down, 'v_res_up': v_res_up, 'w_br_a': w_br_a, 'w_br_b': w_br_b,
            'w_br_c': w_br_c, 'w_out': w_out, 'ffn_gu': ffn_gu, 'ffn_down': ffn_down,
            'router_w': router_w, 'router_b': router_b, 'moe_gu': moe_gu, 'moe_down': moe_down}


def reference(x, c, positions, ada_w, ada_b, norm_g, w_in, cmp_pe, cmp_w1, cmp_w2, rwkv_mu, rwkv_vec,
              w_up, a_up, g_up, v_res0, v_res_down, v_res_up, w_br_a, w_br_b, w_br_c, w_out,
              ffn_gu, ffn_down, router_w, router_b, moe_gu, moe_down):
    v_first = None
    cond = jax.nn.silu(c)
    for l in range(DEPTH):
        mod = (cond @ ada_w[l] + ada_b[l])[:, None, :]
        sh_m, sc_m, gt_m, sh_f, sc_f, gt_f = jnp.split(mod, 6, axis=-1)
        h = rms_norm(x, norm_g[l, 0]) * (1.0 + sc_m) + sh_m
        v_res = None if l == 0 else (v_res0[l - 1], v_res_down[l - 1], v_res_up[l - 1])
        y, v_first = hybrid_mixer(h, positions, w_in[l], cmp_pe[l], cmp_w1[l], cmp_w2[l], rwkv_mu[l],
                                  rwkv_vec[l], w_up[l], a_up[l], g_up[l], v_first, v_res,
                                  w_br_a[l], w_br_b[l], w_br_c[l], w_out[l])
        x = x + gt_m * rms_norm(y, norm_g[l, 1])
        h = rms_norm(x, norm_g[l, 2]) * (1.0 + sc_f) + sh_f
        if l % 2 == 0:
            y = swiglu(h, ffn_gu[l // 2], ffn_down[l // 2])
        else:
            y = moe_swiglu(h, router_w[l // 2], router_b[l // 2], moe_gu[l // 2], moe_down[l // 2])
        x = x + gt_f * rms_norm(y, norm_g[l, 3])
    return x
```

```python
import functools

import numpy as np
import jax
import jax.numpy as jnp
from jax import lax
from jax.experimental import pallas as pl
from jax.experimental.pallas import tpu as pltpu

F32 = jnp.float32
BF16 = jnp.bfloat16

HEAD_DIM = 128
ROPE_DIM = HEAD_DIM // 4
ROPE_HALF = ROPE_DIM // 2
ROPE_THETA = 500000.0
NORM_EPS = 1e-6

NSA_HEADS = 8
NSA_KV_HEADS = 2
NSA_GROUP = NSA_HEADS // NSA_KV_HEADS
CMP_BLOCK = 32
CMP_STRIDE = 16
SLC_BLOCK = 64
SLC_TOPN = 16
WIN_SIZE = 512
FORCE_BONUS = 1e4

RWKV_HEADS = 16
RWKV_HEAD = 64
RWKV_WIDTH = RWKV_HEADS * RWKV_HEAD
LORA_W = 96
LORA_A = 96
LORA_G = 256
LNX_EPS = 64e-5
RWKV_CHUNK = 64
RWKV_HEADS_PER_STEP = 8

DIL_PATTERNS = ((128, 1), (512, 4), (2048, 16))
DIL_HPG = 4
DIL_HEADS = DIL_HPG * len(DIL_PATTERNS)
DIL_OUT = DIL_HPG * HEAD_DIM

N_EXPERTS = 8
TOP_K = 2

LANE = 128
VMEM_LIMIT_BYTES = 56 * 1024 * 1024
MASKED = -1e30


def _params(*sem):
    return pltpu.CompilerParams(dimension_semantics=sem, vmem_limit_bytes=VMEM_LIMIT_BYTES)


def _sigmoid(x):
    return 1.0 / (1.0 + jnp.exp(-x))


def _silu(x):
    return x * _sigmoid(x)


def _dot(a, b):
    return jnp.dot(a, b, preferred_element_type=F32)


def _dot_nt(a, b):
    return lax.dot_general(a, b, (((1,), (1,)), ((), ())), preferred_element_type=F32)


def _dot_tn(a, b):
    return lax.dot_general(a, b, (((0,), (0,)), ((), ())), preferred_element_type=F32)


def _rope(t, cosf, sinf):
    lane = lax.broadcasted_iota(jnp.int32, t.shape, 1)
    swapped = jnp.where(lane < ROPE_HALF, pltpu.roll(t, LANE - ROPE_HALF, 1), pltpu.roll(t, ROPE_HALF, 1))
    return t * cosf + swapped * sinf


def _rms(y):
    return y * lax.rsqrt(jnp.mean(y * y, axis=-1, keepdims=True) + NORM_EPS)


def _matmul_kernel(*refs, act, rope_tiles):
    if rope_tiles:
        a_ref, w_ref, cos_ref, sin_ref, o_ref = refs
    else:
        a_ref, w_ref, o_ref = refs
    acc = _dot(a_ref[...].astype(BF16), w_ref[...].astype(BF16))
    if act == "sigmoid":
        acc = _sigmoid(acc)
    elif act == "tanh":
        acc = jnp.tanh(acc)
    if not rope_tiles:
        o_ref[...] = acc.astype(o_ref.dtype)
        return
    j = pl.program_id(1)
    pred = functools.reduce(jnp.logical_or, [j == t for t in rope_tiles])

    @pl.when(pred)
    def _():
        cosf = cos_ref[...]
        sinf = sin_ref[...]
        for c in range(acc.shape[1] // LANE):
            sl = slice(c * LANE, (c + 1) * LANE)
            o_ref[:, sl] = _rope(acc[:, sl], cosf, sinf).astype(o_ref.dtype)

    @pl.when(jnp.logical_not(pred))
    def _():
        o_ref[...] = acc.astype(o_ref.dtype)


def matmul(a, w, *, out_dtype, tm, tn, act=None, rope=None):
    m, k = a.shape
    n = w.shape[1]
    tm = min(tm, m)
    assert m % tm == 0 and n % tn == 0, (m, n, tm, tn)
    in_specs = [pl.BlockSpec((tm, k), lambda i, j: (i, 0)), pl.BlockSpec((k, tn), lambda i, j: (0, j))]
    args = [a, w]
    tiles = ()
    if rope is not None:
        cosf, sinf, tiles = rope
        in_specs += [pl.BlockSpec((tm, LANE), lambda i, j: (i, 0))] * 2
        args += [cosf, sinf]
    return pl.pallas_call(
        functools.partial(_matmul_kernel, act=act, rope_tiles=tuple(tiles)),
        grid=(m // tm, n // tn),
        in_specs=in_specs,
        out_specs=pl.BlockSpec((tm, tn), lambda i, j: (i, j)),
        out_shape=jax.ShapeDtypeStruct((m, n), out_dtype),
        compiler_params=_params("parallel", "parallel"),
    )(*args)


def _swiglu_up_kernel(a_ref, wg_ref, wu_ref, o_ref):
    a = a_ref[...]
    g = _dot(a, wg_ref[...])
    u = _dot(a, wu_ref[...])
    o_ref[...] = (_silu(g) * u).astype(o_ref.dtype)


def swiglu_up(a, w_gu, *, tm, tn):
    m, k = a.shape
    f = w_gu.shape[1] // 2
    assert m % tm == 0 and f % tn == 0
    nj = f // tn
    return pl.pallas_call(
        _swiglu_up_kernel,
        grid=(m // tm, nj),
        in_specs=[pl.BlockSpec((tm, k), lambda i, j: (i, 0)),
                  pl.BlockSpec((k, tn), lambda i, j: (0, j)),
                  pl.BlockSpec((k, tn), lambda i, j: (0, j + nj))],
        out_specs=pl.BlockSpec((tm, tn), lambda i, j: (i, j)),
        out_shape=jax.ShapeDtypeStruct((m, f), BF16),
        compiler_params=_params("parallel", "parallel"),
    )(a, w_gu, w_gu)


def _close_sublayer(x, y, gpost, gt, nxt):
    xn = x + gt * (_rms(y) * gpost)
    if nxt is None:
        return xn, None
    gnext, sc, sh = nxt
    return xn, (_rms(xn) * gnext) * (1.0 + sc) + sh


def _matmul_close_kernel(*refs, with_next):
    if with_next:
        a_ref, w_ref, x_ref, gp_ref, gt_ref, gn_ref, sc_ref, sh_ref, xo_ref, ho_ref, acc_ref = refs
    else:
        a_ref, w_ref, x_ref, gp_ref, gt_ref, xo_ref, acc_ref = refs
    kk = pl.program_id(1)

    @pl.when(kk == 0)
    def _():
        acc_ref[...] = jnp.zeros_like(acc_ref)

    acc_ref[...] += _dot(a_ref[...], w_ref[...])

    @pl.when(kk == pl.num_programs(1) - 1)
    def _():
        nxt = (gn_ref[...], sc_ref[0], sh_ref[0]) if with_next else None
        xn, h = _close_sublayer(x_ref[...], acc_ref[...], gp_ref[...], gt_ref[0], nxt)
        xo_ref[...] = xn
        if with_next:
            ho_ref[...] = h.astype(ho_ref.dtype)


def _close_specs(tm, d, seq, with_next, nidx):
    def row(i, *_):
        return (i, 0)

    def const(*_):
        return (0, 0)

    def batch(i, *_):
        return ((i * tm) // seq, 0, 0)

    specs = [pl.BlockSpec((tm, d), row), pl.BlockSpec((1, d), const), pl.BlockSpec((1, 1, d), batch)]
    if with_next:
        specs += [pl.BlockSpec((1, d), const), pl.BlockSpec((1, 1, d), batch), pl.BlockSpec((1, 1, d), batch)]
    return specs


def matmul_close(a, w, x, gpost, gt, nxt, *, seq, tm, tk):
    m, k = a.shape
    d = w.shape[1]
    assert m % tm == 0 and k % tk == 0 and seq % tm == 0
    with_next = nxt is not None
    args = [a, w, x, gpost, gt] + (list(nxt) if with_next else [])
    in_specs = [pl.BlockSpec((tm, tk), lambda i, kk: (i, kk)), pl.BlockSpec((tk, d), lambda i, kk: (kk, 0))]
    in_specs += _close_specs(tm, d, seq, with_next, 2)
    out_shape = [jax.ShapeDtypeStruct((m, d), F32)]
    out_specs = [pl.BlockSpec((tm, d), lambda i, kk: (i, 0))]
    if with_next:
        out_shape.append(jax.ShapeDtypeStruct((m, d), BF16))
        out_specs.append(pl.BlockSpec((tm, d), lambda i, kk: (i, 0)))
    res = pl.pallas_call(
        functools.partial(_matmul_close_kernel, with_next=with_next),
        grid=(m // tm, k // tk),
        in_specs=in_specs,
        out_specs=out_specs,
        out_shape=out_shape,
        scratch_shapes=[pltpu.VMEM((tm, d), F32)],
        compiler_params=_params("parallel", "arbitrary"),
    )(*args)
    return (res[0], res[1]) if with_next else (res[0], None)


def _close_kernel(*refs, with_next):
    if with_next:
        y_ref, x_ref, gp_ref, gt_ref, gn_ref, sc_ref, sh_ref, xo_ref, ho_ref = refs
    else:
        y_ref, x_ref, gp_ref, gt_ref, xo_ref = refs
    nxt = (gn_ref[...], sc_ref[0], sh_ref[0]) if with_next else None
    xn, h = _close_sublayer(x_ref[...], y_ref[...].astype(F32), gp_ref[...], gt_ref[0], nxt)
    xo_ref[...] = xn
    if with_next:
        ho_ref[...] = h.astype(ho_ref.dtype)


def close_sublayer(y, x, gpost, gt, nxt, *, seq, tm):
    m, d = x.shape
    with_next = nxt is not None
    args = [y, x, gpost, gt] + (list(nxt) if with_next else [])
    in_specs = [pl.BlockSpec((tm, d), lambda i: (i, 0))] + _close_specs(tm, d, seq, with_next, 1)
    out_shape = [jax.ShapeDtypeStruct((m, d), F32)]
    out_specs = [pl.BlockSpec((tm, d), lambda i: (i, 0))]
    if with_next:
        out_shape.append(jax.ShapeDtypeStruct((m, d), BF16))
        out_specs.append(pl.BlockSpec((tm, d), lambda i: (i, 0)))
    res = pl.pallas_call(
        functools.partial(_close_kernel, with_next=with_next),
        grid=(m // tm,),
        in_specs=in_specs,
        out_specs=out_specs,
        out_shape=out_shape,
        compiler_params=_params("parallel"),
    )(*args)
    return (res[0], res[1]) if with_next else (res[0], None)


def _norm_mod_kernel(x_ref, g_ref, sc_ref, sh_ref, o_ref):
    o_ref[...] = ((_rms(x_ref[...]) * g_ref[...]) * (1.0 + sc_ref[0]) + sh_ref[0]).astype(o_ref.dtype)


def norm_mod(x, g, sc, sh, *, seq, tm):
    m, d = x.shape
    batch = lambda i: ((i * tm) // seq, 0, 0)
    return pl.pallas_call(
        _norm_mod_kernel,
        grid=(m // tm,),
        in_specs=[pl.BlockSpec((tm, d), lambda i: (i, 0)), pl.BlockSpec((1, d), lambda i: (0, 0)),
                  pl.BlockSpec((1, 1, d), batch), pl.BlockSpec((1, 1, d), batch)],
        out_specs=pl.BlockSpec((tm, d), lambda i: (i, 0)),
        out_shape=jax.ShapeDtypeStruct((m, d), BF16),
        compiler_params=_params("parallel"),
    )(x, g, sc, sh)


def _branch_merge_kernel(ya_ref, yb_ref, yc_ref, wa_ref, wb_ref, wc_ref, ga_ref, gb_ref, gc_ref, o_ref):
    acc = ga_ref[...].astype(F32) * _dot(ya_ref[...], wa_ref[...])
    acc += gb_ref[...].astype(F32) * _dot(yb_ref[...], wb_ref[...])
    acc += gc_ref[...].astype(F32) * _dot(yc_ref[...], wc_ref[...])
    o_ref[...] = acc.astype(o_ref.dtype)


def branch_merge(ya, yb, yc, wa, wb, wc, gates, *, tm, tn):
    m = ya.shape[0]
    d = wa.shape[1]
    nj = d // tn
    row = lambda width: pl.BlockSpec((tm, width), lambda i, j: (i, 0))
    wsp = lambda kdim: pl.BlockSpec((kdim, tn), lambda i, j: (0, j))
    gsp = lambda off: pl.BlockSpec((tm, tn), lambda i, j: (i, j + off * nj))
    return pl.pallas_call(
        _branch_merge_kernel,
        grid=(m // tm, nj),
        in_specs=[row(ya.shape[1]), row(yb.shape[1]), row(yc.shape[1]),
                  wsp(wa.shape[0]), wsp(wb.shape[0]), wsp(wc.shape[0]), gsp(0), gsp(1), gsp(2)],
        out_specs=pl.BlockSpec((tm, tn), lambda i, j: (i, j)),
        out_shape=jax.ShapeDtypeStruct((m, d), BF16),
        compiler_params=_params("parallel", "parallel"),
    )(ya, yb, yc, wa, wb, wc, gates, gates, gates)


def _compress_kernel(f_ref, pe_ref, w1_ref, w2_ref, o_ref):
    a = (f_ref[0] + pe_ref[0]).astype(BF16)
    hmid = _silu(_dot(a, w1_ref[0]))
    o_ref[0] = _dot(hmid.astype(BF16), w2_ref[0]).astype(o_ref.dtype)


def nsa_compress(flat, pe, w1, w2):
    two, r, kdim = flat.shape
    tm = min(r, 512)
    assert r % tm == 0
    return pl.pallas_call(
        _compress_kernel,
        grid=(two, r // tm),
        in_specs=[pl.BlockSpec((1, tm, kdim), lambda t, i: (t, i, 0)),
                  pl.BlockSpec((1, 1, kdim), lambda t, i: (t, 0, 0)),
                  pl.BlockSpec((1, kdim, HEAD_DIM), lambda t, i: (t, 0, 0)),
                  pl.BlockSpec((1, HEAD_DIM, HEAD_DIM), lambda t, i: (t, 0, 0))],
        out_specs=pl.BlockSpec((1, tm, HEAD_DIM), lambda t, i: (t, i, 0)),
        out_shape=jax.ShapeDtypeStruct((two, r, HEAD_DIM), BF16),
        compiler_params=_params("parallel", "parallel"),
    )(flat, pe, w1, w2)


def _cmp_select_kernel(q_ref, kc_ref, vct_ref, o_ref, sel_ref, *, tq, nc, ncp, nsp, scale):
    qi = pl.program_id(2)
    kc = kc_ref[0, 0]
    vct = vct_ref[0, 0]
    spos = qi * tq + lax.broadcasted_iota(jnp.int32, (1, tq), 1)
    cidx = lax.broadcasted_iota(jnp.int32, (ncp, 1), 0)
    valid = jnp.logical_and(cidx * CMP_STRIDE + (CMP_BLOCK - 1) <= spos, cidx < nc)
    psum = jnp.zeros((ncp, tq), F32)
    for g in range(NSA_GROUP):
        qg = q_ref[0, :, g * HEAD_DIM:(g + 1) * HEAD_DIM]
        st = _dot_nt(kc, qg) * scale
        st = jnp.where(valid, st, MASKED)
        mx = jnp.max(st, axis=0, keepdims=True)
        e = jnp.where(valid, jnp.exp(st - mx), 0.0)
        den = jnp.sum(e, axis=0, keepdims=True)
        p = e / jnp.where(den > 0, den, 1.0)
        psum = psum + p
        og_t = _dot(vct, p.astype(BF16))
        o_ref[0, :, g * HEAD_DIM:(g + 1) * HEAD_DIM] = og_t.T.astype(o_ref.dtype)
    jrow = lax.broadcasted_iota(jnp.int32, (nsp, ncp), 0)
    ccol = lax.broadcasted_iota(jnp.int32, (nsp, ncp), 1)
    c0 = ccol * CMP_STRIDE
    j0 = jrow * SLC_BLOCK
    cover_t = jnp.logical_and(c0 < j0 + SLC_BLOCK, c0 + CMP_BLOCK > j0).astype(F32)
    imp = jnp.dot(cover_t, psum, preferred_element_type=F32, precision=lax.Precision.HIGHEST)
    j = lax.broadcasted_iota(jnp.int32, (nsp, 1), 0).astype(F32)
    cur = (spos // SLC_BLOCK).astype(F32)
    forced = jnp.logical_or(jnp.logical_or(j == 0, j == cur), j == cur - 1)
    score = jnp.where(j <= cur, imp + FORCE_BONUS * forced.astype(F32), -jnp.inf)
    sel = jnp.zeros((nsp, tq), F32)
    for _ in range(SLC_TOPN):
        mx = jnp.max(score, axis=0, keepdims=True)
        first = jnp.min(jnp.where(score == mx, j, float(nsp)), axis=0, keepdims=True)
        pick = j == first
        sel = jnp.where(pick, 1.0, sel)
        score = jnp.where(pick, -jnp.inf, score)
    sel_ref[0, 0] = sel.T.astype(sel_ref.dtype)


def cmp_select(q, kc, vct, *, nc, tq):
    b, s, _ = q.shape
    ncp = kc.shape[2]
    nsp = LANE
    assert s % tq == 0 and s // SLC_BLOCK <= nsp
    gw = NSA_GROUP * HEAD_DIM
    return pl.pallas_call(
        functools.partial(_cmp_select_kernel, tq=tq, nc=nc, ncp=ncp, nsp=nsp, scale=HEAD_DIM ** -0.5),
        grid=(b, NSA_KV_HEADS, s // tq),
        in_specs=[pl.BlockSpec((1, tq, gw), lambda bi, h, i: (bi, i, h)),
                  pl.BlockSpec((1, 1, ncp, HEAD_DIM), lambda bi, h, i: (bi, h, 0, 0)),
                  pl.BlockSpec((1, 1, HEAD_DIM, ncp), lambda bi, h, i: (bi, h, 0, 0))],
        out_specs=[pl.BlockSpec((1, tq, gw), lambda bi, h, i: (bi, i, h)),
                   pl.BlockSpec((1, 1, tq, nsp), lambda bi, h, i: (bi, h, i, 0))],
        out_shape=[jax.ShapeDtypeStruct((b, s, NSA_HEADS * HEAD_DIM), BF16),
                   jax.ShapeDtypeStruct((b, NSA_KV_HEADS, s, nsp), BF16)],
        compiler_params=_params("parallel", "parallel", "parallel"),
    )(q, kc, vct)


def _slc_kernel(q_ref, k_ref, v_ref, sel_ref, cos_ref, sin_ref, o_ref, qrot_ref, m_ref, l_ref, acc_ref,
                *, tq, scale):
    qi = pl.program_id(2)
    ki = pl.program_id(3)

    @pl.when(ki == 0)
    def _():
        cosf = cos_ref[0]
        sinf = sin_ref[0]
        for g in range(NSA_GROUP):
            sl = slice(g * HEAD_DIM, (g + 1) * HEAD_DIM)
            qrot_ref[:, sl] = _rope(q_ref[0, :, sl].astype(F32), cosf, sinf).astype(BF16)
        m_ref[...] = jnp.full_like(m_ref, MASKED)
        l_ref[...] = jnp.zeros_like(l_ref)
        acc_ref[...] = jnp.zeros_like(acc_ref)

    @pl.when(ki <= qi)
    def _():
        k = k_ref[0]
        v = v_ref[0]
        nsp = sel_ref.shape[3]
        jrow = lax.broadcasted_iota(jnp.int32, (nsp, tq), 0)
        kcol = ki * tq + lax.broadcasted_iota(jnp.int32, (nsp, tq), 1)
        expand = (kcol // SLC_BLOCK == jrow).astype(BF16)
        picked = _dot(sel_ref[0, 0], expand)
        qpos = qi * tq + lax.broadcasted_iota(jnp.int32, (tq, 1), 0)
        kpos = ki * tq + lax.broadcasted_iota(jnp.int32, (1, tq), 1)
        mask = jnp.logical_and(picked > 0.5, kpos <= qpos)
        for g in range(NSA_GROUP):
            sl = slice(g * HEAD_DIM, (g + 1) * HEAD_DIM)
            s = _dot_nt(qrot_ref[:, sl], k) * scale
            s = jnp.where(mask, s, MASKED)
            m_old = m_ref[g]
            m_new = jnp.maximum(m_old, jnp.max(s, axis=-1, keepdims=True))
            alpha = jnp.exp(m_old - m_new)
            p = jnp.where(mask, jnp.exp(s - m_new), 0.0)
            l_ref[g] = alpha * l_ref[g] + jnp.sum(p, axis=-1, keepdims=True)
            acc_ref[g] = alpha * acc_ref[g] + _dot(p.astype(BF16), v)
            m_ref[g] = m_new

    @pl.when(ki == qi)
    def _():
        for g in range(NSA_GROUP):
            sl = slice(g * HEAD_DIM, (g + 1) * HEAD_DIM)
            o_ref[0, :, sl] = (acc_ref[g] / l_ref[g]).astype(o_ref.dtype)


def slc_attention(q, kv, sel, cosf, sinf, *, k_col, v_col, tq):
    b, s, _ = q.shape
    gw = NSA_GROUP * HEAD_DIM
    nq = s // tq
    nsp = sel.shape[3]
    return pl.pallas_call(
        functools.partial(_slc_kernel, tq=tq, scale=HEAD_DIM ** -0.5),
        grid=(b, NSA_KV_HEADS, nq, nq),
        in_specs=[pl.BlockSpec((1, tq, gw), lambda bi, h, i, kk: (bi, i, h)),
                  pl.BlockSpec((1, tq, HEAD_DIM), lambda bi, h, i, kk: (bi, jnp.minimum(kk, i), k_col + h)),
                  pl.BlockSpec((1, tq, HEAD_DIM), lambda bi, h, i, kk: (bi, jnp.minimum(kk, i), v_col + h)),
                  pl.BlockSpec((1, 1, tq, nsp), lambda bi, h, i, kk: (bi, h, i, 0)),
                  pl.BlockSpec((1, tq, LANE), lambda bi, h, i, kk: (bi, i, 0)),
                  pl.BlockSpec((1, tq, LANE), lambda bi, h, i, kk: (bi, i, 0))],
        out_specs=pl.BlockSpec((1, tq, gw), lambda bi, h, i, kk: (bi, i, h)),
        out_shape=jax.ShapeDtypeStruct((b, s, NSA_HEADS * HEAD_DIM), BF16),
        scratch_shapes=[pltpu.VMEM((tq, gw), BF16),
                        pltpu.VMEM((NSA_GROUP, tq, 1), F32),
                        pltpu.VMEM((NSA_GROUP, tq, 1), F32),
                        pltpu.VMEM((NSA_GROUP, tq, HEAD_DIM), F32)],
        compiler_params=_params("parallel", "parallel", "parallel", "arbitrary"),
    )(q, kv, kv, sel, cosf, sinf)


def _band_kernel(*refs, nheads, kv_heads, nkv, tq, max_dist, rope_q, with_lse, scale):
    q_ref = refs[0]
    k_refs = refs[1:1 + nkv]
    v_refs = refs[1 + nkv:1 + 2 * nkv]
    pos = 1 + 2 * nkv
    if rope_q:
        cos_ref, sin_ref = refs[pos:pos + 2]
        pos += 2
    o_ref = refs[pos]
    lse_ref = refs[pos + 1] if with_lse else None
    qi = pl.program_id(2)
    qpos = qi * tq + lax.broadcasted_iota(jnp.int32, (tq, 1), 0)
    kpos = (qi - (nkv - 1)) * tq + lax.broadcasted_iota(jnp.int32, (1, nkv * tq), 1)
    diff = qpos - kpos
    mask = jnp.logical_and(jnp.logical_and(diff >= 0, diff <= max_dist), kpos >= 0)
    lse_acc = jnp.zeros((tq, LANE), F32)
    lane = lax.broadcasted_iota(jnp.int32, (tq, LANE), 1)
    for g in range(nheads):
        sl = slice(g * HEAD_DIM, (g + 1) * HEAD_DIM)
        ksl = sl if kv_heads > 1 else slice(0, HEAD_DIM)
        q = q_ref[0, :, sl]
        if rope_q:
            q = _rope(q.astype(F32), cos_ref[0], sin_ref[0]).astype(BF16)
        kcat = jnp.concatenate([r[0, :, ksl] for r in k_refs], axis=0)
        vcat = jnp.concatenate([r[0, :, ksl] for r in v_refs], axis=0)
        s = _dot_nt(q, kcat) * scale
        s = jnp.where(mask, s, MASKED)
        mx = jnp.max(s, axis=-1, keepdims=True)
        e = jnp.where(mask, jnp.exp(s - mx), 0.0)
        den = jnp.sum(e, axis=-1, keepdims=True)
        o_ref[0, :, sl] = (_dot(e.astype(BF16), vcat) / den).astype(o_ref.dtype)
        if with_lse:
            lse_acc = jnp.where(lane == g, mx + jnp.log(den), lse_acc)
    if with_lse:
        lse_ref[0] = lse_acc


def band_attention(q, k, v, *, nheads, kv_heads, q_col, k_col, v_col, o_cols, ncol, max_dist, tq,
                   rope=None, with_lse=False, out_dtype=BF16):
    b, seq_len, _ = q.shape
    tq = min(tq, seq_len)
    assert seq_len % tq == 0
    nkv = -(-max_dist // tq) + 1
    qw = nheads * HEAD_DIM
    kw = kv_heads * HEAD_DIM

    def kv_spec(col_fn, back):
        return pl.BlockSpec((1, tq, kw), lambda bi, c, i: (bi, jnp.maximum(i - back, 0), col_fn(c)))

    in_specs = [pl.BlockSpec((1, tq, qw), lambda bi, c, i: (bi, i, q_col(c)))]
    in_specs += [kv_spec(k_col, nkv - 1 - t) for t in range(nkv)]
    in_specs += [kv_spec(v_col, nkv - 1 - t) for t in range(nkv)]
    args = [q] + [k] * nkv + [v] * nkv
    if rope is not None:
        in_specs += [pl.BlockSpec((1, tq, LANE), lambda bi, c, i: (bi, i, 0))] * 2
        args += list(rope)
    out_specs = [pl.BlockSpec((1, tq, qw), lambda bi, c, i: (bi, i, c))]
    out_shape = [jax.ShapeDtypeStruct((b, seq_len, o_cols * qw), out_dtype)]
    if with_lse:
        out_specs.append(pl.BlockSpec((1, tq, LANE), lambda bi, c, i: (bi, i, c)))
        out_shape.append(jax.ShapeDtypeStruct((b, seq_len, o_cols * LANE), F32))
    res = pl.pallas_call(
        functools.partial(_band_kernel, nheads=nheads, kv_heads=kv_heads, nkv=nkv, tq=tq, max_dist=max_dist,
                          rope_q=rope is not None, with_lse=with_lse, scale=HEAD_DIM ** -0.5),
        grid=(b, ncol, seq_len // tq),
        in_specs=in_specs,
        out_specs=out_specs,
        out_shape=out_shape,
        compiler_params=_params("parallel", "parallel", "parallel"),
    )(*args)
    return res if with_lse else res[0]


def _wkv_kernel(r_ref, lw_ref, k_ref, v_ref, a_ref, b_ref, y_ref, state_ref, *, nh, c):
    ci = pl.program_id(1)

    @pl.when(ci == 0)
    def _():
        state_ref[...] = jnp.zeros_like(state_ref)

    row = lax.broadcasted_iota(jnp.int32, (c, c), 0)
    col = lax.broadcasted_iota(jnp.int32, (c, c), 1)
    lower = row >= col
    strict = row > col
    tril = lower.astype(F32)
    nrow = lax.broadcasted_iota(jnp.int32, state_ref.shape[1:], 0)
    ncol = lax.broadcasted_iota(jnp.int32, state_ref.shape[1:], 1)
    eye = (row == col).astype(F32)
    eye_n = (nrow == ncol).astype(F32)

    def mm(x, y):
        return _dot(x.astype(BF16), y.astype(BF16))

    def mm_nt(x, y):
        return _dot_nt(x.astype(BF16), y.astype(BF16))

    for h in range(nh):
        r = r_ref[h]
        lw = lw_ref[h]
        k = k_ref[h]
        v = v_ref[h]
        a = a_ref[h]
        b = b_ref[h]
        cum = jnp.dot(tril, lw, preferred_element_type=F32, precision=lax.Precision.HIGHEST)
        last = cum[c - 1:c, :]
        inv = jnp.exp(-cum)
        a_t = a * jnp.exp(cum - lw)
        b_t = b * inv
        k_t = k * inv
        r_t = r * jnp.exp(cum)
        tail = jnp.exp(last - cum)
        b_h = b * tail
        k_h = k * tail
        l_ab = jnp.where(strict, mm_nt(a_t, b_t), 0.0)
        l_ak = jnp.where(strict, mm_nt(a_t, k_t), 0.0)
        m_rb = jnp.where(lower, mm_nt(r_t, b_t), 0.0)
        m_rk = jnp.where(lower, mm_nt(r_t, k_t), 0.0)
        tinv = eye + l_ab
        pw = l_ab
        for _ in range(int(np.log2(c)) - 1):
            pw = mm(pw, pw)
            tinv = tinv + mm(tinv, pw)
        u0 = mm(tinv, mm(l_ak, v))
        t_a = mm(tinv, a_t)
        qeff = r_t + mm(m_rb, t_a)
        y0 = mm(m_rk, v) + mm(m_rb, u0)
        st = state_ref[h]
        y_ref[h] = mm_nt(qeff, st) + y0
        sa_t = mm_nt(st, t_a) + mm_nt(eye_n, u0)
        state_ref[h] = st * jnp.exp(last) + mm(sa_t, b_h) + mm(mm_nt(eye_n, v), k_h)


def wkv7(r, lw, k, v, a, b):
    bh, s, n = r.shape
    c = min(RWKV_CHUNK, s)
    nh = min(RWKV_HEADS_PER_STEP, bh)
    assert s % c == 0 and bh % nh == 0
    spec = pl.BlockSpec((nh, c, n), lambda i, j: (i, j, 0))
    return pl.pallas_call(
        functools.partial(_wkv_kernel, nh=nh, c=c),
        grid=(bh // nh, s // c),
        in_specs=[spec] * 6,
        out_specs=spec,
        out_shape=jax.ShapeDtypeStruct((bh, s, n), F32),
        scratch_shapes=[pltpu.VMEM((nh, n, n), F32)],
        compiler_params=_params("parallel", "arbitrary"),
    )(r, lw, k, v, a, b)


def _moe_up_kernel(te_ref, tv_ref, a_ref, wg_ref, wu_ref, o_ref):
    i = pl.program_id(0)

    @pl.when(tv_ref[i] > 0)
    def _():
        a = a_ref[...]
        g = _dot(a, wg_ref[0])
        u = _dot(a, wu_ref[0])
        o_ref[...] = (_silu(g) * u).astype(o_ref.dtype)

    @pl.when(tv_ref[i] == 0)
    def _():
        o_ref[...] = jnp.zeros_like(o_ref)


def moe_up(tile_expert, tile_valid, xs, w_gu, *, tm, tn):
    r, k = xs.shape
    f = w_gu.shape[2] // 2
    nj = f // tn
    grid_spec = pltpu.PrefetchScalarGridSpec(
        num_scalar_prefetch=2,
        grid=(r // tm, nj),
        in_specs=[pl.BlockSpec((tm, k), lambda i, j, te, tv: (i, 0)),
                  pl.BlockSpec((1, k, tn), lambda i, j, te, tv: (te[i], 0, j)),
                  pl.BlockSpec((1, k, tn), lambda i, j, te, tv: (te[i], 0, j + nj))],
        out_specs=pl.BlockSpec((tm, tn), lambda i, j, te, tv: (i, j)),
    )
    return pl.pallas_call(
        _moe_up_kernel,
        grid_spec=grid_spec,
        out_shape=jax.ShapeDtypeStruct((r, f), BF16),
        compiler_params=_params("parallel", "parallel"),
    )(tile_expert, tile_valid, xs, w_gu, w_gu)


def _moe_down_kernel(te_ref, tv_ref, a_ref, w_ref, o_ref, acc_ref):
    i = pl.program_id(0)
    kk = pl.program_id(1)

    @pl.when(kk == 0)
    def _():
        acc_ref[...] = jnp.zeros_like(acc_ref)

    @pl.when(tv_ref[i] > 0)
    def _():
        acc_ref[...] += _dot(a_ref[...], w_ref[0])

    @pl.when(kk == pl.num_programs(1) - 1)
    def _():
        o_ref[...] = acc_ref[...].astype(o_ref.dtype)


def moe_down(tile_expert, tile_valid, act, w_down, *, tm, tk):
    r, f = act.shape
    d = w_down.shape[2]
    grid_spec = pltpu.PrefetchScalarGridSpec(
        num_scalar_prefetch=2,
        grid=(r // tm, f // tk),
        in_specs=[pl.BlockSpec((tm, tk), lambda i, kk, te, tv: (i, kk)),
                  pl.BlockSpec((1, tk, d), lambda i, kk, te, tv: (te[i], kk, 0))],
        out_specs=pl.BlockSpec((tm, d), lambda i, kk, te, tv: (i, 0)),
        scratch_shapes=[pltpu.VMEM((tm, d), F32)],
    )
    return pl.pallas_call(
        _moe_down_kernel,
        grid_spec=grid_spec,
        out_shape=jax.ShapeDtypeStruct((r, d), F32),
        compiler_params=_params("parallel", "arbitrary"),
    )(tile_expert, tile_valid, act, w_down)


def _rope_tables(positions):
    inv_freq = ROPE_THETA ** (-jnp.arange(ROPE_HALF, dtype=F32) / ROPE_HALF)
    ang = positions.astype(F32)[:, :, None] * inv_freq
    cos = jnp.cos(ang)
    sin = jnp.sin(ang)
    b, s = positions.shape
    pad1 = jnp.ones((b, s, HEAD_DIM - ROPE_DIM), F32)
    pad0 = jnp.zeros((b, s, HEAD_DIM - ROPE_DIM), F32)
    return jnp.concatenate([cos, cos, pad1], axis=-1), jnp.concatenate([-sin, sin, pad0], axis=-1)


def _pad_cols(w, n):
    return w if w.shape[-1] == n else jnp.pad(w, ((0, 0), (0, n - w.shape[-1])))


def nsa_branch(q_a, kv_a, gate_a, cosf, sinf, cmp_pe, cmp_w1, cmp_w2):
    b, s, _ = q_a.shape
    nc = s // CMP_STRIDE - 1
    ncp = -(-nc // LANE) * LANE
    kvw = NSA_KV_HEADS * HEAD_DIM

    def blocks(t):
        ch = t.reshape(b, s // CMP_STRIDE, CMP_STRIDE, NSA_KV_HEADS, HEAD_DIM)
        blk = jnp.concatenate([ch[:, :-1], ch[:, 1:]], axis=2)
        return blk.transpose(0, 1, 3, 2, 4).reshape(b * nc * NSA_KV_HEADS, CMP_BLOCK * HEAD_DIM)

    rows = b * nc * NSA_KV_HEADS
    rows_p = -(-rows // 512) * 512 if rows > 512 else -(-rows // 8) * 8
    flat = jnp.stack([blocks(kv_a[..., :kvw]), blocks(kv_a[..., kvw:2 * kvw])]).astype(F32)
    flat = jnp.pad(flat, ((0, 0), (0, rows_p - rows), (0, 0)))
    comp = nsa_compress(flat, cmp_pe.reshape(2, 1, CMP_BLOCK * HEAD_DIM), cmp_w1.astype(BF16), cmp_w2.astype(BF16))
    comp = comp[:, :rows].reshape(2, b, nc, NSA_KV_HEADS, HEAD_DIM).transpose(0, 1, 3, 2, 4)
    comp = jnp.pad(comp, ((0, 0), (0, 0), (0, 0), (0, ncp - nc), (0, 0)))
    kc = comp[0]
    vct = comp[1].transpose(0, 1, 3, 2)
    tq = min(256, s)
    o_cmp, sel = cmp_select(q_a, kc, vct, nc=nc, tq=tq)
    o_slc = slc_attention(q_a, kv_a, sel, cosf, sinf, k_col=4, v_col=6, tq=tq)
    o_win = band_attention(q_a, kv_a, kv_a, nheads=NSA_GROUP, kv_heads=1, q_col=lambda c: c,
                           k_col=lambda c: 8 + c, v_col=lambda c: 10 + c, o_cols=NSA_KV_HEADS,
                           ncol=NSA_KV_HEADS, max_dist=WIN_SIZE - 1, tq=tq, rope=(cosf, sinf))
    g = gate_a[..., :3 * NSA_HEADS].reshape(b, s, NSA_HEADS, 1, 3)

    def heads(t):
        return t.reshape(b, s, NSA_HEADS, HEAD_DIM).astype(F32)

    o = g[..., 0] * heads(o_cmp) + g[..., 1] * heads(o_slc) + g[..., 2] * heads(o_win)
    return o.reshape(b, s, NSA_HEADS * HEAD_DIM).astype(BF16)


def rwkv_branch(z, mu, vec, w_up, a_up, g_up, v_first, v_res):
    b, s, _ = z.shape
    t = b * s
    z_prev = jnp.pad(z, ((0, 0), (1, 0), (0, 0)))[:, :-1]
    z = z + (z_prev - z) * mu
    w_ = RWKV_WIDTH
    r, k, v = z[..., :w_], z[..., w_:2 * w_], z[..., 2 * w_:3 * w_]
    o = 3 * w_
    zw, za, zg = z[..., o:o + LORA_W], z[..., o + LORA_W:o + LORA_W + LORA_A], z[..., o + LORA_W + LORA_A:]
    w0, a0, k_k, k_a, r_k, lnx_g, lnx_b = [vec[i] for i in range(7)]

    def lora(xin, wmat):
        return matmul(xin.reshape(t, -1).astype(BF16), wmat.astype(BF16), out_dtype=F32, tm=1024,
                      tn=512).reshape(b, s, -1)

    w = -jax.nn.softplus(-(w0 + lora(jnp.tanh(zw), w_up))) - 0.5
    a = jax.nn.sigmoid(a0 + lora(za, a_up))
    g = lora(jax.nn.sigmoid(zg), g_up)
    if v_res is None:
        v_first = v
    else:
        v0, v_down, v_up = v_res
        vd = matmul(v.reshape(t, w_).astype(BF16), _pad_cols(v_down, LANE).astype(BF16), out_dtype=BF16, tm=1024,
                    tn=LANE)
        vu = matmul(vd, jnp.pad(v_up, ((0, LANE - v_up.shape[0]), (0, 0))).astype(BF16), out_dtype=F32, tm=1024,
                    tn=512).reshape(b, s, w_)
        v = v + (v_first - v) * jax.nn.sigmoid(v0 + vu)

    def heads(x):
        return x.reshape(b, s, RWKV_HEADS, RWKV_HEAD).transpose(0, 2, 1, 3).reshape(b * RWKV_HEADS, s, RWKV_HEAD)

    kk = heads(k * k_k)
    kk = kk / jnp.maximum(jnp.sqrt(jnp.sum(kk * kk, axis=-1, keepdims=True)), 1e-12)
    k = k * (1.0 + (a - 1.0) * k_a)
    rh, kh, vh, ah = heads(r), heads(k), heads(v), heads(a)
    log_decay = -jnp.exp(heads(w))
    y = wkv7(rh, log_decay, kh, vh, -kk, kk * ah)
    mean = jnp.mean(y, axis=-1, keepdims=True)
    var = jnp.mean(jnp.square(y - mean), axis=-1, keepdims=True)
    y = (y - mean) * lax.rsqrt(var + LNX_EPS)
    rk_h = jnp.tile(r_k.reshape(RWKV_HEADS, 1, RWKV_HEAD), (b, 1, 1))
    bonus = jnp.sum(rh * kh * rk_h, axis=-1, keepdims=True) * vh

    def unheads(x):
        return x.reshape(b, RWKV_HEADS, s, RWKV_HEAD).transpose(0, 2, 1, 3).reshape(b, s, w_)

    y = (unheads(y) * lnx_g + lnx_b + unheads(bonus)) * g
    return y.astype(BF16), v_first


def dilated_branch(qkv):
    b, s, width = qkv.shape
    blocks_per_tok = width // DIL_OUT
    outs, lses = [], []
    for gi, (win, dil) in enumerate(DIL_PATTERNS):
        view = qkv.reshape(b, s // dil, dil * width)
        o, lse = band_attention(
            view, view, view, nheads=DIL_HPG, kv_heads=DIL_HPG,
            q_col=lambda c, gi=gi: c * blocks_per_tok + gi,
            k_col=lambda c, gi=gi: c * blocks_per_tok + len(DIL_PATTERNS) + gi,
            v_col=lambda c, gi=gi: c * blocks_per_tok + 2 * len(DIL_PATTERNS) + gi,
            o_cols=dil, ncol=dil, max_dist=win // dil, tq=256, with_lse=True, out_dtype=F32)
        outs.append(o.reshape(b, s, DIL_HPG, HEAD_DIM))
        lses.append(lse.reshape(b, s, LANE)[..., :DIL_HPG])
    alpha = jax.nn.softmax(jnp.stack(lses, axis=-1), axis=-1)
    o = sum(alpha[..., gi:gi + 1] * outs[gi] for gi in range(len(DIL_PATTERNS)))
    return o.reshape(b, s, DIL_OUT).astype(BF16)


def moe_ffn(h, router_w, router_b, w_gu, w_down, *, tm):
    t, d = h.shape
    logits = matmul(h, _pad_cols(router_w, LANE).astype(BF16), out_dtype=F32, tm=1024, tn=LANE)[:, :N_EXPERTS]
    logits = logits + router_b
    top_v, top_i = lax.top_k(logits, TOP_K)
    wts = jax.nn.softmax(top_v, axis=-1)
    flat_e = top_i.reshape(-1)
    onehot = (flat_e[:, None] == jnp.arange(N_EXPERTS)[None, :]).astype(jnp.int32)
    rank = jnp.take_along_axis(jnp.cumsum(onehot, axis=0), flat_e[:, None], axis=1)[:, 0] - 1
    counts = jnp.sum(onehot, axis=0)
    tiles_per = (counts + tm - 1) // tm
    tile_end = jnp.cumsum(tiles_per)
    group_start = (tile_end - tiles_per) * tm
    dest = group_start[flat_e] + rank
    ntiles = (TOP_K * t) // tm + N_EXPERTS
    rows = ntiles * tm
    row_token = jnp.zeros((rows,), jnp.int32).at[dest].set(jnp.arange(TOP_K * t, dtype=jnp.int32) // TOP_K)
    tile_ids = jnp.arange(ntiles, dtype=jnp.int32)
    tile_valid = (tile_ids < tile_end[-1]).astype(jnp.int32)
    tile_expert = jnp.minimum(jnp.searchsorted(tile_end, tile_ids, side="right"), N_EXPERTS - 1).astype(jnp.int32)
    xs = jnp.take(h, row_token, axis=0)
    act = moe_up(tile_expert, tile_valid, xs, w_gu, tm=tm, tn=512)
    out = moe_down(tile_expert, tile_valid, act, w_down, tm=tm, tk=512)
    picked = jnp.take(out, dest, axis=0).reshape(t, TOP_K, d)
    return jnp.sum(picked * wts[..., None], axis=1)


def kernel(x, c, positions, ada_w, ada_b, norm_g, w_in, cmp_pe, cmp_w1, cmp_w2, rwkv_mu, rwkv_vec, w_up, a_up, g_up, v_res0, v_res_down, v_res_up, w_br_a, w_br_b, w_br_c, w_out, ffn_gu, ffn_down, router_w, router_b, moe_gu, moe_down):
    b, s, d = x.shape
    depth = ada_w.shape[0]
    t = b * s
    tm_row = min(512, s)
    cosf, sinf = _rope_tables(positions)
    cos_t = cosf.reshape(t, LANE)
    sin_t = sinf.reshape(t, LANE)

    cond = jnp.pad(jax.nn.silu(c), ((0, 8 - b % 8 if b % 8 else 0), (0, 0))).astype(BF16)
    mods = []
    for l in range(depth):
        mod = matmul(cond, ada_w[l], out_dtype=F32, tm=cond.shape[0], tn=512)[:b] + ada_b[l]
        mods.append(mod.reshape(b, 6, 1, d))

    def mod_of(l, i):
        return mods[l][:, i]

    q_cols = NSA_HEADS * HEAD_DIM
    kv_cols = 6 * NSA_KV_HEADS * HEAD_DIM
    gate_cols = 3 * NSA_HEADS
    rwkv_cols = 3 * RWKV_WIDTH + LORA_W + LORA_A + LORA_G
    dil_cols = 3 * DIL_HEADS * HEAD_DIM
    offs = np.cumsum([0, q_cols, kv_cols, gate_cols, rwkv_cols, dil_cols, 3 * d]).tolist()
    rwkv_pad = -(-rwkv_cols // 512) * 512

    xf = x.reshape(t, d)
    h = norm_mod(xf, norm_g[0, 0][None], mod_of(0, 1), mod_of(0, 0), seq=s, tm=tm_row)
    v_first = None
    for l in range(depth):
        wl = w_in[l]
        seg = lambda i: wl[:, offs[i]:offs[i + 1]]
        q_a = matmul(h, seg(0).astype(BF16), out_dtype=BF16, tm=1024, tn=512)
        kv_a = matmul(h, seg(1).astype(BF16), out_dtype=BF16, tm=1024, tn=NSA_KV_HEADS * HEAD_DIM,
                      rope=(cos_t, sin_t, (2, 4)))
        gate_a = matmul(h, _pad_cols(seg(2), LANE).astype(BF16), out_dtype=F32, tm=1024, tn=LANE, act="sigmoid")
        z_b = matmul(h, _pad_cols(seg(3), rwkv_pad).astype(BF16), out_dtype=F32, tm=1024, tn=512)[:, :rwkv_cols]
        qkv_c = matmul(h, seg(4).astype(BF16), out_dtype=BF16, tm=1024, tn=512,
                       rope=(cos_t, sin_t, tuple(range(2 * DIL_HEADS * HEAD_DIM // 512))))
        mg = matmul(h, seg(5).astype(BF16), out_dtype=BF16, tm=1024, tn=512, act="sigmoid")

        y_a = nsa_branch(q_a.reshape(b, s, -1), kv_a.reshape(b, s, -1), gate_a.reshape(b, s, -1), cosf, sinf,
                         cmp_pe[l], cmp_w1[l], cmp_w2[l])
        v_res = None if l == 0 else (v_res0[l - 1], v_res_down[l - 1], v_res_up[l - 1])
        y_b, v_first = rwkv_branch(z_b.reshape(b, s, -1), rwkv_mu[l], rwkv_vec[l], w_up[l], a_up[l], g_up[l],
                                   v_first, v_res)
        y_c = dilated_branch(qkv_c.reshape(b, s, -1))
        merged = branch_merge(y_a.reshape(t, -1), y_b.reshape(t, -1), y_c.reshape(t, -1), w_br_a[l].astype(BF16),
                              w_br_b[l].astype(BF16), w_br_c[l].astype(BF16), mg, tm=1024, tn=512)
        xf, h = matmul_close(merged, w_out[l].astype(BF16), xf, norm_g[l, 1][None], mod_of(l, 2),
                             (norm_g[l, 2][None], mod_of(l, 4), mod_of(l, 3)), seq=s, tm=tm_row, tk=512)

        nxt = None if l == depth - 1 else (norm_g[l + 1, 0][None], mod_of(l + 1, 1), mod_of(l + 1, 0))
        if l % 2 == 0:
            act = swiglu_up(h, ffn_gu[l // 2].astype(BF16), tm=1024, tn=512)
            xf, h = matmul_close(act, ffn_down[l // 2].astype(BF16), xf, norm_g[l, 3][None], mod_of(l, 5), nxt,
                                 seq=s, tm=tm_row, tk=512)
        else:
            y = moe_ffn(h, router_w[l // 2], router_b[l // 2], moe_gu[l // 2].astype(BF16),
                        moe_down[l // 2].astype(BF16), tm=512)
            xf, h = close_sublayer(y, xf, norm_g[l, 3][None], mod_of(l, 5), nxt, seq=s, tm=tm_row)
    return xf.reshape(b, s, d)
```

```python
import functools

import numpy as np
import jax
import jax.numpy as jnp
from jax import lax
from jax.experimental import pallas as pl
from jax.experimental.pallas import tpu as pltpu

F32 = jnp.float32
BF16 = jnp.bfloat16

HEAD_DIM = 128
ROPE_DIM = HEAD_DIM // 4
ROPE_HALF = ROPE_DIM // 2
ROPE_THETA = 500000.0
NORM_EPS = 1e-6

NSA_HEADS = 8
NSA_KV_HEADS = 2
NSA_GROUP = NSA_HEADS // NSA_KV_HEADS
CMP_BLOCK = 32
CMP_STRIDE = 16
SLC_BLOCK = 64
SLC_TOPN = 16
WIN_SIZE = 512
FORCE_BONUS = 1e4
SLC_PICK_BIAS = 8192.0

RWKV_HEADS = 16
RWKV_HEAD = 64
RWKV_WIDTH = RWKV_HEADS * RWKV_HEAD
LORA_W = 96
LORA_A = 96
LORA_G = 256
LNX_EPS = 64e-5
RWKV_CHUNK = 64
RWKV_CHUNKS_PER_STEP = 8

DIL_PATTERNS = ((128, 1), (512, 4), (2048, 16))
DIL_HPG = 4
DIL_HEADS = DIL_HPG * len(DIL_PATTERNS)
DIL_OUT = DIL_HPG * HEAD_DIM

N_EXPERTS = 8
TOP_K = 2

LANE = 128
VMEM_LIMIT_BYTES = 56 * 1024 * 1024
MASKED = -1e30


def _params(*sem):
    return pltpu.CompilerParams(dimension_semantics=sem, vmem_limit_bytes=VMEM_LIMIT_BYTES)


def _sigmoid(x):
    return 1.0 / (1.0 + jnp.exp(-x))


def _silu(x):
    return x * _sigmoid(x)


def _dot(a, b):
    return jnp.dot(a, b, preferred_element_type=F32)


def _dot_nt(a, b):
    return lax.dot_general(a, b, (((1,), (1,)), ((), ())), preferred_element_type=F32)


def _dot_tn(a, b):
    return lax.dot_general(a, b, (((0,), (0,)), ((), ())), preferred_element_type=F32)


def _rope(t, cosf, sinf):
    lane = lax.broadcasted_iota(jnp.int32, t.shape, 1)
    swapped = jnp.where(lane < ROPE_HALF, pltpu.roll(t, LANE - ROPE_HALF, 1), pltpu.roll(t, ROPE_HALF, 1))
    return t * cosf + swapped * sinf


def _rms(y):
    return y * lax.rsqrt(jnp.mean(y * y, axis=-1, keepdims=True) + NORM_EPS)


def _matmul_kernel(*refs, act, rope_tiles):
    if rope_tiles:
        a_ref, w_ref, cos_ref, sin_ref, o_ref = refs
    else:
        a_ref, w_ref, o_ref = refs
    acc = _dot(a_ref[...].astype(BF16), w_ref[...].astype(BF16))
    if act == "sigmoid":
        acc = _sigmoid(acc)
    elif act == "tanh":
        acc = jnp.tanh(acc)
    if not rope_tiles:
        o_ref[...] = acc.astype(o_ref.dtype)
        return
    j = pl.program_id(1)
    pred = functools.reduce(jnp.logical_or, [j == t for t in rope_tiles])

    @pl.when(pred)
    def _():
        cosf = cos_ref[...]
        sinf = sin_ref[...]
        for c in range(acc.shape[1] // LANE):
            sl = slice(c * LANE, (c + 1) * LANE)
            o_ref[:, sl] = _rope(acc[:, sl], cosf, sinf).astype(o_ref.dtype)

    @pl.when(jnp.logical_not(pred))
    def _():
        o_ref[...] = acc.astype(o_ref.dtype)


def matmul(a, w, *, out_dtype, tm, tn, act=None, rope=None):
    m, k = a.shape
    n = w.shape[1]
    tm = min(tm, m)
    assert m % tm == 0 and n % tn == 0, (m, n, tm, tn)
    in_specs = [pl.BlockSpec((tm, k), lambda i, j: (i, 0)), pl.BlockSpec((k, tn), lambda i, j: (0, j))]
    args = [a, w]
    tiles = ()
    if rope is not None:
        cosf, sinf, tiles = rope
        in_specs += [pl.BlockSpec((tm, LANE), lambda i, j: (i, 0))] * 2
        args += [cosf, sinf]
    return pl.pallas_call(
        functools.partial(_matmul_kernel, act=act, rope_tiles=tuple(tiles)),
        grid=(m // tm, n // tn),
        in_specs=in_specs,
        out_specs=pl.BlockSpec((tm, tn), lambda i, j: (i, j)),
        out_shape=jax.ShapeDtypeStruct((m, n), out_dtype),
        compiler_params=_params("parallel", "parallel"),
    )(*args)


def _swiglu_up_kernel(a_ref, wg_ref, wu_ref, o_ref):
    a = a_ref[...]
    g = _dot(a, wg_ref[...])
    u = _dot(a, wu_ref[...])
    o_ref[...] = (_silu(g) * u).astype(o_ref.dtype)


def swiglu_up(a, w_gu, *, tm, tn):
    m, k = a.shape
    f = w_gu.shape[1] // 2
    assert m % tm == 0 and f % tn == 0
    nj = f // tn
    return pl.pallas_call(
        _swiglu_up_kernel,
        grid=(m // tm, nj),
        in_specs=[pl.BlockSpec((tm, k), lambda i, j: (i, 0)),
                  pl.BlockSpec((k, tn), lambda i, j: (0, j)),
                  pl.BlockSpec((k, tn), lambda i, j: (0, j + nj))],
        out_specs=pl.BlockSpec((tm, tn), lambda i, j: (i, j)),
        out_shape=jax.ShapeDtypeStruct((m, f), BF16),
        compiler_params=_params("parallel", "parallel"),
    )(a, w_gu, w_gu)


def _close_sublayer(x, y, gpost, gt, nxt):
    xn = x + gt * (_rms(y) * gpost)
    if nxt is None:
        return xn, None
    gnext, sc, sh = nxt
    return xn, (_rms(xn) * gnext) * (1.0 + sc) + sh


def _matmul_close_kernel(*refs, with_next):
    if with_next:
        a_ref, w_ref, x_ref, gp_ref, gt_ref, gn_ref, sc_ref, sh_ref, xo_ref, ho_ref, acc_ref = refs
    else:
        a_ref, w_ref, x_ref, gp_ref, gt_ref, xo_ref, acc_ref = refs
    kk = pl.program_id(1)

    @pl.when(kk == 0)
    def _():
        acc_ref[...] = jnp.zeros_like(acc_ref)

    acc_ref[...] += _dot(a_ref[...], w_ref[...])

    @pl.when(kk == pl.num_programs(1) - 1)
    def _():
        nxt = (gn_ref[...], sc_ref[0], sh_ref[0]) if with_next else None
        xn, h = _close_sublayer(x_ref[...], acc_ref[...], gp_ref[...], gt_ref[0], nxt)
        xo_ref[...] = xn
        if with_next:
            ho_ref[...] = h.astype(ho_ref.dtype)


def _close_specs(tm, d, seq, with_next, nidx):
    def row(i, *_):
        return (i, 0)

    def const(*_):
        return (0, 0)

    def batch(i, *_):
        return ((i * tm) // seq, 0, 0)

    specs = [pl.BlockSpec((tm, d), row), pl.BlockSpec((1, d), const), pl.BlockSpec((1, 1, d), batch)]
    if with_next:
        specs += [pl.BlockSpec((1, d), const), pl.BlockSpec((1, 1, d), batch), pl.BlockSpec((1, 1, d), batch)]
    return specs


def matmul_close(a, w, x, gpost, gt, nxt, *, seq, tm, tk):
    m, k = a.shape
    d = w.shape[1]
    assert m % tm == 0 and k % tk == 0 and seq % tm == 0
    with_next = nxt is not None
    args = [a, w, x, gpost, gt] + (list(nxt) if with_next else [])
    in_specs = [pl.BlockSpec((tm, tk), lambda i, kk: (i, kk)), pl.BlockSpec((tk, d), lambda i, kk: (kk, 0))]
    in_specs += _close_specs(tm, d, seq, with_next, 2)
    out_shape = [jax.ShapeDtypeStruct((m, d), F32)]
    out_specs = [pl.BlockSpec((tm, d), lambda i, kk: (i, 0))]
    if with_next:
        out_shape.append(jax.ShapeDtypeStruct((m, d), BF16))
        out_specs.append(pl.BlockSpec((tm, d), lambda i, kk: (i, 0)))
    res = pl.pallas_call(
        functools.partial(_matmul_close_kernel, with_next=with_next),
        grid=(m // tm, k // tk),
        in_specs=in_specs,
        out_specs=out_specs,
        out_shape=out_shape,
        scratch_shapes=[pltpu.VMEM((tm, d), F32)],
        compiler_params=_params("parallel", "arbitrary"),
    )(*args)
    return (res[0], res[1]) if with_next else (res[0], None)


def _close_kernel(*refs, with_next):
    if with_next:
        y_ref, x_ref, gp_ref, gt_ref, gn_ref, sc_ref, sh_ref, xo_ref, ho_ref = refs
    else:
        y_ref, x_ref, gp_ref, gt_ref, xo_ref = refs
    nxt = (gn_ref[...], sc_ref[0], sh_ref[0]) if with_next else None
    xn, h = _close_sublayer(x_ref[...], y_ref[...].astype(F32), gp_ref[...], gt_ref[0], nxt)
    xo_ref[...] = xn
    if with_next:
        ho_ref[...] = h.astype(ho_ref.dtype)


def close_sublayer(y, x, gpost, gt, nxt, *, seq, tm):
    m, d = x.shape
    with_next = nxt is not None
    args = [y, x, gpost, gt] + (list(nxt) if with_next else [])
    in_specs = [pl.BlockSpec((tm, d), lambda i: (i, 0))] + _close_specs(tm, d, seq, with_next, 1)
    out_shape = [jax.ShapeDtypeStruct((m, d), F32)]
    out_specs = [pl.BlockSpec((tm, d), lambda i: (i, 0))]
    if with_next:
        out_shape.append(jax.ShapeDtypeStruct((m, d), BF16))
        out_specs.append(pl.BlockSpec((tm, d), lambda i: (i, 0)))
    res = pl.pallas_call(
        functools.partial(_close_kernel, with_next=with_next),
        grid=(m // tm,),
        in_specs=in_specs,
        out_specs=out_specs,
        out_shape=out_shape,
        compiler_params=_params("parallel"),
    )(*args)
    return (res[0], res[1]) if with_next else (res[0], None)


def _norm_mod_kernel(x_ref, g_ref, sc_ref, sh_ref, o_ref):
    o_ref[...] = ((_rms(x_ref[...]) * g_ref[...]) * (1.0 + sc_ref[0]) + sh_ref[0]).astype(o_ref.dtype)


def norm_mod(x, g, sc, sh, *, seq, tm):
    m, d = x.shape
    batch = lambda i: ((i * tm) // seq, 0, 0)
    return pl.pallas_call(
        _norm_mod_kernel,
        grid=(m // tm,),
        in_specs=[pl.BlockSpec((tm, d), lambda i: (i, 0)), pl.BlockSpec((1, d), lambda i: (0, 0)),
                  pl.BlockSpec((1, 1, d), batch), pl.BlockSpec((1, 1, d), batch)],
        out_specs=pl.BlockSpec((tm, d), lambda i: (i, 0)),
        out_shape=jax.ShapeDtypeStruct((m, d), BF16),
        compiler_params=_params("parallel"),
    )(x, g, sc, sh)


def _branch_merge_kernel(ya_ref, yb_ref, yc_ref, wa_ref, wb_ref, wc_ref, ga_ref, gb_ref, gc_ref, o_ref):
    acc = ga_ref[...].astype(F32) * _dot(ya_ref[...], wa_ref[...])
    acc += gb_ref[...].astype(F32) * _dot(yb_ref[...], wb_ref[...])
    acc += gc_ref[...].astype(F32) * _dot(yc_ref[...], wc_ref[...])
    o_ref[...] = acc.astype(o_ref.dtype)


def branch_merge(ya, yb, yc, wa, wb, wc, gates, *, tm, tn):
    m = ya.shape[0]
    d = wa.shape[1]
    nj = d // tn
    row = lambda width: pl.BlockSpec((tm, width), lambda i, j: (i, 0))
    wsp = lambda kdim: pl.BlockSpec((kdim, tn), lambda i, j: (0, j))
    gsp = lambda off: pl.BlockSpec((tm, tn), lambda i, j: (i, j + off * nj))
    return pl.pallas_call(
        _branch_merge_kernel,
        grid=(m // tm, nj),
        in_specs=[row(ya.shape[1]), row(yb.shape[1]), row(yc.shape[1]),
                  wsp(wa.shape[0]), wsp(wb.shape[0]), wsp(wc.shape[0]), gsp(0), gsp(1), gsp(2)],
        out_specs=pl.BlockSpec((tm, tn), lambda i, j: (i, j)),
        out_shape=jax.ShapeDtypeStruct((m, d), BF16),
        compiler_params=_params("parallel", "parallel"),
    )(ya, yb, yc, wa, wb, wc, gates, gates, gates)


def _compress_kernel(f_ref, pe_ref, w1_ref, w2_ref, o_ref):
    a = (f_ref[0] + pe_ref[0]).astype(BF16)
    hmid = _silu(_dot(a, w1_ref[0]))
    o_ref[0] = _dot(hmid.astype(BF16), w2_ref[0]).astype(o_ref.dtype)


def nsa_compress(flat, pe, w1, w2):
    two, r, kdim = flat.shape
    tm = min(r, 512)
    assert r % tm == 0
    return pl.pallas_call(
        _compress_kernel,
        grid=(two, r // tm),
        in_specs=[pl.BlockSpec((1, tm, kdim), lambda t, i: (t, i, 0)),
                  pl.BlockSpec((1, 1, kdim), lambda t, i: (t, 0, 0)),
                  pl.BlockSpec((1, kdim, HEAD_DIM), lambda t, i: (t, 0, 0)),
                  pl.BlockSpec((1, HEAD_DIM, HEAD_DIM), lambda t, i: (t, 0, 0))],
        out_specs=pl.BlockSpec((1, tm, HEAD_DIM), lambda t, i: (t, i, 0)),
        out_shape=jax.ShapeDtypeStruct((two, r, HEAD_DIM), BF16),
        compiler_params=_params("parallel", "parallel"),
    )(flat, pe, w1, w2)


def _cmp_select_kernel(q_ref, kc_ref, vct_ref, o_ref, sel_ref, *, tq, nc, ncp, nsp, scale):
    qi = pl.program_id(2)
    kc = kc_ref[0, 0]
    vct = vct_ref[0, 0]
    spos = qi * tq + lax.broadcasted_iota(jnp.int32, (1, tq), 1)
    cidx = lax.broadcasted_iota(jnp.int32, (ncp, 1), 0)
    valid = jnp.logical_and(cidx * CMP_STRIDE + (CMP_BLOCK - 1) <= spos, cidx < nc)
    psum = jnp.zeros((ncp, tq), F32)
    for g in range(NSA_GROUP):
        qg = q_ref[0, :, g * HEAD_DIM:(g + 1) * HEAD_DIM]
        st = _dot_nt(kc, qg) * scale
        st = jnp.where(valid, st, MASKED)
        mx = jnp.max(st, axis=0, keepdims=True)
        e = jnp.where(valid, jnp.exp(st - mx), 0.0)
        den = jnp.sum(e, axis=0, keepdims=True)
        p = e / jnp.where(den > 0, den, 1.0)
        psum = psum + p
        og_t = _dot(vct, p.astype(BF16))
        o_ref[0, :, g * HEAD_DIM:(g + 1) * HEAD_DIM] = og_t.T.astype(o_ref.dtype)
    jrow = lax.broadcasted_iota(jnp.int32, (nsp, ncp), 0)
    ccol = lax.broadcasted_iota(jnp.int32, (nsp, ncp), 1)
    c0 = ccol * CMP_STRIDE
    j0 = jrow * SLC_BLOCK
    cover_t = jnp.logical_and(c0 < j0 + SLC_BLOCK, c0 + CMP_BLOCK > j0).astype(F32)
    imp = jnp.dot(cover_t, psum, preferred_element_type=F32, precision=lax.Precision.HIGHEST)
    j = lax.broadcasted_iota(jnp.int32, (nsp, 1), 0).astype(F32)
    cur = (spos // SLC_BLOCK).astype(F32)
    forced = jnp.logical_or(jnp.logical_or(j == 0, j == cur), j == cur - 1)
    score = jnp.where(j <= cur, imp + FORCE_BONUS * forced.astype(F32), -jnp.inf)
    sel = jnp.zeros((nsp, tq), F32)
    for _ in range(SLC_TOPN):
        mx = jnp.max(score, axis=0, keepdims=True)
        first = jnp.min(jnp.where(score == mx, j, float(nsp)), axis=0, keepdims=True)
        pick = j == first
        sel = jnp.where(pick, 1.0, sel)
        score = jnp.where(pick, -jnp.inf, score)
    sel_ref[0, 0] = sel.T.astype(sel_ref.dtype)


def cmp_select(q, kc, vct, *, nc, tq):
    b, s, _ = q.shape
    ncp = kc.shape[2]
    nsp = LANE
    assert s % tq == 0 and s // SLC_BLOCK <= nsp
    gw = NSA_GROUP * HEAD_DIM
    return pl.pallas_call(
        functools.partial(_cmp_select_kernel, tq=tq, nc=nc, ncp=ncp, nsp=nsp, scale=HEAD_DIM ** -0.5),
        grid=(b, NSA_KV_HEADS, s // tq),
        in_specs=[pl.BlockSpec((1, tq, gw), lambda bi, h, i: (bi, i, h)),
                  pl.BlockSpec((1, 1, ncp, HEAD_DIM), lambda bi, h, i: (bi, h, 0, 0)),
                  pl.BlockSpec((1, 1, HEAD_DIM, ncp), lambda bi, h, i: (bi, h, 0, 0))],
        out_specs=[pl.BlockSpec((1, tq, gw), lambda bi, h, i: (bi, i, h)),
                   pl.BlockSpec((1, 1, tq, nsp), lambda bi, h, i: (bi, h, i, 0))],
        out_shape=[jax.ShapeDtypeStruct((b, s, NSA_HEADS * HEAD_DIM), BF16),
                   jax.ShapeDtypeStruct((b, NSA_KV_HEADS, s, nsp), BF16)],
        compiler_params=_params("parallel", "parallel", "parallel"),
    )(q, kc, vct)


def _slc_kernel(qi_ref, ki_ref, q_ref, k_ref, v_ref, sel_ref, cos_ref, sin_ref, o_ref, qaug_ref, m_ref, l_ref,
                acc_ref, *, tq, scale):
    step = pl.program_id(2)
    qi = qi_ref[step]
    ki = ki_ref[step]

    @pl.when(ki == 0)
    def _():
        cosf = cos_ref[0]
        sinf = sin_ref[0]
        pick = sel_ref[0, 0] * SLC_PICK_BIAS
        for g in range(NSA_GROUP):
            sl = slice(g * HEAD_DIM, (g + 1) * HEAD_DIM)
            qg = _rope(q_ref[0, :, sl].astype(F32), cosf, sinf) * scale
            qaug_ref[g] = jnp.concatenate([qg.astype(BF16), pick], axis=1)
        m_ref[...] = jnp.full_like(m_ref, MASKED)
        l_ref[...] = jnp.zeros_like(l_ref)
        acc_ref[...] = jnp.zeros_like(acc_ref)

    def accumulate(causal):
        nsp = sel_ref.shape[3]
        block_of_key = (ki * tq + lax.broadcasted_iota(jnp.int32, (tq, nsp), 0)) // SLC_BLOCK
        onehot = (block_of_key == lax.broadcasted_iota(jnp.int32, (tq, nsp), 1)).astype(BF16)
        kaug = jnp.concatenate([k_ref[0], onehot], axis=1)
        v = v_ref[0]
        if causal:
            qpos = lax.broadcasted_iota(jnp.int32, (tq, tq), 0)
            kpos = lax.broadcasted_iota(jnp.int32, (tq, tq), 1)
            visible = kpos <= qpos
        for g in range(NSA_GROUP):
            s = _dot_nt(qaug_ref[g], kaug)
            if causal:
                s = jnp.where(visible, s, MASKED)
            m_old = m_ref[g]
            m_new = jnp.maximum(m_old, jnp.max(s, axis=-1, keepdims=True))
            alpha = jnp.exp(m_old - m_new)
            p = jnp.exp(s - pltpu.repeat(m_new, tq // LANE, axis=1))
            l_ref[g] = alpha * l_ref[g] + jnp.sum(p, axis=-1, keepdims=True)
            acc_ref[g] = alpha * acc_ref[g] + _dot(p.astype(BF16), v)
            m_ref[g] = m_new

    @pl.when(ki < qi)
    def _():
        accumulate(False)

    @pl.when(ki == qi)
    def _():
        accumulate(True)
        for g in range(NSA_GROUP):
            sl = slice(g * HEAD_DIM, (g + 1) * HEAD_DIM)
            o_ref[0, :, sl] = (acc_ref[g] / l_ref[g]).astype(o_ref.dtype)


def slc_attention(q, kv, sel, cosf, sinf, *, k_col, v_col, tq):
    b, s, _ = q.shape
    gw = NSA_GROUP * HEAD_DIM
    nq = s // tq
    nsp = sel.shape[3]
    pairs = [(i, j) for i in range(nq) for j in range(i + 1)]
    qi_tab = jnp.asarray([p[0] for p in pairs], jnp.int32)
    ki_tab = jnp.asarray([p[1] for p in pairs], jnp.int32)
    grid_spec = pltpu.PrefetchScalarGridSpec(
        num_scalar_prefetch=2,
        grid=(b, NSA_KV_HEADS, len(pairs)),
        in_specs=[pl.BlockSpec((1, tq, gw), lambda bi, h, t, qt, kt: (bi, qt[t], h)),
                  pl.BlockSpec((1, tq, HEAD_DIM), lambda bi, h, t, qt, kt: (bi, kt[t], k_col + h)),
                  pl.BlockSpec((1, tq, HEAD_DIM), lambda bi, h, t, qt, kt: (bi, kt[t], v_col + h)),
                  pl.BlockSpec((1, 1, tq, nsp), lambda bi, h, t, qt, kt: (bi, h, qt[t], 0)),
                  pl.BlockSpec((1, tq, LANE), lambda bi, h, t, qt, kt: (bi, qt[t], 0)),
                  pl.BlockSpec((1, tq, LANE), lambda bi, h, t, qt, kt: (bi, qt[t], 0))],
        out_specs=pl.BlockSpec((1, tq, gw), lambda bi, h, t, qt, kt: (bi, qt[t], h)),
        scratch_shapes=[pltpu.VMEM((NSA_GROUP, tq, 2 * HEAD_DIM), BF16),
                        pltpu.VMEM((NSA_GROUP, tq, LANE), F32),
                        pltpu.VMEM((NSA_GROUP, tq, LANE), F32),
                        pltpu.VMEM((NSA_GROUP, tq, HEAD_DIM), F32)],
    )
    return pl.pallas_call(
        functools.partial(_slc_kernel, tq=tq, scale=HEAD_DIM ** -0.5),
        grid_spec=grid_spec,
        out_shape=jax.ShapeDtypeStruct((b, s, NSA_HEADS * HEAD_DIM), BF16),
        compiler_params=_params("parallel", "parallel", "arbitrary"),
    )(qi_tab, ki_tab, q, kv, kv, sel, cosf, sinf)


def _band_kernel(*refs, nheads, kv_heads, nkv, tq, max_dist, rope_q, with_lse, scale):
    q_ref = refs[0]
    k_refs = refs[1:1 + nkv]
    v_refs = refs[1 + nkv:1 + 2 * nkv]
    pos = 1 + 2 * nkv
    if rope_q:
        cos_ref, sin_ref = refs[pos:pos + 2]
        pos += 2
    o_ref = refs[pos]
    lse_ref = refs[pos + 1] if with_lse else None
    qi = pl.program_id(2)
    qpos = qi * tq + lax.broadcasted_iota(jnp.int32, (tq, 1), 0)
    kpos = (qi - (nkv - 1)) * tq + lax.broadcasted_iota(jnp.int32, (1, nkv * tq), 1)
    diff = qpos - kpos
    mask = jnp.logical_and(jnp.logical_and(diff >= 0, diff <= max_dist), kpos >= 0)
    lse_acc = jnp.zeros((tq, LANE), F32)
    lane = lax.broadcasted_iota(jnp.int32, (tq, LANE), 1)
    for g in range(nheads):
        sl = slice(g * HEAD_DIM, (g + 1) * HEAD_DIM)
        ksl = sl if kv_heads > 1 else slice(0, HEAD_DIM)
        q = q_ref[0, :, sl]
        if rope_q:
            q = _rope(q.astype(F32), cos_ref[0], sin_ref[0]).astype(BF16)
        kcat = jnp.concatenate([r[0, :, ksl] for r in k_refs], axis=0)
        vcat = jnp.concatenate([r[0, :, ksl] for r in v_refs], axis=0)
        s = _dot_nt(q, kcat) * scale
        s = jnp.where(mask, s, MASKED)
        mx = jnp.max(s, axis=-1, keepdims=True)
        e = jnp.where(mask, jnp.exp(s - mx), 0.0)
        den = jnp.sum(e, axis=-1, keepdims=True)
        o_ref[0, :, sl] = (_dot(e.astype(BF16), vcat) / den).astype(o_ref.dtype)
        if with_lse:
            lse_acc = jnp.where(lane == g, mx + jnp.log(den), lse_acc)
    if with_lse:
        lse_ref[0] = lse_acc


def band_attention(q, k, v, *, nheads, kv_heads, q_col, k_col, v_col, o_cols, ncol, max_dist, tq,
                   rope=None, with_lse=False, out_dtype=BF16):
    b, seq_len, _ = q.shape
    tq = min(tq, seq_len)
    assert seq_len % tq == 0
    nkv = -(-max_dist // tq) + 1
    qw = nheads * HEAD_DIM
    kw = kv_heads * HEAD_DIM

    def kv_spec(col_fn, back):
        return pl.BlockSpec((1, tq, kw), lambda bi, c, i: (bi, jnp.maximum(i - back, 0), col_fn(c)))

    in_specs = [pl.BlockSpec((1, tq, qw), lambda bi, c, i: (bi, i, q_col(c)))]
    in_specs += [kv_spec(k_col, nkv - 1 - t) for t in range(nkv)]
    in_specs += [kv_spec(v_col, nkv - 1 - t) for t in range(nkv)]
    args = [q] + [k] * nkv + [v] * nkv
    if rope is not None:
        in_specs += [pl.BlockSpec((1, tq, LANE), lambda bi, c, i: (bi, i, 0))] * 2
        args += list(rope)
    out_specs = [pl.BlockSpec((1, tq, qw), lambda bi, c, i: (bi, i, c))]
    out_shape = [jax.ShapeDtypeStruct((b, seq_len, o_cols * qw), out_dtype)]
    if with_lse:
        out_specs.append(pl.BlockSpec((1, tq, LANE), lambda bi, c, i: (bi, i, c)))
        out_shape.append(jax.ShapeDtypeStruct((b, seq_len, o_cols * LANE), F32))
    res = pl.pallas_call(
        functools.partial(_band_kernel, nheads=nheads, kv_heads=kv_heads, nkv=nkv, tq=tq, max_dist=max_dist,
                          rope_q=rope is not None, with_lse=with_lse, scale=HEAD_DIM ** -0.5),
        grid=(b, ncol, seq_len // tq),
        in_specs=in_specs,
        out_specs=out_specs,
        out_shape=out_shape,
        compiler_params=_params("parallel", "parallel", "parallel"),
    )(*args)
    return res if with_lse else res[0]


def _bmm(x, y):
    return jnp.einsum("bij,bjk->bik", x.astype(BF16), y.astype(BF16), preferred_element_type=F32)


def _bmm_nt(x, y):
    return jnp.einsum("bik,bjk->bij", x.astype(BF16), y.astype(BF16), preferred_element_type=F32)


def _wkv_chunk_kernel(r_ref, lw_ref, k_ref, v_ref, a_ref, b_ref, q_ref, y0_ref, gt_ref, ht_ref, *, nb, c):
    pair = 2 * RWKV_HEAD
    c2 = 2 * c

    def load(ref):
        return ref[0].reshape(nb, c, pair)

    r, lw, k, v, a, b = [load(ref) for ref in (r_ref, lw_ref, k_ref, v_ref, a_ref, b_ref)]
    row = lax.broadcasted_iota(jnp.int32, (c, c), 0)
    col = lax.broadcasted_iota(jnp.int32, (c, c), 1)
    tril = jnp.broadcast_to((row >= col).astype(BF16), (nb, c, c))
    hi = lw.astype(BF16)
    rem = lw - hi.astype(F32)
    mid = rem.astype(BF16)
    lo = (rem - mid.astype(F32)).astype(BF16)
    cum = _bmm(tril, hi) + _bmm(tril, mid) + _bmm(tril, lo)
    last = cum[:, c - 1:c, :]
    inv = jnp.exp(-cum)
    tail = jnp.exp(last - cum)
    lane = lax.broadcasted_iota(jnp.int32, (1, 1, pair), 2)
    first = lane < RWKV_HEAD

    def stack(x):
        return jnp.concatenate([jnp.where(first, x, 0.0), jnp.where(first, 0.0, x)], axis=1)

    a_s = stack(a * jnp.exp(cum - lw))
    r_s = stack(r * jnp.exp(cum))
    b_s = stack(b * inv)
    k_s = stack(k * inv)
    v_s = stack(v)
    bh_s = stack(b * tail)
    kh_s = stack(k * tail)
    ar = jnp.concatenate([a_s, r_s], axis=1)
    pb = _bmm_nt(ar, b_s)
    pk = _bmm_nt(ar, k_s)
    row2 = lax.broadcasted_iota(jnp.int32, (c2, c2), 0) % c
    col2 = lax.broadcasted_iota(jnp.int32, (c2, c2), 1) % c
    strict = row2 > col2
    lower = row2 >= col2
    l_ab = jnp.where(strict, pb[:, :c2], 0.0)
    m_rb = jnp.where(lower, pb[:, c2:], 0.0)
    l_ak = jnp.where(strict, pk[:, :c2], 0.0)
    m_rk = jnp.where(lower, pk[:, c2:], 0.0)
    eye = (lax.broadcasted_iota(jnp.int32, (c2, c2), 0) == lax.broadcasted_iota(jnp.int32, (c2, c2), 1))
    eye = eye.astype(F32)
    tinv = eye + l_ab
    pw = l_ab
    for _ in range(int(np.log2(c)) - 1):
        pw = _bmm(pw, pw)
        tinv = tinv + _bmm(tinv, pw)
    tu = _bmm(tinv, jnp.concatenate([_bmm(l_ak, v_s), a_s], axis=2))
    u0_s = tu[:, :, :pair]
    ta_s = tu[:, :, pair:]
    mu = _bmm(m_rb, jnp.concatenate([ta_s, u0_s], axis=2))
    q_s = r_s + mu[:, :, :pair]
    y0_s = _bmm(m_rk, v_s) + mu[:, :, pair:]
    q_ref[0] = (q_s[:, :c] + q_s[:, c:]).reshape(nb * c, pair).astype(q_ref.dtype)
    y0_ref[0] = (y0_s[:, :c] + y0_s[:, c:]).reshape(nb * c, pair)
    eye_b = jnp.broadcast_to(eye.astype(BF16), (nb, c2, c2))
    tr = _bmm_nt(eye_b, jnp.concatenate([bh_s, kh_s], axis=1))
    bh_t = tr[:, :, :c2]
    kh_t = tr[:, :, c2:]
    gh = _bmm(bh_t, jnp.concatenate([ta_s, u0_s], axis=2))
    gt_ref[0, :, 0] = (eye * jnp.exp(last) + gh[:, :, :pair]).astype(gt_ref.dtype)
    ht_ref[0, :, 0] = (gh[:, :, pair:] + _bmm(kh_t, v_s)).astype(ht_ref.dtype)


def _wkv_scan_kernel(q_ref, y0_ref, gt_ref, ht_ref, y_ref, state_ref, *, npair):
    pair = 2 * RWKV_HEAD

    @pl.when(pl.program_id(1) == 0)
    def _():
        state_ref[...] = jnp.zeros_like(state_ref)

    sls = [slice(p * pair, (p + 1) * pair) for p in range(npair)]
    st = [state_ref[p].astype(BF16) for p in range(npair)]
    ys = [_dot(q_ref[0, :, sls[p]], st[p]) + y0_ref[0, :, sls[p]] for p in range(npair)]
    new = [_dot(gt_ref[0, 0, p], st[p]) + ht_ref[0, 0, p].astype(F32) for p in range(npair)]
    for p in range(npair):
        y_ref[0, :, sls[p]] = ys[p]
        state_ref[p] = new[p]


def wkv7(r, lw, k, v, a, b):
    bsz, s, width = r.shape
    pair = 2 * RWKV_HEAD
    npair = width // pair
    c = min(RWKV_CHUNK, s)
    nb = min(RWKV_CHUNKS_PER_STEP, s // c)
    nch = s // c
    assert s % (nb * c) == 0 and width % pair == 0
    spec = pl.BlockSpec((1, nb * c, pair), lambda bi, p, j: (bi, j, p))
    mat = pl.BlockSpec((1, nb, 1, pair, pair), lambda bi, p, j: (bi, j, p, 0, 0))
    q, y0, gt, ht = pl.pallas_call(
        functools.partial(_wkv_chunk_kernel, nb=nb, c=c),
        grid=(bsz, npair, nch // nb),
        in_specs=[spec] * 6,
        out_specs=[spec, spec, mat, mat],
        out_shape=[jax.ShapeDtypeStruct((bsz, s, width), BF16), jax.ShapeDtypeStruct((bsz, s, width), F32),
                   jax.ShapeDtypeStruct((bsz, nch, npair, pair, pair), BF16),
                   jax.ShapeDtypeStruct((bsz, nch, npair, pair, pair), BF16)],
        compiler_params=_params("parallel", "parallel", "parallel"),
    )(r, lw, k, v, a, b)
    row = pl.BlockSpec((1, c, width), lambda bi, j: (bi, j, 0))
    mats = pl.BlockSpec((1, 1, npair, pair, pair), lambda bi, j: (bi, j, 0, 0, 0))
    return pl.pallas_call(
        functools.partial(_wkv_scan_kernel, npair=npair),
        grid=(bsz, nch),
        in_specs=[row, row, mats, mats],
        out_specs=row,
        out_shape=jax.ShapeDtypeStruct((bsz, s, width), F32),
        scratch_shapes=[pltpu.VMEM((npair, pair, pair), F32)],
        compiler_params=_params("parallel", "arbitrary"),
    )(q, y0, gt, ht)


def _moe_up_kernel(te_ref, tv_ref, a_ref, wg_ref, wu_ref, o_ref, wg_bf, wu_bf):
    i = pl.program_id(1)
    changed = jnp.logical_or(i == 0, te_ref[i] != te_ref[jnp.maximum(i - 1, 0)])

    @pl.when(changed)
    def _():
        wg_bf[...] = wg_ref[0].astype(BF16)
        wu_bf[...] = wu_ref[0].astype(BF16)

    @pl.when(tv_ref[i] > 0)
    def _():
        a = a_ref[...]
        g = _dot(a, wg_bf[...])
        u = _dot(a, wu_bf[...])
        o_ref[...] = (_silu(g) * u).astype(o_ref.dtype)

    @pl.when(tv_ref[i] == 0)
    def _():
        o_ref[...] = jnp.zeros_like(o_ref)


def moe_up(tile_expert, tile_valid, xs, w_gu, *, tm, tn):
    r, k = xs.shape
    f = w_gu.shape[2] // 2
    nj = f // tn
    grid_spec = pltpu.PrefetchScalarGridSpec(
        num_scalar_prefetch=2,
        grid=(nj, r // tm),
        in_specs=[pl.BlockSpec((tm, k), lambda j, i, te, tv: (i, 0)),
                  pl.BlockSpec((1, k, tn), lambda j, i, te, tv: (te[i], 0, j)),
                  pl.BlockSpec((1, k, tn), lambda j, i, te, tv: (te[i], 0, j + nj))],
        out_specs=pl.BlockSpec((tm, tn), lambda j, i, te, tv: (i, j)),
        scratch_shapes=[pltpu.VMEM((k, tn), BF16), pltpu.VMEM((k, tn), BF16)],
    )
    return pl.pallas_call(
        _moe_up_kernel,
        grid_spec=grid_spec,
        out_shape=jax.ShapeDtypeStruct((r, f), BF16),
        compiler_params=_params("parallel", "arbitrary"),
    )(tile_expert, tile_valid, xs, w_gu, w_gu)


def _moe_down_kernel(te_ref, tv_ref, a_ref, w_ref, o_ref):
    i = pl.program_id(0)

    @pl.when(tv_ref[i] > 0)
    def _():
        o_ref[...] = _dot(a_ref[...], w_ref[0]).astype(o_ref.dtype)

    @pl.when(tv_ref[i] == 0)
    def _():
        o_ref[...] = jnp.zeros_like(o_ref)


def moe_down(tile_expert, tile_valid, act, w_down, *, tm):
    r, f = act.shape
    d = w_down.shape[2]
    grid_spec = pltpu.PrefetchScalarGridSpec(
        num_scalar_prefetch=2,
        grid=(r // tm,),
        in_specs=[pl.BlockSpec((tm, f), lambda i, te, tv: (i, 0)),
                  pl.BlockSpec((1, f, d), lambda i, te, tv: (te[i], 0, 0))],
        out_specs=pl.BlockSpec((tm, d), lambda i, te, tv: (i, 0)),
    )
    return pl.pallas_call(
        _moe_down_kernel,
        grid_spec=grid_spec,
        out_shape=jax.ShapeDtypeStruct((r, d), F32),
        compiler_params=_params("arbitrary"),
    )(tile_expert, tile_valid, act, w_down)


def _rope_tables(positions):
    inv_freq = ROPE_THETA ** (-jnp.arange(ROPE_HALF, dtype=F32) / ROPE_HALF)
    ang = positions.astype(F32)[:, :, None] * inv_freq
    cos = jnp.cos(ang)
    sin = jnp.sin(ang)
    b, s = positions.shape
    pad1 = jnp.ones((b, s, HEAD_DIM - ROPE_DIM), F32)
    pad0 = jnp.zeros((b, s, HEAD_DIM - ROPE_DIM), F32)
    return jnp.concatenate([cos, cos, pad1], axis=-1), jnp.concatenate([-sin, sin, pad0], axis=-1)


def _pad_cols(w, n):
    return w if w.shape[-1] == n else jnp.pad(w, ((0, 0), (0, n - w.shape[-1])))


def nsa_branch(q_a, kv_a, gate_a, cosf, sinf, cmp_pe, cmp_w1, cmp_w2):
    b, s, _ = q_a.shape
    nc = s // CMP_STRIDE - 1
    ncp = -(-nc // LANE) * LANE
    kvw = NSA_KV_HEADS * HEAD_DIM

    def blocks(t):
        ch = t.reshape(b, s // CMP_STRIDE, CMP_STRIDE, NSA_KV_HEADS, HEAD_DIM)
        blk = jnp.concatenate([ch[:, :-1], ch[:, 1:]], axis=2)
        return blk.transpose(0, 1, 3, 2, 4).reshape(b * nc * NSA_KV_HEADS, CMP_BLOCK * HEAD_DIM)

    rows = b * nc * NSA_KV_HEADS
    rows_p = -(-rows // 512) * 512 if rows > 512 else -(-rows // 8) * 8
    flat = jnp.stack([blocks(kv_a[..., :kvw]), blocks(kv_a[..., kvw:2 * kvw])]).astype(F32)
    flat = jnp.pad(flat, ((0, 0), (0, rows_p - rows), (0, 0)))
    comp = nsa_compress(flat, cmp_pe.reshape(2, 1, CMP_BLOCK * HEAD_DIM), cmp_w1.astype(BF16), cmp_w2.astype(BF16))
    comp = comp[:, :rows].reshape(2, b, nc, NSA_KV_HEADS, HEAD_DIM).transpose(0, 1, 3, 2, 4)
    comp = jnp.pad(comp, ((0, 0), (0, 0), (0, 0), (0, ncp - nc), (0, 0)))
    kc = comp[0]
    vct = comp[1].transpose(0, 1, 3, 2)
    tq = min(256, s)
    o_cmp, sel = cmp_select(q_a, kc, vct, nc=nc, tq=tq)
    o_slc = slc_attention(q_a, kv_a, sel, cosf, sinf, k_col=4, v_col=6, tq=min(512, s))
    o_win = band_attention(q_a, kv_a, kv_a, nheads=NSA_GROUP, kv_heads=1, q_col=lambda c: c,
                           k_col=lambda c: 8 + c, v_col=lambda c: 10 + c, o_cols=NSA_KV_HEADS,
                           ncol=NSA_KV_HEADS, max_dist=WIN_SIZE - 1, tq=tq, rope=(cosf, sinf))
    g = gate_a[..., :3 * NSA_HEADS].reshape(b, s, NSA_HEADS, 1, 3)

    def heads(t):
        return t.reshape(b, s, NSA_HEADS, HEAD_DIM).astype(F32)

    o = g[..., 0] * heads(o_cmp) + g[..., 1] * heads(o_slc) + g[..., 2] * heads(o_win)
    return o.reshape(b, s, NSA_HEADS * HEAD_DIM).astype(BF16)


def rwkv_branch(z, mu, vec, w_up, a_up, g_up, v_first, v_res):
    b, s, _ = z.shape
    t = b * s
    z_prev = jnp.pad(z, ((0, 0), (1, 0), (0, 0)))[:, :-1]
    z = z + (z_prev - z) * mu
    w_ = RWKV_WIDTH
    r, k, v = z[..., :w_], z[..., w_:2 * w_], z[..., 2 * w_:3 * w_]
    o = 3 * w_
    zw, za, zg = z[..., o:o + LORA_W], z[..., o + LORA_W:o + LORA_W + LORA_A], z[..., o + LORA_W + LORA_A:]
    w0, a0, k_k, k_a, r_k, lnx_g, lnx_b = [vec[i] for i in range(7)]

    def lora(xin, wmat):
        return matmul(xin.reshape(t, -1).astype(BF16), wmat.astype(BF16), out_dtype=F32, tm=1024,
                      tn=512).reshape(b, s, -1)

    w = -jax.nn.softplus(-(w0 + lora(jnp.tanh(zw), w_up))) - 0.5
    a = jax.nn.sigmoid(a0 + lora(za, a_up))
    g = lora(jax.nn.sigmoid(zg), g_up)
    if v_res is None:
        v_first = v
    else:
        v0, v_down, v_up = v_res
        vd = matmul(v.reshape(t, w_).astype(BF16), _pad_cols(v_down, LANE).astype(BF16), out_dtype=BF16, tm=1024,
                    tn=LANE)
        vu = matmul(vd, jnp.pad(v_up, ((0, LANE - v_up.shape[0]), (0, 0))).astype(BF16), out_dtype=F32, tm=1024,
                    tn=512).reshape(b, s, w_)
        v = v + (v_first - v) * jax.nn.sigmoid(v0 + vu)

    def heads(x):
        return x.reshape(b, s, RWKV_HEADS, RWKV_HEAD)

    def flat(x):
        return x.reshape(b, s, w_)

    kk = heads(k * k_k)
    kk = flat(kk / jnp.maximum(jnp.sqrt(jnp.sum(kk * kk, axis=-1, keepdims=True)), 1e-12))
    k = k * (1.0 + (a - 1.0) * k_a)
    y = heads(wkv7(r, -jnp.exp(w), k, v, -kk, kk * a))
    mean = jnp.mean(y, axis=-1, keepdims=True)
    var = jnp.mean(jnp.square(y - mean), axis=-1, keepdims=True)
    y = flat((y - mean) * lax.rsqrt(var + LNX_EPS))
    bonus = flat(jnp.sum(heads(r * k * r_k), axis=-1, keepdims=True) * heads(v))
    y = (y * lnx_g + lnx_b + bonus) * g
    return y.astype(BF16), v_first


def dilated_branch(qkv):
    b, s, width = qkv.shape
    blocks_per_tok = width // DIL_OUT
    outs, lses = [], []
    for gi, (win, dil) in enumerate(DIL_PATTERNS):
        view = qkv.reshape(b, s // dil, dil * width)
        o, lse = band_attention(
            view, view, view, nheads=DIL_HPG, kv_heads=DIL_HPG,
            q_col=lambda c, gi=gi: c * blocks_per_tok + gi,
            k_col=lambda c, gi=gi: c * blocks_per_tok + len(DIL_PATTERNS) + gi,
            v_col=lambda c, gi=gi: c * blocks_per_tok + 2 * len(DIL_PATTERNS) + gi,
            o_cols=dil, ncol=dil, max_dist=win // dil, tq=256, with_lse=True, out_dtype=F32)
        outs.append(o.reshape(b, s, DIL_HPG, HEAD_DIM))
        lses.append(lse.reshape(b, s, LANE)[..., :DIL_HPG])
    alpha = jax.nn.softmax(jnp.stack(lses, axis=-1), axis=-1)
    o = sum(alpha[..., gi:gi + 1] * outs[gi] for gi in range(len(DIL_PATTERNS)))
    return o.reshape(b, s, DIL_OUT).astype(BF16)


def moe_ffn(h, router_w, router_b, w_gu, w_down, *, tm):
    t, d = h.shape
    logits = matmul(h, _pad_cols(router_w, LANE).astype(BF16), out_dtype=F32, tm=1024, tn=LANE)[:, :N_EXPERTS]
    logits = logits + router_b
    top_v, top_i = lax.top_k(logits, TOP_K)
    wts = jax.nn.softmax(top_v, axis=-1)
    flat_e = top_i.reshape(-1)
    onehot = (flat_e[:, None] == jnp.arange(N_EXPERTS)[None, :]).astype(jnp.int32)
    rank = jnp.take_along_axis(jnp.cumsum(onehot, axis=0), flat_e[:, None], axis=1)[:, 0] - 1
    counts = jnp.sum(onehot, axis=0)
    tiles_per = (counts + tm - 1) // tm
    tile_end = jnp.cumsum(tiles_per)
    group_start = (tile_end - tiles_per) * tm
    dest = group_start[flat_e] + rank
    ntiles = (TOP_K * t) // tm + N_EXPERTS
    rows = ntiles * tm
    row_token = jnp.zeros((rows,), jnp.int32).at[dest].set(jnp.arange(TOP_K * t, dtype=jnp.int32) // TOP_K)
    tile_ids = jnp.arange(ntiles, dtype=jnp.int32)
    tile_valid = (tile_ids < tile_end[-1]).astype(jnp.int32)
    tile_expert = jnp.minimum(jnp.searchsorted(tile_end, tile_ids, side="right"), N_EXPERTS - 1).astype(jnp.int32)
    xs = jnp.take(h, row_token, axis=0)
    act = moe_up(tile_expert, tile_valid, xs, w_gu, tm=tm, tn=512)
    out = moe_down(tile_expert, tile_valid, act, w_down, tm=tm)
    picked = jnp.take(out, dest, axis=0).reshape(t, TOP_K, d)
    return jnp.sum(picked * wts[..., None], axis=1)


def kernel(x, c, positions, ada_w, ada_b, norm_g, w_in, cmp_pe, cmp_w1, cmp_w2, rwkv_mu, rwkv_vec, w_up, a_up, g_up, v_res0, v_res_down, v_res_up, w_br_a, w_br_b, w_br_c, w_out, ffn_gu, ffn_down, router_w, router_b, moe_gu, moe_down):
    b, s, d = x.shape
    depth = ada_w.shape[0]
    t = b * s
    tm_row = min(512, s)
    cosf, sinf = _rope_tables(positions)
    cos_t = cosf.reshape(t, LANE)
    sin_t = sinf.reshape(t, LANE)

    cond = jnp.pad(jax.nn.silu(c), ((0, 8 - b % 8 if b % 8 else 0), (0, 0))).astype(BF16)
    mods = []
    for l in range(depth):
        mod = matmul(cond, ada_w[l], out_dtype=F32, tm=cond.shape[0], tn=512)[:b] + ada_b[l]
        mods.append(mod.reshape(b, 6, 1, d))

    def mod_of(l, i):
        return mods[l][:, i]

    q_cols = NSA_HEADS * HEAD_DIM
    kv_cols = 6 * NSA_KV_HEADS * HEAD_DIM
    gate_cols = 3 * NSA_HEADS
    rwkv_cols = 3 * RWKV_WIDTH + LORA_W + LORA_A + LORA_G
    dil_cols = 3 * DIL_HEADS * HEAD_DIM
    offs = np.cumsum([0, q_cols, kv_cols, gate_cols, rwkv_cols, dil_cols, 3 * d]).tolist()
    rwkv_pad = -(-rwkv_cols // 512) * 512

    xf = x.reshape(t, d)
    h = norm_mod(xf, norm_g[0, 0][None], mod_of(0, 1), mod_of(0, 0), seq=s, tm=tm_row)
    v_first = None
    for l in range(depth):
        wl = w_in[l]
        seg = lambda i: wl[:, offs[i]:offs[i + 1]]
        q_a = matmul(h, seg(0).astype(BF16), out_dtype=BF16, tm=1024, tn=512)
        kv_a = matmul(h, seg(1).astype(BF16), out_dtype=BF16, tm=1024, tn=NSA_KV_HEADS * HEAD_DIM,
                      rope=(cos_t, sin_t, (2, 4)))
        gate_a = matmul(h, _pad_cols(seg(2), LANE).astype(BF16), out_dtype=F32, tm=1024, tn=LANE, act="sigmoid")
        z_b = matmul(h, _pad_cols(seg(3), rwkv_pad).astype(BF16), out_dtype=F32, tm=1024, tn=512)[:, :rwkv_cols]
        qkv_c = matmul(h, seg(4).astype(BF16), out_dtype=BF16, tm=1024, tn=512,
                       rope=(cos_t, sin_t, tuple(range(2 * DIL_HEADS * HEAD_DIM // 512))))
        mg = matmul(h, seg(5).astype(BF16), out_dtype=BF16, tm=1024, tn=512, act="sigmoid")

        y_a = nsa_branch(q_a.reshape(b, s, -1), kv_a.reshape(b, s, -1), gate_a.reshape(b, s, -1), cosf, sinf,
                         cmp_pe[l], cmp_w1[l], cmp_w2[l])
        v_res = None if l == 0 else (v_res0[l - 1], v_res_down[l - 1], v_res_up[l - 1])
        y_b, v_first = rwkv_branch(z_b.reshape(b, s, -1), rwkv_mu[l], rwkv_vec[l], w_up[l], a_up[l], g_up[l],
                                   v_first, v_res)
        y_c = dilated_branch(qkv_c.reshape(b, s, -1))
        merged = branch_merge(y_a.reshape(t, -1), y_b.reshape(t, -1), y_c.reshape(t, -1), w_br_a[l].astype(BF16),
                              w_br_b[l].astype(BF16), w_br_c[l].astype(BF16), mg, tm=1024, tn=512)
        xf, h = matmul_close(merged, w_out[l].astype(BF16), xf, norm_g[l, 1][None], mod_of(l, 2),
                             (norm_g[l, 2][None], mod_of(l, 4), mod_of(l, 3)), seq=s, tm=tm_row, tk=512)

        nxt = None if l == depth - 1 else (norm_g[l + 1, 0][None], mod_of(l + 1, 1), mod_of(l + 1, 0))
        if l % 2 == 0:
            act = swiglu_up(h, ffn_gu[l // 2].astype(BF16), tm=1024, tn=512)
            xf, h = matmul_close(act, ffn_down[l // 2].astype(BF16), xf, norm_g[l, 3][None], mod_of(l, 5), nxt,
                                 seq=s, tm=tm_row, tk=512)
        else:
            y = moe_ffn(h, router_w[l // 2], router_b[l // 2], moe_gu[l // 2], moe_down[l // 2].astype(BF16),
                        tm=512)
            xf, h = close_sublayer(y, xf, norm_g[l, 3][None], mod_of(l, 5), nxt, seq=s, tm=tm_row)
    return xf.reshape(b, s, d)
```

```python
import functools

import numpy as np
import jax
import jax.numpy as jnp
from jax import lax
from jax.experimental import pallas as pl
from jax.experimental.pallas import tpu as pltpu

F32 = jnp.float32
BF16 = jnp.bfloat16

HEAD_DIM = 128
ROPE_DIM = HEAD_DIM // 4
ROPE_HALF = ROPE_DIM // 2
ROPE_THETA = 500000.0
NORM_EPS = 1e-6

NSA_HEADS = 8
NSA_KV_HEADS = 2
NSA_GROUP = NSA_HEADS // NSA_KV_HEADS
CMP_BLOCK = 32
CMP_STRIDE = 16
SLC_BLOCK = 64
SLC_TOPN = 16
WIN_SIZE = 512
FORCE_BONUS = 1e4
SLC_PICK_BIAS = 8192.0

RWKV_HEADS = 16
RWKV_HEAD = 64
RWKV_WIDTH = RWKV_HEADS * RWKV_HEAD
LORA_W = 96
LORA_A = 96
LORA_G = 256
LNX_EPS = 64e-5
RWKV_CHUNK = 64
RWKV_CHUNKS_PER_STEP = 8

DIL_PATTERNS = ((128, 1), (512, 4), (2048, 16))
DIL_HPG = 4
DIL_HEADS = DIL_HPG * len(DIL_PATTERNS)
DIL_OUT = DIL_HPG * HEAD_DIM

N_EXPERTS = 8
TOP_K = 2

LANE = 128
VMEM_LIMIT_BYTES = 56 * 1024 * 1024
MASKED = -1e30


def _params(*sem):
    return pltpu.CompilerParams(dimension_semantics=sem, vmem_limit_bytes=VMEM_LIMIT_BYTES)


def _sigmoid(x):
    return 1.0 / (1.0 + jnp.exp(-x))


def _silu(x):
    return x * _sigmoid(x)


def _dot(a, b):
    return jnp.dot(a, b, preferred_element_type=F32)


def _dot_nt(a, b):
    return lax.dot_general(a, b, (((1,), (1,)), ((), ())), preferred_element_type=F32)


def _dot_tn(a, b):
    return lax.dot_general(a, b, (((0,), (0,)), ((), ())), preferred_element_type=F32)


def _rope(t, cosf, sinf):
    lane = lax.broadcasted_iota(jnp.int32, t.shape, 1)
    swapped = jnp.where(lane < ROPE_HALF, pltpu.roll(t, LANE - ROPE_HALF, 1), pltpu.roll(t, ROPE_HALF, 1))
    return t * cosf + swapped * sinf


def _rms(y):
    return y * lax.rsqrt(jnp.mean(y * y, axis=-1, keepdims=True) + NORM_EPS)


def _matmul_kernel(*refs, act, rope_tiles):
    if rope_tiles:
        a_ref, w_ref, cos_ref, sin_ref, o_ref = refs
    else:
        a_ref, w_ref, o_ref = refs
    acc = _dot(a_ref[...].astype(BF16), w_ref[...].astype(BF16))
    if act == "sigmoid":
        acc = _sigmoid(acc)
    elif act == "tanh":
        acc = jnp.tanh(acc)
    if not rope_tiles:
        o_ref[...] = acc.astype(o_ref.dtype)
        return
    j = pl.program_id(1)
    pred = functools.reduce(jnp.logical_or, [j == t for t in rope_tiles])

    @pl.when(pred)
    def _():
        cosf = cos_ref[...]
        sinf = sin_ref[...]
        for c in range(acc.shape[1] // LANE):
            sl = slice(c * LANE, (c + 1) * LANE)
            o_ref[:, sl] = _rope(acc[:, sl], cosf, sinf).astype(o_ref.dtype)

    @pl.when(jnp.logical_not(pred))
    def _():
        o_ref[...] = acc.astype(o_ref.dtype)


def _matmul_shift_kernel(a_ref, ap_ref, w_ref, mu_ref, o_ref, *, seq, tm):
    w = w_ref[...]
    z = _dot(a_ref[...], w)
    zp = _dot(ap_ref[...], w)
    at_start = (pl.program_id(0) * tm) % seq == 0
    last = zp.shape[0] - 1
    prev_row = jnp.where(at_start, 0.0, zp[last:last + 1, :])
    rowid = lax.broadcasted_iota(jnp.int32, z.shape, 0)
    shifted = jnp.where(rowid == 0, prev_row, pltpu.roll(z, 1, 0))
    o_ref[...] = (z + (shifted - z) * mu_ref[...]).astype(o_ref.dtype)


BF16_SUBLANES = 16


def matmul_token_shift(a, w, mu, *, seq, tm, tn):
    m, k = a.shape
    n = w.shape[1]
    assert m % tm == 0 and n % tn == 0 and seq % tm == 0
    per = tm // BF16_SUBLANES
    return pl.pallas_call(
        functools.partial(_matmul_shift_kernel, seq=seq, tm=tm),
        grid=(m // tm, n // tn),
        in_specs=[pl.BlockSpec((tm, k), lambda i, j: (i, 0)),
                  pl.BlockSpec((BF16_SUBLANES, k), lambda i, j: (jnp.maximum(i * per - 1, 0), 0)),
                  pl.BlockSpec((k, tn), lambda i, j: (0, j)),
                  pl.BlockSpec((1, tn), lambda i, j: (0, j))],
        out_specs=pl.BlockSpec((tm, tn), lambda i, j: (i, j)),
        out_shape=jax.ShapeDtypeStruct((m, n), F32),
        compiler_params=_params("parallel", "parallel"),
    )(a, a, w, mu)


def matmul(a, w, *, out_dtype, tm, tn, act=None, rope=None, layer=None):
    m, k = a.shape
    n = w.shape[-1]
    tm = min(tm, m)
    assert m % tm == 0 and n % tn == 0, (m, n, tm, tn)
    if layer is None:
        w_spec = pl.BlockSpec((k, tn), lambda i, j: (0, j))
    else:
        w_spec = pl.BlockSpec((None, k, tn), lambda i, j: (layer, 0, j))
    in_specs = [pl.BlockSpec((tm, k), lambda i, j: (i, 0)), w_spec]
    args = [a, w]
    tiles = ()
    if rope is not None:
        cosf, sinf, tiles = rope
        in_specs += [pl.BlockSpec((tm, LANE), lambda i, j: (i, 0))] * 2
        args += [cosf, sinf]
    return pl.pallas_call(
        functools.partial(_matmul_kernel, act=act, rope_tiles=tuple(tiles)),
        grid=(m // tm, n // tn),
        in_specs=in_specs,
        out_specs=pl.BlockSpec((tm, tn), lambda i, j: (i, j)),
        out_shape=jax.ShapeDtypeStruct((m, n), out_dtype),
        compiler_params=_params("parallel", "parallel"),
    )(*args)


def _swiglu_up_kernel(a_ref, wg_ref, wu_ref, o_ref):
    a = a_ref[...]
    g = _dot(a, wg_ref[...])
    u = _dot(a, wu_ref[...])
    o_ref[...] = (_silu(g) * u).astype(o_ref.dtype)


def swiglu_up(a, w_gu, *, tm, tn):
    m, k = a.shape
    f = w_gu.shape[1] // 2
    assert m % tm == 0 and f % tn == 0
    nj = f // tn
    return pl.pallas_call(
        _swiglu_up_kernel,
        grid=(m // tm, nj),
        in_specs=[pl.BlockSpec((tm, k), lambda i, j: (i, 0)),
                  pl.BlockSpec((k, tn), lambda i, j: (0, j)),
                  pl.BlockSpec((k, tn), lambda i, j: (0, j + nj))],
        out_specs=pl.BlockSpec((tm, tn), lambda i, j: (i, j)),
        out_shape=jax.ShapeDtypeStruct((m, f), BF16),
        compiler_params=_params("parallel", "parallel"),
    )(a, w_gu, w_gu)


def _close_sublayer(x, y, gpost, gt, nxt):
    xn = x + gt * (_rms(y) * gpost)
    if nxt is None:
        return xn, None
    gnext, sc, sh = nxt
    return xn, (_rms(xn) * gnext) * (1.0 + sc) + sh


def _matmul_close_kernel(*refs, with_next):
    if with_next:
        a_ref, w_ref, x_ref, gp_ref, gt_ref, gn_ref, sc_ref, sh_ref, xo_ref, ho_ref, acc_ref = refs
    else:
        a_ref, w_ref, x_ref, gp_ref, gt_ref, xo_ref, acc_ref = refs
    kk = pl.program_id(1)

    @pl.when(kk == 0)
    def _():
        acc_ref[...] = jnp.zeros_like(acc_ref)

    acc_ref[...] += _dot(a_ref[...], w_ref[...])

    @pl.when(kk == pl.num_programs(1) - 1)
    def _():
        nxt = (gn_ref[...], sc_ref[0], sh_ref[0]) if with_next else None
        xn, h = _close_sublayer(x_ref[...], acc_ref[...], gp_ref[...], gt_ref[0], nxt)
        xo_ref[...] = xn
        if with_next:
            ho_ref[...] = h.astype(ho_ref.dtype)


def _close_specs(tm, d, seq, with_next, nidx):
    def row(i, *_):
        return (i, 0)

    def const(*_):
        return (0, 0)

    def batch(i, *_):
        return ((i * tm) // seq, 0, 0)

    specs = [pl.BlockSpec((tm, d), row), pl.BlockSpec((1, d), const), pl.BlockSpec((1, 1, d), batch)]
    if with_next:
        specs += [pl.BlockSpec((1, d), const), pl.BlockSpec((1, 1, d), batch), pl.BlockSpec((1, 1, d), batch)]
    return specs


def matmul_close(a, w, x, gpost, gt, nxt, *, seq, tm, tk):
    m, k = a.shape
    d = w.shape[1]
    assert m % tm == 0 and k % tk == 0 and seq % tm == 0
    with_next = nxt is not None
    args = [a, w, x, gpost, gt] + (list(nxt) if with_next else [])
    in_specs = [pl.BlockSpec((tm, tk), lambda i, kk: (i, kk)), pl.BlockSpec((tk, d), lambda i, kk: (kk, 0))]
    in_specs += _close_specs(tm, d, seq, with_next, 2)
    out_shape = [jax.ShapeDtypeStruct((m, d), F32)]
    out_specs = [pl.BlockSpec((tm, d), lambda i, kk: (i, 0))]
    if with_next:
        out_shape.append(jax.ShapeDtypeStruct((m, d), BF16))
        out_specs.append(pl.BlockSpec((tm, d), lambda i, kk: (i, 0)))
    res = pl.pallas_call(
        functools.partial(_matmul_close_kernel, with_next=with_next),
        grid=(m // tm, k // tk),
        in_specs=in_specs,
        out_specs=out_specs,
        out_shape=out_shape,
        scratch_shapes=[pltpu.VMEM((tm, d), F32)],
        compiler_params=_params("parallel", "arbitrary"),
    )(*args)
    return (res[0], res[1]) if with_next else (res[0], None)


def _close_kernel(*refs, with_next):
    if with_next:
        ya_ref, yb_ref, wt_ref, x_ref, gp_ref, gt_ref, gn_ref, sc_ref, sh_ref, xo_ref, ho_ref = refs
    else:
        ya_ref, yb_ref, wt_ref, x_ref, gp_ref, gt_ref, xo_ref = refs
    nxt = (gn_ref[...], sc_ref[0], sh_ref[0]) if with_next else None
    wt = wt_ref[...]
    y = wt[:, 0:1] * ya_ref[...].astype(F32) + wt[:, 1:2] * yb_ref[...].astype(F32)
    xn, h = _close_sublayer(x_ref[...], y, gp_ref[...], gt_ref[0], nxt)
    xo_ref[...] = xn
    if with_next:
        ho_ref[...] = h.astype(ho_ref.dtype)


def close_sublayer(ya, yb, wt, x, gpost, gt, nxt, *, seq, tm):
    m, d = x.shape
    with_next = nxt is not None
    args = [ya, yb, wt, x, gpost, gt] + (list(nxt) if with_next else [])
    in_specs = [pl.BlockSpec((tm, d), lambda i: (i, 0)), pl.BlockSpec((tm, d), lambda i: (i, 0)),
                pl.BlockSpec((tm, LANE), lambda i: (i, 0))] + _close_specs(tm, d, seq, with_next, 1)
    out_shape = [jax.ShapeDtypeStruct((m, d), F32)]
    out_specs = [pl.BlockSpec((tm, d), lambda i: (i, 0))]
    if with_next:
        out_shape.append(jax.ShapeDtypeStruct((m, d), BF16))
        out_specs.append(pl.BlockSpec((tm, d), lambda i: (i, 0)))
    res = pl.pallas_call(
        functools.partial(_close_kernel, with_next=with_next),
        grid=(m // tm,),
        in_specs=in_specs,
        out_specs=out_specs,
        out_shape=out_shape,
        compiler_params=_params("parallel"),
    )(*args)
    return (res[0], res[1]) if with_next else (res[0], None)


def _norm_mod_kernel(x_ref, g_ref, sc_ref, sh_ref, o_ref):
    o_ref[...] = ((_rms(x_ref[...]) * g_ref[...]) * (1.0 + sc_ref[0]) + sh_ref[0]).astype(o_ref.dtype)


def norm_mod(x, g, sc, sh, *, seq, tm):
    m, d = x.shape
    batch = lambda i: ((i * tm) // seq, 0, 0)
    return pl.pallas_call(
        _norm_mod_kernel,
        grid=(m // tm,),
        in_specs=[pl.BlockSpec((tm, d), lambda i: (i, 0)), pl.BlockSpec((1, d), lambda i: (0, 0)),
                  pl.BlockSpec((1, 1, d), batch), pl.BlockSpec((1, 1, d), batch)],
        out_specs=pl.BlockSpec((tm, d), lambda i: (i, 0)),
        out_shape=jax.ShapeDtypeStruct((m, d), BF16),
        compiler_params=_params("parallel"),
    )(x, g, sc, sh)


def _branch_merge_kernel(ya_ref, yb_ref, yc_ref, wa_ref, wb_ref, wc_ref, ga_ref, gb_ref, gc_ref, o_ref):
    acc = ga_ref[...].astype(F32) * _dot(ya_ref[...], wa_ref[...])
    acc += gb_ref[...].astype(F32) * _dot(yb_ref[...], wb_ref[...])
    acc += gc_ref[...].astype(F32) * _dot(yc_ref[...], wc_ref[...])
    o_ref[...] = acc.astype(o_ref.dtype)


def branch_merge(ya, yb, yc, wa, wb, wc, gates, *, tm, tn):
    m = ya.shape[0]
    d = wa.shape[1]
    nj = d // tn
    row = lambda width: pl.BlockSpec((tm, width), lambda i, j: (i, 0))
    wsp = lambda kdim: pl.BlockSpec((kdim, tn), lambda i, j: (0, j))
    gsp = lambda off: pl.BlockSpec((tm, tn), lambda i, j: (i, j + off * nj))
    return pl.pallas_call(
        _branch_merge_kernel,
        grid=(m // tm, nj),
        in_specs=[row(ya.shape[1]), row(yb.shape[1]), row(yc.shape[1]),
                  wsp(wa.shape[0]), wsp(wb.shape[0]), wsp(wc.shape[0]), gsp(0), gsp(1), gsp(2)],
        out_specs=pl.BlockSpec((tm, tn), lambda i, j: (i, j)),
        out_shape=jax.ShapeDtypeStruct((m, d), BF16),
        compiler_params=_params("parallel", "parallel"),
    )(ya, yb, yc, wa, wb, wc, gates, gates, gates)


def _compress_kernel(f_ref, pe_ref, w1_ref, w2_ref, o_ref):
    a = (f_ref[0] + pe_ref[0]).astype(BF16)
    hmid = _silu(_dot(a, w1_ref[0]))
    o_ref[0] = _dot(hmid.astype(BF16), w2_ref[0]).astype(o_ref.dtype)


def nsa_compress(flat, pe, w1, w2):
    two, r, kdim = flat.shape
    tm = min(r, 512)
    assert r % tm == 0
    return pl.pallas_call(
        _compress_kernel,
        grid=(two, r // tm),
        in_specs=[pl.BlockSpec((1, tm, kdim), lambda t, i: (t, i, 0)),
                  pl.BlockSpec((1, 1, kdim), lambda t, i: (t, 0, 0)),
                  pl.BlockSpec((1, kdim, HEAD_DIM), lambda t, i: (t, 0, 0)),
                  pl.BlockSpec((1, HEAD_DIM, HEAD_DIM), lambda t, i: (t, 0, 0))],
        out_specs=pl.BlockSpec((1, tm, HEAD_DIM), lambda t, i: (t, i, 0)),
        out_shape=jax.ShapeDtypeStruct((two, r, HEAD_DIM), BF16),
        compiler_params=_params("parallel", "parallel"),
    )(flat, pe, w1, w2)


def _cmp_select_kernel(q_ref, kc_ref, vct_ref, o_ref, sel_ref, *, tq, nc, ncp, nsp, scale):
    qi = pl.program_id(2)
    kc = kc_ref[0, 0]
    vct = vct_ref[0, 0]
    spos = qi * tq + lax.broadcasted_iota(jnp.int32, (1, tq), 1)
    cidx = lax.broadcasted_iota(jnp.int32, (ncp, 1), 0)
    valid = jnp.logical_and(cidx * CMP_STRIDE + (CMP_BLOCK - 1) <= spos, cidx < nc)
    psum = jnp.zeros((ncp, tq), F32)
    for g in range(NSA_GROUP):
        qg = q_ref[0, :, g * HEAD_DIM:(g + 1) * HEAD_DIM]
        st = _dot_nt(kc, qg) * scale
        st = jnp.where(valid, st, MASKED)
        mx = jnp.max(st, axis=0, keepdims=True)
        e = jnp.where(valid, jnp.exp(st - mx), 0.0)
        den = jnp.sum(e, axis=0, keepdims=True)
        p = e / jnp.where(den > 0, den, 1.0)
        psum = psum + p
        og_t = _dot(vct, p.astype(BF16))
        o_ref[0, :, g * HEAD_DIM:(g + 1) * HEAD_DIM] = og_t.T.astype(o_ref.dtype)
    jrow = lax.broadcasted_iota(jnp.int32, (nsp, ncp), 0)
    ccol = lax.broadcasted_iota(jnp.int32, (nsp, ncp), 1)
    c0 = ccol * CMP_STRIDE
    j0 = jrow * SLC_BLOCK
    cover_t = jnp.logical_and(c0 < j0 + SLC_BLOCK, c0 + CMP_BLOCK > j0).astype(F32)
    imp = jnp.dot(cover_t, psum, preferred_element_type=F32, precision=lax.Precision.HIGHEST)
    j = lax.broadcasted_iota(jnp.int32, (nsp, 1), 0).astype(F32)
    cur = (spos // SLC_BLOCK).astype(F32)
    forced = jnp.logical_or(jnp.logical_or(j == 0, j == cur), j == cur - 1)
    score = jnp.where(j <= cur, imp + FORCE_BONUS * forced.astype(F32), -jnp.inf)
    sel = jnp.zeros((nsp, tq), F32)
    for _ in range(SLC_TOPN):
        mx = jnp.max(score, axis=0, keepdims=True)
        first = jnp.min(jnp.where(score == mx, j, float(nsp)), axis=0, keepdims=True)
        pick = j == first
        sel = jnp.where(pick, 1.0, sel)
        score = jnp.where(pick, -jnp.inf, score)
    sel_ref[0, 0] = sel.T.astype(sel_ref.dtype)


def cmp_select(q, kc, vct, *, nc, tq):
    b, s, _ = q.shape
    ncp = kc.shape[2]
    nsp = LANE
    assert s % tq == 0 and s // SLC_BLOCK <= nsp
    gw = NSA_GROUP * HEAD_DIM
    return pl.pallas_call(
        functools.partial(_cmp_select_kernel, tq=tq, nc=nc, ncp=ncp, nsp=nsp, scale=HEAD_DIM ** -0.5),
        grid=(b, NSA_KV_HEADS, s // tq),
        in_specs=[pl.BlockSpec((1, tq, gw), lambda bi, h, i: (bi, i, h)),
                  pl.BlockSpec((1, 1, ncp, HEAD_DIM), lambda bi, h, i: (bi, h, 0, 0)),
                  pl.BlockSpec((1, 1, HEAD_DIM, ncp), lambda bi, h, i: (bi, h, 0, 0))],
        out_specs=[pl.BlockSpec((1, tq, gw), lambda bi, h, i: (bi, i, h)),
                   pl.BlockSpec((1, 1, tq, nsp), lambda bi, h, i: (bi, h, i, 0))],
        out_shape=[jax.ShapeDtypeStruct((b, s, NSA_HEADS * HEAD_DIM), BF16),
                   jax.ShapeDtypeStruct((b, NSA_KV_HEADS, s, nsp), BF16)],
        compiler_params=_params("parallel", "parallel", "parallel"),
    )(q, kc, vct)


def _slc_kernel(qi_ref, ki_ref, q_ref, k_ref, v_ref, sel_ref, cos_ref, sin_ref, o_ref, qaug_ref, m_ref, l_ref,
                acc_ref, *, tq, scale):
    step = pl.program_id(2)
    qi = qi_ref[step]
    ki = ki_ref[step]

    @pl.when(ki == 0)
    def _():
        cosf = cos_ref[0]
        sinf = sin_ref[0]
        pick = sel_ref[0, 0] * SLC_PICK_BIAS
        for g in range(NSA_GROUP):
            sl = slice(g * HEAD_DIM, (g + 1) * HEAD_DIM)
            qg = _rope(q_ref[0, :, sl].astype(F32), cosf, sinf) * scale
            qaug_ref[g] = jnp.concatenate([qg.astype(BF16), pick], axis=1)
        m_ref[...] = jnp.full_like(m_ref, MASKED)
        l_ref[...] = jnp.zeros_like(l_ref)
        acc_ref[...] = jnp.zeros_like(acc_ref)

    def accumulate(causal):
        nsp = sel_ref.shape[3]
        block_of_key = (ki * tq + lax.broadcasted_iota(jnp.int32, (tq, nsp), 0)) // SLC_BLOCK
        onehot = (block_of_key == lax.broadcasted_iota(jnp.int32, (tq, nsp), 1)).astype(BF16)
        kaug = jnp.concatenate([k_ref[0], onehot], axis=1)
        v = v_ref[0]
        if causal:
            qpos = lax.broadcasted_iota(jnp.int32, (tq, tq), 0)
            kpos = lax.broadcasted_iota(jnp.int32, (tq, tq), 1)
            visible = kpos <= qpos
        for g in range(NSA_GROUP):
            s = _dot_nt(qaug_ref[g], kaug)
            if causal:
                s = jnp.where(visible, s, MASKED)
            m_old = m_ref[g]
            m_new = jnp.maximum(m_old, jnp.max(s, axis=-1, keepdims=True))
            alpha = jnp.exp(m_old - m_new)
            p = jnp.exp(s - jnp.concatenate([m_new] * (tq // LANE), axis=1))
            l_ref[g] = alpha * l_ref[g] + jnp.sum(p, axis=-1, keepdims=True)
            acc_ref[g] = alpha * acc_ref[g] + _dot(p.astype(BF16), v)
            m_ref[g] = m_new

    @pl.when(ki < qi)
    def _():
        accumulate(False)

    @pl.when(ki == qi)
    def _():
        accumulate(True)
        for g in range(NSA_GROUP):
            sl = slice(g * HEAD_DIM, (g + 1) * HEAD_DIM)
            o_ref[0, :, sl] = (acc_ref[g] / l_ref[g]).astype(o_ref.dtype)


def slc_attention(q, kv, sel, cosf, sinf, *, k_col, v_col, tq):
    b, s, _ = q.shape
    gw = NSA_GROUP * HEAD_DIM
    nq = s // tq
    nsp = sel.shape[3]
    pairs = [(i, j) for i in range(nq) for j in range(i + 1)]
    qi_tab = jnp.asarray([p[0] for p in pairs], jnp.int32)
    ki_tab = jnp.asarray([p[1] for p in pairs], jnp.int32)
    grid_spec = pltpu.PrefetchScalarGridSpec(
        num_scalar_prefetch=2,
        grid=(b, NSA_KV_HEADS, len(pairs)),
        in_specs=[pl.BlockSpec((1, tq, gw), lambda bi, h, t, qt, kt: (bi, qt[t], h)),
                  pl.BlockSpec((1, tq, HEAD_DIM), lambda bi, h, t, qt, kt: (bi, kt[t], k_col + h)),
                  pl.BlockSpec((1, tq, HEAD_DIM), lambda bi, h, t, qt, kt: (bi, kt[t], v_col + h)),
                  pl.BlockSpec((1, 1, tq, nsp), lambda bi, h, t, qt, kt: (bi, h, qt[t], 0)),
                  pl.BlockSpec((1, tq, LANE), lambda bi, h, t, qt, kt: (bi, qt[t], 0)),
                  pl.BlockSpec((1, tq, LANE), lambda bi, h, t, qt, kt: (bi, qt[t], 0))],
        out_specs=pl.BlockSpec((1, tq, gw), lambda bi, h, t, qt, kt: (bi, qt[t], h)),
        scratch_shapes=[pltpu.VMEM((NSA_GROUP, tq, 2 * HEAD_DIM), BF16),
                        pltpu.VMEM((NSA_GROUP, tq, LANE), F32),
                        pltpu.VMEM((NSA_GROUP, tq, LANE), F32),
                        pltpu.VMEM((NSA_GROUP, tq, HEAD_DIM), F32)],
    )
    return pl.pallas_call(
        functools.partial(_slc_kernel, tq=tq, scale=HEAD_DIM ** -0.5),
        grid_spec=grid_spec,
        out_shape=jax.ShapeDtypeStruct((b, s, NSA_HEADS * HEAD_DIM), BF16),
        compiler_params=_params("parallel", "parallel", "arbitrary"),
    )(qi_tab, ki_tab, q, kv, kv, sel, cosf, sinf)


def _band_kernel(*refs, nheads, kv_heads, nkv, tq, max_dist, rope_q, with_lse, scale):
    q_ref = refs[0]
    k_refs = refs[1:1 + nkv]
    v_refs = refs[1 + nkv:1 + 2 * nkv]
    pos = 1 + 2 * nkv
    if rope_q:
        cos_ref, sin_ref = refs[pos:pos + 2]
        pos += 2
    o_ref = refs[pos]
    lse_ref = refs[pos + 1] if with_lse else None
    qi = pl.program_id(2)
    qpos = qi * tq + lax.broadcasted_iota(jnp.int32, (tq, 1), 0)
    kpos = (qi - (nkv - 1)) * tq + lax.broadcasted_iota(jnp.int32, (1, nkv * tq), 1)
    diff = qpos - kpos
    mask = jnp.logical_and(jnp.logical_and(diff >= 0, diff <= max_dist), kpos >= 0)
    lse_acc = jnp.zeros((tq, LANE), F32)
    lane = lax.broadcasted_iota(jnp.int32, (tq, LANE), 1)
    for g in range(nheads):
        sl = slice(g * HEAD_DIM, (g + 1) * HEAD_DIM)
        ksl = sl if kv_heads > 1 else slice(0, HEAD_DIM)
        q = q_ref[0, :, sl]
        if rope_q:
            q = _rope(q.astype(F32), cos_ref[0], sin_ref[0]).astype(BF16)
        kcat = jnp.concatenate([r[0, :, ksl] for r in k_refs], axis=0)
        vcat = jnp.concatenate([r[0, :, ksl] for r in v_refs], axis=0)
        s = _dot_nt(q, kcat) * scale
        s = jnp.where(mask, s, MASKED)
        mx = jnp.max(s, axis=-1, keepdims=True)
        e = jnp.where(mask, jnp.exp(s - mx), 0.0)
        den = jnp.sum(e, axis=-1, keepdims=True)
        o_ref[0, :, sl] = (_dot(e.astype(BF16), vcat) / den).astype(o_ref.dtype)
        if with_lse:
            lse_acc = jnp.where(lane == g, mx + jnp.log(den), lse_acc)
    if with_lse:
        lse_ref[0] = lse_acc


def band_attention(q, k, v, *, nheads, kv_heads, q_col, k_col, v_col, o_cols, ncol, max_dist, tq,
                   rope=None, with_lse=False, out_dtype=BF16):
    b, seq_len, _ = q.shape
    tq = min(tq, seq_len)
    assert seq_len % tq == 0
    nkv = -(-max_dist // tq) + 1
    qw = nheads * HEAD_DIM
    kw = kv_heads * HEAD_DIM

    def kv_spec(col_fn, back):
        return pl.BlockSpec((1, tq, kw), lambda bi, c, i: (bi, jnp.maximum(i - back, 0), col_fn(c)))

    in_specs = [pl.BlockSpec((1, tq, qw), lambda bi, c, i: (bi, i, q_col(c)))]
    in_specs += [kv_spec(k_col, nkv - 1 - t) for t in range(nkv)]
    in_specs += [kv_spec(v_col, nkv - 1 - t) for t in range(nkv)]
    args = [q] + [k] * nkv + [v] * nkv
    if rope is not None:
        in_specs += [pl.BlockSpec((1, tq, LANE), lambda bi, c, i: (bi, i, 0))] * 2
        args += list(rope)
    out_specs = [pl.BlockSpec((1, tq, qw), lambda bi, c, i: (bi, i, c))]
    out_shape = [jax.ShapeDtypeStruct((b, seq_len, o_cols * qw), out_dtype)]
    if with_lse:
        out_specs.append(pl.BlockSpec((1, tq, LANE), lambda bi, c, i: (bi, i, c)))
        out_shape.append(jax.ShapeDtypeStruct((b, seq_len, o_cols * LANE), F32))
    res = pl.pallas_call(
        functools.partial(_band_kernel, nheads=nheads, kv_heads=kv_heads, nkv=nkv, tq=tq, max_dist=max_dist,
                          rope_q=rope is not None, with_lse=with_lse, scale=HEAD_DIM ** -0.5),
        grid=(b, ncol, seq_len // tq),
        in_specs=in_specs,
        out_specs=out_specs,
        out_shape=out_shape,
        compiler_params=_params("parallel", "parallel", "parallel"),
    )(*args)
    return res if with_lse else res[0]


def _bmm(x, y):
    return jnp.einsum("bij,bjk->bik", x.astype(BF16), y.astype(BF16), preferred_element_type=F32)


def _bmm_nt(x, y):
    return jnp.einsum("bik,bjk->bij", x.astype(BF16), y.astype(BF16), preferred_element_type=F32)


def _head_sums(x2, scale=1.0):
    n = x2.shape[1]
    blk = (lax.broadcasted_iota(jnp.int32, (n, n), 0) // RWKV_HEAD
           == lax.broadcasted_iota(jnp.int32, (n, n), 1) // RWKV_HEAD)
    ones = jnp.where(blk, scale, 0.0).astype(BF16)
    hi = x2.astype(BF16)
    lo = (x2 - hi.astype(F32)).astype(BF16)
    return _dot(hi, ones) + _dot(lo, ones)


def _wkv_chunk_kernel(*refs, nb, c, mix):
    if mix:
        zr_ref, zk_ref, zv_ref, wl_ref, al_ref, vu_ref, vf_ref, vec_ref = refs[:8]
    else:
        zr_ref, zk_ref, zv_ref, wl_ref, al_ref, vec_ref = refs[:6]
    q_ref, y0_ref, gt_ref, ht_ref, bonus_ref = refs[-5:]
    pair = 2 * RWKV_HEAD
    c2 = 2 * c
    rows = nb * c
    vec = vec_ref[...]
    w0, a0, k_k, k_a, r_k, v0 = [vec[i:i + 1] for i in range(6)]
    r2 = zr_ref[0]
    kraw = zk_ref[0]
    v2 = zv_ref[0]
    x = w0 + wl_ref[0]
    softplus_neg = jnp.maximum(-x, 0.0) + jnp.log(1.0 + jnp.exp(-jnp.abs(x)))
    lw2 = -jnp.exp(-softplus_neg - 0.5)
    a_gate = _sigmoid(a0 + al_ref[0])
    if mix:
        v2 = v2 + (vf_ref[0] - v2) * _sigmoid(v0 + vu_ref[0])
    kk = kraw * k_k
    kk = kk * lax.rsqrt(jnp.maximum(_head_sums(kk * kk), 1e-24))
    k2 = kraw * (1.0 + (a_gate - 1.0) * k_a)
    bonus_ref[0] = _head_sums(r2 * k2 * r_k) * v2

    def chunks(x2):
        return x2.reshape(nb, c, pair)

    r, lw, k, v, a, b = [chunks(t) for t in (r2, lw2, k2, v2, -kk, kk * a_gate)]
    row = lax.broadcasted_iota(jnp.int32, (c, c), 0)
    col = lax.broadcasted_iota(jnp.int32, (c, c), 1)
    tril = jnp.broadcast_to((row >= col).astype(BF16), (nb, c, c))
    hi = lw.astype(BF16)
    rem = lw - hi.astype(F32)
    mid = rem.astype(BF16)
    lo = (rem - mid.astype(F32)).astype(BF16)
    cum = _bmm(tril, hi) + _bmm(tril, mid) + _bmm(tril, lo)
    last = cum[:, c - 1:c, :]
    inv = jnp.exp(-cum)
    tail = jnp.exp(last - cum)
    lane = lax.broadcasted_iota(jnp.int32, (1, 1, pair), 2)
    first = lane < RWKV_HEAD

    def stack(x):
        return jnp.concatenate([jnp.where(first, x, 0.0), jnp.where(first, 0.0, x)], axis=1)

    a_s = stack(a * jnp.exp(cum - lw))
    r_s = stack(r * jnp.exp(cum))
    b_s = stack(b * inv)
    k_s = stack(k * inv)
    v_s = stack(v)
    bh_s = stack(b * tail)
    kh_s = stack(k * tail)
    ar = jnp.concatenate([a_s, r_s], axis=1)
    pb = _bmm_nt(ar, b_s)
    pk = _bmm_nt(ar, k_s)
    row2 = lax.broadcasted_iota(jnp.int32, (c2, c2), 0) % c
    col2 = lax.broadcasted_iota(jnp.int32, (c2, c2), 1) % c
    strict = row2 > col2
    lower = row2 >= col2
    l_ab = jnp.where(strict, pb[:, :c2], 0.0)
    m_rb = jnp.where(lower, pb[:, c2:], 0.0)
    l_ak = jnp.where(strict, pk[:, :c2], 0.0)
    m_rk = jnp.where(lower, pk[:, c2:], 0.0)
    eye = (lax.broadcasted_iota(jnp.int32, (c2, c2), 0) == lax.broadcasted_iota(jnp.int32, (c2, c2), 1))
    eye = eye.astype(F32)
    tinv = eye + l_ab
    pw = l_ab
    for _ in range(int(np.log2(c)) - 1):
        pw = _bmm(pw, pw)
        tinv = tinv + _bmm(tinv, pw)
    tu = _bmm(tinv, jnp.concatenate([_bmm(l_ak, v_s), a_s], axis=2))
    u0_s = tu[:, :, :pair]
    ta_s = tu[:, :, pair:]
    mu = _bmm(m_rb, jnp.concatenate([ta_s, u0_s], axis=2))
    q_s = r_s + mu[:, :, :pair]
    y0_s = _bmm(m_rk, v_s) + mu[:, :, pair:]
    q_ref[0] = (q_s[:, :c] + q_s[:, c:]).reshape(nb * c, pair).astype(q_ref.dtype)
    y0_ref[0] = (y0_s[:, :c] + y0_s[:, c:]).reshape(nb * c, pair)
    eye_b = jnp.broadcast_to(eye.astype(BF16), (nb, c2, c2))
    tr = _bmm_nt(eye_b, jnp.concatenate([bh_s, kh_s], axis=1))
    bh_t = tr[:, :, :c2]
    kh_t = tr[:, :, c2:]
    gh = _bmm(bh_t, jnp.concatenate([ta_s, u0_s], axis=2))
    gt_ref[0, :, 0] = (eye * jnp.exp(last) + gh[:, :, :pair]).astype(gt_ref.dtype)
    ht_ref[0, :, 0] = (gh[:, :, pair:] + _bmm(kh_t, v_s)).astype(ht_ref.dtype)


def _wkv_scan_kernel(q_ref, y0_ref, gt_ref, ht_ref, bonus_ref, g_ref, lnx_ref, o_ref, state_ref, *, npair):
    pair = 2 * RWKV_HEAD
    c = q_ref.shape[1]

    @pl.when(pl.program_id(1) == 0)
    def _():
        state_ref[...] = jnp.zeros_like(state_ref)

    sls = [slice(p * pair, (p + 1) * pair) for p in range(npair)]
    st = [state_ref[p].astype(BF16) for p in range(npair)]
    ys = [_dot(q_ref[0, :, sls[p]], st[p]) + y0_ref[0, :, sls[p]] for p in range(npair)]
    new = [_dot(gt_ref[0, 0, p], st[p]) + ht_ref[0, 0, p].astype(F32) for p in range(npair)]
    for p in range(npair):
        state_ref[p] = new[p]
    y = jnp.concatenate(ys, axis=0)
    dev = y - _head_sums(y, 1.0 / RWKV_HEAD)
    yn = dev * lax.rsqrt(_head_sums(dev * dev, 1.0 / RWKV_HEAD) + LNX_EPS)
    for p in range(npair):
        ln = yn[p * c:(p + 1) * c] * lnx_ref[0:1, sls[p]] + lnx_ref[1:2, sls[p]]
        o_ref[0, :, sls[p]] = ((ln + bonus_ref[0, :, sls[p]]) * g_ref[0, :, sls[p]]).astype(o_ref.dtype)


def wkv7(zs, zs_first, wl, al, vu, vec, g, lnx):
    bsz, s, _ = zs.shape
    width = wl.shape[2]
    pair = 2 * RWKV_HEAD
    npair = width // pair
    c = min(RWKV_CHUNK, s)
    nb = min(RWKV_CHUNKS_PER_STEP, s // c)
    nch = s // c
    mix = vu is not None
    assert s % (nb * c) == 0 and width % pair == 0

    def col(off):
        return pl.BlockSpec((1, nb * c, pair), lambda bi, p, j: (bi, j, off + p))

    vec_spec = pl.BlockSpec((8, pair), lambda bi, p, j: (0, p))
    mat = pl.BlockSpec((1, nb, 1, pair, pair), lambda bi, p, j: (bi, j, p, 0, 0))
    in_specs = [col(0), col(npair), col(2 * npair), col(0), col(0)]
    args = [zs, zs, zs, wl, al]
    if mix:
        in_specs += [col(0), col(2 * npair)]
        args += [vu, zs_first]
    q, y0, gt, ht, bonus = pl.pallas_call(
        functools.partial(_wkv_chunk_kernel, nb=nb, c=c, mix=mix),
        grid=(bsz, npair, nch // nb),
        in_specs=in_specs + [vec_spec],
        out_specs=[col(0), col(0), mat, mat, col(0)],
        out_shape=[jax.ShapeDtypeStruct((bsz, s, width), BF16), jax.ShapeDtypeStruct((bsz, s, width), F32),
                   jax.ShapeDtypeStruct((bsz, nch, npair, pair, pair), BF16),
                   jax.ShapeDtypeStruct((bsz, nch, npair, pair, pair), BF16),
                   jax.ShapeDtypeStruct((bsz, s, width), F32)],
        compiler_params=_params("parallel", "parallel", "parallel"),
    )(*args, vec)
    row = pl.BlockSpec((1, c, width), lambda bi, j: (bi, j, 0))
    mats = pl.BlockSpec((1, 1, npair, pair, pair), lambda bi, j: (bi, j, 0, 0, 0))
    return pl.pallas_call(
        functools.partial(_wkv_scan_kernel, npair=npair),
        grid=(bsz, nch),
        in_specs=[row, row, mats, mats, row, row, pl.BlockSpec((8, width), lambda bi, j: (0, 0))],
        out_specs=row,
        out_shape=jax.ShapeDtypeStruct((bsz, s, width), BF16),
        scratch_shapes=[pltpu.VMEM((npair, pair, pair), F32)],
        compiler_params=_params("parallel", "arbitrary"),
    )(q, y0, gt, ht, bonus, g, lnx)


def _moe_up_kernel(te_ref, tv_ref, a_ref, wg_ref, wu_ref, o_ref, wg_bf, wu_bf):
    i = pl.program_id(1)
    changed = jnp.logical_or(i == 0, te_ref[i] != te_ref[jnp.maximum(i - 1, 0)])

    @pl.when(changed)
    def _():
        wg_bf[...] = wg_ref[...].astype(BF16)
        wu_bf[...] = wu_ref[...].astype(BF16)

    @pl.when(tv_ref[i] > 0)
    def _():
        a = a_ref[...]
        g = _dot(a, wg_bf[...])
        u = _dot(a, wu_bf[...])
        o_ref[...] = (_silu(g) * u).astype(o_ref.dtype)

    @pl.when(tv_ref[i] == 0)
    def _():
        o_ref[...] = jnp.zeros_like(o_ref)


def moe_up(tile_expert, tile_valid, xs, w_gu, layer, *, tm, tn):
    r, k = xs.shape
    f = w_gu.shape[3] // 2
    nj = f // tn
    grid_spec = pltpu.PrefetchScalarGridSpec(
        num_scalar_prefetch=2,
        grid=(nj, r // tm),
        in_specs=[pl.BlockSpec((tm, k), lambda j, i, te, tv: (i, 0)),
                  pl.BlockSpec((None, None, k, tn), lambda j, i, te, tv: (layer, te[i], 0, j)),
                  pl.BlockSpec((None, None, k, tn), lambda j, i, te, tv: (layer, te[i], 0, j + nj))],
        out_specs=pl.BlockSpec((tm, tn), lambda j, i, te, tv: (i, j)),
        scratch_shapes=[pltpu.VMEM((k, tn), BF16), pltpu.VMEM((k, tn), BF16)],
    )
    return pl.pallas_call(
        _moe_up_kernel,
        grid_spec=grid_spec,
        out_shape=jax.ShapeDtypeStruct((r, f), BF16),
        compiler_params=_params("parallel", "arbitrary"),
    )(tile_expert, tile_valid, xs, w_gu, w_gu)


def _moe_down_kernel(te_ref, tv_ref, a_ref, w_ref, o_ref):
    i = pl.program_id(0)

    @pl.when(tv_ref[i] > 0)
    def _():
        o_ref[...] = _dot(a_ref[...], w_ref[...]).astype(o_ref.dtype)

    @pl.when(tv_ref[i] == 0)
    def _():
        o_ref[...] = jnp.zeros_like(o_ref)


def moe_down(tile_expert, tile_valid, act, w_down, layer, *, tm):
    r, f = act.shape
    d = w_down.shape[3]
    grid_spec = pltpu.PrefetchScalarGridSpec(
        num_scalar_prefetch=2,
        grid=(r // tm,),
        in_specs=[pl.BlockSpec((tm, f), lambda i, te, tv: (i, 0)),
                  pl.BlockSpec((None, None, f, d), lambda i, te, tv: (layer, te[i], 0, 0))],
        out_specs=pl.BlockSpec((tm, d), lambda i, te, tv: (i, 0)),
    )
    return pl.pallas_call(
        _moe_down_kernel,
        grid_spec=grid_spec,
        out_shape=jax.ShapeDtypeStruct((r, d), BF16),
        compiler_params=_params("arbitrary"),
    )(tile_expert, tile_valid, act, w_down)


def _rope_tables(positions):
    inv_freq = ROPE_THETA ** (-jnp.arange(ROPE_HALF, dtype=F32) / ROPE_HALF)
    ang = positions.astype(F32)[:, :, None] * inv_freq
    cos = jnp.cos(ang)
    sin = jnp.sin(ang)
    b, s = positions.shape
    pad1 = jnp.ones((b, s, HEAD_DIM - ROPE_DIM), F32)
    pad0 = jnp.zeros((b, s, HEAD_DIM - ROPE_DIM), F32)
    return jnp.concatenate([cos, cos, pad1], axis=-1), jnp.concatenate([-sin, sin, pad0], axis=-1)


def _pad_cols(w, n):
    return w if w.shape[-1] == n else jnp.pad(w, ((0, 0), (0, n - w.shape[-1])))


def nsa_branch(q_a, kv_a, gate_a, cosf, sinf, cmp_pe, cmp_w1, cmp_w2):
    b, s, _ = q_a.shape
    nc = s // CMP_STRIDE - 1
    ncp = -(-nc // LANE) * LANE
    kvw = NSA_KV_HEADS * HEAD_DIM

    def blocks(t):
        ch = t.reshape(b, s // CMP_STRIDE, CMP_STRIDE, NSA_KV_HEADS, HEAD_DIM)
        blk = jnp.concatenate([ch[:, :-1], ch[:, 1:]], axis=2)
        return blk.transpose(0, 1, 3, 2, 4).reshape(b * nc * NSA_KV_HEADS, CMP_BLOCK * HEAD_DIM)

    rows = b * nc * NSA_KV_HEADS
    rows_p = -(-rows // 512) * 512 if rows > 512 else -(-rows // 8) * 8
    flat = jnp.stack([blocks(kv_a[..., :kvw]), blocks(kv_a[..., kvw:2 * kvw])]).astype(F32)
    flat = jnp.pad(flat, ((0, 0), (0, rows_p - rows), (0, 0)))
    comp = nsa_compress(flat, cmp_pe.reshape(2, 1, CMP_BLOCK * HEAD_DIM), cmp_w1.astype(BF16), cmp_w2.astype(BF16))
    comp = comp[:, :rows].reshape(2, b, nc, NSA_KV_HEADS, HEAD_DIM).transpose(0, 1, 3, 2, 4)
    comp = jnp.pad(comp, ((0, 0), (0, 0), (0, 0), (0, ncp - nc), (0, 0)))
    kc = comp[0]
    vct = comp[1].transpose(0, 1, 3, 2)
    tq = min(256, s)
    o_cmp, sel = cmp_select(q_a, kc, vct, nc=nc, tq=tq)
    o_slc = slc_attention(q_a, kv_a, sel, cosf, sinf, k_col=4, v_col=6, tq=min(512, s))
    o_win = band_attention(q_a, kv_a, kv_a, nheads=NSA_GROUP, kv_heads=1, q_col=lambda c: c,
                           k_col=lambda c: 8 + c, v_col=lambda c: 10 + c, o_cols=NSA_KV_HEADS,
                           ncol=NSA_KV_HEADS, max_dist=WIN_SIZE - 1, tq=tq, rope=(cosf, sinf))
    g = gate_a[..., :3 * NSA_HEADS].reshape(b, s, NSA_HEADS, 1, 3)

    def heads(t):
        return t.reshape(b, s, NSA_HEADS, HEAD_DIM).astype(F32)

    o = g[..., 0] * heads(o_cmp) + g[..., 1] * heads(o_slc) + g[..., 2] * heads(o_win)
    return o.reshape(b, s, NSA_HEADS * HEAD_DIM).astype(BF16)


def rwkv_branch(zs, zs_first, vec, w_up, a_up, g_up, v_res):
    b, s, _ = zs.shape
    t = b * s
    w_ = RWKV_WIDTH
    o = 3 * w_
    zw, za = zs[..., o:o + LORA_W], zs[..., o + LORA_W:o + LORA_W + LORA_A]
    zg = zs[..., o + LORA_W + LORA_A:o + LORA_W + LORA_A + LORA_G]
    w0, a0, k_k, k_a, r_k, lnx_g, lnx_b = [vec[i] for i in range(7)]

    def lora(xin, wmat):
        return matmul(xin.reshape(t, -1).astype(BF16), wmat.astype(BF16), out_dtype=F32, tm=1024,
                      tn=512).reshape(b, s, -1)

    wl = lora(jnp.tanh(zw), w_up)
    al = lora(za, a_up)
    g = lora(jax.nn.sigmoid(zg), g_up)
    zero = jnp.zeros_like(w0)
    if v_res is None:
        vu, v0 = None, zero
    else:
        v0, v_down, v_up = v_res
        v = zs[..., 2 * w_:3 * w_]
        vd = matmul(v.reshape(t, w_).astype(BF16), _pad_cols(v_down, LANE).astype(BF16), out_dtype=BF16, tm=1024,
                    tn=LANE)
        vu = matmul(vd, jnp.pad(v_up, ((0, LANE - v_up.shape[0]), (0, 0))).astype(BF16), out_dtype=F32, tm=1024,
                    tn=512).reshape(b, s, w_)
    vecs = jnp.stack([w0, a0, k_k, k_a, r_k, v0, zero, zero])
    lnx = jnp.stack([lnx_g, lnx_b] + [zero] * 6)
    return wkv7(zs, zs_first, wl, al, vu, vecs, g, lnx)


def dilated_branch(qkv):
    b, s, width = qkv.shape
    blocks_per_tok = width // DIL_OUT
    outs, lses = [], []
    for gi, (win, dil) in enumerate(DIL_PATTERNS):
        view = qkv.reshape(b, s // dil, dil * width)
        o, lse = band_attention(
            view, view, view, nheads=DIL_HPG, kv_heads=DIL_HPG,
            q_col=lambda c, gi=gi: c * blocks_per_tok + gi,
            k_col=lambda c, gi=gi: c * blocks_per_tok + len(DIL_PATTERNS) + gi,
            v_col=lambda c, gi=gi: c * blocks_per_tok + 2 * len(DIL_PATTERNS) + gi,
            o_cols=dil, ncol=dil, max_dist=win // dil, tq=256, with_lse=True, out_dtype=F32)
        outs.append(o.reshape(b, s, DIL_HPG, HEAD_DIM))
        lses.append(lse.reshape(b, s, LANE)[..., :DIL_HPG])
    alpha = jax.nn.softmax(jnp.stack(lses, axis=-1), axis=-1)
    o = sum(alpha[..., gi:gi + 1] * outs[gi] for gi in range(len(DIL_PATTERNS)))
    return o.reshape(b, s, DIL_OUT).astype(BF16)


def moe_ffn(h, router_w, router_b, w_gu, w_down, layer, *, tm):
    t, d = h.shape
    logits = matmul(h, _pad_cols(router_w, LANE).astype(BF16), out_dtype=F32, tm=1024, tn=LANE)[:, :N_EXPERTS]
    logits = logits + router_b
    top_v, top_i = lax.top_k(logits, TOP_K)
    wts = jax.nn.softmax(top_v, axis=-1)
    flat_e = top_i.reshape(-1)
    onehot = (flat_e[:, None] == jnp.arange(N_EXPERTS)[None, :]).astype(jnp.int32)
    rank = jnp.take_along_axis(jnp.cumsum(onehot, axis=0), flat_e[:, None], axis=1)[:, 0] - 1
    counts = jnp.sum(onehot, axis=0)
    tiles_per = (counts + tm - 1) // tm
    tile_end = jnp.cumsum(tiles_per)
    group_start = (tile_end - tiles_per) * tm
    dest = group_start[flat_e] + rank
    ntiles = (TOP_K * t) // tm + N_EXPERTS
    rows = ntiles * tm
    row_token = jnp.zeros((rows,), jnp.int32).at[dest].set(jnp.arange(TOP_K * t, dtype=jnp.int32) // TOP_K)
    tile_ids = jnp.arange(ntiles, dtype=jnp.int32)
    tile_valid = (tile_ids < tile_end[-1]).astype(jnp.int32)
    tile_expert = jnp.minimum(jnp.searchsorted(tile_end, tile_ids, side="right"), N_EXPERTS - 1).astype(jnp.int32)
    xs = jnp.take(h, row_token, axis=0, mode="clip")
    act = moe_up(tile_expert, tile_valid, xs, w_gu, layer, tm=tm, tn=512)
    out = moe_down(tile_expert, tile_valid, act, w_down, layer, tm=tm)
    dest = dest.reshape(t, TOP_K)
    ya = jnp.take(out, dest[:, 0], axis=0, mode="clip")
    yb = jnp.take(out, dest[:, 1], axis=0, mode="clip")
    return ya, yb, jnp.pad(wts, ((0, 0), (0, LANE - TOP_K)))


def kernel(x, c, positions, ada_w, ada_b, norm_g, w_in, cmp_pe, cmp_w1, cmp_w2, rwkv_mu, rwkv_vec, w_up, a_up, g_up, v_res0, v_res_down, v_res_up, w_br_a, w_br_b, w_br_c, w_out, ffn_gu, ffn_down, router_w, router_b, moe_gu, moe_down):
    b, s, d = x.shape
    depth = ada_w.shape[0]
    t = b * s
    tm_row = min(512, s)
    cosf, sinf = _rope_tables(positions)
    cos_t = cosf.reshape(t, LANE)
    sin_t = sinf.reshape(t, LANE)

    cond = jnp.pad(jax.nn.silu(c), ((0, 8 - b % 8 if b % 8 else 0), (0, 0))).astype(BF16)
    mods = []
    for l in range(depth):
        mod = matmul(cond, ada_w, out_dtype=F32, tm=cond.shape[0], tn=512, layer=l)[:b] + ada_b[l]
        mods.append(mod.reshape(b, 6, 1, d))

    def mod_of(l, i):
        return mods[l][:, i]

    q_cols = NSA_HEADS * HEAD_DIM
    kv_cols = 6 * NSA_KV_HEADS * HEAD_DIM
    gate_cols = 3 * NSA_HEADS
    rwkv_cols = 3 * RWKV_WIDTH + LORA_W + LORA_A + LORA_G
    dil_cols = 3 * DIL_HEADS * HEAD_DIM
    offs = np.cumsum([0, q_cols, kv_cols, gate_cols, rwkv_cols, dil_cols, 3 * d]).tolist()
    rwkv_pad = -(-rwkv_cols // 512) * 512

    xf = x.reshape(t, d)
    h = norm_mod(xf, norm_g[0, 0][None], mod_of(0, 1), mod_of(0, 0), seq=s, tm=tm_row)
    zs_first = None
    moe_down_bf = moe_down.astype(BF16)
    for l in range(depth):
        wl = w_in[l]
        seg = lambda i: wl[:, offs[i]:offs[i + 1]]
        q_a = matmul(h, seg(0).astype(BF16), out_dtype=BF16, tm=1024, tn=512)
        kv_a = matmul(h, seg(1).astype(BF16), out_dtype=BF16, tm=1024, tn=NSA_KV_HEADS * HEAD_DIM,
                      rope=(cos_t, sin_t, (2, 4)))
        gate_a = matmul(h, _pad_cols(seg(2), LANE).astype(BF16), out_dtype=F32, tm=1024, tn=LANE, act="sigmoid")
        zs = matmul_token_shift(h, _pad_cols(seg(3), rwkv_pad).astype(BF16),
                                _pad_cols(rwkv_mu[l][None], rwkv_pad), seq=s, tm=1024, tn=512).reshape(b, s, -1)
        zs_first = zs if l == 0 else zs_first
        qkv_c = matmul(h, seg(4).astype(BF16), out_dtype=BF16, tm=1024, tn=512,
                       rope=(cos_t, sin_t, tuple(range(2 * DIL_HEADS * HEAD_DIM // 512))))
        mg = matmul(h, seg(5).astype(BF16), out_dtype=BF16, tm=1024, tn=512, act="sigmoid")

        y_a = nsa_branch(q_a.reshape(b, s, -1), kv_a.reshape(b, s, -1), gate_a.reshape(b, s, -1), cosf, sinf,
                         cmp_pe[l], cmp_w1[l], cmp_w2[l])
        v_res = None if l == 0 else (v_res0[l - 1], v_res_down[l - 1], v_res_up[l - 1])
        y_b = rwkv_branch(zs, zs_first, rwkv_vec[l], w_up[l], a_up[l], g_up[l], v_res)
        y_c = dilated_branch(qkv_c.reshape(b, s, -1))
        merged = branch_merge(y_a.reshape(t, -1), y_b.reshape(t, -1), y_c.reshape(t, -1), w_br_a[l].astype(BF16),
                              w_br_b[l].astype(BF16), w_br_c[l].astype(BF16), mg, tm=1024, tn=512)
        xf, h = matmul_close(merged, w_out[l].astype(BF16), xf, norm_g[l, 1][None], mod_of(l, 2),
                             (norm_g[l, 2][None], mod_of(l, 4), mod_of(l, 3)), seq=s, tm=tm_row, tk=512)

        nxt = None if l == depth - 1 else (norm_g[l + 1, 0][None], mod_of(l + 1, 1), mod_of(l + 1, 0))
        if l % 2 == 0:
            act = swiglu_up(h, ffn_gu[l // 2].astype(BF16), tm=1024, tn=512)
            xf, h = matmul_close(act, ffn_down[l // 2].astype(BF16), xf, norm_g[l, 3][None], mod_of(l, 5), nxt,
                                 seq=s, tm=tm_row, tk=512)
        else:
            ya, yb, wts = moe_ffn(h, router_w[l // 2], router_b[l // 2], moe_gu, moe_down_bf, l // 2, tm=512)
            xf, h = close_sublayer(ya, yb, wts, xf, norm_g[l, 3][None], mod_of(l, 5), nxt, seq=s, tm=tm_row)
    return xf.reshape(b, s, d)
```

```python
import functools

import numpy as np
import jax
import jax.numpy as jnp
from jax import lax
from jax.experimental import pallas as pl
from jax.experimental.pallas import tpu as pltpu

F32 = jnp.float32
BF16 = jnp.bfloat16

HEAD_DIM = 128
ROPE_DIM = HEAD_DIM // 4
ROPE_HALF = ROPE_DIM // 2
ROPE_THETA = 500000.0
NORM_EPS = 1e-6

NSA_HEADS = 8
NSA_KV_HEADS = 2
NSA_GROUP = NSA_HEADS // NSA_KV_HEADS
CMP_BLOCK = 32
CMP_STRIDE = 16
SLC_BLOCK = 64
SLC_TOPN = 16
WIN_SIZE = 512
FORCE_BONUS = 1e4
SLC_PICK_BIAS = 8192.0

RWKV_HEADS = 16
RWKV_HEAD = 64
RWKV_WIDTH = RWKV_HEADS * RWKV_HEAD
LORA_W = 96
LORA_A = 96
LORA_G = 256
LNX_EPS = 64e-5
RWKV_CHUNK = 64
RWKV_CHUNKS_PER_STEP = 8

DIL_PATTERNS = ((128, 1), (512, 4), (2048, 16))
DIL_HPG = 4
DIL_HEADS = DIL_HPG * len(DIL_PATTERNS)
DIL_OUT = DIL_HPG * HEAD_DIM

N_EXPERTS = 8
TOP_K = 2

LANE = 128
VMEM_LIMIT_BYTES = 56 * 1024 * 1024
MASKED = -1e30


def _params(*sem):
    return pltpu.CompilerParams(dimension_semantics=sem, vmem_limit_bytes=VMEM_LIMIT_BYTES)


def _sigmoid(x):
    return 1.0 / (1.0 + jnp.exp(-x))


def _silu(x):
    return x * _sigmoid(x)


def _dot(a, b):
    return jnp.dot(a, b, preferred_element_type=F32)


def _dot_nt(a, b):
    return lax.dot_general(a, b, (((1,), (1,)), ((), ())), preferred_element_type=F32)


def _dot_tn(a, b):
    return lax.dot_general(a, b, (((0,), (0,)), ((), ())), preferred_element_type=F32)


def _rope(t, cosf, sinf):
    lane = lax.broadcasted_iota(jnp.int32, t.shape, 1)
    swapped = jnp.where(lane < ROPE_HALF, pltpu.roll(t, LANE - ROPE_HALF, 1), pltpu.roll(t, ROPE_HALF, 1))
    return t * cosf + swapped * sinf


def _rms(y):
    return y * lax.rsqrt(jnp.mean(y * y, axis=-1, keepdims=True) + NORM_EPS)


def _matmul_kernel(*refs, act, rope_tiles):
    if rope_tiles:
        a_ref, w_ref, cos_ref, sin_ref, o_ref = refs
    else:
        a_ref, w_ref, o_ref = refs
    acc = _dot(a_ref[...].astype(BF16), w_ref[...].astype(BF16))
    if act == "sigmoid":
        acc = _sigmoid(acc)
    elif act == "tanh":
        acc = jnp.tanh(acc)
    if not rope_tiles:
        o_ref[...] = acc.astype(o_ref.dtype)
        return
    j = pl.program_id(1)
    pred = functools.reduce(jnp.logical_or, [j == t for t in rope_tiles])

    @pl.when(pred)
    def _():
        cosf = cos_ref[...]
        sinf = sin_ref[...]
        for c in range(acc.shape[1] // LANE):
            sl = slice(c * LANE, (c + 1) * LANE)
            o_ref[:, sl] = _rope(acc[:, sl], cosf, sinf).astype(o_ref.dtype)

    @pl.when(jnp.logical_not(pred))
    def _():
        o_ref[...] = acc.astype(o_ref.dtype)


def _matmul_shift_kernel(a_ref, ap_ref, w_ref, mu_ref, o_ref, *, seq, tm):
    w = w_ref[...]
    z = _dot(a_ref[...], w)
    zp = _dot(ap_ref[...], w)
    at_start = (pl.program_id(0) * tm) % seq == 0
    last = zp.shape[0] - 1
    prev_row = jnp.where(at_start, 0.0, zp[last:last + 1, :])
    rowid = lax.broadcasted_iota(jnp.int32, z.shape, 0)
    shifted = jnp.where(rowid == 0, prev_row, pltpu.roll(z, 1, 0))
    o_ref[...] = (z + (shifted - z) * mu_ref[...]).astype(o_ref.dtype)


BF16_SUBLANES = 16


def matmul_token_shift(a, w, mu, *, seq, tm, tn):
    m, k = a.shape
    n = w.shape[1]
    assert m % tm == 0 and n % tn == 0 and seq % tm == 0
    per = tm // BF16_SUBLANES
    return pl.pallas_call(
        functools.partial(_matmul_shift_kernel, seq=seq, tm=tm),
        grid=(m // tm, n // tn),
        in_specs=[pl.BlockSpec((tm, k), lambda i, j: (i, 0)),
                  pl.BlockSpec((BF16_SUBLANES, k), lambda i, j: (jnp.maximum(i * per - 1, 0), 0)),
                  pl.BlockSpec((k, tn), lambda i, j: (0, j)),
                  pl.BlockSpec((1, tn), lambda i, j: (0, j))],
        out_specs=pl.BlockSpec((tm, tn), lambda i, j: (i, j)),
        out_shape=jax.ShapeDtypeStruct((m, n), F32),
        compiler_params=_params("parallel", "parallel"),
    )(a, a, w, mu)


def matmul(a, w, *, out_dtype, tm, tn, act=None, rope=None, layer=None, a_col=0):
    m = a.shape[0]
    k = w.shape[-2]
    n = w.shape[-1]
    tm = min(tm, m)
    assert m % tm == 0 and n % tn == 0, (m, n, tm, tn)
    if layer is None:
        w_spec = pl.BlockSpec((k, tn), lambda i, j: (0, j))
    else:
        w_spec = pl.BlockSpec((None, k, tn), lambda i, j: (layer, 0, j))
    in_specs = [pl.BlockSpec((tm, k), lambda i, j: (i, a_col)), w_spec]
    args = [a, w]
    tiles = ()
    if rope is not None:
        cosf, sinf, tiles = rope
        in_specs += [pl.BlockSpec((tm, LANE), lambda i, j: (i, 0))] * 2
        args += [cosf, sinf]
    return pl.pallas_call(
        functools.partial(_matmul_kernel, act=act, rope_tiles=tuple(tiles)),
        grid=(m // tm, n // tn),
        in_specs=in_specs,
        out_specs=pl.BlockSpec((tm, tn), lambda i, j: (i, j)),
        out_shape=jax.ShapeDtypeStruct((m, n), out_dtype),
        compiler_params=_params("parallel", "parallel"),
    )(*args)


def _swiglu_up_kernel(a_ref, wg_ref, wu_ref, o_ref):
    a = a_ref[...]
    g = _dot(a, wg_ref[...])
    u = _dot(a, wu_ref[...])
    o_ref[...] = (_silu(g) * u).astype(o_ref.dtype)


def swiglu_up(a, w_gu, *, tm, tn):
    m, k = a.shape
    f = w_gu.shape[1] // 2
    assert m % tm == 0 and f % tn == 0
    nj = f // tn
    return pl.pallas_call(
        _swiglu_up_kernel,
        grid=(m // tm, nj),
        in_specs=[pl.BlockSpec((tm, k), lambda i, j: (i, 0)),
                  pl.BlockSpec((k, tn), lambda i, j: (0, j)),
                  pl.BlockSpec((k, tn), lambda i, j: (0, j + nj))],
        out_specs=pl.BlockSpec((tm, tn), lambda i, j: (i, j)),
        out_shape=jax.ShapeDtypeStruct((m, f), BF16),
        compiler_params=_params("parallel", "parallel"),
    )(a, w_gu, w_gu)


def _close_sublayer(x, y, gpost, gt, nxt):
    xn = x + gt * (_rms(y) * gpost)
    if nxt is None:
        return xn, None
    gnext, sc, sh = nxt
    return xn, (_rms(xn) * gnext) * (1.0 + sc) + sh


def _matmul_close_kernel(*refs, with_next):
    if with_next:
        a_ref, w_ref, x_ref, gp_ref, gt_ref, gn_ref, sc_ref, sh_ref, xo_ref, ho_ref, acc_ref = refs
    else:
        a_ref, w_ref, x_ref, gp_ref, gt_ref, xo_ref, acc_ref = refs
    kk = pl.program_id(1)

    @pl.when(kk == 0)
    def _():
        acc_ref[...] = jnp.zeros_like(acc_ref)

    acc_ref[...] += _dot(a_ref[...], w_ref[...])

    @pl.when(kk == pl.num_programs(1) - 1)
    def _():
        nxt = (gn_ref[...], sc_ref[0], sh_ref[0]) if with_next else None
        xn, h = _close_sublayer(x_ref[...], acc_ref[...], gp_ref[...], gt_ref[0], nxt)
        xo_ref[...] = xn
        if with_next:
            ho_ref[...] = h.astype(ho_ref.dtype)


def _close_specs(tm, d, seq, with_next, nidx):
    def row(i, *_):
        return (i, 0)

    def const(*_):
        return (0, 0)

    def batch(i, *_):
        return ((i * tm) // seq, 0, 0)

    specs = [pl.BlockSpec((tm, d), row), pl.BlockSpec((1, d), const), pl.BlockSpec((1, 1, d), batch)]
    if with_next:
        specs += [pl.BlockSpec((1, d), const), pl.BlockSpec((1, 1, d), batch), pl.BlockSpec((1, 1, d), batch)]
    return specs


def matmul_close(a, w, x, gpost, gt, nxt, *, seq, tm, tk):
    m, k = a.shape
    d = w.shape[1]
    assert m % tm == 0 and k % tk == 0 and seq % tm == 0
    with_next = nxt is not None
    args = [a, w, x, gpost, gt] + (list(nxt) if with_next else [])
    in_specs = [pl.BlockSpec((tm, tk), lambda i, kk: (i, kk)), pl.BlockSpec((tk, d), lambda i, kk: (kk, 0))]
    in_specs += _close_specs(tm, d, seq, with_next, 2)
    out_shape = [jax.ShapeDtypeStruct((m, d), F32)]
    out_specs = [pl.BlockSpec((tm, d), lambda i, kk: (i, 0))]
    if with_next:
        out_shape.append(jax.ShapeDtypeStruct((m, d), BF16))
        out_specs.append(pl.BlockSpec((tm, d), lambda i, kk: (i, 0)))
    res = pl.pallas_call(
        functools.partial(_matmul_close_kernel, with_next=with_next),
        grid=(m // tm, k // tk),
        in_specs=in_specs,
        out_specs=out_specs,
        out_shape=out_shape,
        scratch_shapes=[pltpu.VMEM((tm, d), F32)],
        compiler_params=_params("parallel", "arbitrary"),
    )(*args)
    return (res[0], res[1]) if with_next else (res[0], None)


def _close_kernel(*refs, with_next):
    if with_next:
        ya_ref, yb_ref, wt_ref, x_ref, gp_ref, gt_ref, gn_ref, sc_ref, sh_ref, xo_ref, ho_ref = refs
    else:
        ya_ref, yb_ref, wt_ref, x_ref, gp_ref, gt_ref, xo_ref = refs
    nxt = (gn_ref[...], sc_ref[0], sh_ref[0]) if with_next else None
    wt = wt_ref[...]
    y = wt[:, 0:1] * ya_ref[...].astype(F32) + wt[:, 1:2] * yb_ref[...].astype(F32)
    xn, h = _close_sublayer(x_ref[...], y, gp_ref[...], gt_ref[0], nxt)
    xo_ref[...] = xn
    if with_next:
        ho_ref[...] = h.astype(ho_ref.dtype)


def close_sublayer(ya, yb, wt, x, gpost, gt, nxt, *, seq, tm):
    m, d = x.shape
    with_next = nxt is not None
    args = [ya, yb, wt, x, gpost, gt] + (list(nxt) if with_next else [])
    in_specs = [pl.BlockSpec((tm, d), lambda i: (i, 0)), pl.BlockSpec((tm, d), lambda i: (i, 0)),
                pl.BlockSpec((tm, LANE), lambda i: (i, 0))] + _close_specs(tm, d, seq, with_next, 1)
    out_shape = [jax.ShapeDtypeStruct((m, d), F32)]
    out_specs = [pl.BlockSpec((tm, d), lambda i: (i, 0))]
    if with_next:
        out_shape.append(jax.ShapeDtypeStruct((m, d), BF16))
        out_specs.append(pl.BlockSpec((tm, d), lambda i: (i, 0)))
    res = pl.pallas_call(
        functools.partial(_close_kernel, with_next=with_next),
        grid=(m // tm,),
        in_specs=in_specs,
        out_specs=out_specs,
        out_shape=out_shape,
        compiler_params=_params("parallel"),
    )(*args)
    return (res[0], res[1]) if with_next else (res[0], None)


def _norm_mod_kernel(x_ref, g_ref, sc_ref, sh_ref, o_ref):
    o_ref[...] = ((_rms(x_ref[...]) * g_ref[...]) * (1.0 + sc_ref[0]) + sh_ref[0]).astype(o_ref.dtype)


def norm_mod(x, g, sc, sh, *, seq, tm):
    m, d = x.shape
    batch = lambda i: ((i * tm) // seq, 0, 0)
    return pl.pallas_call(
        _norm_mod_kernel,
        grid=(m // tm,),
        in_specs=[pl.BlockSpec((tm, d), lambda i: (i, 0)), pl.BlockSpec((1, d), lambda i: (0, 0)),
                  pl.BlockSpec((1, 1, d), batch), pl.BlockSpec((1, 1, d), batch)],
        out_specs=pl.BlockSpec((tm, d), lambda i: (i, 0)),
        out_shape=jax.ShapeDtypeStruct((m, d), BF16),
        compiler_params=_params("parallel"),
    )(x, g, sc, sh)


def _branch_merge_kernel(oc_ref, os_ref, ow_ref, ng_ref, yb_ref, yc_ref, wa_ref, wb_ref, wc_ref, ga_ref, gb_ref,
                         gc_ref, o_ref, ya_ref):
    @pl.when(pl.program_id(1) == 0)
    def _():
        ng = ng_ref[...]
        for hd in range(NSA_HEADS):
            sl = slice(hd * HEAD_DIM, (hd + 1) * HEAD_DIM)
            ya = ng[:, 3 * hd:3 * hd + 1] * oc_ref[:, sl].astype(F32)
            ya += ng[:, 3 * hd + 1:3 * hd + 2] * os_ref[:, sl].astype(F32)
            ya += ng[:, 3 * hd + 2:3 * hd + 3] * ow_ref[:, sl].astype(F32)
            ya_ref[:, sl] = ya.astype(ya_ref.dtype)

    acc = ga_ref[...].astype(F32) * _dot(ya_ref[...], wa_ref[...])
    acc += gb_ref[...].astype(F32) * _dot(yb_ref[...], wb_ref[...])
    acc += gc_ref[...].astype(F32) * _dot(yc_ref[...], wc_ref[...])
    o_ref[...] = acc.astype(o_ref.dtype)


def branch_merge(o_cmp, o_slc, o_win, nsa_gate, yb, yc, wa, wb, wc, gates, *, tm, tn):
    m = yb.shape[0]
    d = wa.shape[1]
    nj = d // tn
    row = lambda width: pl.BlockSpec((tm, width), lambda i, j: (i, 0))
    wsp = lambda kdim: pl.BlockSpec((kdim, tn), lambda i, j: (0, j))
    gsp = lambda off: pl.BlockSpec((tm, tn), lambda i, j: (i, j + off * nj))
    wa_rows = wa.shape[0]
    return pl.pallas_call(
        _branch_merge_kernel,
        grid=(m // tm, nj),
        in_specs=[row(wa_rows), row(wa_rows), row(wa_rows), row(LANE), row(yb.shape[1]), row(yc.shape[1]),
                  wsp(wa_rows), wsp(wb.shape[0]), wsp(wc.shape[0]), gsp(0), gsp(1), gsp(2)],
        out_specs=pl.BlockSpec((tm, tn), lambda i, j: (i, j)),
        out_shape=jax.ShapeDtypeStruct((m, d), BF16),
        scratch_shapes=[pltpu.VMEM((tm, wa_rows), BF16)],
        compiler_params=_params("parallel", "arbitrary"),
    )(o_cmp, o_slc, o_win, nsa_gate, yb, yc, wa, wb, wc, gates, gates, gates)


def _compress_kernel(f_ref, pe_ref, w1_ref, w2_ref, o_ref):
    a = (f_ref[0] + pe_ref[0]).astype(BF16)
    hmid = _silu(_dot(a, w1_ref[0]))
    o_ref[0] = _dot(hmid.astype(BF16), w2_ref[0]).astype(o_ref.dtype)


def nsa_compress(flat, pe, w1, w2):
    two, r, kdim = flat.shape
    tm = min(r, 512)
    assert r % tm == 0
    return pl.pallas_call(
        _compress_kernel,
        grid=(two, r // tm),
        in_specs=[pl.BlockSpec((1, tm, kdim), lambda t, i: (t, i, 0)),
                  pl.BlockSpec((1, 1, kdim), lambda t, i: (t, 0, 0)),
                  pl.BlockSpec((1, kdim, HEAD_DIM), lambda t, i: (t, 0, 0)),
                  pl.BlockSpec((1, HEAD_DIM, HEAD_DIM), lambda t, i: (t, 0, 0))],
        out_specs=pl.BlockSpec((1, tm, HEAD_DIM), lambda t, i: (t, i, 0)),
        out_shape=jax.ShapeDtypeStruct((two, r, HEAD_DIM), BF16),
        compiler_params=_params("parallel", "parallel"),
    )(flat, pe, w1, w2)


def _cmp_select_kernel(q_ref, kc_ref, vct_ref, o_ref, sel_ref, *, tq, nc, ncp, nsp, scale):
    qi = pl.program_id(2)
    kc = kc_ref[0, 0]
    vct = vct_ref[0, 0]
    spos = qi * tq + lax.broadcasted_iota(jnp.int32, (1, tq), 1)
    cidx = lax.broadcasted_iota(jnp.int32, (ncp, 1), 0)
    valid = jnp.logical_and(cidx * CMP_STRIDE + (CMP_BLOCK - 1) <= spos, cidx < nc)
    psum = jnp.zeros((ncp, tq), F32)
    for g in range(NSA_GROUP):
        qg = q_ref[0, :, g * HEAD_DIM:(g + 1) * HEAD_DIM]
        st = _dot_nt(kc, qg) * scale
        st = jnp.where(valid, st, MASKED)
        mx = jnp.max(st, axis=0, keepdims=True)
        e = jnp.where(valid, jnp.exp(st - mx), 0.0)
        den = jnp.sum(e, axis=0, keepdims=True)
        p = e / jnp.where(den > 0, den, 1.0)
        psum = psum + p
        og_t = _dot(vct, p.astype(BF16))
        o_ref[0, :, g * HEAD_DIM:(g + 1) * HEAD_DIM] = og_t.T.astype(o_ref.dtype)
    jrow = lax.broadcasted_iota(jnp.int32, (nsp, ncp), 0)
    ccol = lax.broadcasted_iota(jnp.int32, (nsp, ncp), 1)
    c0 = ccol * CMP_STRIDE
    j0 = jrow * SLC_BLOCK
    cover_t = jnp.logical_and(c0 < j0 + SLC_BLOCK, c0 + CMP_BLOCK > j0).astype(F32)
    imp = jnp.dot(cover_t, psum, preferred_element_type=F32, precision=lax.Precision.HIGHEST)
    j = lax.broadcasted_iota(jnp.int32, (nsp, 1), 0).astype(F32)
    cur = (spos // SLC_BLOCK).astype(F32)
    forced = jnp.logical_or(jnp.logical_or(j == 0, j == cur), j == cur - 1)
    score = jnp.where(j <= cur, imp + FORCE_BONUS * forced.astype(F32), -jnp.inf)
    sel = jnp.zeros((nsp, tq), F32)
    for _ in range(SLC_TOPN):
        mx = jnp.max(score, axis=0, keepdims=True)
        first = jnp.min(jnp.where(score == mx, j, float(nsp)), axis=0, keepdims=True)
        pick = j == first
        sel = jnp.where(pick, 1.0, sel)
        score = jnp.where(pick, -jnp.inf, score)
    sel_ref[0, 0] = sel.T.astype(sel_ref.dtype)


def cmp_select(q, kc, vct, *, nc, tq):
    b, s, _ = q.shape
    ncp = kc.shape[2]
    nsp = LANE
    assert s % tq == 0 and s // SLC_BLOCK <= nsp
    gw = NSA_GROUP * HEAD_DIM
    return pl.pallas_call(
        functools.partial(_cmp_select_kernel, tq=tq, nc=nc, ncp=ncp, nsp=nsp, scale=HEAD_DIM ** -0.5),
        grid=(b, NSA_KV_HEADS, s // tq),
        in_specs=[pl.BlockSpec((1, tq, gw), lambda bi, h, i: (bi, i, h)),
                  pl.BlockSpec((1, 1, ncp, HEAD_DIM), lambda bi, h, i: (bi, h, 0, 0)),
                  pl.BlockSpec((1, 1, HEAD_DIM, ncp), lambda bi, h, i: (bi, h, 0, 0))],
        out_specs=[pl.BlockSpec((1, tq, gw), lambda bi, h, i: (bi, i, h)),
                   pl.BlockSpec((1, 1, tq, nsp), lambda bi, h, i: (bi, h, i, 0))],
        out_shape=[jax.ShapeDtypeStruct((b, s, NSA_HEADS * HEAD_DIM), BF16),
                   jax.ShapeDtypeStruct((b, NSA_KV_HEADS, s, nsp), BF16)],
        compiler_params=_params("parallel", "parallel", "parallel"),
    )(q, kc, vct)


def _slc_kernel(qi_ref, ki_ref, q_ref, k_ref, v_ref, sel_ref, cos_ref, sin_ref, o_ref, qaug_ref, m_ref, l_ref,
                acc_ref, *, tq, scale):
    step = pl.program_id(2)
    qi = qi_ref[step]
    ki = ki_ref[step]

    @pl.when(ki == 0)
    def _():
        cosf = cos_ref[0]
        sinf = sin_ref[0]
        pick = sel_ref[0, 0] * SLC_PICK_BIAS
        for g in range(NSA_GROUP):
            sl = slice(g * HEAD_DIM, (g + 1) * HEAD_DIM)
            qg = _rope(q_ref[0, :, sl].astype(F32), cosf, sinf) * scale
            qaug_ref[g] = jnp.concatenate([qg.astype(BF16), pick], axis=1)
        m_ref[...] = jnp.full_like(m_ref, MASKED)
        l_ref[...] = jnp.zeros_like(l_ref)
        acc_ref[...] = jnp.zeros_like(acc_ref)

    def accumulate(causal):
        nsp = sel_ref.shape[3]
        block_of_key = (ki * tq + lax.broadcasted_iota(jnp.int32, (tq, nsp), 0)) // SLC_BLOCK
        onehot = (block_of_key == lax.broadcasted_iota(jnp.int32, (tq, nsp), 1)).astype(BF16)
        kaug = jnp.concatenate([k_ref[0], onehot], axis=1)
        v = v_ref[0]
        if causal:
            qpos = lax.broadcasted_iota(jnp.int32, (tq, tq), 0)
            kpos = lax.broadcasted_iota(jnp.int32, (tq, tq), 1)
            visible = kpos <= qpos
        for g in range(NSA_GROUP):
            s = _dot_nt(qaug_ref[g], kaug)
            if causal:
                s = jnp.where(visible, s, MASKED)
            m_old = m_ref[g]
            m_new = jnp.maximum(m_old, jnp.max(s, axis=-1, keepdims=True))
            alpha = jnp.exp(m_old - m_new)
            p = jnp.exp(s - jnp.concatenate([m_new] * (tq // LANE), axis=1))
            l_ref[g] = alpha * l_ref[g] + jnp.sum(p, axis=-1, keepdims=True)
            acc_ref[g] = alpha * acc_ref[g] + _dot(p.astype(BF16), v)
            m_ref[g] = m_new

    @pl.when(ki < qi)
    def _():
        accumulate(False)

    @pl.when(ki == qi)
    def _():
        accumulate(True)
        for g in range(NSA_GROUP):
            sl = slice(g * HEAD_DIM, (g + 1) * HEAD_DIM)
            o_ref[0, :, sl] = (acc_ref[g] / l_ref[g]).astype(o_ref.dtype)


def slc_attention(q, kv, sel, cosf, sinf, *, k_col, v_col, tq):
    b, s, _ = q.shape
    gw = NSA_GROUP * HEAD_DIM
    nq = s // tq
    nsp = sel.shape[3]
    pairs = [(i, j) for i in range(nq) for j in range(i + 1)]
    qi_tab = jnp.asarray([p[0] for p in pairs], jnp.int32)
    ki_tab = jnp.asarray([p[1] for p in pairs], jnp.int32)
    grid_spec = pltpu.PrefetchScalarGridSpec(
        num_scalar_prefetch=2,
        grid=(b, NSA_KV_HEADS, len(pairs)),
        in_specs=[pl.BlockSpec((1, tq, gw), lambda bi, h, t, qt, kt: (bi, qt[t], h)),
                  pl.BlockSpec((1, tq, HEAD_DIM), lambda bi, h, t, qt, kt: (bi, kt[t], k_col + h)),
                  pl.BlockSpec((1, tq, HEAD_DIM), lambda bi, h, t, qt, kt: (bi, kt[t], v_col + h)),
                  pl.BlockSpec((1, 1, tq, nsp), lambda bi, h, t, qt, kt: (bi, h, qt[t], 0)),
                  pl.BlockSpec((1, tq, LANE), lambda bi, h, t, qt, kt: (bi, qt[t], 0)),
                  pl.BlockSpec((1, tq, LANE), lambda bi, h, t, qt, kt: (bi, qt[t], 0))],
        out_specs=pl.BlockSpec((1, tq, gw), lambda bi, h, t, qt, kt: (bi, qt[t], h)),
        scratch_shapes=[pltpu.VMEM((NSA_GROUP, tq, 2 * HEAD_DIM), BF16),
                        pltpu.VMEM((NSA_GROUP, tq, LANE), F32),
                        pltpu.VMEM((NSA_GROUP, tq, LANE), F32),
                        pltpu.VMEM((NSA_GROUP, tq, HEAD_DIM), F32)],
    )
    return pl.pallas_call(
        functools.partial(_slc_kernel, tq=tq, scale=HEAD_DIM ** -0.5),
        grid_spec=grid_spec,
        out_shape=jax.ShapeDtypeStruct((b, s, NSA_HEADS * HEAD_DIM), BF16),
        compiler_params=_params("parallel", "parallel", "arbitrary"),
    )(qi_tab, ki_tab, q, kv, kv, sel, cosf, sinf)


def _band_kernel(*refs, nheads, kv_heads, nkv, tq, max_dist, rope_q, with_lse, scale):
    q_ref = refs[0]
    k_refs = refs[1:1 + nkv]
    v_refs = refs[1 + nkv:1 + 2 * nkv]
    pos = 1 + 2 * nkv
    if rope_q:
        cos_ref, sin_ref = refs[pos:pos + 2]
        pos += 2
    o_ref = refs[pos]
    lse_ref = refs[pos + 1] if with_lse else None
    qi = pl.program_id(2)
    qpos = qi * tq + lax.broadcasted_iota(jnp.int32, (tq, 1), 0)
    kpos = (qi - (nkv - 1)) * tq + lax.broadcasted_iota(jnp.int32, (1, nkv * tq), 1)
    diff = qpos - kpos
    mask = jnp.logical_and(jnp.logical_and(diff >= 0, diff <= max_dist), kpos >= 0)
    lse_acc = jnp.zeros((tq, LANE), F32)
    lane = lax.broadcasted_iota(jnp.int32, (tq, LANE), 1)
    for g in range(nheads):
        sl = slice(g * HEAD_DIM, (g + 1) * HEAD_DIM)
        ksl = sl if kv_heads > 1 else slice(0, HEAD_DIM)
        q = q_ref[0, :, sl]
        if rope_q:
            q = _rope(q.astype(F32), cos_ref[0], sin_ref[0]).astype(BF16)
        kcat = jnp.concatenate([r[0, :, ksl] for r in k_refs], axis=0)
        vcat = jnp.concatenate([r[0, :, ksl] for r in v_refs], axis=0)
        s = _dot_nt(q, kcat) * scale
        s = jnp.where(mask, s, MASKED)
        mx = jnp.max(s, axis=-1, keepdims=True)
        e = jnp.where(mask, jnp.exp(s - mx), 0.0)
        den = jnp.sum(e, axis=-1, keepdims=True)
        o_ref[0, :, sl] = (_dot(e.astype(BF16), vcat) / den).astype(o_ref.dtype)
        if with_lse:
            lse_acc = jnp.where(lane == g, mx + jnp.log(den), lse_acc)
    if with_lse:
        lse_ref[0] = lse_acc


def band_attention(q, k, v, *, nheads, kv_heads, q_col, k_col, v_col, o_cols, ncol, max_dist, tq,
                   rope=None, with_lse=False, out_dtype=BF16):
    b, seq_len, _ = q.shape
    tq = min(tq, seq_len)
    assert seq_len % tq == 0
    nkv = -(-max_dist // tq) + 1
    qw = nheads * HEAD_DIM
    kw = kv_heads * HEAD_DIM

    def kv_spec(col_fn, back):
        return pl.BlockSpec((1, tq, kw), lambda bi, c, i: (bi, jnp.maximum(i - back, 0), col_fn(c)))

    in_specs = [pl.BlockSpec((1, tq, qw), lambda bi, c, i: (bi, i, q_col(c)))]
    in_specs += [kv_spec(k_col, nkv - 1 - t) for t in range(nkv)]
    in_specs += [kv_spec(v_col, nkv - 1 - t) for t in range(nkv)]
    args = [q] + [k] * nkv + [v] * nkv
    if rope is not None:
        in_specs += [pl.BlockSpec((1, tq, LANE), lambda bi, c, i: (bi, i, 0))] * 2
        args += list(rope)
    out_specs = [pl.BlockSpec((1, tq, qw), lambda bi, c, i: (bi, i, c))]
    out_shape = [jax.ShapeDtypeStruct((b, seq_len, o_cols * qw), out_dtype)]
    if with_lse:
        out_specs.append(pl.BlockSpec((1, tq, LANE), lambda bi, c, i: (bi, i, c)))
        out_shape.append(jax.ShapeDtypeStruct((b, seq_len, o_cols * LANE), F32))
    res = pl.pallas_call(
        functools.partial(_band_kernel, nheads=nheads, kv_heads=kv_heads, nkv=nkv, tq=tq, max_dist=max_dist,
                          rope_q=rope is not None, with_lse=with_lse, scale=HEAD_DIM ** -0.5),
        grid=(b, ncol, seq_len // tq),
        in_specs=in_specs,
        out_specs=out_specs,
        out_shape=out_shape,
        compiler_params=_params("parallel", "parallel", "parallel"),
    )(*args)
    return res if with_lse else res[0]


def _bmm(x, y):
    return jnp.einsum("bij,bjk->bik", x.astype(BF16), y.astype(BF16), preferred_element_type=F32)


def _bmm_nt(x, y):
    return jnp.einsum("bik,bjk->bij", x.astype(BF16), y.astype(BF16), preferred_element_type=F32)


def _head_sums(x2, scale=1.0):
    n = x2.shape[1]
    blk = (lax.broadcasted_iota(jnp.int32, (n, n), 0) // RWKV_HEAD
           == lax.broadcasted_iota(jnp.int32, (n, n), 1) // RWKV_HEAD)
    ones = jnp.where(blk, scale, 0.0).astype(BF16)
    hi = x2.astype(BF16)
    lo = (x2 - hi.astype(F32)).astype(BF16)
    return _dot(hi, ones) + _dot(lo, ones)


def _wkv_chunk_kernel(*refs, nb, c, mix):
    if mix:
        zr_ref, zk_ref, zv_ref, wl_ref, al_ref, vu_ref, vf_ref, vec_ref = refs[:8]
    else:
        zr_ref, zk_ref, zv_ref, wl_ref, al_ref, vec_ref = refs[:6]
    q_ref, y0_ref, gt_ref, ht_ref, bonus_ref = refs[-5:]
    pair = 2 * RWKV_HEAD
    c2 = 2 * c
    rows = nb * c
    vec = vec_ref[...]
    w0, a0, k_k, k_a, r_k, v0 = [vec[i:i + 1] for i in range(6)]
    r2 = zr_ref[0]
    kraw = zk_ref[0]
    v2 = zv_ref[0]
    x = w0 + wl_ref[0]
    softplus_neg = jnp.maximum(-x, 0.0) + jnp.log(1.0 + jnp.exp(-jnp.abs(x)))
    lw2 = -jnp.exp(-softplus_neg - 0.5)
    a_gate = _sigmoid(a0 + al_ref[0])
    if mix:
        v2 = v2 + (vf_ref[0] - v2) * _sigmoid(v0 + vu_ref[0])
    kk = kraw * k_k
    kk = kk * lax.rsqrt(jnp.maximum(_head_sums(kk * kk), 1e-24))
    k2 = kraw * (1.0 + (a_gate - 1.0) * k_a)
    bonus_ref[0] = _head_sums(r2 * k2 * r_k) * v2

    def chunks(x2):
        return x2.reshape(nb, c, pair)

    r, lw, k, v, a, b = [chunks(t) for t in (r2, lw2, k2, v2, -kk, kk * a_gate)]
    row = lax.broadcasted_iota(jnp.int32, (c, c), 0)
    col = lax.broadcasted_iota(jnp.int32, (c, c), 1)
    tril = jnp.broadcast_to((row >= col).astype(BF16), (nb, c, c))
    hi = lw.astype(BF16)
    rem = lw - hi.astype(F32)
    mid = rem.astype(BF16)
    lo = (rem - mid.astype(F32)).astype(BF16)
    cum = _bmm(tril, hi) + _bmm(tril, mid) + _bmm(tril, lo)
    last = cum[:, c - 1:c, :]
    inv = jnp.exp(-cum)
    tail = jnp.exp(last - cum)
    lane = lax.broadcasted_iota(jnp.int32, (1, 1, pair), 2)
    first = lane < RWKV_HEAD

    def stack(x):
        return jnp.concatenate([jnp.where(first, x, 0.0), jnp.where(first, 0.0, x)], axis=1)

    a_s = stack(a * jnp.exp(cum - lw))
    r_s = stack(r * jnp.exp(cum))
    b_s = stack(b * inv)
    k_s = stack(k * inv)
    v_s = stack(v)
    bh_s = stack(b * tail)
    kh_s = stack(k * tail)
    ar = jnp.concatenate([a_s, r_s], axis=1)
    pb = _bmm_nt(ar, b_s)
    pk = _bmm_nt(ar, k_s)
    row2 = lax.broadcasted_iota(jnp.int32, (c2, c2), 0) % c
    col2 = lax.broadcasted_iota(jnp.int32, (c2, c2), 1) % c
    strict = row2 > col2
    lower = row2 >= col2
    l_ab = jnp.where(strict, pb[:, :c2], 0.0)
    m_rb = jnp.where(lower, pb[:, c2:], 0.0)
    l_ak = jnp.where(strict, pk[:, :c2], 0.0)
    m_rk = jnp.where(lower, pk[:, c2:], 0.0)
    eye = (lax.broadcasted_iota(jnp.int32, (c2, c2), 0) == lax.broadcasted_iota(jnp.int32, (c2, c2), 1))
    eye = eye.astype(F32)
    tinv = eye + l_ab
    pw = l_ab
    for _ in range(int(np.log2(c)) - 1):
        pw = _bmm(pw, pw)
        tinv = tinv + _bmm(tinv, pw)
    tu = _bmm(tinv, jnp.concatenate([_bmm(l_ak, v_s), a_s], axis=2))
    u0_s = tu[:, :, :pair]
    ta_s = tu[:, :, pair:]
    mu = _bmm(m_rb, jnp.concatenate([ta_s, u0_s], axis=2))
    q_s = r_s + mu[:, :, :pair]
    y0_s = _bmm(m_rk, v_s) + mu[:, :, pair:]
    q_ref[0] = (q_s[:, :c] + q_s[:, c:]).reshape(nb * c, pair).astype(q_ref.dtype)
    y0_ref[0] = (y0_s[:, :c] + y0_s[:, c:]).reshape(nb * c, pair)
    eye_b = jnp.broadcast_to(eye.astype(BF16), (nb, c2, c2))
    tr = _bmm_nt(eye_b, jnp.concatenate([bh_s, kh_s], axis=1))
    bh_t = tr[:, :, :c2]
    kh_t = tr[:, :, c2:]
    gh = _bmm(bh_t, jnp.concatenate([ta_s, u0_s], axis=2))
    gt_ref[0, :, 0] = (eye * jnp.exp(last) + gh[:, :, :pair]).astype(gt_ref.dtype)
    ht_ref[0, :, 0] = (gh[:, :, pair:] + _bmm(kh_t, v_s)).astype(ht_ref.dtype)


def _wkv_scan_kernel(q_ref, y0_ref, gt_ref, ht_ref, bonus_ref, g_ref, lnx_ref, o_ref, state_ref, *, npair):
    pair = 2 * RWKV_HEAD
    c = q_ref.shape[1]

    @pl.when(pl.program_id(1) == 0)
    def _():
        state_ref[...] = jnp.zeros_like(state_ref)

    sls = [slice(p * pair, (p + 1) * pair) for p in range(npair)]
    st = [state_ref[p].astype(BF16) for p in range(npair)]
    ys = [_dot(q_ref[0, :, sls[p]], st[p]) + y0_ref[0, :, sls[p]] for p in range(npair)]
    new = [_dot(gt_ref[0, 0, p], st[p]) + ht_ref[0, 0, p].astype(F32) for p in range(npair)]
    for p in range(npair):
        state_ref[p] = new[p]
    y = jnp.concatenate(ys, axis=0)
    dev = y - _head_sums(y, 1.0 / RWKV_HEAD)
    yn = dev * lax.rsqrt(_head_sums(dev * dev, 1.0 / RWKV_HEAD) + LNX_EPS)
    for p in range(npair):
        ln = yn[p * c:(p + 1) * c] * lnx_ref[0:1, sls[p]] + lnx_ref[1:2, sls[p]]
        o_ref[0, :, sls[p]] = ((ln + bonus_ref[0, :, sls[p]]) * g_ref[0, :, sls[p]]).astype(o_ref.dtype)


def wkv7(zs, zs_first, wl, al, vu, vec, g, lnx):
    bsz, s, _ = zs.shape
    width = wl.shape[2]
    pair = 2 * RWKV_HEAD
    npair = width // pair
    c = min(RWKV_CHUNK, s)
    nb = min(RWKV_CHUNKS_PER_STEP, s // c)
    nch = s // c
    mix = vu is not None
    assert s % (nb * c) == 0 and width % pair == 0

    def col(off):
        return pl.BlockSpec((1, nb * c, pair), lambda bi, p, j: (bi, j, off + p))

    vec_spec = pl.BlockSpec((8, pair), lambda bi, p, j: (0, p))
    mat = pl.BlockSpec((1, nb, 1, pair, pair), lambda bi, p, j: (bi, j, p, 0, 0))
    in_specs = [col(0), col(npair), col(2 * npair), col(0), col(0)]
    args = [zs, zs, zs, wl, al]
    if mix:
        in_specs += [col(0), col(2 * npair)]
        args += [vu, zs_first]
    q, y0, gt, ht, bonus = pl.pallas_call(
        functools.partial(_wkv_chunk_kernel, nb=nb, c=c, mix=mix),
        grid=(bsz, npair, nch // nb),
        in_specs=in_specs + [vec_spec],
        out_specs=[col(0), col(0), mat, mat, col(0)],
        out_shape=[jax.ShapeDtypeStruct((bsz, s, width), BF16), jax.ShapeDtypeStruct((bsz, s, width), F32),
                   jax.ShapeDtypeStruct((bsz, nch, npair, pair, pair), BF16),
                   jax.ShapeDtypeStruct((bsz, nch, npair, pair, pair), BF16),
                   jax.ShapeDtypeStruct((bsz, s, width), F32)],
        compiler_params=_params("parallel", "parallel", "parallel"),
    )(*args, vec)
    row = pl.BlockSpec((1, c, width), lambda bi, j: (bi, j, 0))
    mats = pl.BlockSpec((1, 1, npair, pair, pair), lambda bi, j: (bi, j, 0, 0, 0))
    return pl.pallas_call(
        functools.partial(_wkv_scan_kernel, npair=npair),
        grid=(bsz, nch),
        in_specs=[row, row, mats, mats, row, row, pl.BlockSpec((8, width), lambda bi, j: (0, 0))],
        out_specs=row,
        out_shape=jax.ShapeDtypeStruct((bsz, s, width), BF16),
        scratch_shapes=[pltpu.VMEM((npair, pair, pair), F32)],
        compiler_params=_params("parallel", "arbitrary"),
    )(q, y0, gt, ht, bonus, g, lnx)


def _moe_up_kernel(te_ref, tv_ref, a_ref, wg_ref, wu_ref, o_ref, wg_bf, wu_bf):
    i = pl.program_id(1)
    changed = jnp.logical_or(i == 0, te_ref[i] != te_ref[jnp.maximum(i - 1, 0)])

    @pl.when(changed)
    def _():
        wg_bf[...] = wg_ref[...].astype(BF16)
        wu_bf[...] = wu_ref[...].astype(BF16)

    @pl.when(tv_ref[i] > 0)
    def _():
        a = a_ref[...]
        g = _dot(a, wg_bf[...])
        u = _dot(a, wu_bf[...])
        o_ref[...] = (_silu(g) * u).astype(o_ref.dtype)

    @pl.when(tv_ref[i] == 0)
    def _():
        o_ref[...] = jnp.zeros_like(o_ref)


def moe_up(tile_expert, tile_valid, xs, w_gu, layer, *, tm, tn):
    r, k = xs.shape
    f = w_gu.shape[3] // 2
    nj = f // tn
    grid_spec = pltpu.PrefetchScalarGridSpec(
        num_scalar_prefetch=2,
        grid=(nj, r // tm),
        in_specs=[pl.BlockSpec((tm, k), lambda j, i, te, tv: (i, 0)),
                  pl.BlockSpec((None, None, k, tn), lambda j, i, te, tv: (layer, te[i], 0, j)),
                  pl.BlockSpec((None, None, k, tn), lambda j, i, te, tv: (layer, te[i], 0, j + nj))],
        out_specs=pl.BlockSpec((tm, tn), lambda j, i, te, tv: (i, j)),
        scratch_shapes=[pltpu.VMEM((k, tn), BF16), pltpu.VMEM((k, tn), BF16)],
    )
    return pl.pallas_call(
        _moe_up_kernel,
        grid_spec=grid_spec,
        out_shape=jax.ShapeDtypeStruct((r, f), BF16),
        compiler_params=_params("parallel", "arbitrary"),
    )(tile_expert, tile_valid, xs, w_gu, w_gu)


def _moe_down_kernel(te_ref, tv_ref, a_ref, w_ref, o_ref):
    i = pl.program_id(0)

    @pl.when(tv_ref[i] > 0)
    def _():
        o_ref[...] = _dot(a_ref[...], w_ref[...]).astype(o_ref.dtype)

    @pl.when(tv_ref[i] == 0)
    def _():
        o_ref[...] = jnp.zeros_like(o_ref)


def moe_down(tile_expert, tile_valid, act, w_down, layer, *, tm):
    r, f = act.shape
    d = w_down.shape[3]
    grid_spec = pltpu.PrefetchScalarGridSpec(
        num_scalar_prefetch=2,
        grid=(r // tm,),
        in_specs=[pl.BlockSpec((tm, f), lambda i, te, tv: (i, 0)),
                  pl.BlockSpec((None, None, f, d), lambda i, te, tv: (layer, te[i], 0, 0))],
        out_specs=pl.BlockSpec((tm, d), lambda i, te, tv: (i, 0)),
    )
    return pl.pallas_call(
        _moe_down_kernel,
        grid_spec=grid_spec,
        out_shape=jax.ShapeDtypeStruct((r, d), BF16),
        compiler_params=_params("arbitrary"),
    )(tile_expert, tile_valid, act, w_down)


def _rope_tables(positions):
    inv_freq = ROPE_THETA ** (-jnp.arange(ROPE_HALF, dtype=F32) / ROPE_HALF)
    ang = positions.astype(F32)[:, :, None] * inv_freq
    cos = jnp.cos(ang)
    sin = jnp.sin(ang)
    b, s = positions.shape
    pad1 = jnp.ones((b, s, HEAD_DIM - ROPE_DIM), F32)
    pad0 = jnp.zeros((b, s, HEAD_DIM - ROPE_DIM), F32)
    return jnp.concatenate([cos, cos, pad1], axis=-1), jnp.concatenate([-sin, sin, pad0], axis=-1)


def _pad_cols(w, n):
    return w if w.shape[-1] == n else jnp.pad(w, ((0, 0), (0, n - w.shape[-1])))


def nsa_branch(q_a, kv_a, cosf, sinf, cmp_pe, cmp_w1, cmp_w2):
    b, s, _ = q_a.shape
    nc = s // CMP_STRIDE - 1
    ncp = -(-nc // LANE) * LANE
    kvw = NSA_KV_HEADS * HEAD_DIM

    def blocks(t):
        ch = t.reshape(b, s // CMP_STRIDE, CMP_STRIDE, NSA_KV_HEADS, HEAD_DIM)
        blk = jnp.concatenate([ch[:, :-1], ch[:, 1:]], axis=2)
        return blk.transpose(0, 1, 3, 2, 4).reshape(b * nc * NSA_KV_HEADS, CMP_BLOCK * HEAD_DIM)

    rows = b * nc * NSA_KV_HEADS
    rows_p = -(-rows // 512) * 512 if rows > 512 else -(-rows // 8) * 8
    flat = jnp.stack([blocks(kv_a[..., :kvw]), blocks(kv_a[..., kvw:2 * kvw])]).astype(F32)
    flat = jnp.pad(flat, ((0, 0), (0, rows_p - rows), (0, 0)))
    comp = nsa_compress(flat, cmp_pe.reshape(2, 1, CMP_BLOCK * HEAD_DIM), cmp_w1.astype(BF16), cmp_w2.astype(BF16))
    comp = comp[:, :rows].reshape(2, b, nc, NSA_KV_HEADS, HEAD_DIM).transpose(0, 1, 3, 2, 4)
    comp = jnp.pad(comp, ((0, 0), (0, 0), (0, 0), (0, ncp - nc), (0, 0)))
    kc = comp[0]
    vct = comp[1].transpose(0, 1, 3, 2)
    tq = min(256, s)
    o_cmp, sel = cmp_select(q_a, kc, vct, nc=nc, tq=tq)
    o_slc = slc_attention(q_a, kv_a, sel, cosf, sinf, k_col=4, v_col=6, tq=min(512, s))
    o_win = band_attention(q_a, kv_a, kv_a, nheads=NSA_GROUP, kv_heads=1, q_col=lambda c: c,
                           k_col=lambda c: 8 + c, v_col=lambda c: 10 + c, o_cols=NSA_KV_HEADS,
                           ncol=NSA_KV_HEADS, max_dist=WIN_SIZE - 1, tq=tq, rope=(cosf, sinf))
    return o_cmp, o_slc, o_win


def rwkv_branch(zs, zs_first, vec, w_up, a_up, g_up, v_res):
    b, s, _ = zs.shape
    t = b * s
    w_ = RWKV_WIDTH
    o = 3 * w_
    zw, za = zs[..., o:o + LORA_W], zs[..., o + LORA_W:o + LORA_W + LORA_A]
    zg = zs[..., o + LORA_W + LORA_A:o + LORA_W + LORA_A + LORA_G]
    w0, a0, k_k, k_a, r_k, lnx_g, lnx_b = [vec[i] for i in range(7)]

    def lora(xin, wmat):
        return matmul(xin.reshape(t, -1).astype(BF16), wmat.astype(BF16), out_dtype=F32, tm=1024,
                      tn=512).reshape(b, s, -1)

    wl = lora(jnp.tanh(zw), w_up)
    al = lora(za, a_up)
    g = lora(jax.nn.sigmoid(zg), g_up)
    zero = jnp.zeros_like(w0)
    if v_res is None:
        vu, v0 = None, zero
    else:
        v0, v_down, v_up = v_res
        vd = matmul(zs.reshape(t, -1), _pad_cols(v_down, LANE).astype(BF16), out_dtype=BF16, tm=1024, tn=LANE,
                    a_col=2)
        vu = matmul(vd, jnp.pad(v_up, ((0, LANE - v_up.shape[0]), (0, 0))).astype(BF16), out_dtype=F32, tm=1024,
                    tn=512).reshape(b, s, w_)
    vecs = jnp.stack([w0, a0, k_k, k_a, r_k, v0, zero, zero])
    lnx = jnp.stack([lnx_g, lnx_b] + [zero] * 6)
    return wkv7(zs, zs_first, wl, al, vu, vecs, g, lnx)


def dilated_branch(qkvs):
    b, s, width = qkvs[0].shape
    outs, lses = [], []
    for gi, (win, dil) in enumerate(DIL_PATTERNS):
        view = qkvs[gi].reshape(b, s // dil, dil * width)
        o, lse = band_attention(
            view, view, view, nheads=DIL_HPG, kv_heads=DIL_HPG,
            q_col=lambda c: 3 * c, k_col=lambda c: 3 * c + 1, v_col=lambda c: 3 * c + 2,
            o_cols=dil, ncol=dil, max_dist=win // dil, tq=256, with_lse=True, out_dtype=F32)
        outs.append(o.reshape(b, s, DIL_HPG, HEAD_DIM))
        lses.append(lse.reshape(b, s, LANE)[..., :DIL_HPG])
    alpha = jax.nn.softmax(jnp.stack(lses, axis=-1), axis=-1)
    o = sum(alpha[..., gi:gi + 1] * outs[gi] for gi in range(len(DIL_PATTERNS)))
    return o.reshape(b, s, DIL_OUT).astype(BF16)


def moe_ffn(h, router_w, router_b, w_gu, w_down, layer, *, tm):
    t, d = h.shape
    logits = matmul(h, _pad_cols(router_w, LANE).astype(BF16), out_dtype=F32, tm=1024, tn=LANE)[:, :N_EXPERTS]
    logits = logits + router_b
    top_v, top_i = lax.top_k(logits, TOP_K)
    wts = jax.nn.softmax(top_v, axis=-1)
    flat_e = top_i.reshape(-1)
    onehot = (flat_e[:, None] == jnp.arange(N_EXPERTS)[None, :]).astype(jnp.int32)
    rank = jnp.take_along_axis(jnp.cumsum(onehot, axis=0), flat_e[:, None], axis=1)[:, 0] - 1
    counts = jnp.sum(onehot, axis=0)
    tiles_per = (counts + tm - 1) // tm
    tile_end = jnp.cumsum(tiles_per)
    group_start = (tile_end - tiles_per) * tm
    dest = group_start[flat_e] + rank
    ntiles = (TOP_K * t) // tm + N_EXPERTS
    rows = ntiles * tm
    row_token = jnp.zeros((rows,), jnp.int32).at[dest].set(jnp.arange(TOP_K * t, dtype=jnp.int32) // TOP_K)
    tile_ids = jnp.arange(ntiles, dtype=jnp.int32)
    tile_valid = (tile_ids < tile_end[-1]).astype(jnp.int32)
    tile_expert = jnp.minimum(jnp.searchsorted(tile_end, tile_ids, side="right"), N_EXPERTS - 1).astype(jnp.int32)
    xs = jnp.take(h, row_token, axis=0, mode="clip")
    act = moe_up(tile_expert, tile_valid, xs, w_gu, layer, tm=tm, tn=512)
    out = moe_down(tile_expert, tile_valid, act, w_down, layer, tm=tm)
    dest = dest.reshape(t, TOP_K)
    ya = jnp.take(out, dest[:, 0], axis=0, mode="clip")
    yb = jnp.take(out, dest[:, 1], axis=0, mode="clip")
    return ya, yb, jnp.pad(wts, ((0, 0), (0, LANE - TOP_K)))


def kernel(x, c, positions, ada_w, ada_b, norm_g, w_in, cmp_pe, cmp_w1, cmp_w2, rwkv_mu, rwkv_vec, w_up, a_up, g_up, v_res0, v_res_down, v_res_up, w_br_a, w_br_b, w_br_c, w_out, ffn_gu, ffn_down, router_w, router_b, moe_gu, moe_down):
    b, s, d = x.shape
    depth = ada_w.shape[0]
    t = b * s
    tm_row = min(512, s)
    cosf, sinf = _rope_tables(positions)
    cos_t = cosf.reshape(t, LANE)
    sin_t = sinf.reshape(t, LANE)

    cond = jnp.pad(jax.nn.silu(c), ((0, 8 - b % 8 if b % 8 else 0), (0, 0))).astype(BF16)
    mods = []
    for l in range(depth):
        mod = matmul(cond, ada_w, out_dtype=F32, tm=cond.shape[0], tn=512, layer=l)[:b] + ada_b[l]
        mods.append(mod.reshape(b, 6, 1, d))

    def mod_of(l, i):
        return mods[l][:, i]

    q_cols = NSA_HEADS * HEAD_DIM
    kv_cols = 6 * NSA_KV_HEADS * HEAD_DIM
    gate_cols = 3 * NSA_HEADS
    rwkv_cols = 3 * RWKV_WIDTH + LORA_W + LORA_A + LORA_G
    dil_cols = 3 * DIL_HEADS * HEAD_DIM
    offs = np.cumsum([0, q_cols, kv_cols, gate_cols, rwkv_cols, dil_cols, 3 * d]).tolist()
    rwkv_pad = -(-rwkv_cols // 512) * 512

    xf = x.reshape(t, d)
    h = norm_mod(xf, norm_g[0, 0][None], mod_of(0, 1), mod_of(0, 0), seq=s, tm=tm_row)
    zs_first = None
    moe_down_bf = moe_down.astype(BF16)
    for l in range(depth):
        wl = w_in[l]
        seg = lambda i: wl[:, offs[i]:offs[i + 1]]
        q_a = matmul(h, seg(0).astype(BF16), out_dtype=BF16, tm=1024, tn=512)
        kv_a = matmul(h, seg(1).astype(BF16), out_dtype=BF16, tm=1024, tn=NSA_KV_HEADS * HEAD_DIM,
                      rope=(cos_t, sin_t, (2, 4)))
        gate_a = matmul(h, _pad_cols(seg(2), LANE).astype(BF16), out_dtype=F32, tm=1024, tn=LANE, act="sigmoid")
        zs = matmul_token_shift(h, _pad_cols(seg(3), rwkv_pad).astype(BF16),
                                _pad_cols(rwkv_mu[l][None], rwkv_pad), seq=s, tm=1024, tn=512).reshape(b, s, -1)
        zs_first = zs if l == 0 else zs_first
        wc = seg(4).reshape(d, 3, len(DIL_PATTERNS), DIL_OUT)
        qkv_c = [matmul(h, wc[:, :, gi].reshape(d, 3 * DIL_OUT).astype(BF16), out_dtype=BF16, tm=1024, tn=DIL_OUT,
                        rope=(cos_t, sin_t, (0, 1))).reshape(b, s, -1) for gi in range(len(DIL_PATTERNS))]
        mg = matmul(h, seg(5).astype(BF16), out_dtype=BF16, tm=1024, tn=512, act="sigmoid")

        o_cmp, o_slc, o_win = nsa_branch(q_a.reshape(b, s, -1), kv_a.reshape(b, s, -1), cosf, sinf,
                                         cmp_pe[l], cmp_w1[l], cmp_w2[l])
        v_res = None if l == 0 else (v_res0[l - 1], v_res_down[l - 1], v_res_up[l - 1])
        y_b = rwkv_branch(zs, zs_first, rwkv_vec[l], w_up[l], a_up[l], g_up[l], v_res)
        y_c = dilated_branch(qkv_c)
        merged = branch_merge(o_cmp.reshape(t, -1), o_slc.reshape(t, -1), o_win.reshape(t, -1), gate_a,
                              y_b.reshape(t, -1), y_c.reshape(t, -1), w_br_a[l].astype(BF16),
                              w_br_b[l].astype(BF16), w_br_c[l].astype(BF16), mg, tm=1024, tn=512)
        xf, h = matmul_close(merged, w_out[l].astype(BF16), xf, norm_g[l, 1][None], mod_of(l, 2),
                             (norm_g[l, 2][None], mod_of(l, 4), mod_of(l, 3)), seq=s, tm=tm_row, tk=d)

        nxt = None if l == depth - 1 else (norm_g[l + 1, 0][None], mod_of(l + 1, 1), mod_of(l + 1, 0))
        if l % 2 == 0:
            act = swiglu_up(h, ffn_gu[l // 2].astype(BF16), tm=1024, tn=512)
            xf, h = matmul_close(act, ffn_down[l // 2].astype(BF16), xf, norm_g[l, 3][None], mod_of(l, 5), nxt,
                                 seq=s, tm=tm_row, tk=act.shape[1] // 4)
        else:
            ya, yb, wts = moe_ffn(h, router_w[l // 2], router_b[l // 2], moe_gu, moe_down_bf, l // 2, tm=512)
            xf, h = close_sublayer(ya, yb, wts, xf, norm_g[l, 3][None], mod_of(l, 5), nxt, seq=s, tm=tm_row)
    return xf.reshape(b, s, d)
```

```python
import functools

import numpy as np
import jax
import jax.numpy as jnp
from jax import lax
from jax.experimental import pallas as pl
from jax.experimental.pallas import tpu as pltpu

F32 = jnp.float32
BF16 = jnp.bfloat16

HEAD_DIM = 128
ROPE_DIM = HEAD_DIM // 4
ROPE_HALF = ROPE_DIM // 2
ROPE_THETA = 500000.0
NORM_EPS = 1e-6

NSA_HEADS = 8
NSA_KV_HEADS = 2
NSA_GROUP = NSA_HEADS // NSA_KV_HEADS
CMP_BLOCK = 32
CMP_STRIDE = 16
SLC_BLOCK = 64
SLC_TOPN = 16
WIN_SIZE = 512
FORCE_BONUS = 1e4
SLC_PICK_BIAS = 8192.0

RWKV_HEADS = 16
RWKV_HEAD = 64
RWKV_WIDTH = RWKV_HEADS * RWKV_HEAD
LORA_W = 96
LORA_A = 96
LORA_G = 256
LNX_EPS = 64e-5
RWKV_CHUNK = 64
RWKV_CHUNKS_PER_STEP = 8

DIL_PATTERNS = ((128, 1), (512, 4), (2048, 16))
DIL_HPG = 4
DIL_HEADS = DIL_HPG * len(DIL_PATTERNS)
DIL_OUT = DIL_HPG * HEAD_DIM

N_EXPERTS = 8
TOP_K = 2

LANE = 128
VMEM_LIMIT_BYTES = 56 * 1024 * 1024
MASKED = -1e30


def _params(*sem):
    return pltpu.CompilerParams(dimension_semantics=sem, vmem_limit_bytes=VMEM_LIMIT_BYTES)


def _sigmoid(x):
    return 1.0 / (1.0 + jnp.exp(-x))


def _silu(x):
    return x * _sigmoid(x)


def _dot(a, b):
    return jnp.dot(a, b, preferred_element_type=F32)


def _dot_nt(a, b):
    return lax.dot_general(a, b, (((1,), (1,)), ((), ())), preferred_element_type=F32)


def _dot_tn(a, b):
    return lax.dot_general(a, b, (((0,), (0,)), ((), ())), preferred_element_type=F32)


def _rope(t, cosf, sinf):
    lane = lax.broadcasted_iota(jnp.int32, t.shape, 1)
    swapped = jnp.where(lane < ROPE_HALF, pltpu.roll(t, LANE - ROPE_HALF, 1), pltpu.roll(t, ROPE_HALF, 1))
    return t * cosf + swapped * sinf


def _rms(y):
    return y * lax.rsqrt(jnp.mean(y * y, axis=-1, keepdims=True) + NORM_EPS)


def _matmul_kernel(*refs, act, rope_chunks, ntiles):
    if rope_chunks:
        a_ref, w_ref, cos_ref, sin_ref, o_ref = refs
    else:
        a_ref, w_ref, o_ref = refs
    acc = _dot(a_ref[...].astype(BF16), w_ref[...].astype(BF16))
    if act == "sigmoid":
        acc = _sigmoid(acc)
    elif act == "tanh":
        acc = jnp.tanh(acc)
    if not rope_chunks:
        o_ref[...] = acc.astype(o_ref.dtype)
        return
    per_tile = acc.shape[1] // LANE

    def store(tile):
        for c in range(per_tile):
            sl = slice(c * LANE, (c + 1) * LANE)
            if tile is not None and tile * per_tile + c in rope_chunks:
                o_ref[:, sl] = _rope(acc[:, sl], cos_ref[...], sin_ref[...]).astype(o_ref.dtype)
            else:
                o_ref[:, sl] = acc[:, sl].astype(o_ref.dtype)

    if ntiles == 1:
        store(0)
        return
    j = pl.program_id(1)
    tiles = sorted({c // per_tile for c in rope_chunks})
    for t in tiles:
        pl.when(j == t)(functools.partial(store, t))
    pl.when(functools.reduce(jnp.logical_and, [j != t for t in tiles]))(functools.partial(store, None))


def _matmul_shift_kernel(a_ref, ap_ref, w_ref, mu_ref, o_ref, *, seq, tm):
    w = w_ref[...]
    z = _dot(a_ref[...], w)
    zp = _dot(ap_ref[...], w)
    at_start = (pl.program_id(0) * tm) % seq == 0
    last = zp.shape[0] - 1
    prev_row = jnp.where(at_start, 0.0, zp[last:last + 1, :])
    rowid = lax.broadcasted_iota(jnp.int32, z.shape, 0)
    shifted = jnp.where(rowid == 0, prev_row, pltpu.roll(z, 1, 0))
    o_ref[...] = (z + (shifted - z) * mu_ref[...]).astype(o_ref.dtype)


BF16_SUBLANES = 16


def matmul_token_shift(a, w, mu, *, seq, tm, tn):
    m, k = a.shape
    n = w.shape[1]
    assert m % tm == 0 and n % tn == 0 and seq % tm == 0
    per = tm // BF16_SUBLANES
    return pl.pallas_call(
        functools.partial(_matmul_shift_kernel, seq=seq, tm=tm),
        grid=(m // tm, n // tn),
        in_specs=[pl.BlockSpec((tm, k), lambda i, j: (i, 0)),
                  pl.BlockSpec((BF16_SUBLANES, k), lambda i, j: (jnp.maximum(i * per - 1, 0), 0)),
                  pl.BlockSpec((k, tn), lambda i, j: (0, j)),
                  pl.BlockSpec((1, tn), lambda i, j: (0, j))],
        out_specs=pl.BlockSpec((tm, tn), lambda i, j: (i, j)),
        out_shape=jax.ShapeDtypeStruct((m, n), F32),
        compiler_params=_params("parallel", "parallel"),
    )(a, a, w, mu)


def matmul(a, w, *, out_dtype, tm, tn, act=None, rope=None, layer=None, a_col=0):
    m = a.shape[0]
    k = w.shape[-2]
    n = w.shape[-1]
    tm = min(tm, m)
    assert m % tm == 0 and n % tn == 0, (m, n, tm, tn)
    if layer is None:
        w_spec = pl.BlockSpec((k, tn), lambda i, j: (0, j))
    else:
        w_spec = pl.BlockSpec((None, k, tn), lambda i, j: (layer, 0, j))
    in_specs = [pl.BlockSpec((tm, k), lambda i, j: (i, a_col)), w_spec]
    args = [a, w]
    tiles = ()
    if rope is not None:
        cosf, sinf, tiles = rope
        in_specs += [pl.BlockSpec((tm, LANE), lambda i, j: (i, 0))] * 2
        args += [cosf, sinf]
    return pl.pallas_call(
        functools.partial(_matmul_kernel, act=act, rope_chunks=frozenset(tiles), ntiles=n // tn),
        grid=(m // tm, n // tn),
        in_specs=in_specs,
        out_specs=pl.BlockSpec((tm, tn), lambda i, j: (i, j)),
        out_shape=jax.ShapeDtypeStruct((m, n), out_dtype),
        compiler_params=_params("parallel", "parallel"),
    )(*args)


def _swiglu_up_kernel(a_ref, wg_ref, wu_ref, o_ref):
    a = a_ref[...]
    g = _dot(a, wg_ref[...])
    u = _dot(a, wu_ref[...])
    o_ref[...] = (_silu(g) * u).astype(o_ref.dtype)


def swiglu_up(a, w_gu, *, tm, tn):
    m, k = a.shape
    f = w_gu.shape[1] // 2
    assert m % tm == 0 and f % tn == 0
    nj = f // tn
    return pl.pallas_call(
        _swiglu_up_kernel,
        grid=(m // tm, nj),
        in_specs=[pl.BlockSpec((tm, k), lambda i, j: (i, 0)),
                  pl.BlockSpec((k, tn), lambda i, j: (0, j)),
                  pl.BlockSpec((k, tn), lambda i, j: (0, j + nj))],
        out_specs=pl.BlockSpec((tm, tn), lambda i, j: (i, j)),
        out_shape=jax.ShapeDtypeStruct((m, f), BF16),
        compiler_params=_params("parallel", "parallel"),
    )(a, w_gu, w_gu)


def _close_sublayer(x, y, gpost, gt, nxt):
    xn = x + gt * (_rms(y) * gpost)
    if nxt is None:
        return xn, None
    gnext, sc, sh = nxt
    return xn, (_rms(xn) * gnext) * (1.0 + sc) + sh


def _matmul_close_kernel(*refs, with_next):
    if with_next:
        a_ref, w_ref, x_ref, gp_ref, gt_ref, gn_ref, sc_ref, sh_ref, xo_ref, ho_ref, acc_ref = refs
    else:
        a_ref, w_ref, x_ref, gp_ref, gt_ref, xo_ref, acc_ref = refs
    kk = pl.program_id(1)

    @pl.when(kk == 0)
    def _():
        acc_ref[...] = jnp.zeros_like(acc_ref)

    acc_ref[...] += _dot(a_ref[...], w_ref[...])

    @pl.when(kk == pl.num_programs(1) - 1)
    def _():
        nxt = (gn_ref[...], sc_ref[0], sh_ref[0]) if with_next else None
        xn, h = _close_sublayer(x_ref[...], acc_ref[...], gp_ref[...], gt_ref[0], nxt)
        xo_ref[...] = xn
        if with_next:
            ho_ref[...] = h.astype(ho_ref.dtype)


def _close_specs(tm, d, seq, with_next, nidx):
    def row(i, *_):
        return (i, 0)

    def const(*_):
        return (0, 0)

    def batch(i, *_):
        return ((i * tm) // seq, 0, 0)

    specs = [pl.BlockSpec((tm, d), row), pl.BlockSpec((1, d), const), pl.BlockSpec((1, 1, d), batch)]
    if with_next:
        specs += [pl.BlockSpec((1, d), const), pl.BlockSpec((1, 1, d), batch), pl.BlockSpec((1, 1, d), batch)]
    return specs


def matmul_close(a, w, x, gpost, gt, nxt, *, seq, tm, tk):
    m, k = a.shape
    d = w.shape[1]
    assert m % tm == 0 and k % tk == 0 and seq % tm == 0
    with_next = nxt is not None
    args = [a, w, x, gpost, gt] + (list(nxt) if with_next else [])
    in_specs = [pl.BlockSpec((tm, tk), lambda i, kk: (i, kk)), pl.BlockSpec((tk, d), lambda i, kk: (kk, 0))]
    in_specs += _close_specs(tm, d, seq, with_next, 2)
    out_shape = [jax.ShapeDtypeStruct((m, d), F32)]
    out_specs = [pl.BlockSpec((tm, d), lambda i, kk: (i, 0))]
    if with_next:
        out_shape.append(jax.ShapeDtypeStruct((m, d), BF16))
        out_specs.append(pl.BlockSpec((tm, d), lambda i, kk: (i, 0)))
    res = pl.pallas_call(
        functools.partial(_matmul_close_kernel, with_next=with_next),
        grid=(m // tm, k // tk),
        in_specs=in_specs,
        out_specs=out_specs,
        out_shape=out_shape,
        scratch_shapes=[pltpu.VMEM((tm, d), F32)],
        compiler_params=_params("parallel", "arbitrary"),
    )(*args)
    return (res[0], res[1]) if with_next else (res[0], None)


def _close_kernel(*refs, with_next):
    if with_next:
        ya_ref, yb_ref, wt_ref, x_ref, gp_ref, gt_ref, gn_ref, sc_ref, sh_ref, xo_ref, ho_ref = refs
    else:
        ya_ref, yb_ref, wt_ref, x_ref, gp_ref, gt_ref, xo_ref = refs
    nxt = (gn_ref[...], sc_ref[0], sh_ref[0]) if with_next else None
    wt = wt_ref[...]
    y = wt[:, 0:1] * ya_ref[...].astype(F32) + wt[:, 1:2] * yb_ref[...].astype(F32)
    xn, h = _close_sublayer(x_ref[...], y, gp_ref[...], gt_ref[0], nxt)
    xo_ref[...] = xn
    if with_next:
        ho_ref[...] = h.astype(ho_ref.dtype)


def close_sublayer(ya, yb, wt, x, gpost, gt, nxt, *, seq, tm):
    m, d = x.shape
    with_next = nxt is not None
    args = [ya, yb, wt, x, gpost, gt] + (list(nxt) if with_next else [])
    in_specs = [pl.BlockSpec((tm, d), lambda i: (i, 0)), pl.BlockSpec((tm, d), lambda i: (i, 0)),
                pl.BlockSpec((tm, LANE), lambda i: (i, 0))] + _close_specs(tm, d, seq, with_next, 1)
    out_shape = [jax.ShapeDtypeStruct((m, d), F32)]
    out_specs = [pl.BlockSpec((tm, d), lambda i: (i, 0))]
    if with_next:
        out_shape.append(jax.ShapeDtypeStruct((m, d), BF16))
        out_specs.append(pl.BlockSpec((tm, d), lambda i: (i, 0)))
    res = pl.pallas_call(
        functools.partial(_close_kernel, with_next=with_next),
        grid=(m // tm,),
        in_specs=in_specs,
        out_specs=out_specs,
        out_shape=out_shape,
        compiler_params=_params("parallel"),
    )(*args)
    return (res[0], res[1]) if with_next else (res[0], None)


def _norm_mod_kernel(x_ref, g_ref, sc_ref, sh_ref, o_ref):
    o_ref[...] = ((_rms(x_ref[...]) * g_ref[...]) * (1.0 + sc_ref[0]) + sh_ref[0]).astype(o_ref.dtype)


def norm_mod(x, g, sc, sh, *, seq, tm):
    m, d = x.shape
    batch = lambda i: ((i * tm) // seq, 0, 0)
    return pl.pallas_call(
        _norm_mod_kernel,
        grid=(m // tm,),
        in_specs=[pl.BlockSpec((tm, d), lambda i: (i, 0)), pl.BlockSpec((1, d), lambda i: (0, 0)),
                  pl.BlockSpec((1, 1, d), batch), pl.BlockSpec((1, 1, d), batch)],
        out_specs=pl.BlockSpec((tm, d), lambda i: (i, 0)),
        out_shape=jax.ShapeDtypeStruct((m, d), BF16),
        compiler_params=_params("parallel"),
    )(x, g, sc, sh)


def _branch_merge_kernel(oc_ref, os_ref, ow_ref, ng_ref, yb_ref, yc_ref, wa_ref, wb_ref, wc_ref, ga_ref, gb_ref,
                         gc_ref, o_ref, ya_ref):
    @pl.when(pl.program_id(1) == 0)
    def _():
        ng = ng_ref[...]
        for hd in range(NSA_HEADS):
            sl = slice(hd * HEAD_DIM, (hd + 1) * HEAD_DIM)
            ya = ng[:, 3 * hd:3 * hd + 1] * oc_ref[:, sl].astype(F32)
            ya += ng[:, 3 * hd + 1:3 * hd + 2] * os_ref[:, sl].astype(F32)
            ya += ng[:, 3 * hd + 2:3 * hd + 3] * ow_ref[:, sl].astype(F32)
            ya_ref[:, sl] = ya.astype(ya_ref.dtype)

    acc = ga_ref[...].astype(F32) * _dot(ya_ref[...], wa_ref[...])
    acc += gb_ref[...].astype(F32) * _dot(yb_ref[...], wb_ref[...])
    acc += gc_ref[...].astype(F32) * _dot(yc_ref[...], wc_ref[...])
    o_ref[...] = acc.astype(o_ref.dtype)


def branch_merge(o_cmp, o_slc, o_win, nsa_gate, yb, yc, wa, wb, wc, gates, *, tm, tn):
    m = yb.shape[0]
    d = wa.shape[1]
    nj = d // tn
    row = lambda width: pl.BlockSpec((tm, width), lambda i, j: (i, 0))
    wsp = lambda kdim: pl.BlockSpec((kdim, tn), lambda i, j: (0, j))
    gsp = lambda off: pl.BlockSpec((tm, tn), lambda i, j: (i, j + off * nj))
    wa_rows = wa.shape[0]
    return pl.pallas_call(
        _branch_merge_kernel,
        grid=(m // tm, nj),
        in_specs=[row(wa_rows), row(wa_rows), row(wa_rows), row(LANE), row(yb.shape[1]), row(yc.shape[1]),
                  wsp(wa_rows), wsp(wb.shape[0]), wsp(wc.shape[0]), gsp(0), gsp(1), gsp(2)],
        out_specs=pl.BlockSpec((tm, tn), lambda i, j: (i, j)),
        out_shape=jax.ShapeDtypeStruct((m, d), BF16),
        scratch_shapes=[pltpu.VMEM((tm, wa_rows), BF16)],
        compiler_params=_params("parallel", "arbitrary"),
    )(o_cmp, o_slc, o_win, nsa_gate, yb, yc, wa, wb, wc, gates, gates, gates)


def _compress_kernel(f_ref, pe_ref, w1_ref, w2_ref, o_ref):
    a = (f_ref[0] + pe_ref[0]).astype(BF16)
    hmid = _silu(_dot(a, w1_ref[0]))
    o_ref[0] = _dot(hmid.astype(BF16), w2_ref[0]).astype(o_ref.dtype)


def nsa_compress(flat, pe, w1, w2):
    two, r, kdim = flat.shape
    tm = min(r, 512)
    assert r % tm == 0
    return pl.pallas_call(
        _compress_kernel,
        grid=(two, r // tm),
        in_specs=[pl.BlockSpec((1, tm, kdim), lambda t, i: (t, i, 0)),
                  pl.BlockSpec((1, 1, kdim), lambda t, i: (t, 0, 0)),
                  pl.BlockSpec((1, kdim, HEAD_DIM), lambda t, i: (t, 0, 0)),
                  pl.BlockSpec((1, HEAD_DIM, HEAD_DIM), lambda t, i: (t, 0, 0))],
        out_specs=pl.BlockSpec((1, tm, HEAD_DIM), lambda t, i: (t, i, 0)),
        out_shape=jax.ShapeDtypeStruct((two, r, HEAD_DIM), BF16),
        compiler_params=_params("parallel", "parallel"),
    )(flat, pe, w1, w2)


def _cmp_select_kernel(q_ref, kc_ref, vct_ref, o_ref, sel_ref, *, tq, nc, ncp, nsp, scale):
    qi = pl.program_id(2)
    kc = kc_ref[0, 0]
    vct = vct_ref[0, 0]
    spos = qi * tq + lax.broadcasted_iota(jnp.int32, (1, tq), 1)
    cidx = lax.broadcasted_iota(jnp.int32, (ncp, 1), 0)
    valid = jnp.logical_and(cidx * CMP_STRIDE + (CMP_BLOCK - 1) <= spos, cidx < nc)
    psum = jnp.zeros((ncp, tq), F32)
    for g in range(NSA_GROUP):
        qg = q_ref[0, :, g * HEAD_DIM:(g + 1) * HEAD_DIM]
        st = _dot_nt(kc, qg) * scale
        st = jnp.where(valid, st, MASKED)
        mx = jnp.max(st, axis=0, keepdims=True)
        e = jnp.where(valid, jnp.exp(st - mx), 0.0)
        den = jnp.sum(e, axis=0, keepdims=True)
        p = e / jnp.where(den > 0, den, 1.0)
        psum = psum + p
        og_t = _dot(vct, p.astype(BF16))
        o_ref[0, :, g * HEAD_DIM:(g + 1) * HEAD_DIM] = og_t.T.astype(o_ref.dtype)
    jrow = lax.broadcasted_iota(jnp.int32, (nsp, ncp), 0)
    ccol = lax.broadcasted_iota(jnp.int32, (nsp, ncp), 1)
    c0 = ccol * CMP_STRIDE
    j0 = jrow * SLC_BLOCK
    cover_t = jnp.logical_and(c0 < j0 + SLC_BLOCK, c0 + CMP_BLOCK > j0).astype(F32)
    imp = jnp.dot(cover_t, psum, preferred_element_type=F32, precision=lax.Precision.HIGHEST)
    j = lax.broadcasted_iota(jnp.int32, (nsp, 1), 0).astype(F32)
    cur = (spos // SLC_BLOCK).astype(F32)
    forced = jnp.logical_or(jnp.logical_or(j == 0, j == cur), j == cur - 1)
    score = jnp.where(j <= cur, imp + FORCE_BONUS * forced.astype(F32), -jnp.inf)
    sel = jnp.zeros((nsp, tq), F32)
    for _ in range(SLC_TOPN):
        mx = jnp.max(score, axis=0, keepdims=True)
        first = jnp.min(jnp.where(score == mx, j, float(nsp)), axis=0, keepdims=True)
        pick = j == first
        sel = jnp.where(pick, 1.0, sel)
        score = jnp.where(pick, -jnp.inf, score)
    sel_ref[0, 0] = sel.T.astype(sel_ref.dtype)


def cmp_select(q, kc, vct, *, nc, tq):
    b, s, _ = q.shape
    ncp = kc.shape[2]
    nsp = LANE
    assert s % tq == 0 and s // SLC_BLOCK <= nsp
    gw = NSA_GROUP * HEAD_DIM
    return pl.pallas_call(
        functools.partial(_cmp_select_kernel, tq=tq, nc=nc, ncp=ncp, nsp=nsp, scale=HEAD_DIM ** -0.5),
        grid=(b, NSA_KV_HEADS, s // tq),
        in_specs=[pl.BlockSpec((1, tq, gw), lambda bi, h, i: (bi, i, h)),
                  pl.BlockSpec((1, 1, ncp, HEAD_DIM), lambda bi, h, i: (bi, h, 0, 0)),
                  pl.BlockSpec((1, 1, HEAD_DIM, ncp), lambda bi, h, i: (bi, h, 0, 0))],
        out_specs=[pl.BlockSpec((1, tq, gw), lambda bi, h, i: (bi, i, h)),
                   pl.BlockSpec((1, 1, tq, nsp), lambda bi, h, i: (bi, h, i, 0))],
        out_shape=[jax.ShapeDtypeStruct((b, s, NSA_HEADS * HEAD_DIM), BF16),
                   jax.ShapeDtypeStruct((b, NSA_KV_HEADS, s, nsp), BF16)],
        compiler_params=_params("parallel", "parallel", "parallel"),
    )(q, kc, vct)


def _slc_kernel(qi_ref, ki_ref, q_ref, k_ref, v_ref, sel_ref, cos_ref, sin_ref, o_ref, qaug_ref, m_ref, acc_ref,
                *, tq, scale):
    step = pl.program_id(2)
    qi = qi_ref[step]
    ki = ki_ref[step]

    @pl.when(ki == 0)
    def _():
        cosf = cos_ref[0]
        sinf = sin_ref[0]
        pick = sel_ref[0, 0] * SLC_PICK_BIAS
        for g in range(NSA_GROUP):
            sl = slice(g * HEAD_DIM, (g + 1) * HEAD_DIM)
            qg = _rope(q_ref[0, :, sl].astype(F32), cosf, sinf) * scale
            qaug_ref[g] = jnp.concatenate([qg.astype(BF16), pick], axis=1)
        m_ref[...] = jnp.full_like(m_ref, MASKED)
        acc_ref[...] = jnp.zeros_like(acc_ref)

    def accumulate(causal):
        nsp = sel_ref.shape[3]
        block_of_key = (ki * tq + lax.broadcasted_iota(jnp.int32, (tq, nsp), 0)) // SLC_BLOCK
        onehot = (block_of_key == lax.broadcasted_iota(jnp.int32, (tq, nsp), 1)).astype(BF16)
        kaug = jnp.concatenate([k_ref[0], onehot], axis=1)
        vaug = jnp.concatenate([v_ref[0], jnp.ones((tq, LANE), BF16)], axis=1)
        if causal:
            qpos = lax.broadcasted_iota(jnp.int32, (tq, tq), 0)
            kpos = lax.broadcasted_iota(jnp.int32, (tq, tq), 1)
            visible = kpos <= qpos
        for g in range(NSA_GROUP):
            s = _dot_nt(qaug_ref[g], kaug)
            if causal:
                s = jnp.where(visible, s, MASKED)
            m_old = m_ref[g]
            m_new = jnp.maximum(m_old, jnp.max(s, axis=-1, keepdims=True))
            alpha = jnp.exp(m_old - m_new)
            p = jnp.exp((s - jnp.concatenate([m_new] * (tq // LANE), axis=1)).astype(BF16))
            acc_ref[g] = jnp.concatenate([alpha, alpha], axis=1) * acc_ref[g] + _dot(p, vaug)
            m_ref[g] = m_new

    @pl.when(ki < qi)
    def _():
        accumulate(False)

    @pl.when(ki == qi)
    def _():
        accumulate(True)
        for g in range(NSA_GROUP):
            sl = slice(g * HEAD_DIM, (g + 1) * HEAD_DIM)
            o_ref[0, :, sl] = (acc_ref[g, :, :HEAD_DIM] / acc_ref[g, :, HEAD_DIM:]).astype(o_ref.dtype)


def slc_attention(q, kv, sel, cosf, sinf, *, k_col, v_col, tq):
    b, s, _ = q.shape
    gw = NSA_GROUP * HEAD_DIM
    nq = s // tq
    nsp = sel.shape[3]
    pairs = [(i, j) for i in range(nq) for j in range(i + 1)]
    qi_tab = jnp.asarray([p[0] for p in pairs], jnp.int32)
    ki_tab = jnp.asarray([p[1] for p in pairs], jnp.int32)
    grid_spec = pltpu.PrefetchScalarGridSpec(
        num_scalar_prefetch=2,
        grid=(b, NSA_KV_HEADS, len(pairs)),
        in_specs=[pl.BlockSpec((1, tq, gw), lambda bi, h, t, qt, kt: (bi, qt[t], h)),
                  pl.BlockSpec((1, tq, HEAD_DIM), lambda bi, h, t, qt, kt: (bi, kt[t], k_col + h)),
                  pl.BlockSpec((1, tq, HEAD_DIM), lambda bi, h, t, qt, kt: (bi, kt[t], v_col + h)),
                  pl.BlockSpec((1, 1, tq, nsp), lambda bi, h, t, qt, kt: (bi, h, qt[t], 0)),
                  pl.BlockSpec((1, tq, LANE), lambda bi, h, t, qt, kt: (bi, qt[t], 0)),
                  pl.BlockSpec((1, tq, LANE), lambda bi, h, t, qt, kt: (bi, qt[t], 0))],
        out_specs=pl.BlockSpec((1, tq, gw), lambda bi, h, t, qt, kt: (bi, qt[t], h)),
        scratch_shapes=[pltpu.VMEM((NSA_GROUP, tq, 2 * HEAD_DIM), BF16),
                        pltpu.VMEM((NSA_GROUP, tq, LANE), F32),
                        pltpu.VMEM((NSA_GROUP, tq, 2 * HEAD_DIM), F32)],
    )
    return pl.pallas_call(
        functools.partial(_slc_kernel, tq=tq, scale=HEAD_DIM ** -0.5),
        grid_spec=grid_spec,
        out_shape=jax.ShapeDtypeStruct((b, s, NSA_HEADS * HEAD_DIM), BF16),
        compiler_params=_params("parallel", "parallel", "arbitrary"),
    )(qi_tab, ki_tab, q, kv, kv, sel, cosf, sinf)


def _band_kernel(*refs, nheads, kv_heads, nkv, tq, max_dist, rope_q, with_lse, scale):
    q_ref = refs[0]
    k_refs = refs[1:1 + nkv]
    v_refs = refs[1 + nkv:1 + 2 * nkv]
    pos = 1 + 2 * nkv
    if rope_q:
        cos_ref, sin_ref = refs[pos:pos + 2]
        pos += 2
    o_ref = refs[pos]
    lse_ref = refs[pos + 1] if with_lse else None
    qi = pl.program_id(2)
    qpos = qi * tq + lax.broadcasted_iota(jnp.int32, (tq, 1), 0)
    kpos = (qi - (nkv - 1)) * tq + lax.broadcasted_iota(jnp.int32, (1, nkv * tq), 1)
    diff = qpos - kpos
    mask = jnp.logical_and(jnp.logical_and(diff >= 0, diff <= max_dist), kpos >= 0)
    lse_acc = jnp.zeros((tq, LANE), F32)
    lane = lax.broadcasted_iota(jnp.int32, (tq, LANE), 1)
    for g in range(nheads):
        sl = slice(g * HEAD_DIM, (g + 1) * HEAD_DIM)
        ksl = sl if kv_heads > 1 else slice(0, HEAD_DIM)
        q = q_ref[0, :, sl]
        if rope_q:
            q = _rope(q.astype(F32), cos_ref[0], sin_ref[0]).astype(BF16)
        kcat = jnp.concatenate([r[0, :, ksl] for r in k_refs], axis=0)
        vcat = jnp.concatenate([r[0, :, ksl] for r in v_refs], axis=0)
        vaug = jnp.concatenate([vcat, jnp.ones(vcat.shape, BF16)], axis=1)
        s = _dot_nt(q, kcat) * scale
        s = jnp.where(mask, s, MASKED)
        mx = jnp.max(s, axis=-1, keepdims=True)
        e = jnp.exp((s - mx).astype(BF16))
        pv = _dot(e, vaug)
        den = pv[:, HEAD_DIM:]
        o_ref[0, :, sl] = (pv[:, :HEAD_DIM] / den).astype(o_ref.dtype)
        if with_lse:
            lse_acc = jnp.where(lane == g, mx + jnp.log(den), lse_acc)
    if with_lse:
        lse_ref[0] = lse_acc


def band_attention(q, k, v, *, nheads, kv_heads, q_col, k_col, v_col, o_cols, ncol, max_dist, tq,
                   rope=None, with_lse=False, out_dtype=BF16):
    b, seq_len, _ = q.shape
    tq = min(tq, seq_len)
    assert seq_len % tq == 0
    nkv = -(-max_dist // tq) + 1
    qw = nheads * HEAD_DIM
    kw = kv_heads * HEAD_DIM

    def kv_spec(col_fn, back):
        return pl.BlockSpec((1, tq, kw), lambda bi, c, i: (bi, jnp.maximum(i - back, 0), col_fn(c)))

    in_specs = [pl.BlockSpec((1, tq, qw), lambda bi, c, i: (bi, i, q_col(c)))]
    in_specs += [kv_spec(k_col, nkv - 1 - t) for t in range(nkv)]
    in_specs += [kv_spec(v_col, nkv - 1 - t) for t in range(nkv)]
    args = [q] + [k] * nkv + [v] * nkv
    if rope is not None:
        in_specs += [pl.BlockSpec((1, tq, LANE), lambda bi, c, i: (bi, i, 0))] * 2
        args += list(rope)
    out_specs = [pl.BlockSpec((1, tq, qw), lambda bi, c, i: (bi, i, c))]
    out_shape = [jax.ShapeDtypeStruct((b, seq_len, o_cols * qw), out_dtype)]
    if with_lse:
        out_specs.append(pl.BlockSpec((1, tq, LANE), lambda bi, c, i: (bi, i, c)))
        out_shape.append(jax.ShapeDtypeStruct((b, seq_len, o_cols * LANE), F32))
    res = pl.pallas_call(
        functools.partial(_band_kernel, nheads=nheads, kv_heads=kv_heads, nkv=nkv, tq=tq, max_dist=max_dist,
                          rope_q=rope is not None, with_lse=with_lse, scale=HEAD_DIM ** -0.5),
        grid=(b, ncol, seq_len // tq),
        in_specs=in_specs,
        out_specs=out_specs,
        out_shape=out_shape,
        compiler_params=_params("parallel", "parallel", "parallel"),
    )(*args)
    return res if with_lse else res[0]


def _bmm(x, y):
    return jnp.einsum("bij,bjk->bik", x.astype(BF16), y.astype(BF16), preferred_element_type=F32)


def _bmm_nt(x, y):
    return jnp.einsum("bik,bjk->bij", x.astype(BF16), y.astype(BF16), preferred_element_type=F32)


def _head_sums(x2, scale=1.0):
    n = x2.shape[1]
    blk = (lax.broadcasted_iota(jnp.int32, (n, n), 0) // RWKV_HEAD
           == lax.broadcasted_iota(jnp.int32, (n, n), 1) // RWKV_HEAD)
    ones = jnp.where(blk, scale, 0.0).astype(BF16)
    hi = x2.astype(BF16)
    lo = (x2 - hi.astype(F32)).astype(BF16)
    return _dot(hi, ones) + _dot(lo, ones)


def _wkv_chunk_kernel(*refs, nb, c, mix):
    if mix:
        zr_ref, zk_ref, zv_ref, wl_ref, al_ref, vu_ref, vf_ref, vec_ref = refs[:8]
    else:
        zr_ref, zk_ref, zv_ref, wl_ref, al_ref, vec_ref = refs[:6]
    q_ref, y0_ref, gt_ref, ht_ref, bonus_ref = refs[-5:]
    pair = 2 * RWKV_HEAD
    c2 = 2 * c
    rows = nb * c
    vec = vec_ref[...]
    w0, a0, k_k, k_a, r_k, v0 = [vec[i:i + 1] for i in range(6)]
    r2 = zr_ref[0]
    kraw = zk_ref[0]
    v2 = zv_ref[0]
    x = w0 + wl_ref[0]
    softplus_neg = jnp.maximum(-x, 0.0) + jnp.log(1.0 + jnp.exp(-jnp.abs(x)))
    lw2 = -jnp.exp(-softplus_neg - 0.5)
    a_gate = _sigmoid(a0 + al_ref[0])
    if mix:
        v2 = v2 + (vf_ref[0] - v2) * _sigmoid(v0 + vu_ref[0])
    kk = kraw * k_k
    kk = kk * lax.rsqrt(jnp.maximum(_head_sums(kk * kk), 1e-24))
    k2 = kraw * (1.0 + (a_gate - 1.0) * k_a)
    bonus_ref[0] = (_head_sums(r2 * k2 * r_k) * v2).astype(bonus_ref.dtype)

    def chunks(x2):
        return x2.reshape(nb, c, pair)

    r, lw, k, v, a, b = [chunks(t) for t in (r2, lw2, k2, v2, -kk, kk * a_gate)]
    row = lax.broadcasted_iota(jnp.int32, (c, c), 0)
    col = lax.broadcasted_iota(jnp.int32, (c, c), 1)
    tril = jnp.broadcast_to((row >= col).astype(BF16), (nb, c, c))
    hi = lw.astype(BF16)
    rem = lw - hi.astype(F32)
    mid = rem.astype(BF16)
    lo = (rem - mid.astype(F32)).astype(BF16)
    cum = _bmm(tril, hi) + _bmm(tril, mid) + _bmm(tril, lo)
    last = cum[:, c - 1:c, :]
    inv = jnp.exp(-cum)
    tail = jnp.exp(last - cum)
    lane = lax.broadcasted_iota(jnp.int32, (1, 1, pair), 2)
    first = lane < RWKV_HEAD

    def stack(x):
        return jnp.concatenate([jnp.where(first, x, 0.0), jnp.where(first, 0.0, x)], axis=1)

    a_s = stack(a * jnp.exp(cum - lw))
    r_s = stack(r * jnp.exp(cum))
    b_s = stack(b * inv)
    k_s = stack(k * inv)
    v_s = stack(v)
    bh_s = stack(b * tail)
    kh_s = stack(k * tail)
    ar = jnp.concatenate([a_s, r_s], axis=1)
    pb = _bmm_nt(ar, b_s)
    pk = _bmm_nt(ar, k_s)
    row2 = lax.broadcasted_iota(jnp.int32, (c2, c2), 0) % c
    col2 = lax.broadcasted_iota(jnp.int32, (c2, c2), 1) % c
    strict = row2 > col2
    lower = row2 >= col2
    l_ab = jnp.where(strict, pb[:, :c2], 0.0)
    m_rb = jnp.where(lower, pb[:, c2:], 0.0)
    l_ak = jnp.where(strict, pk[:, :c2], 0.0)
    m_rk = jnp.where(lower, pk[:, c2:], 0.0)
    eye = (lax.broadcasted_iota(jnp.int32, (c2, c2), 0) == lax.broadcasted_iota(jnp.int32, (c2, c2), 1))
    eye = eye.astype(F32)
    tinv = eye + l_ab
    pw = l_ab
    for _ in range(int(np.log2(c)) - 1):
        pw = _bmm(pw, pw)
        tinv = tinv + _bmm(tinv, pw)
    tu = _bmm(tinv, jnp.concatenate([_bmm(l_ak, v_s), a_s], axis=2))
    u0_s = tu[:, :, :pair]
    ta_s = tu[:, :, pair:]
    mu = _bmm(m_rb, jnp.concatenate([ta_s, u0_s], axis=2))
    q_s = r_s + mu[:, :, :pair]
    y0_s = _bmm(m_rk, v_s) + mu[:, :, pair:]
    q_ref[0] = (q_s[:, :c] + q_s[:, c:]).reshape(nb * c, pair).astype(q_ref.dtype)
    y0_ref[0] = (y0_s[:, :c] + y0_s[:, c:]).reshape(nb * c, pair).astype(y0_ref.dtype)
    eye_b = jnp.broadcast_to(eye.astype(BF16), (nb, c2, c2))
    tr = _bmm_nt(eye_b, jnp.concatenate([bh_s, kh_s], axis=1))
    bh_t = tr[:, :, :c2]
    kh_t = tr[:, :, c2:]
    gh = _bmm(bh_t, jnp.concatenate([ta_s, u0_s], axis=2))
    gt_ref[0, :, 0] = (eye * jnp.exp(last) + gh[:, :, :pair]).astype(gt_ref.dtype)
    ht_ref[0, :, 0] = (gh[:, :, pair:] + _bmm(kh_t, v_s)).astype(ht_ref.dtype)


def _wkv_scan_kernel(q_ref, y0_ref, gt_ref, ht_ref, bonus_ref, g_ref, lnx_ref, o_ref, state_ref, *, npair):
    pair = 2 * RWKV_HEAD
    bsz, c, _ = q_ref.shape

    @pl.when(pl.program_id(0) == 0)
    def _():
        state_ref[...] = jnp.zeros_like(state_ref)

    sls = [slice(p * pair, (p + 1) * pair) for p in range(npair)]
    idx = [(bi, p) for bi in range(bsz) for p in range(npair)]
    st = [state_ref[bi, p].astype(BF16) for bi, p in idx]
    ys = [_dot(q_ref[bi, :, sls[p]], s_) + y0_ref[bi, :, sls[p]].astype(F32) for (bi, p), s_ in zip(idx, st)]
    new = [_dot(gt_ref[bi, 0, p], s_) + ht_ref[bi, 0, p].astype(F32) for (bi, p), s_ in zip(idx, st)]
    for (bi, p), s_ in zip(idx, new):
        state_ref[bi, p] = s_
    y = jnp.concatenate(ys, axis=0)
    dev = y - _head_sums(y, 1.0 / RWKV_HEAD)
    yn = dev * lax.rsqrt(_head_sums(dev * dev, 1.0 / RWKV_HEAD) + LNX_EPS)
    for n, (bi, p) in enumerate(idx):
        ln = yn[n * c:(n + 1) * c] * lnx_ref[0:1, sls[p]] + lnx_ref[1:2, sls[p]]
        out = (ln + bonus_ref[bi, :, sls[p]].astype(F32)) * g_ref[bi, :, sls[p]].astype(F32)
        o_ref[bi, :, sls[p]] = out.astype(o_ref.dtype)


def wkv7(zs, zs_first, wl, al, vu, vec, g, lnx):
    bsz, s, _ = zs.shape
    width = wl.shape[2]
    pair = 2 * RWKV_HEAD
    npair = width // pair
    c = min(RWKV_CHUNK, s)
    nb = min(RWKV_CHUNKS_PER_STEP, s // c)
    nch = s // c
    mix = vu is not None
    assert s % (nb * c) == 0 and width % pair == 0

    def col(off):
        return pl.BlockSpec((1, nb * c, pair), lambda bi, p, j: (bi, j, off + p))

    vec_spec = pl.BlockSpec((8, pair), lambda bi, p, j: (0, p))
    mat = pl.BlockSpec((1, nb, 1, pair, pair), lambda bi, p, j: (bi, j, p, 0, 0))
    in_specs = [col(0), col(npair), col(2 * npair), col(0), col(0)]
    args = [zs, zs, zs, wl, al]
    if mix:
        in_specs += [col(0), col(2 * npair)]
        args += [vu, zs_first]
    q, y0, gt, ht, bonus = pl.pallas_call(
        functools.partial(_wkv_chunk_kernel, nb=nb, c=c, mix=mix),
        grid=(bsz, npair, nch // nb),
        in_specs=in_specs + [vec_spec],
        out_specs=[col(0), col(0), mat, mat, col(0)],
        out_shape=[jax.ShapeDtypeStruct((bsz, s, width), BF16), jax.ShapeDtypeStruct((bsz, s, width), BF16),
                   jax.ShapeDtypeStruct((bsz, nch, npair, pair, pair), BF16),
                   jax.ShapeDtypeStruct((bsz, nch, npair, pair, pair), BF16),
                   jax.ShapeDtypeStruct((bsz, s, width), BF16)],
        compiler_params=_params("parallel", "parallel", "parallel"),
    )(*args, vec)
    row = pl.BlockSpec((bsz, c, width), lambda j: (0, j, 0))
    mats = pl.BlockSpec((bsz, 1, npair, pair, pair), lambda j: (0, j, 0, 0, 0))
    return pl.pallas_call(
        functools.partial(_wkv_scan_kernel, npair=npair),
        grid=(nch,),
        in_specs=[row, row, mats, mats, row, row, pl.BlockSpec((8, width), lambda j: (0, 0))],
        out_specs=row,
        out_shape=jax.ShapeDtypeStruct((bsz, s, width), BF16),
        scratch_shapes=[pltpu.VMEM((bsz, npair, pair, pair), F32)],
        compiler_params=_params("arbitrary"),
    )(q, y0, gt, ht, bonus, g, lnx)


def _moe_up_kernel(te_ref, tv_ref, a_ref, wg_ref, wu_ref, o_ref, wg_bf, wu_bf):
    i = pl.program_id(1)
    changed = jnp.logical_or(i == 0, te_ref[i] != te_ref[jnp.maximum(i - 1, 0)])

    @pl.when(changed)
    def _():
        wg_bf[...] = wg_ref[...].astype(BF16)
        wu_bf[...] = wu_ref[...].astype(BF16)

    @pl.when(tv_ref[i] > 0)
    def _():
        a = a_ref[...]
        g = _dot(a, wg_bf[...])
        u = _dot(a, wu_bf[...])
        o_ref[...] = (_silu(g) * u).astype(o_ref.dtype)

    @pl.when(tv_ref[i] == 0)
    def _():
        o_ref[...] = jnp.zeros_like(o_ref)


def moe_up(tile_expert, tile_valid, xs, w_gu, layer, *, tm, tn):
    r, k = xs.shape
    f = w_gu.shape[3] // 2
    nj = f // tn
    grid_spec = pltpu.PrefetchScalarGridSpec(
        num_scalar_prefetch=2,
        grid=(nj, r // tm),
        in_specs=[pl.BlockSpec((tm, k), lambda j, i, te, tv: (i, 0)),
                  pl.BlockSpec((None, None, k, tn), lambda j, i, te, tv: (layer, te[i], 0, j)),
                  pl.BlockSpec((None, None, k, tn), lambda j, i, te, tv: (layer, te[i], 0, j + nj))],
        out_specs=pl.BlockSpec((tm, tn), lambda j, i, te, tv: (i, j)),
        scratch_shapes=[pltpu.VMEM((k, tn), BF16), pltpu.VMEM((k, tn), BF16)],
    )
    return pl.pallas_call(
        _moe_up_kernel,
        grid_spec=grid_spec,
        out_shape=jax.ShapeDtypeStruct((r, f), BF16),
        compiler_params=_params("parallel", "arbitrary"),
    )(tile_expert, tile_valid, xs, w_gu, w_gu)


def _moe_down_kernel(te_ref, tv_ref, a_ref, w_ref, o_ref, w_bf):
    i = pl.program_id(1)
    changed = jnp.logical_or(i == 0, te_ref[i] != te_ref[jnp.maximum(i - 1, 0)])

    @pl.when(changed)
    def _():
        w_bf[...] = w_ref[...].astype(BF16)

    @pl.when(tv_ref[i] > 0)
    def _():
        o_ref[...] = _dot(a_ref[...], w_bf[...]).astype(o_ref.dtype)

    @pl.when(tv_ref[i] == 0)
    def _():
        o_ref[...] = jnp.zeros_like(o_ref)


def moe_down(tile_expert, tile_valid, act, w_down, layer, *, tm, tn):
    r, f = act.shape
    d = w_down.shape[3]
    grid_spec = pltpu.PrefetchScalarGridSpec(
        num_scalar_prefetch=2,
        grid=(d // tn, r // tm),
        in_specs=[pl.BlockSpec((tm, f), lambda j, i, te, tv: (i, 0)),
                  pl.BlockSpec((None, None, f, tn), lambda j, i, te, tv: (layer, te[i], 0, j))],
        out_specs=pl.BlockSpec((tm, tn), lambda j, i, te, tv: (i, j)),
        scratch_shapes=[pltpu.VMEM((f, tn), BF16)],
    )
    return pl.pallas_call(
        _moe_down_kernel,
        grid_spec=grid_spec,
        out_shape=jax.ShapeDtypeStruct((r, d), BF16),
        compiler_params=_params("parallel", "arbitrary"),
    )(tile_expert, tile_valid, act, w_down)


def _rope_tables(positions):
    inv_freq = ROPE_THETA ** (-jnp.arange(ROPE_HALF, dtype=F32) / ROPE_HALF)
    ang = positions.astype(F32)[:, :, None] * inv_freq
    cos = jnp.cos(ang)
    sin = jnp.sin(ang)
    b, s = positions.shape
    pad1 = jnp.ones((b, s, HEAD_DIM - ROPE_DIM), F32)
    pad0 = jnp.zeros((b, s, HEAD_DIM - ROPE_DIM), F32)
    return jnp.concatenate([cos, cos, pad1], axis=-1), jnp.concatenate([-sin, sin, pad0], axis=-1)


def _pad_cols(w, n):
    return w if w.shape[-1] == n else jnp.pad(w, ((0, 0), (0, n - w.shape[-1])))


def nsa_branch(q_a, kv_a, cosf, sinf, cmp_pe, cmp_w1, cmp_w2):
    b, s, _ = q_a.shape
    nc = s // CMP_STRIDE - 1
    ncp = -(-nc // LANE) * LANE
    kvw = NSA_KV_HEADS * HEAD_DIM

    def blocks(t):
        ch = t.reshape(b, s // CMP_STRIDE, CMP_STRIDE, NSA_KV_HEADS, HEAD_DIM)
        blk = jnp.concatenate([ch[:, :-1], ch[:, 1:]], axis=2)
        return blk.transpose(0, 1, 3, 2, 4).reshape(b * nc * NSA_KV_HEADS, CMP_BLOCK * HEAD_DIM)

    rows = b * nc * NSA_KV_HEADS
    rows_p = -(-rows // 512) * 512 if rows > 512 else -(-rows // 8) * 8
    flat = jnp.stack([blocks(kv_a[..., :kvw]), blocks(kv_a[..., kvw:2 * kvw])]).astype(F32)
    flat = jnp.pad(flat, ((0, 0), (0, rows_p - rows), (0, 0)))
    comp = nsa_compress(flat, cmp_pe.reshape(2, 1, CMP_BLOCK * HEAD_DIM), cmp_w1.astype(BF16), cmp_w2.astype(BF16))
    comp = comp[:, :rows].reshape(2, b, nc, NSA_KV_HEADS, HEAD_DIM).transpose(0, 1, 3, 2, 4)
    comp = jnp.pad(comp, ((0, 0), (0, 0), (0, 0), (0, ncp - nc), (0, 0)))
    kc = comp[0]
    vct = comp[1].transpose(0, 1, 3, 2)
    tq = min(256, s)
    o_cmp, sel = cmp_select(q_a, kc, vct, nc=nc, tq=tq)
    o_slc = slc_attention(q_a, kv_a, sel, cosf, sinf, k_col=4, v_col=6, tq=min(512, s))
    o_win = band_attention(q_a, kv_a, kv_a, nheads=NSA_GROUP, kv_heads=1, q_col=lambda c: c,
                           k_col=lambda c: 8 + c, v_col=lambda c: 10 + c, o_cols=NSA_KV_HEADS,
                           ncol=NSA_KV_HEADS, max_dist=WIN_SIZE - 1, tq=tq, rope=(cosf, sinf))
    return o_cmp, o_slc, o_win


def rwkv_branch(zs, zs_first, vec, w_up, a_up, g_up, v_res):
    b, s, _ = zs.shape
    t = b * s
    w_ = RWKV_WIDTH
    o = 3 * w_
    zw, za = zs[..., o:o + LORA_W], zs[..., o + LORA_W:o + LORA_W + LORA_A]
    zg = zs[..., o + LORA_W + LORA_A:o + LORA_W + LORA_A + LORA_G]
    w0, a0, k_k, k_a, r_k, lnx_g, lnx_b = [vec[i] for i in range(7)]

    def lora(xin, wmat, out_dtype=F32):
        return matmul(xin.reshape(t, -1).astype(BF16), wmat.astype(BF16), out_dtype=out_dtype, tm=1024,
                      tn=wmat.shape[1]).reshape(b, s, -1)

    wl = lora(jnp.tanh(zw), w_up)
    al = lora(za, a_up)
    g = lora(jax.nn.sigmoid(zg), g_up, BF16)
    zero = jnp.zeros_like(w0)
    if v_res is None:
        vu, v0 = None, zero
    else:
        v0, v_down, v_up = v_res
        vd = matmul(zs.reshape(t, -1), _pad_cols(v_down, LANE).astype(BF16), out_dtype=BF16, tm=1024, tn=LANE,
                    a_col=2)
        vu = matmul(vd, jnp.pad(v_up, ((0, LANE - v_up.shape[0]), (0, 0))).astype(BF16), out_dtype=F32, tm=1024,
                    tn=512).reshape(b, s, w_)
    vecs = jnp.stack([w0, a0, k_k, k_a, r_k, v0, zero, zero])
    lnx = jnp.stack([lnx_g, lnx_b] + [zero] * 6)
    return wkv7(zs, zs_first, wl, al, vu, vecs, g, lnx)


def dilated_branch(qkvs):
    b, s, width = qkvs[0].shape
    outs, lses = [], []
    for gi, (win, dil) in enumerate(DIL_PATTERNS):
        view = qkvs[gi].reshape(b, s // dil, dil * width)
        o, lse = band_attention(
            view, view, view, nheads=DIL_HPG, kv_heads=DIL_HPG,
            q_col=lambda c: 3 * c, k_col=lambda c: 3 * c + 1, v_col=lambda c: 3 * c + 2,
            o_cols=dil, ncol=dil, max_dist=win // dil, tq=256, with_lse=True, out_dtype=F32)
        outs.append(o.reshape(b, s, DIL_HPG, HEAD_DIM))
        lses.append(lse.reshape(b, s, LANE)[..., :DIL_HPG])
    alpha = jax.nn.softmax(jnp.stack(lses, axis=-1), axis=-1)
    o = sum(alpha[..., gi:gi + 1] * outs[gi] for gi in range(len(DIL_PATTERNS)))
    return o.reshape(b, s, DIL_OUT).astype(BF16)


def moe_ffn(h, router_w, router_b, w_gu, w_down, layer, *, tm):
    t, d = h.shape
    logits = matmul(h, _pad_cols(router_w, LANE).astype(BF16), out_dtype=F32, tm=1024, tn=LANE)[:, :N_EXPERTS]
    logits = logits + router_b
    top_v, top_i = lax.top_k(logits, TOP_K)
    wts = jax.nn.softmax(top_v, axis=-1)
    flat_e = top_i.reshape(-1)
    onehot = (flat_e[:, None] == jnp.arange(N_EXPERTS)[None, :]).astype(jnp.int32)
    rank = jnp.take_along_axis(jnp.cumsum(onehot, axis=0), flat_e[:, None], axis=1)[:, 0] - 1
    counts = jnp.sum(onehot, axis=0)
    tiles_per = (counts + tm - 1) // tm
    tile_end = jnp.cumsum(tiles_per)
    group_start = (tile_end - tiles_per) * tm
    dest = group_start[flat_e] + rank
    ntiles = (TOP_K * t) // tm + N_EXPERTS
    rows = ntiles * tm
    row_token = jnp.zeros((rows,), jnp.int32).at[dest].set(jnp.arange(TOP_K * t, dtype=jnp.int32) // TOP_K)
    tile_ids = jnp.arange(ntiles, dtype=jnp.int32)
    tile_valid = (tile_ids < tile_end[-1]).astype(jnp.int32)
    tile_expert = jnp.minimum(jnp.searchsorted(tile_end, tile_ids, side="right"), N_EXPERTS - 1).astype(jnp.int32)
    xs = jnp.take(h, row_token, axis=0, mode="clip")
    act = moe_up(tile_expert, tile_valid, xs, w_gu, layer, tm=tm, tn=min(1024, w_gu.shape[3] // 2))
    out = moe_down(tile_expert, tile_valid, act, w_down, layer, tm=tm, tn=512)
    dest = dest.reshape(t, TOP_K)
    ya = jnp.take(out, dest[:, 0], axis=0, mode="clip")
    yb = jnp.take(out, dest[:, 1], axis=0, mode="clip")
    return ya, yb, jnp.pad(wts, ((0, 0), (0, LANE - TOP_K)))


def kernel(x, c, positions, ada_w, ada_b, norm_g, w_in, cmp_pe, cmp_w1, cmp_w2, rwkv_mu, rwkv_vec, w_up, a_up, g_up, v_res0, v_res_down, v_res_up, w_br_a, w_br_b, w_br_c, w_out, ffn_gu, ffn_down, router_w, router_b, moe_gu, moe_down):
    b, s, d = x.shape
    depth = ada_w.shape[0]
    t = b * s
    tm_row = min(512, s)
    cosf, sinf = _rope_tables(positions)
    cos_t = cosf.reshape(t, LANE)
    sin_t = sinf.reshape(t, LANE)

    cond = jnp.pad(jax.nn.silu(c), ((0, 8 - b % 8 if b % 8 else 0), (0, 0))).astype(BF16)
    mods = []
    for l in range(depth):
        mod = matmul(cond, ada_w, out_dtype=F32, tm=cond.shape[0], tn=512, layer=l)[:b] + ada_b[l]
        mods.append(mod.reshape(b, 6, 1, d))

    def mod_of(l, i):
        return mods[l][:, i]

    q_cols = NSA_HEADS * HEAD_DIM
    kv_cols = 6 * NSA_KV_HEADS * HEAD_DIM
    gate_cols = 3 * NSA_HEADS
    rwkv_cols = 3 * RWKV_WIDTH + LORA_W + LORA_A + LORA_G
    dil_cols = 3 * DIL_HEADS * HEAD_DIM
    offs = np.cumsum([0, q_cols, kv_cols, gate_cols, rwkv_cols, dil_cols, 3 * d]).tolist()
    rwkv_pad = -(-rwkv_cols // 512) * 512

    xf = x.reshape(t, d)
    h = norm_mod(xf, norm_g[0, 0][None], mod_of(0, 1), mod_of(0, 0), seq=s, tm=tm_row)
    zs_first = None
    for l in range(depth):
        wl = w_in[l]
        seg = lambda i: wl[:, offs[i]:offs[i + 1]]
        q_a = matmul(h, seg(0).astype(BF16), out_dtype=BF16, tm=1024, tn=q_cols)
        kv_a = matmul(h, seg(1).astype(BF16), out_dtype=BF16, tm=1024, tn=kv_cols, rope=(cos_t, sin_t, (4, 5, 8, 9)))
        gate_a = matmul(h, _pad_cols(seg(2), LANE).astype(BF16), out_dtype=F32, tm=1024, tn=LANE, act="sigmoid")
        zs = matmul_token_shift(h, _pad_cols(seg(3), rwkv_pad).astype(BF16), _pad_cols(rwkv_mu[l][None], rwkv_pad),
                                seq=s, tm=1024, tn=rwkv_pad // 4).reshape(b, s, -1)
        zs_first = zs if l == 0 else zs_first
        wc = seg(4).reshape(d, 3, len(DIL_PATTERNS), DIL_OUT)
        qkv_c = [matmul(h, wc[:, :, gi].reshape(d, 3 * DIL_OUT).astype(BF16), out_dtype=BF16, tm=1024,
                        tn=3 * DIL_OUT, rope=(cos_t, sin_t, tuple(range(2 * DIL_HPG)))).reshape(b, s, -1)
                 for gi in range(len(DIL_PATTERNS))]
        mg = matmul(h, seg(5).astype(BF16), out_dtype=BF16, tm=1024, tn=1024, act="sigmoid")

        o_cmp, o_slc, o_win = nsa_branch(q_a.reshape(b, s, -1), kv_a.reshape(b, s, -1), cosf, sinf,
                                         cmp_pe[l], cmp_w1[l], cmp_w2[l])
        v_res = None if l == 0 else (v_res0[l - 1], v_res_down[l - 1], v_res_up[l - 1])
        y_b = rwkv_branch(zs, zs_first, rwkv_vec[l], w_up[l], a_up[l], g_up[l], v_res)
        y_c = dilated_branch(qkv_c)
        merged = branch_merge(o_cmp.reshape(t, -1), o_slc.reshape(t, -1), o_win.reshape(t, -1), gate_a,
                              y_b.reshape(t, -1), y_c.reshape(t, -1), w_br_a[l].astype(BF16),
                              w_br_b[l].astype(BF16), w_br_c[l].astype(BF16), mg, tm=1024, tn=512)
        xf, h = matmul_close(merged, w_out[l].astype(BF16), xf, norm_g[l, 1][None], mod_of(l, 2),
                             (norm_g[l, 2][None], mod_of(l, 4), mod_of(l, 3)), seq=s, tm=tm_row, tk=d)

        nxt = None if l == depth - 1 else (norm_g[l + 1, 0][None], mod_of(l + 1, 1), mod_of(l + 1, 0))
        if l % 2 == 0:
            act = swiglu_up(h, ffn_gu[l // 2].astype(BF16), tm=1024, tn=ffn_gu.shape[2] // 8)
            xf, h = matmul_close(act, ffn_down[l // 2].astype(BF16), xf, norm_g[l, 3][None], mod_of(l, 5), nxt,
                                 seq=s, tm=tm_row, tk=act.shape[1] // 4)
        else:
            ya, yb, wts = moe_ffn(h, router_w[l // 2], router_b[l // 2], moe_gu, moe_down, l // 2, tm=512)
            xf, h = close_sublayer(ya, yb, wts, xf, norm_g[l, 3][None], mod_of(l, 5), nxt, seq=s, tm=tm_row)
    return xf.reshape(b, s, d)
```

```python
import functools

import numpy as np
import jax
import jax.numpy as jnp
from jax import lax
from jax.experimental import pallas as pl
from jax.experimental.pallas import tpu as pltpu

F32 = jnp.float32
BF16 = jnp.bfloat16

HEAD_DIM = 128
ROPE_DIM = HEAD_DIM // 4
ROPE_HALF = ROPE_DIM // 2
ROPE_THETA = 500000.0
NORM_EPS = 1e-6

NSA_HEADS = 8
NSA_KV_HEADS = 2
NSA_GROUP = NSA_HEADS // NSA_KV_HEADS
CMP_BLOCK = 32
CMP_STRIDE = 16
SLC_BLOCK = 64
SLC_TOPN = 16
WIN_SIZE = 512
FORCE_BONUS = 1e4
SLC_PICK_BIAS = 8192.0

RWKV_HEADS = 16
RWKV_HEAD = 64
RWKV_WIDTH = RWKV_HEADS * RWKV_HEAD
LORA_W = 96
LORA_A = 96
LORA_G = 256
LNX_EPS = 64e-5
RWKV_CHUNK = 64
RWKV_CHUNKS_PER_STEP = 8

DIL_PATTERNS = ((128, 1), (512, 4), (2048, 16))
DIL_HPG = 4
DIL_HEADS = DIL_HPG * len(DIL_PATTERNS)
DIL_OUT = DIL_HPG * HEAD_DIM

N_EXPERTS = 8
TOP_K = 2

LANE = 128
VMEM_LIMIT_BYTES = 56 * 1024 * 1024
MASKED = -1e30


def _params(*sem):
    return pltpu.CompilerParams(dimension_semantics=sem, vmem_limit_bytes=VMEM_LIMIT_BYTES)


def _sigmoid(x):
    return 1.0 / (1.0 + jnp.exp(-x))


def _silu(x):
    return x * _sigmoid(x)


def _dot(a, b):
    return jnp.dot(a, b, preferred_element_type=F32)


def _dot_nt(a, b):
    return lax.dot_general(a, b, (((1,), (1,)), ((), ())), preferred_element_type=F32)


def _dot_tn(a, b):
    return lax.dot_general(a, b, (((0,), (0,)), ((), ())), preferred_element_type=F32)


def _rope(t, cosf, sinf):
    lane = lax.broadcasted_iota(jnp.int32, t.shape, 1)
    swapped = jnp.where(lane < ROPE_HALF, pltpu.roll(t, LANE - ROPE_HALF, 1), pltpu.roll(t, ROPE_HALF, 1))
    return t * cosf + swapped * sinf


def _rms(y):
    return y * lax.rsqrt(jnp.mean(y * y, axis=-1, keepdims=True) + NORM_EPS)


def _matmul_kernel(*refs, act, rope_chunks, ntiles, permute=False):
    if permute:
        a_ref, w_ref, cos_ref, sin_ref, perm_ref, o_ref = refs
    elif rope_chunks:
        a_ref, w_ref, cos_ref, sin_ref, o_ref = refs
    else:
        a_ref, w_ref, o_ref = refs
    acc = _dot(a_ref[...].astype(BF16), w_ref[...].astype(BF16))
    if act == "sigmoid":
        acc = _sigmoid(acc)
    elif act == "tanh":
        acc = jnp.tanh(acc)
    if not rope_chunks:
        o_ref[...] = acc.astype(o_ref.dtype)
        return
    per_tile = acc.shape[1] // LANE

    def store(tile):
        for c in range(per_tile):
            sl = slice(c * LANE, (c + 1) * LANE)
            if tile is not None and tile * per_tile + c in rope_chunks:
                o_ref[:, sl] = _rope(acc[:, sl], cos_ref[...], sin_ref[...]).astype(o_ref.dtype)
            else:
                o_ref[:, sl] = acc[:, sl].astype(o_ref.dtype)

    if ntiles == 1:
        store(0)
        if permute:
            o_ref[...] = _dot(perm_ref[...], o_ref[...]).astype(o_ref.dtype)
        return
    assert not permute
    j = pl.program_id(1)
    tiles = sorted({c // per_tile for c in rope_chunks})
    for t in tiles:
        pl.when(j == t)(functools.partial(store, t))
    pl.when(functools.reduce(jnp.logical_and, [j != t for t in tiles]))(functools.partial(store, None))


def _matmul_shift_kernel(a_ref, ap_ref, w_ref, mu_ref, o_ref, *, seq, tm):
    w = w_ref[...]
    z = _dot(a_ref[...], w)
    zp = _dot(ap_ref[...], w)
    at_start = (pl.program_id(0) * tm) % seq == 0
    last = zp.shape[0] - 1
    prev_row = jnp.where(at_start, 0.0, zp[last:last + 1, :])
    rowid = lax.broadcasted_iota(jnp.int32, z.shape, 0)
    shifted = jnp.where(rowid == 0, prev_row, pltpu.roll(z, 1, 0))
    o_ref[...] = (z + (shifted - z) * mu_ref[...]).astype(o_ref.dtype)


BF16_SUBLANES = 16


def matmul_token_shift(a, w, mu, *, seq, tm, tn):
    m, k = a.shape
    n = w.shape[1]
    assert m % tm == 0 and n % tn == 0 and seq % tm == 0
    per = tm // BF16_SUBLANES
    return pl.pallas_call(
        functools.partial(_matmul_shift_kernel, seq=seq, tm=tm),
        grid=(m // tm, n // tn),
        in_specs=[pl.BlockSpec((tm, k), lambda i, j: (i, 0)),
                  pl.BlockSpec((BF16_SUBLANES, k), lambda i, j: (jnp.maximum(i * per - 1, 0), 0)),
                  pl.BlockSpec((k, tn), lambda i, j: (0, j)),
                  pl.BlockSpec((1, tn), lambda i, j: (0, j))],
        out_specs=pl.BlockSpec((tm, tn), lambda i, j: (i, j)),
        out_shape=jax.ShapeDtypeStruct((m, n), F32),
        compiler_params=_params("parallel", "parallel"),
    )(a, a, w, mu)


def matmul(a, w, *, out_dtype, tm, tn, act=None, rope=None, layer=None, a_col=0, row_perm=None):
    m = a.shape[0]
    k = w.shape[-2]
    n = w.shape[-1]
    tm = min(tm, m)
    assert m % tm == 0 and n % tn == 0, (m, n, tm, tn)
    if layer is None:
        w_spec = pl.BlockSpec((k, tn), lambda i, j: (0, j))
    else:
        w_spec = pl.BlockSpec((None, k, tn), lambda i, j: (layer, 0, j))
    in_specs = [pl.BlockSpec((tm, k), lambda i, j: (i, a_col)), w_spec]
    args = [a, w]
    tiles = ()
    if rope is not None:
        cosf, sinf, tiles = rope
        in_specs += [pl.BlockSpec((tm, LANE), lambda i, j: (i, 0))] * 2
        args += [cosf, sinf]
    if row_perm is not None:
        assert row_perm.shape == (tm, tm) and rope is not None
        in_specs.append(pl.BlockSpec((tm, tm), lambda i, j: (0, 0)))
        args.append(row_perm)
    return pl.pallas_call(
        functools.partial(_matmul_kernel, act=act, rope_chunks=frozenset(tiles), ntiles=n // tn,
                          permute=row_perm is not None),
        grid=(m // tm, n // tn),
        in_specs=in_specs,
        out_specs=pl.BlockSpec((tm, tn), lambda i, j: (i, j)),
        out_shape=jax.ShapeDtypeStruct((m, n), out_dtype),
        compiler_params=_params("parallel", "parallel"),
    )(*args)


def _swiglu_up_kernel(a_ref, wg_ref, wu_ref, o_ref):
    a = a_ref[...]
    g = _dot(a, wg_ref[...])
    u = _dot(a, wu_ref[...])
    o_ref[...] = (_silu(g) * u).astype(o_ref.dtype)


def swiglu_up(a, w_gu, *, tm, tn):
    m, k = a.shape
    f = w_gu.shape[1] // 2
    assert m % tm == 0 and f % tn == 0
    nj = f // tn
    return pl.pallas_call(
        _swiglu_up_kernel,
        grid=(m // tm, nj),
        in_specs=[pl.BlockSpec((tm, k), lambda i, j: (i, 0)),
                  pl.BlockSpec((k, tn), lambda i, j: (0, j)),
                  pl.BlockSpec((k, tn), lambda i, j: (0, j + nj))],
        out_specs=pl.BlockSpec((tm, tn), lambda i, j: (i, j)),
        out_shape=jax.ShapeDtypeStruct((m, f), BF16),
        compiler_params=_params("parallel", "parallel"),
    )(a, w_gu, w_gu)


def _close_sublayer(x, y, gpost, gt, nxt):
    xn = x + gt * (_rms(y) * gpost)
    if nxt is None:
        return xn, None
    gnext, sc, sh = nxt
    return xn, (_rms(xn) * gnext) * (1.0 + sc) + sh


def _matmul_close_kernel(*refs, with_next):
    if with_next:
        a_ref, w_ref, x_ref, gp_ref, gt_ref, gn_ref, sc_ref, sh_ref, xo_ref, ho_ref, acc_ref = refs
    else:
        a_ref, w_ref, x_ref, gp_ref, gt_ref, xo_ref, acc_ref = refs
    kk = pl.program_id(1)

    @pl.when(kk == 0)
    def _():
        acc_ref[...] = jnp.zeros_like(acc_ref)

    acc_ref[...] += _dot(a_ref[...], w_ref[...])

    @pl.when(kk == pl.num_programs(1) - 1)
    def _():
        nxt = (gn_ref[...], sc_ref[0], sh_ref[0]) if with_next else None
        xn, h = _close_sublayer(x_ref[...], acc_ref[...], gp_ref[...], gt_ref[0], nxt)
        xo_ref[...] = xn
        if with_next:
            ho_ref[...] = h.astype(ho_ref.dtype)


def _close_specs(tm, d, seq, with_next, nidx):
    def row(i, *_):
        return (i, 0)

    def const(*_):
        return (0, 0)

    def batch(i, *_):
        return ((i * tm) // seq, 0, 0)

    specs = [pl.BlockSpec((tm, d), row), pl.BlockSpec((1, d), const), pl.BlockSpec((1, 1, d), batch)]
    if with_next:
        specs += [pl.BlockSpec((1, d), const), pl.BlockSpec((1, 1, d), batch), pl.BlockSpec((1, 1, d), batch)]
    return specs


def matmul_close(a, w, x, gpost, gt, nxt, *, seq, tm, tk):
    m, k = a.shape
    d = w.shape[1]
    assert m % tm == 0 and k % tk == 0 and seq % tm == 0
    with_next = nxt is not None
    args = [a, w, x, gpost, gt] + (list(nxt) if with_next else [])
    in_specs = [pl.BlockSpec((tm, tk), lambda i, kk: (i, kk)), pl.BlockSpec((tk, d), lambda i, kk: (kk, 0))]
    in_specs += _close_specs(tm, d, seq, with_next, 2)
    out_shape = [jax.ShapeDtypeStruct((m, d), F32)]
    out_specs = [pl.BlockSpec((tm, d), lambda i, kk: (i, 0))]
    if with_next:
        out_shape.append(jax.ShapeDtypeStruct((m, d), BF16))
        out_specs.append(pl.BlockSpec((tm, d), lambda i, kk: (i, 0)))
    res = pl.pallas_call(
        functools.partial(_matmul_close_kernel, with_next=with_next),
        grid=(m // tm, k // tk),
        in_specs=in_specs,
        out_specs=out_specs,
        out_shape=out_shape,
        scratch_shapes=[pltpu.VMEM((tm, d), F32)],
        compiler_params=_params("parallel", "arbitrary"),
    )(*args)
    return (res[0], res[1]) if with_next else (res[0], None)


def _close_kernel(*refs, with_next):
    if with_next:
        ya_ref, yb_ref, wt_ref, x_ref, gp_ref, gt_ref, gn_ref, sc_ref, sh_ref, xo_ref, ho_ref = refs
    else:
        ya_ref, yb_ref, wt_ref, x_ref, gp_ref, gt_ref, xo_ref = refs
    nxt = (gn_ref[...], sc_ref[0], sh_ref[0]) if with_next else None
    wt = wt_ref[...]
    y = wt[:, 0:1] * ya_ref[...].astype(F32) + wt[:, 1:2] * yb_ref[...].astype(F32)
    xn, h = _close_sublayer(x_ref[...], y, gp_ref[...], gt_ref[0], nxt)
    xo_ref[...] = xn
    if with_next:
        ho_ref[...] = h.astype(ho_ref.dtype)


def close_sublayer(ya, yb, wt, x, gpost, gt, nxt, *, seq, tm):
    m, d = x.shape
    with_next = nxt is not None
    args = [ya, yb, wt, x, gpost, gt] + (list(nxt) if with_next else [])
    in_specs = [pl.BlockSpec((tm, d), lambda i: (i, 0)), pl.BlockSpec((tm, d), lambda i: (i, 0)),
                pl.BlockSpec((tm, LANE), lambda i: (i, 0))] + _close_specs(tm, d, seq, with_next, 1)
    out_shape = [jax.ShapeDtypeStruct((m, d), F32)]
    out_specs = [pl.BlockSpec((tm, d), lambda i: (i, 0))]
    if with_next:
        out_shape.append(jax.ShapeDtypeStruct((m, d), BF16))
        out_specs.append(pl.BlockSpec((tm, d), lambda i: (i, 0)))
    res = pl.pallas_call(
        functools.partial(_close_kernel, with_next=with_next),
        grid=(m // tm,),
        in_specs=in_specs,
        out_specs=out_specs,
        out_shape=out_shape,
        compiler_params=_params("parallel"),
    )(*args)
    return (res[0], res[1]) if with_next else (res[0], None)


def _norm_mod_kernel(x_ref, g_ref, sc_ref, sh_ref, o_ref):
    o_ref[...] = ((_rms(x_ref[...]) * g_ref[...]) * (1.0 + sc_ref[0]) + sh_ref[0]).astype(o_ref.dtype)


def norm_mod(x, g, sc, sh, *, seq, tm):
    m, d = x.shape
    batch = lambda i: ((i * tm) // seq, 0, 0)
    return pl.pallas_call(
        _norm_mod_kernel,
        grid=(m // tm,),
        in_specs=[pl.BlockSpec((tm, d), lambda i: (i, 0)), pl.BlockSpec((1, d), lambda i: (0, 0)),
                  pl.BlockSpec((1, 1, d), batch), pl.BlockSpec((1, 1, d), batch)],
        out_specs=pl.BlockSpec((tm, d), lambda i: (i, 0)),
        out_shape=jax.ShapeDtypeStruct((m, d), BF16),
        compiler_params=_params("parallel"),
    )(x, g, sc, sh)


def _branch_merge_kernel(oc_ref, os_ref, ow_ref, ng_ref, yb_ref, yc_ref, wa_ref, wb_ref, wc_ref, ga_ref, gb_ref,
                         gc_ref, o_ref, ya_ref):
    @pl.when(pl.program_id(1) == 0)
    def _():
        ng = ng_ref[...]
        for hd in range(NSA_HEADS):
            sl = slice(hd * HEAD_DIM, (hd + 1) * HEAD_DIM)
            ya = ng[:, 3 * hd:3 * hd + 1] * oc_ref[:, sl].astype(F32)
            ya += ng[:, 3 * hd + 1:3 * hd + 2] * os_ref[:, sl].astype(F32)
            ya += ng[:, 3 * hd + 2:3 * hd + 3] * ow_ref[:, sl].astype(F32)
            ya_ref[:, sl] = ya.astype(ya_ref.dtype)

    acc = ga_ref[...].astype(F32) * _dot(ya_ref[...], wa_ref[...])
    acc += gb_ref[...].astype(F32) * _dot(yb_ref[...], wb_ref[...])
    acc += gc_ref[...].astype(F32) * _dot(yc_ref[...], wc_ref[...])
    o_ref[...] = acc.astype(o_ref.dtype)


def branch_merge(o_cmp, o_slc, o_win, nsa_gate, yb, yc, wa, wb, wc, gates, *, tm, tn):
    m = yb.shape[0]
    d = wa.shape[1]
    nj = d // tn
    row = lambda width: pl.BlockSpec((tm, width), lambda i, j: (i, 0))
    wsp = lambda kdim: pl.BlockSpec((kdim, tn), lambda i, j: (0, j))
    gsp = lambda off: pl.BlockSpec((tm, tn), lambda i, j: (i, j + off * nj))
    wa_rows = wa.shape[0]
    return pl.pallas_call(
        _branch_merge_kernel,
        grid=(m // tm, nj),
        in_specs=[row(wa_rows), row(wa_rows), row(wa_rows), row(LANE), row(yb.shape[1]), row(yc.shape[1]),
                  wsp(wa_rows), wsp(wb.shape[0]), wsp(wc.shape[0]), gsp(0), gsp(1), gsp(2)],
        out_specs=pl.BlockSpec((tm, tn), lambda i, j: (i, j)),
        out_shape=jax.ShapeDtypeStruct((m, d), BF16),
        scratch_shapes=[pltpu.VMEM((tm, wa_rows), BF16)],
        compiler_params=_params("parallel", "arbitrary"),
    )(o_cmp, o_slc, o_win, nsa_gate, yb, yc, wa, wb, wc, gates, gates, gates)


def _compress_kernel(f_ref, pe_ref, w1_ref, w2_ref, o_ref):
    a = (f_ref[0] + pe_ref[0]).astype(BF16)
    hmid = _silu(_dot(a, w1_ref[0]))
    o_ref[0] = _dot(hmid.astype(BF16), w2_ref[0]).astype(o_ref.dtype)


def nsa_compress(flat, pe, w1, w2):
    two, r, kdim = flat.shape
    tm = min(r, 512)
    assert r % tm == 0
    return pl.pallas_call(
        _compress_kernel,
        grid=(two, r // tm),
        in_specs=[pl.BlockSpec((1, tm, kdim), lambda t, i: (t, i, 0)),
                  pl.BlockSpec((1, 1, kdim), lambda t, i: (t, 0, 0)),
                  pl.BlockSpec((1, kdim, HEAD_DIM), lambda t, i: (t, 0, 0)),
                  pl.BlockSpec((1, HEAD_DIM, HEAD_DIM), lambda t, i: (t, 0, 0))],
        out_specs=pl.BlockSpec((1, tm, HEAD_DIM), lambda t, i: (t, i, 0)),
        out_shape=jax.ShapeDtypeStruct((two, r, HEAD_DIM), BF16),
        compiler_params=_params("parallel", "parallel"),
    )(flat, pe, w1, w2)


def _cmp_select_kernel(q_ref, kc_ref, vct_ref, o_ref, sel_ref, *, tq, nc, ncp, ns, nsp, scale):
    qi = pl.program_id(2)
    kc = kc_ref[0, 0]
    vct = vct_ref[0, 0]
    spos = qi * tq + lax.broadcasted_iota(jnp.int32, (1, tq), 1)
    cidx = lax.broadcasted_iota(jnp.int32, (ncp, 1), 0)
    valid = jnp.logical_and(cidx * CMP_STRIDE + (CMP_BLOCK - 1) <= spos, cidx < nc)
    psum = jnp.zeros((ncp, tq), F32)
    for g in range(NSA_GROUP):
        qg = q_ref[0, :, g * HEAD_DIM:(g + 1) * HEAD_DIM]
        st = _dot_nt(kc, qg) * scale
        st = jnp.where(valid, st, MASKED)
        mx = jnp.max(st, axis=0, keepdims=True)
        e = jnp.where(valid, jnp.exp(st - mx), 0.0)
        den = jnp.sum(e, axis=0, keepdims=True)
        p = e / jnp.where(den > 0, den, 1.0)
        psum = psum + p
        og_t = _dot(vct, p.astype(BF16))
        o_ref[0, :, g * HEAD_DIM:(g + 1) * HEAD_DIM] = og_t.T.astype(o_ref.dtype)
    jrow = lax.broadcasted_iota(jnp.int32, (ns, ncp), 0)
    ccol = lax.broadcasted_iota(jnp.int32, (ns, ncp), 1)
    c0 = ccol * CMP_STRIDE
    j0 = jrow * SLC_BLOCK
    cover_t = jnp.logical_and(c0 < j0 + SLC_BLOCK, c0 + CMP_BLOCK > j0).astype(F32)
    hi = psum.astype(BF16)
    rem = psum - hi.astype(F32)
    mid = rem.astype(BF16)
    lo = (rem - mid.astype(F32)).astype(BF16)
    imp3 = _dot(cover_t.astype(BF16), jnp.concatenate([hi, mid, lo], axis=1))
    imp = imp3[:, :tq] + imp3[:, tq:2 * tq] + imp3[:, 2 * tq:]
    j = lax.broadcasted_iota(jnp.int32, (ns, 1), 0).astype(F32)
    cur = (spos // SLC_BLOCK).astype(F32)
    forced = jnp.logical_or(jnp.logical_or(j == 0, j == cur), j == cur - 1)
    score = jnp.where(j <= cur, imp + FORCE_BONUS * forced.astype(F32), -jnp.inf)
    sel = jnp.zeros((ns, tq), F32)
    for _ in range(SLC_TOPN):
        mx = jnp.max(score, axis=0, keepdims=True)
        first = jnp.min(jnp.where(score == mx, j, float(ns)), axis=0, keepdims=True)
        pick = j == first
        sel = jnp.where(pick, 1.0, sel)
        score = jnp.where(pick, -jnp.inf, score)
    if ns < nsp:
        sel = jnp.concatenate([sel, jnp.zeros((nsp - ns, tq), F32)], axis=0)
    sel_ref[0, 0] = sel.T.astype(sel_ref.dtype)


def cmp_select(q, kc, vct, *, nc, tq):
    b, s, _ = q.shape
    ncp = kc.shape[2]
    nsp = LANE
    assert s % tq == 0 and s // SLC_BLOCK <= nsp
    gw = NSA_GROUP * HEAD_DIM
    return pl.pallas_call(
        functools.partial(_cmp_select_kernel, tq=tq, nc=nc, ncp=ncp, ns=-(-(s // SLC_BLOCK) // 8) * 8, nsp=nsp,
                          scale=HEAD_DIM ** -0.5),
        grid=(b, NSA_KV_HEADS, s // tq),
        in_specs=[pl.BlockSpec((1, tq, gw), lambda bi, h, i: (bi, i, h)),
                  pl.BlockSpec((1, 1, ncp, HEAD_DIM), lambda bi, h, i: (bi, h, 0, 0)),
                  pl.BlockSpec((1, 1, HEAD_DIM, ncp), lambda bi, h, i: (bi, h, 0, 0))],
        out_specs=[pl.BlockSpec((1, tq, gw), lambda bi, h, i: (bi, i, h)),
                   pl.BlockSpec((1, 1, tq, nsp), lambda bi, h, i: (bi, h, i, 0))],
        out_shape=[jax.ShapeDtypeStruct((b, s, NSA_HEADS * HEAD_DIM), BF16),
                   jax.ShapeDtypeStruct((b, NSA_KV_HEADS, s, nsp), BF16)],
        compiler_params=_params("parallel", "parallel", "parallel"),
    )(q, kc, vct)


def _slc_kernel(qi_ref, ki_ref, q_ref, k_ref, v_ref, sel_ref, cos_ref, sin_ref, o_ref, qaug_ref, m_ref, acc_ref,
                *, tq, scale):
    step = pl.program_id(2)
    qi = qi_ref[step]
    ki = ki_ref[step]

    @pl.when(ki == 0)
    def _():
        cosf = cos_ref[0]
        sinf = sin_ref[0]
        pick = sel_ref[0, 0] * SLC_PICK_BIAS
        for g in range(NSA_GROUP):
            sl = slice(g * HEAD_DIM, (g + 1) * HEAD_DIM)
            qg = _rope(q_ref[0, :, sl].astype(F32), cosf, sinf) * scale
            qaug_ref[g] = jnp.concatenate([qg.astype(BF16), pick], axis=1)
        m_ref[...] = jnp.full_like(m_ref, MASKED)
        acc_ref[...] = jnp.zeros_like(acc_ref)

    def accumulate(causal):
        nsp = sel_ref.shape[3]
        block_of_key = (ki * tq + lax.broadcasted_iota(jnp.int32, (tq, nsp), 0)) // SLC_BLOCK
        onehot = (block_of_key == lax.broadcasted_iota(jnp.int32, (tq, nsp), 1)).astype(BF16)
        kaug = jnp.concatenate([k_ref[0], onehot], axis=1)
        vaug = jnp.concatenate([v_ref[0], jnp.ones((tq, LANE), BF16)], axis=1)
        if causal:
            qpos = lax.broadcasted_iota(jnp.int32, (tq, tq), 0)
            kpos = lax.broadcasted_iota(jnp.int32, (tq, tq), 1)
            visible = kpos <= qpos
        for g in range(NSA_GROUP):
            s = _dot_nt(qaug_ref[g], kaug)
            if causal:
                s = jnp.where(visible, s, MASKED)
            m_old = m_ref[g]
            m_new = jnp.maximum(m_old, jnp.max(s, axis=-1, keepdims=True))
            alpha = jnp.exp(m_old - m_new)
            p = jnp.exp((s - jnp.concatenate([m_new] * (tq // LANE), axis=1)).astype(BF16))
            acc_ref[g] = jnp.concatenate([alpha, alpha], axis=1) * acc_ref[g] + _dot(p, vaug)
            m_ref[g] = m_new

    @pl.when(ki < qi)
    def _():
        accumulate(False)

    @pl.when(ki == qi)
    def _():
        accumulate(True)
        for g in range(NSA_GROUP):
            sl = slice(g * HEAD_DIM, (g + 1) * HEAD_DIM)
            o_ref[0, :, sl] = (acc_ref[g, :, :HEAD_DIM] / acc_ref[g, :, HEAD_DIM:]).astype(o_ref.dtype)


def slc_attention(q, kv, sel, cosf, sinf, *, k_col, v_col, tq):
    b, s, _ = q.shape
    gw = NSA_GROUP * HEAD_DIM
    nq = s // tq
    nsp = sel.shape[3]
    pairs = [(i, j) for i in range(nq) for j in range(i + 1)]
    qi_tab = jnp.asarray([p[0] for p in pairs], jnp.int32)
    ki_tab = jnp.asarray([p[1] for p in pairs], jnp.int32)
    grid_spec = pltpu.PrefetchScalarGridSpec(
        num_scalar_prefetch=2,
        grid=(b, NSA_KV_HEADS, len(pairs)),
        in_specs=[pl.BlockSpec((1, tq, gw), lambda bi, h, t, qt, kt: (bi, qt[t], h)),
                  pl.BlockSpec((1, tq, HEAD_DIM), lambda bi, h, t, qt, kt: (bi, kt[t], k_col + h)),
                  pl.BlockSpec((1, tq, HEAD_DIM), lambda bi, h, t, qt, kt: (bi, kt[t], v_col + h)),
                  pl.BlockSpec((1, 1, tq, nsp), lambda bi, h, t, qt, kt: (bi, h, qt[t], 0)),
                  pl.BlockSpec((1, tq, LANE), lambda bi, h, t, qt, kt: (bi, qt[t], 0)),
                  pl.BlockSpec((1, tq, LANE), lambda bi, h, t, qt, kt: (bi, qt[t], 0))],
        out_specs=pl.BlockSpec((1, tq, gw), lambda bi, h, t, qt, kt: (bi, qt[t], h)),
        scratch_shapes=[pltpu.VMEM((NSA_GROUP, tq, 2 * HEAD_DIM), BF16),
                        pltpu.VMEM((NSA_GROUP, tq, LANE), F32),
                        pltpu.VMEM((NSA_GROUP, tq, 2 * HEAD_DIM), F32)],
    )
    return pl.pallas_call(
        functools.partial(_slc_kernel, tq=tq, scale=HEAD_DIM ** -0.5),
        grid_spec=grid_spec,
        out_shape=jax.ShapeDtypeStruct((b, s, NSA_HEADS * HEAD_DIM), BF16),
        compiler_params=_params("parallel", "parallel", "arbitrary"),
    )(qi_tab, ki_tab, q, kv, kv, sel, cosf, sinf)


def _band_kernel(*refs, nheads, kv_heads, nkv, tq, max_dist, rope_q, with_lse, scale):
    q_ref = refs[0]
    k_refs = refs[1:1 + nkv]
    v_refs = refs[1 + nkv:1 + 2 * nkv]
    pos = 1 + 2 * nkv
    if rope_q:
        cos_ref, sin_ref = refs[pos:pos + 2]
        pos += 2
    o_ref = refs[pos]
    lse_ref = refs[pos + 1] if with_lse else None
    qi = pl.program_id(2)
    qpos = qi * tq + lax.broadcasted_iota(jnp.int32, (tq, 1), 0)
    kpos = (qi - (nkv - 1)) * tq + lax.broadcasted_iota(jnp.int32, (1, nkv * tq), 1)
    diff = qpos - kpos
    mask = jnp.logical_and(jnp.logical_and(diff >= 0, diff <= max_dist), kpos >= 0)
    lse_acc = jnp.zeros((tq, LANE), F32)
    lane = lax.broadcasted_iota(jnp.int32, (tq, LANE), 1)
    for g in range(nheads):
        sl = slice(g * HEAD_DIM, (g + 1) * HEAD_DIM)
        ksl = sl if kv_heads > 1 else slice(0, HEAD_DIM)
        q = q_ref[0, :, sl]
        if rope_q:
            q = _rope(q.astype(F32), cos_ref[0], sin_ref[0]).astype(BF16)
        kcat = jnp.concatenate([r[0, :, ksl] for r in k_refs], axis=0)
        vcat = jnp.concatenate([r[0, :, ksl] for r in v_refs], axis=0)
        vaug = jnp.concatenate([vcat, jnp.ones(vcat.shape, BF16)], axis=1)
        s = _dot_nt(q, kcat) * scale
        s = jnp.where(mask, s, MASKED)
        mx = jnp.max(s, axis=-1, keepdims=True)
        e = jnp.exp((s - mx).astype(BF16))
        pv = _dot(e, vaug)
        den = pv[:, HEAD_DIM:]
        o_ref[0, :, sl] = (pv[:, :HEAD_DIM] / den).astype(o_ref.dtype)
        if with_lse:
            lse_acc = jnp.where(lane == g, mx + jnp.log(den), lse_acc)
    if with_lse:
        lse_ref[0] = lse_acc


def band_attention(q, k, v, *, nheads, kv_heads, q_col, k_col, v_col, o_cols, ncol, max_dist, tq,
                   rope=None, with_lse=False, out_dtype=BF16, seq_len=None, row_block=None):
    b = q.shape[0]
    rows = q.shape[1]
    seq_len = rows if seq_len is None else seq_len
    tq = min(tq, seq_len)
    assert seq_len % tq == 0
    nkv = -(-max_dist // tq) + 1
    qw = nheads * HEAD_DIM
    kw = kv_heads * HEAD_DIM
    if row_block is None:
        row_block = lambda c, i: i
        o_col = lambda c: c
    else:
        o_col = lambda c: 0

    def kv_spec(col_fn, back):
        return pl.BlockSpec((1, tq, kw), lambda bi, c, i: (bi, row_block(c, jnp.maximum(i - back, 0)), col_fn(c)))

    in_specs = [pl.BlockSpec((1, tq, qw), lambda bi, c, i: (bi, row_block(c, i), q_col(c)))]
    in_specs += [kv_spec(k_col, nkv - 1 - t) for t in range(nkv)]
    in_specs += [kv_spec(v_col, nkv - 1 - t) for t in range(nkv)]
    args = [q] + [k] * nkv + [v] * nkv
    if rope is not None:
        in_specs += [pl.BlockSpec((1, tq, LANE), lambda bi, c, i: (bi, i, 0))] * 2
        args += list(rope)
    out_specs = [pl.BlockSpec((1, tq, qw), lambda bi, c, i: (bi, row_block(c, i), o_col(c)))]
    out_shape = [jax.ShapeDtypeStruct((b, rows, o_cols * qw), out_dtype)]
    if with_lse:
        out_specs.append(pl.BlockSpec((1, tq, LANE), lambda bi, c, i: (bi, row_block(c, i), o_col(c))))
        out_shape.append(jax.ShapeDtypeStruct((b, rows, o_cols * LANE), F32))
    res = pl.pallas_call(
        functools.partial(_band_kernel, nheads=nheads, kv_heads=kv_heads, nkv=nkv, tq=tq, max_dist=max_dist,
                          rope_q=rope is not None, with_lse=with_lse, scale=HEAD_DIM ** -0.5),
        grid=(b, ncol, seq_len // tq),
        in_specs=in_specs,
        out_specs=out_specs,
        out_shape=out_shape,
        compiler_params=_params("parallel", "parallel", "parallel"),
    )(*args)
    return res if with_lse else res[0]


def _bmm(x, y):
    return jnp.einsum("bij,bjk->bik", x.astype(BF16), y.astype(BF16), preferred_element_type=F32)


def _bmm_nt(x, y):
    return jnp.einsum("bik,bjk->bij", x.astype(BF16), y.astype(BF16), preferred_element_type=F32)


def _head_sums(x2, scale=1.0):
    n = x2.shape[1]
    blk = (lax.broadcasted_iota(jnp.int32, (n, n), 0) // RWKV_HEAD
           == lax.broadcasted_iota(jnp.int32, (n, n), 1) // RWKV_HEAD)
    ones = jnp.where(blk, scale, 0.0).astype(BF16)
    hi = x2.astype(BF16)
    lo = (x2 - hi.astype(F32)).astype(BF16)
    return _dot(hi, ones) + _dot(lo, ones)


def _wkv_chunk_kernel(*refs, nb, c, mix):
    if mix:
        zr_ref, zk_ref, zv_ref, wl_ref, al_ref, vu_ref, vf_ref, vec_ref = refs[:8]
    else:
        zr_ref, zk_ref, zv_ref, wl_ref, al_ref, vec_ref = refs[:6]
    q_ref, y0_ref, gt_ref, ht_ref, bonus_ref = refs[-5:]
    pair = 2 * RWKV_HEAD
    c2 = 2 * c
    rows = nb * c
    vec = vec_ref[...]
    w0, a0, k_k, k_a, r_k, v0 = [vec[i:i + 1] for i in range(6)]
    r2 = zr_ref[0]
    kraw = zk_ref[0]
    v2 = zv_ref[0]
    x = w0 + wl_ref[0]
    softplus_neg = jnp.maximum(-x, 0.0) + jnp.log(1.0 + jnp.exp(-jnp.abs(x)))
    lw2 = -jnp.exp(-softplus_neg - 0.5)
    a_gate = _sigmoid(a0 + al_ref[0])
    if mix:
        v2 = v2 + (vf_ref[0] - v2) * _sigmoid(v0 + vu_ref[0])
    kk = kraw * k_k
    kk = kk * lax.rsqrt(jnp.maximum(_head_sums(kk * kk), 1e-24))
    k2 = kraw * (1.0 + (a_gate - 1.0) * k_a)
    bonus_ref[0] = (_head_sums(r2 * k2 * r_k) * v2).astype(bonus_ref.dtype)

    def chunks(x2):
        return x2.reshape(nb, c, pair)

    r, lw, k, v, a, b = [chunks(t) for t in (r2, lw2, k2, v2, -kk, kk * a_gate)]
    row = lax.broadcasted_iota(jnp.int32, (c, c), 0)
    col = lax.broadcasted_iota(jnp.int32, (c, c), 1)
    tril = jnp.broadcast_to((row >= col).astype(BF16), (nb, c, c))
    hi = lw.astype(BF16)
    rem = lw - hi.astype(F32)
    mid = rem.astype(BF16)
    lo = (rem - mid.astype(F32)).astype(BF16)
    cum3 = _bmm(tril, jnp.concatenate([hi, mid, lo], axis=2))
    cum = cum3[:, :, :pair] + cum3[:, :, pair:2 * pair] + cum3[:, :, 2 * pair:]
    last = cum[:, c - 1:c, :]
    inv = jnp.exp(-cum)
    tail = jnp.exp(last - cum)
    lane = lax.broadcasted_iota(jnp.int32, (1, 1, pair), 2)
    first = lane < RWKV_HEAD

    def stack(x):
        return jnp.concatenate([jnp.where(first, x, 0.0), jnp.where(first, 0.0, x)], axis=1)

    a_s = stack(a * jnp.exp(cum - lw))
    r_s = stack(r * jnp.exp(cum))
    b_s = stack(b * inv)
    k_s = stack(k * inv)
    v_s = stack(v)
    bh_s = stack(b * tail)
    kh_s = stack(k * tail)
    ar = jnp.concatenate([a_s, r_s], axis=1)
    pbk = _bmm_nt(ar, jnp.concatenate([b_s, k_s], axis=1))
    pb = pbk[:, :, :c2]
    pk = pbk[:, :, c2:]
    row2 = lax.broadcasted_iota(jnp.int32, (c2, c2), 0) % c
    col2 = lax.broadcasted_iota(jnp.int32, (c2, c2), 1) % c
    strict = row2 > col2
    lower = row2 >= col2
    l_ab = jnp.where(strict, pb[:, :c2], 0.0)
    m_rb = jnp.where(lower, pb[:, c2:], 0.0)
    l_ak = jnp.where(strict, pk[:, :c2], 0.0)
    m_rk = jnp.where(lower, pk[:, c2:], 0.0)
    eye = (lax.broadcasted_iota(jnp.int32, (c2, c2), 0) == lax.broadcasted_iota(jnp.int32, (c2, c2), 1))
    eye = eye.astype(F32)
    tinv = eye + l_ab
    pw = _bmm(l_ab, l_ab)
    for _ in range(int(np.log2(c)) - 2):
        both = _bmm(jnp.concatenate([tinv, pw], axis=1), pw)
        tinv = tinv + both[:, :c2]
        pw = both[:, c2:]
    tinv = tinv + _bmm(tinv, pw)
    tu = _bmm(tinv, jnp.concatenate([_bmm(l_ak, v_s), a_s], axis=2))
    u0_s = tu[:, :, :pair]
    ta_s = tu[:, :, pair:]
    mu = _bmm(m_rb, jnp.concatenate([ta_s, u0_s], axis=2))
    q_s = r_s + mu[:, :, :pair]
    y0_s = _bmm(m_rk, v_s) + mu[:, :, pair:]
    q_ref[0] = (q_s[:, :c] + q_s[:, c:]).reshape(nb * c, pair).astype(q_ref.dtype)
    y0_ref[0] = (y0_s[:, :c] + y0_s[:, c:]).reshape(nb * c, pair).astype(y0_ref.dtype)
    eye_b = jnp.broadcast_to(eye.astype(BF16), (nb, c2, c2))
    tr = _bmm_nt(eye_b, jnp.concatenate([bh_s, kh_s], axis=1))
    bh_t = tr[:, :, :c2]
    kh_t = tr[:, :, c2:]
    gh = _bmm(bh_t, jnp.concatenate([ta_s, u0_s], axis=2))
    gt_ref[0, :, 0] = (eye * jnp.exp(last) + gh[:, :, :pair]).astype(gt_ref.dtype)
    ht_ref[0, :, 0] = (gh[:, :, pair:] + _bmm(kh_t, v_s)).astype(ht_ref.dtype)


def _wkv_scan_kernel(q_ref, y0_ref, gt_ref, ht_ref, bonus_ref, g_ref, lnx_ref, o_ref, state_ref, *, npair):
    pair = 2 * RWKV_HEAD
    bsz, c, _ = q_ref.shape

    @pl.when(pl.program_id(0) == 0)
    def _():
        state_ref[...] = jnp.zeros_like(state_ref)

    sls = [slice(p * pair, (p + 1) * pair) for p in range(npair)]
    idx = [(bi, p) for bi in range(bsz) for p in range(npair)]
    st = [state_ref[bi, p].astype(BF16) for bi, p in idx]
    ys = [_dot(q_ref[bi, :, sls[p]], s_) + y0_ref[bi, :, sls[p]].astype(F32) for (bi, p), s_ in zip(idx, st)]
    new = [_dot(gt_ref[bi, 0, p], s_) + ht_ref[bi, 0, p].astype(F32) for (bi, p), s_ in zip(idx, st)]
    for (bi, p), s_ in zip(idx, new):
        state_ref[bi, p] = s_
    y = jnp.concatenate(ys, axis=0)
    dev = y - _head_sums(y, 1.0 / RWKV_HEAD)
    yn = dev * lax.rsqrt(_head_sums(dev * dev, 1.0 / RWKV_HEAD) + LNX_EPS)
    for n, (bi, p) in enumerate(idx):
        ln = yn[n * c:(n + 1) * c] * lnx_ref[0:1, sls[p]] + lnx_ref[1:2, sls[p]]
        out = (ln + bonus_ref[bi, :, sls[p]].astype(F32)) * g_ref[bi, :, sls[p]].astype(F32)
        o_ref[bi, :, sls[p]] = out.astype(o_ref.dtype)


def wkv7(zs, zs_first, wl, al, vu, vec, g, lnx):
    bsz, s, _ = zs.shape
    width = wl.shape[2]
    pair = 2 * RWKV_HEAD
    npair = width // pair
    c = min(RWKV_CHUNK, s)
    nb = min(RWKV_CHUNKS_PER_STEP, s // c)
    nch = s // c
    mix = vu is not None
    assert s % (nb * c) == 0 and width % pair == 0

    def col(off):
        return pl.BlockSpec((1, nb * c, pair), lambda bi, p, j: (bi, j, off + p))

    vec_spec = pl.BlockSpec((8, pair), lambda bi, p, j: (0, p))
    mat = pl.BlockSpec((1, nb, 1, pair, pair), lambda bi, p, j: (bi, j, p, 0, 0))
    in_specs = [col(0), col(npair), col(2 * npair), col(0), col(0)]
    args = [zs, zs, zs, wl, al]
    if mix:
        in_specs += [col(0), col(2 * npair)]
        args += [vu, zs_first]
    q, y0, gt, ht, bonus = pl.pallas_call(
        functools.partial(_wkv_chunk_kernel, nb=nb, c=c, mix=mix),
        grid=(bsz, npair, nch // nb),
        in_specs=in_specs + [vec_spec],
        out_specs=[col(0), col(0), mat, mat, col(0)],
        out_shape=[jax.ShapeDtypeStruct((bsz, s, width), BF16), jax.ShapeDtypeStruct((bsz, s, width), BF16),
                   jax.ShapeDtypeStruct((bsz, nch, npair, pair, pair), BF16),
                   jax.ShapeDtypeStruct((bsz, nch, npair, pair, pair), BF16),
                   jax.ShapeDtypeStruct((bsz, s, width), BF16)],
        compiler_params=_params("parallel", "parallel", "parallel"),
    )(*args, vec)
    row = pl.BlockSpec((bsz, c, width), lambda j: (0, j, 0))
    mats = pl.BlockSpec((bsz, 1, npair, pair, pair), lambda j: (0, j, 0, 0, 0))
    return pl.pallas_call(
        functools.partial(_wkv_scan_kernel, npair=npair),
        grid=(nch,),
        in_specs=[row, row, mats, mats, row, row, pl.BlockSpec((8, width), lambda j: (0, 0))],
        out_specs=row,
        out_shape=jax.ShapeDtypeStruct((bsz, s, width), BF16),
        scratch_shapes=[pltpu.VMEM((bsz, npair, pair, pair), F32)],
        compiler_params=_params("arbitrary"),
    )(q, y0, gt, ht, bonus, g, lnx)


def _moe_up_kernel(te_ref, tv_ref, a_ref, wg_ref, wu_ref, o_ref, wg_bf, wu_bf):
    i = pl.program_id(1)
    changed = jnp.logical_or(i == 0, te_ref[i] != te_ref[jnp.maximum(i - 1, 0)])

    @pl.when(changed)
    def _():
        wg_bf[...] = wg_ref[...].astype(BF16)
        wu_bf[...] = wu_ref[...].astype(BF16)

    @pl.when(tv_ref[i] > 0)
    def _():
        a = a_ref[...]
        g = _dot(a, wg_bf[...])
        u = _dot(a, wu_bf[...])
        o_ref[...] = (_silu(g) * u).astype(o_ref.dtype)

    @pl.when(tv_ref[i] == 0)
    def _():
        o_ref[...] = jnp.zeros_like(o_ref)


def moe_up(tile_expert, tile_valid, xs, w_gu, layer, *, tm, tn):
    r, k = xs.shape
    f = w_gu.shape[3] // 2
    nj = f // tn
    grid_spec = pltpu.PrefetchScalarGridSpec(
        num_scalar_prefetch=2,
        grid=(nj, r // tm),
        in_specs=[pl.BlockSpec((tm, k), lambda j, i, te, tv: (i, 0)),
                  pl.BlockSpec((None, None, k, tn), lambda j, i, te, tv: (layer, te[i], 0, j)),
                  pl.BlockSpec((None, None, k, tn), lambda j, i, te, tv: (layer, te[i], 0, j + nj))],
        out_specs=pl.BlockSpec((tm, tn), lambda j, i, te, tv: (i, j)),
        scratch_shapes=[pltpu.VMEM((k, tn), BF16), pltpu.VMEM((k, tn), BF16)],
    )
    return pl.pallas_call(
        _moe_up_kernel,
        grid_spec=grid_spec,
        out_shape=jax.ShapeDtypeStruct((r, f), BF16),
        compiler_params=_params("parallel", "arbitrary"),
    )(tile_expert, tile_valid, xs, w_gu, w_gu)


def _moe_down_kernel(te_ref, tv_ref, a_ref, w_ref, o_ref, w_bf):
    i = pl.program_id(1)
    changed = jnp.logical_or(i == 0, te_ref[i] != te_ref[jnp.maximum(i - 1, 0)])

    @pl.when(changed)
    def _():
        w_bf[...] = w_ref[...].astype(BF16)

    @pl.when(tv_ref[i] > 0)
    def _():
        o_ref[...] = _dot(a_ref[...], w_bf[...]).astype(o_ref.dtype)

    @pl.when(tv_ref[i] == 0)
    def _():
        o_ref[...] = jnp.zeros_like(o_ref)


def moe_down(tile_expert, tile_valid, act, w_down, layer, *, tm, tn):
    r, f = act.shape
    d = w_down.shape[3]
    grid_spec = pltpu.PrefetchScalarGridSpec(
        num_scalar_prefetch=2,
        grid=(d // tn, r // tm),
        in_specs=[pl.BlockSpec((tm, f), lambda j, i, te, tv: (i, 0)),
                  pl.BlockSpec((None, None, f, tn), lambda j, i, te, tv: (layer, te[i], 0, j))],
        out_specs=pl.BlockSpec((tm, tn), lambda j, i, te, tv: (i, j)),
        scratch_shapes=[pltpu.VMEM((f, tn), BF16)],
    )
    return pl.pallas_call(
        _moe_down_kernel,
        grid_spec=grid_spec,
        out_shape=jax.ShapeDtypeStruct((r, d), BF16),
        compiler_params=_params("parallel", "arbitrary"),
    )(tile_expert, tile_valid, act, w_down)


def _rope_tables(positions):
    inv_freq = ROPE_THETA ** (-jnp.arange(ROPE_HALF, dtype=F32) / ROPE_HALF)
    ang = positions.astype(F32)[:, :, None] * inv_freq
    cos = jnp.cos(ang)
    sin = jnp.sin(ang)
    b, s = positions.shape
    pad1 = jnp.ones((b, s, HEAD_DIM - ROPE_DIM), F32)
    pad0 = jnp.zeros((b, s, HEAD_DIM - ROPE_DIM), F32)
    return jnp.concatenate([cos, cos, pad1], axis=-1), jnp.concatenate([-sin, sin, pad0], axis=-1)


def _pad_cols(w, n):
    return w if w.shape[-1] == n else jnp.pad(w, ((0, 0), (0, n - w.shape[-1])))


def nsa_branch(q_a, kv_a, cosf, sinf, cmp_pe, cmp_w1, cmp_w2):
    b, s, _ = q_a.shape
    nc = s // CMP_STRIDE - 1
    ncp = -(-nc // LANE) * LANE
    kvw = NSA_KV_HEADS * HEAD_DIM

    def blocks(t):
        ch = t.reshape(b, s // CMP_STRIDE, CMP_STRIDE, NSA_KV_HEADS, HEAD_DIM)
        blk = jnp.concatenate([ch[:, :-1], ch[:, 1:]], axis=2)
        return blk.transpose(0, 1, 3, 2, 4).reshape(b * nc * NSA_KV_HEADS, CMP_BLOCK * HEAD_DIM)

    rows = b * nc * NSA_KV_HEADS
    rows_p = -(-rows // 512) * 512 if rows > 512 else -(-rows // 8) * 8
    flat = jnp.stack([blocks(kv_a[..., :kvw]), blocks(kv_a[..., kvw:2 * kvw])]).astype(F32)
    flat = jnp.pad(flat, ((0, 0), (0, rows_p - rows), (0, 0)))
    comp = nsa_compress(flat, cmp_pe.reshape(2, 1, CMP_BLOCK * HEAD_DIM), cmp_w1.astype(BF16), cmp_w2.astype(BF16))
    comp = comp[:, :rows].reshape(2, b, nc, NSA_KV_HEADS, HEAD_DIM).transpose(0, 1, 3, 2, 4)
    comp = jnp.pad(comp, ((0, 0), (0, 0), (0, 0), (0, ncp - nc), (0, 0)))
    kc = comp[0]
    vct = comp[1].transpose(0, 1, 3, 2)
    tq = min(256, s)
    o_cmp, sel = cmp_select(q_a, kc, vct, nc=nc, tq=tq)
    o_slc = slc_attention(q_a, kv_a, sel, cosf, sinf, k_col=4, v_col=6, tq=min(512, s))
    o_win = band_attention(q_a, kv_a, kv_a, nheads=NSA_GROUP, kv_heads=1, q_col=lambda c: c,
                           k_col=lambda c: 8 + c, v_col=lambda c: 10 + c, o_cols=NSA_KV_HEADS,
                           ncol=NSA_KV_HEADS, max_dist=WIN_SIZE - 1, tq=tq, rope=(cosf, sinf))
    return o_cmp, o_slc, o_win


def rwkv_branch(zs, zs_first, vec, w_up, a_up, g_up, v_res):
    b, s, _ = zs.shape
    t = b * s
    w_ = RWKV_WIDTH
    o = 3 * w_
    zw, za = zs[..., o:o + LORA_W], zs[..., o + LORA_W:o + LORA_W + LORA_A]
    zg = zs[..., o + LORA_W + LORA_A:o + LORA_W + LORA_A + LORA_G]
    w0, a0, k_k, k_a, r_k, lnx_g, lnx_b = [vec[i] for i in range(7)]

    def lora(xin, wmat, out_dtype=F32):
        return matmul(xin.reshape(t, -1).astype(BF16), wmat.astype(BF16), out_dtype=out_dtype, tm=1024,
                      tn=wmat.shape[1]).reshape(b, s, -1)

    wl = lora(jnp.tanh(zw), w_up)
    al = lora(za, a_up)
    g = lora(jax.nn.sigmoid(zg), g_up, BF16)
    zero = jnp.zeros_like(w0)
    if v_res is None:
        vu, v0 = None, zero
    else:
        v0, v_down, v_up = v_res
        vd = matmul(zs.reshape(t, -1), _pad_cols(v_down, LANE).astype(BF16), out_dtype=BF16, tm=1024, tn=LANE,
                    a_col=2)
        vu = matmul(vd, jnp.pad(v_up, ((0, LANE - v_up.shape[0]), (0, 0))).astype(BF16), out_dtype=F32, tm=1024,
                    tn=512).reshape(b, s, w_)
    vecs = jnp.stack([w0, a0, k_k, k_a, r_k, v0, zero, zero])
    lnx = jnp.stack([lnx_g, lnx_b] + [zero] * 6)
    return wkv7(zs, zs_first, wl, al, vu, vecs, g, lnx)


DIL_ROW_TILE = 1024


def residue_perm(dil, transpose=False):
    i = np.arange(DIL_ROW_TILE)
    per = DIL_ROW_TILE // dil
    p = np.zeros((DIL_ROW_TILE, DIL_ROW_TILE), np.float32)
    p[i, (i % per) * dil + i // per] = 1.0
    return jnp.asarray(p.T if transpose else p, BF16)


def _dilated_merge_kernel(*refs, ngroups, nperm):
    o_refs = refs[:ngroups]
    l_refs = refs[ngroups:2 * ngroups]
    p_refs = refs[2 * ngroups:2 * ngroups + nperm]
    y_ref = refs[-1]
    outs, lses = [], []
    for g in range(ngroups):
        if g < ngroups - nperm:
            outs.append(o_refs[g][...].astype(F32))
            lses.append(l_refs[g][...])
            continue
        pt = p_refs[g - (ngroups - nperm)][...]
        outs.append(_dot(pt, o_refs[g][...]))
        lse = l_refs[g][...]
        hi = lse.astype(BF16)
        lo = (lse - hi.astype(F32)).astype(BF16)
        both = _dot(pt, jnp.concatenate([hi, lo], axis=1))
        lses.append(both[:, :LANE] + both[:, LANE:])
    mx = functools.reduce(jnp.maximum, lses)
    es = [jnp.exp(lse - mx) for lse in lses]
    inv = 1.0 / functools.reduce(jnp.add, es)
    for hd in range(DIL_HPG):
        sl = slice(hd * HEAD_DIM, (hd + 1) * HEAD_DIM)
        acc = functools.reduce(jnp.add, [e[:, hd:hd + 1] * o[:, sl] for e, o in zip(es, outs)])
        y_ref[:, sl] = (acc * inv[:, hd:hd + 1]).astype(y_ref.dtype)


def dilated_branch(qkvs):
    b, s, width = qkvs[0].shape
    t = b * s
    assert s % DIL_ROW_TILE == 0
    outs, lses, perms = [], [], []
    for gi, (win, dil) in enumerate(DIL_PATTERNS):
        common = dict(nheads=DIL_HPG, kv_heads=DIL_HPG, q_col=lambda c: 0, k_col=lambda c: 1, v_col=lambda c: 2,
                      o_cols=1, max_dist=win // dil, with_lse=True, out_dtype=BF16)
        if dil == 1:
            assert not perms
            o, lse = band_attention(qkvs[gi], qkvs[gi], qkvs[gi], ncol=1, tq=256, **common)
        else:
            o, lse = band_attention(qkvs[gi], qkvs[gi], qkvs[gi], ncol=dil, tq=DIL_ROW_TILE // dil,
                                    seq_len=s // dil, row_block=lambda c, i, dil=dil: i * dil + c, **common)
            perms.append(residue_perm(dil, transpose=True))
        outs.append(o.reshape(t, DIL_OUT))
        lses.append(lse.reshape(t, LANE))
    ng = len(DIL_PATTERNS)
    row = lambda width_: pl.BlockSpec((DIL_ROW_TILE, width_), lambda i: (i, 0))
    const = pl.BlockSpec((DIL_ROW_TILE, DIL_ROW_TILE), lambda i: (0, 0))
    return pl.pallas_call(
        functools.partial(_dilated_merge_kernel, ngroups=ng, nperm=len(perms)),
        grid=(t // DIL_ROW_TILE,),
        in_specs=[row(DIL_OUT)] * ng + [row(LANE)] * ng + [const] * len(perms),
        out_specs=row(DIL_OUT),
        out_shape=jax.ShapeDtypeStruct((t, DIL_OUT), BF16),
        compiler_params=_params("parallel"),
    )(*outs, *lses, *perms)


def moe_ffn(h, router_w, router_b, w_gu, w_down, layer, *, tm):
    t, d = h.shape
    logits = matmul(h, _pad_cols(router_w, LANE).astype(BF16), out_dtype=F32, tm=1024, tn=LANE)[:, :N_EXPERTS]
    logits = logits + router_b
    top_v, top_i = lax.top_k(logits, TOP_K)
    wts = jax.nn.softmax(top_v, axis=-1)
    flat_e = top_i.reshape(-1)
    onehot = (flat_e[:, None] == jnp.arange(N_EXPERTS)[None, :]).astype(jnp.int32)
    rank = jnp.take_along_axis(jnp.cumsum(onehot, axis=0), flat_e[:, None], axis=1)[:, 0] - 1
    counts = jnp.sum(onehot, axis=0)
    tiles_per = (counts + tm - 1) // tm
    tile_end = jnp.cumsum(tiles_per)
    group_start = (tile_end - tiles_per) * tm
    dest = group_start[flat_e] + rank
    ntiles = (TOP_K * t) // tm + N_EXPERTS
    rows = ntiles * tm
    row_token = jnp.zeros((rows,), jnp.int32).at[dest].set(jnp.arange(TOP_K * t, dtype=jnp.int32) // TOP_K)
    tile_ids = jnp.arange(ntiles, dtype=jnp.int32)
    tile_valid = (tile_ids < tile_end[-1]).astype(jnp.int32)
    tile_expert = jnp.minimum(jnp.searchsorted(tile_end, tile_ids, side="right"), N_EXPERTS - 1).astype(jnp.int32)
    xs = jnp.take(h, row_token, axis=0, mode="clip")
    act = moe_up(tile_expert, tile_valid, xs, w_gu, layer, tm=tm, tn=min(1024, w_gu.shape[3] // 2))
    out = moe_down(tile_expert, tile_valid, act, w_down, layer, tm=tm, tn=512)
    dest = dest.reshape(t, TOP_K)
    ya = jnp.take(out, dest[:, 0], axis=0, mode="clip")
    yb = jnp.take(out, dest[:, 1], axis=0, mode="clip")
    return ya, yb, jnp.pad(wts, ((0, 0), (0, LANE - TOP_K)))


def kernel(x, c, positions, ada_w, ada_b, norm_g, w_in, cmp_pe, cmp_w1, cmp_w2, rwkv_mu, rwkv_vec, w_up, a_up, g_up, v_res0, v_res_down, v_res_up, w_br_a, w_br_b, w_br_c, w_out, ffn_gu, ffn_down, router_w, router_b, moe_gu, moe_down):
    b, s, d = x.shape
    depth = ada_w.shape[0]
    t = b * s
    tm_row = min(512, s)
    cosf, sinf = _rope_tables(positions)
    cos_t = cosf.reshape(t, LANE)
    sin_t = sinf.reshape(t, LANE)

    cond = jnp.pad(jax.nn.silu(c), ((0, 8 - b % 8 if b % 8 else 0), (0, 0))).astype(BF16)
    mods = []
    for l in range(depth):
        mod = matmul(cond, ada_w, out_dtype=F32, tm=cond.shape[0], tn=512, layer=l)[:b] + ada_b[l]
        mods.append(mod.reshape(b, 6, 1, d))

    def mod_of(l, i):
        return mods[l][:, i]

    q_cols = NSA_HEADS * HEAD_DIM
    kv_cols = 6 * NSA_KV_HEADS * HEAD_DIM
    gate_cols = 3 * NSA_HEADS
    rwkv_cols = 3 * RWKV_WIDTH + LORA_W + LORA_A + LORA_G
    dil_cols = 3 * DIL_HEADS * HEAD_DIM
    offs = np.cumsum([0, q_cols, kv_cols, gate_cols, rwkv_cols, dil_cols, 3 * d]).tolist()
    rwkv_pad = -(-rwkv_cols // 512) * 512

    xf = x.reshape(t, d)
    h = norm_mod(xf, norm_g[0, 0][None], mod_of(0, 1), mod_of(0, 0), seq=s, tm=tm_row)
    zs_first = None
    for l in range(depth):
        wl = w_in[l]
        seg = lambda i: wl[:, offs[i]:offs[i + 1]]
        q_a = matmul(h, seg(0).astype(BF16), out_dtype=BF16, tm=1024, tn=q_cols)
        kv_a = matmul(h, seg(1).astype(BF16), out_dtype=BF16, tm=1024, tn=kv_cols, rope=(cos_t, sin_t, (4, 5, 8, 9)))
        gate_a = matmul(h, _pad_cols(seg(2), LANE).astype(BF16), out_dtype=F32, tm=1024, tn=LANE, act="sigmoid")
        zs = matmul_token_shift(h, _pad_cols(seg(3), rwkv_pad).astype(BF16), _pad_cols(rwkv_mu[l][None], rwkv_pad),
                                seq=s, tm=1024, tn=rwkv_pad // 4).reshape(b, s, -1)
        zs_first = zs if l == 0 else zs_first
        wc = seg(4).reshape(d, 3, len(DIL_PATTERNS), DIL_OUT)
        qkv_c = [matmul(h, wc[:, :, gi].reshape(d, 3 * DIL_OUT).astype(BF16), out_dtype=BF16, tm=DIL_ROW_TILE,
                        tn=3 * DIL_OUT, rope=(cos_t, sin_t, tuple(range(2 * DIL_HPG))),
                        row_perm=None if dil == 1 else residue_perm(dil)).reshape(b, s, -1)
                 for gi, (_, dil) in enumerate(DIL_PATTERNS)]
        mg = matmul(h, seg(5).astype(BF16), out_dtype=BF16, tm=1024, tn=1024, act="sigmoid")

        o_cmp, o_slc, o_win = nsa_branch(q_a.reshape(b, s, -1), kv_a.reshape(b, s, -1), cosf, sinf,
                                         cmp_pe[l], cmp_w1[l], cmp_w2[l])
        v_res = None if l == 0 else (v_res0[l - 1], v_res_down[l - 1], v_res_up[l - 1])
        y_b = rwkv_branch(zs, zs_first, rwkv_vec[l], w_up[l], a_up[l], g_up[l], v_res)
        y_c = dilated_branch(qkv_c)
        merged = branch_merge(o_cmp.reshape(t, -1), o_slc.reshape(t, -1), o_win.reshape(t, -1), gate_a,
                              y_b.reshape(t, -1), y_c.reshape(t, -1), w_br_a[l].astype(BF16),
                              w_br_b[l].astype(BF16), w_br_c[l].astype(BF16), mg, tm=1024, tn=512)
        xf, h = matmul_close(merged, w_out[l].astype(BF16), xf, norm_g[l, 1][None], mod_of(l, 2),
                             (norm_g[l, 2][None], mod_of(l, 4), mod_of(l, 3)), seq=s, tm=tm_row, tk=d)

        nxt = None if l == depth - 1 else (norm_g[l + 1, 0][None], mod_of(l + 1, 1), mod_of(l + 1, 0))
        if l % 2 == 0:
            act = swiglu_up(h, ffn_gu[l // 2].astype(BF16), tm=1024, tn=512)
            xf, h = matmul_close(act, ffn_down[l // 2].astype(BF16), xf, norm_g[l, 3][None], mod_of(l, 5), nxt,
                                 seq=s, tm=tm_row, tk=act.shape[1] // 4)
        else:
            ya, yb, wts = moe_ffn(h, router_w[l // 2], router_b[l // 2], moe_gu, moe_down, l // 2, tm=512)
            xf, h = close_sublayer(ya, yb, wts, xf, norm_g[l, 3][None], mod_of(l, 5), nxt, seq=s, tm=tm_row)
    return xf.reshape(b, s, d)
```

```python
import functools

import numpy as np
import jax
import jax.numpy as jnp
from jax import lax
from jax.experimental import pallas as pl
from jax.experimental.pallas import tpu as pltpu

F32 = jnp.float32
BF16 = jnp.bfloat16

HEAD_DIM = 128
ROPE_DIM = HEAD_DIM // 4
ROPE_HALF = ROPE_DIM // 2
ROPE_THETA = 500000.0
NORM_EPS = 1e-6

NSA_HEADS = 8
NSA_KV_HEADS = 2
NSA_GROUP = NSA_HEADS // NSA_KV_HEADS
CMP_BLOCK = 32
CMP_STRIDE = 16
SLC_BLOCK = 64
SLC_TOPN = 16
WIN_SIZE = 512
FORCE_BONUS = 1e4
SLC_PICK_BIAS = 8192.0

RWKV_HEADS = 16
RWKV_HEAD = 64
RWKV_WIDTH = RWKV_HEADS * RWKV_HEAD
LORA_W = 96
LORA_A = 96
LORA_G = 256
LNX_EPS = 64e-5
RWKV_CHUNK = 64
RWKV_CHUNKS_PER_STEP = 8

DIL_PATTERNS = ((128, 1), (512, 4), (2048, 16))
DIL_HPG = 4
DIL_HEADS = DIL_HPG * len(DIL_PATTERNS)
DIL_OUT = DIL_HPG * HEAD_DIM

N_EXPERTS = 8
TOP_K = 2

LANE = 128
VMEM_LIMIT_BYTES = 56 * 1024 * 1024
MASKED = -1e30


def _params(*sem):
    return pltpu.CompilerParams(dimension_semantics=sem, vmem_limit_bytes=VMEM_LIMIT_BYTES)


def _sigmoid(x):
    return 1.0 / (1.0 + jnp.exp(-x))


def _silu(x):
    return x * _sigmoid(x)


def _dot(a, b):
    return jnp.dot(a, b, preferred_element_type=F32)


def _dot_nt(a, b):
    return lax.dot_general(a, b, (((1,), (1,)), ((), ())), preferred_element_type=F32)


def _dot_tn(a, b):
    return lax.dot_general(a, b, (((0,), (0,)), ((), ())), preferred_element_type=F32)


def _rope(t, cosf, sinf):
    lane = lax.broadcasted_iota(jnp.int32, t.shape, 1)
    swapped = jnp.where(lane < ROPE_HALF, pltpu.roll(t, LANE - ROPE_HALF, 1), pltpu.roll(t, ROPE_HALF, 1))
    return t * cosf + swapped * sinf


def _rms(y):
    return y * lax.rsqrt(jnp.mean(y * y, axis=-1, keepdims=True) + NORM_EPS)


def _matmul_kernel(*refs, act, rope_chunks, ntiles, permute=False):
    if permute:
        a_ref, w_ref, cos_ref, sin_ref, perm_ref, o_ref = refs
    elif rope_chunks:
        a_ref, w_ref, cos_ref, sin_ref, o_ref = refs
    else:
        a_ref, w_ref, o_ref = refs
    acc = _dot(a_ref[...].astype(BF16), w_ref[...].astype(BF16))
    if act == "sigmoid":
        acc = _sigmoid(acc)
    elif act == "tanh":
        acc = jnp.tanh(acc)
    if not rope_chunks:
        o_ref[...] = acc.astype(o_ref.dtype)
        return
    per_tile = acc.shape[1] // LANE

    def store(tile):
        for c in range(per_tile):
            sl = slice(c * LANE, (c + 1) * LANE)
            if tile is not None and tile * per_tile + c in rope_chunks:
                o_ref[:, sl] = _rope(acc[:, sl], cos_ref[...], sin_ref[...]).astype(o_ref.dtype)
            else:
                o_ref[:, sl] = acc[:, sl].astype(o_ref.dtype)

    if ntiles == 1:
        store(0)
        if permute:
            o_ref[...] = _dot(perm_ref[...], o_ref[...]).astype(o_ref.dtype)
        return
    assert not permute
    j = pl.program_id(1)
    tiles = sorted({c // per_tile for c in rope_chunks})
    for t in tiles:
        pl.when(j == t)(functools.partial(store, t))
    pl.when(functools.reduce(jnp.logical_and, [j != t for t in tiles]))(functools.partial(store, None))


def _matmul_shift_kernel(a_ref, ap_ref, w_ref, mu_ref, o_ref, *, seq, tm):
    w = w_ref[...]
    z = _dot(a_ref[...], w)
    zp = _dot(ap_ref[...], w)
    at_start = (pl.program_id(0) * tm) % seq == 0
    last = zp.shape[0] - 1
    prev_row = jnp.where(at_start, 0.0, zp[last:last + 1, :])
    rowid = lax.broadcasted_iota(jnp.int32, z.shape, 0)
    shifted = jnp.where(rowid == 0, prev_row, pltpu.roll(z, 1, 0))
    o_ref[...] = (z + (shifted - z) * mu_ref[...]).astype(o_ref.dtype)


BF16_SUBLANES = 16


def matmul_token_shift(a, w, mu, *, seq, tm, tn):
    m, k = a.shape
    n = w.shape[1]
    assert m % tm == 0 and n % tn == 0 and seq % tm == 0
    per = tm // BF16_SUBLANES
    return pl.pallas_call(
        functools.partial(_matmul_shift_kernel, seq=seq, tm=tm),
        grid=(m // tm, n // tn),
        in_specs=[pl.BlockSpec((tm, k), lambda i, j: (i, 0)),
                  pl.BlockSpec((BF16_SUBLANES, k), lambda i, j: (jnp.maximum(i * per - 1, 0), 0)),
                  pl.BlockSpec((k, tn), lambda i, j: (0, j)),
                  pl.BlockSpec((1, tn), lambda i, j: (0, j))],
        out_specs=pl.BlockSpec((tm, tn), lambda i, j: (i, j)),
        out_shape=jax.ShapeDtypeStruct((m, n), F32),
        compiler_params=_params("parallel", "parallel"),
    )(a, a, w, mu)


def matmul(a, w, *, out_dtype, tm, tn, act=None, rope=None, layer=None, a_col=0, row_perm=None):
    m = a.shape[0]
    k = w.shape[-2]
    n = w.shape[-1]
    tm = min(tm, m)
    assert m % tm == 0 and n % tn == 0, (m, n, tm, tn)
    if layer is None:
        w_spec = pl.BlockSpec((k, tn), lambda i, j: (0, j))
    else:
        w_spec = pl.BlockSpec((None, k, tn), lambda i, j: (layer, 0, j))
    in_specs = [pl.BlockSpec((tm, k), lambda i, j: (i, a_col)), w_spec]
    args = [a, w]
    tiles = ()
    if rope is not None:
        cosf, sinf, tiles = rope
        in_specs += [pl.BlockSpec((tm, LANE), lambda i, j: (i, 0))] * 2
        args += [cosf, sinf]
    if row_perm is not None:
        assert row_perm.shape == (tm, tm) and rope is not None
        in_specs.append(pl.BlockSpec((tm, tm), lambda i, j: (0, 0)))
        args.append(row_perm)
    return pl.pallas_call(
        functools.partial(_matmul_kernel, act=act, rope_chunks=frozenset(tiles), ntiles=n // tn,
                          permute=row_perm is not None),
        grid=(m // tm, n // tn),
        in_specs=in_specs,
        out_specs=pl.BlockSpec((tm, tn), lambda i, j: (i, j)),
        out_shape=jax.ShapeDtypeStruct((m, n), out_dtype),
        compiler_params=_params("parallel", "parallel"),
    )(*args)


def _swiglu_up_kernel(a_ref, wg_ref, wu_ref, o_ref):
    a = a_ref[...]
    g = _dot(a, wg_ref[...])
    u = _dot(a, wu_ref[...])
    o_ref[...] = (_silu(g) * u).astype(o_ref.dtype)


def swiglu_up(a, w_gu, *, tm, tn):
    m, k = a.shape
    f = w_gu.shape[1] // 2
    assert m % tm == 0 and f % tn == 0
    nj = f // tn
    return pl.pallas_call(
        _swiglu_up_kernel,
        grid=(m // tm, nj),
        in_specs=[pl.BlockSpec((tm, k), lambda i, j: (i, 0)),
                  pl.BlockSpec((k, tn), lambda i, j: (0, j)),
                  pl.BlockSpec((k, tn), lambda i, j: (0, j + nj))],
        out_specs=pl.BlockSpec((tm, tn), lambda i, j: (i, j)),
        out_shape=jax.ShapeDtypeStruct((m, f), BF16),
        compiler_params=_params("parallel", "parallel"),
    )(a, w_gu, w_gu)


def _close_sublayer(x, y, gpost, gt, nxt):
    xn = x + gt * (_rms(y) * gpost)
    if nxt is None:
        return xn, None
    gnext, sc, sh = nxt
    return xn, (_rms(xn) * gnext) * (1.0 + sc) + sh


def _matmul_close_kernel(*refs, with_next):
    if with_next:
        a_ref, w_ref, x_ref, gp_ref, gt_ref, gn_ref, sc_ref, sh_ref, xo_ref, ho_ref, acc_ref = refs
    else:
        a_ref, w_ref, x_ref, gp_ref, gt_ref, xo_ref, acc_ref = refs
    kk = pl.program_id(1)

    @pl.when(kk == 0)
    def _():
        acc_ref[...] = jnp.zeros_like(acc_ref)

    acc_ref[...] += _dot(a_ref[...], w_ref[...])

    @pl.when(kk == pl.num_programs(1) - 1)
    def _():
        nxt = (gn_ref[...], sc_ref[0], sh_ref[0]) if with_next else None
        xn, h = _close_sublayer(x_ref[...], acc_ref[...], gp_ref[...], gt_ref[0], nxt)
        xo_ref[...] = xn
        if with_next:
            ho_ref[...] = h.astype(ho_ref.dtype)


def _close_specs(tm, d, seq, with_next, nidx):
    def row(i, *_):
        return (i, 0)

    def const(*_):
        return (0, 0)

    def batch(i, *_):
        return ((i * tm) // seq, 0, 0)

    specs = [pl.BlockSpec((tm, d), row), pl.BlockSpec((1, d), const), pl.BlockSpec((1, 1, d), batch)]
    if with_next:
        specs += [pl.BlockSpec((1, d), const), pl.BlockSpec((1, 1, d), batch), pl.BlockSpec((1, 1, d), batch)]
    return specs


def matmul_close(a, w, x, gpost, gt, nxt, *, seq, tm, tk):
    m, k = a.shape
    d = w.shape[1]
    assert m % tm == 0 and k % tk == 0 and seq % tm == 0
    with_next = nxt is not None
    args = [a, w, x, gpost, gt] + (list(nxt) if with_next else [])
    in_specs = [pl.BlockSpec((tm, tk), lambda i, kk: (i, kk)), pl.BlockSpec((tk, d), lambda i, kk: (kk, 0))]
    in_specs += _close_specs(tm, d, seq, with_next, 2)
    out_shape = [jax.ShapeDtypeStruct((m, d), F32)]
    out_specs = [pl.BlockSpec((tm, d), lambda i, kk: (i, 0))]
    if with_next:
        out_shape.append(jax.ShapeDtypeStruct((m, d), BF16))
        out_specs.append(pl.BlockSpec((tm, d), lambda i, kk: (i, 0)))
    res = pl.pallas_call(
        functools.partial(_matmul_close_kernel, with_next=with_next),
        grid=(m // tm, k // tk),
        in_specs=in_specs,
        out_specs=out_specs,
        out_shape=out_shape,
        scratch_shapes=[pltpu.VMEM((tm, d), F32)],
        compiler_params=_params("parallel", "arbitrary"),
    )(*args)
    return (res[0], res[1]) if with_next else (res[0], None)


def _close_kernel(*refs, with_next):
    if with_next:
        ya_ref, yb_ref, wt_ref, x_ref, gp_ref, gt_ref, gn_ref, sc_ref, sh_ref, xo_ref, ho_ref = refs
    else:
        ya_ref, yb_ref, wt_ref, x_ref, gp_ref, gt_ref, xo_ref = refs
    nxt = (gn_ref[...], sc_ref[0], sh_ref[0]) if with_next else None
    wt = wt_ref[...]
    y = wt[:, 0:1] * ya_ref[...].astype(F32) + wt[:, 1:2] * yb_ref[...].astype(F32)
    xn, h = _close_sublayer(x_ref[...], y, gp_ref[...], gt_ref[0], nxt)
    xo_ref[...] = xn
    if with_next:
        ho_ref[...] = h.astype(ho_ref.dtype)


def close_sublayer(yab, wt, x, gpost, gt, nxt, *, seq, tm):
    m, d = x.shape
    with_next = nxt is not None
    args = [yab, yab, wt, x, gpost, gt] + (list(nxt) if with_next else [])
    in_specs = [pl.BlockSpec((tm, d), lambda i: (i, 0)), pl.BlockSpec((tm, d), lambda i: (i + m // tm, 0)),
                pl.BlockSpec((tm, LANE), lambda i: (i, 0))] + _close_specs(tm, d, seq, with_next, 1)
    out_shape = [jax.ShapeDtypeStruct((m, d), F32)]
    out_specs = [pl.BlockSpec((tm, d), lambda i: (i, 0))]
    if with_next:
        out_shape.append(jax.ShapeDtypeStruct((m, d), BF16))
        out_specs.append(pl.BlockSpec((tm, d), lambda i: (i, 0)))
    res = pl.pallas_call(
        functools.partial(_close_kernel, with_next=with_next),
        grid=(m // tm,),
        in_specs=in_specs,
        out_specs=out_specs,
        out_shape=out_shape,
        compiler_params=_params("parallel"),
    )(*args)
    return (res[0], res[1]) if with_next else (res[0], None)


def _norm_mod_kernel(x_ref, g_ref, sc_ref, sh_ref, o_ref):
    o_ref[...] = ((_rms(x_ref[...]) * g_ref[...]) * (1.0 + sc_ref[0]) + sh_ref[0]).astype(o_ref.dtype)


def norm_mod(x, g, sc, sh, *, seq, tm):
    m, d = x.shape
    batch = lambda i: ((i * tm) // seq, 0, 0)
    return pl.pallas_call(
        _norm_mod_kernel,
        grid=(m // tm,),
        in_specs=[pl.BlockSpec((tm, d), lambda i: (i, 0)), pl.BlockSpec((1, d), lambda i: (0, 0)),
                  pl.BlockSpec((1, 1, d), batch), pl.BlockSpec((1, 1, d), batch)],
        out_specs=pl.BlockSpec((tm, d), lambda i: (i, 0)),
        out_shape=jax.ShapeDtypeStruct((m, d), BF16),
        compiler_params=_params("parallel"),
    )(x, g, sc, sh)


def _branch_merge_kernel(oc_ref, os_ref, ow_ref, ng_ref, yb_ref, yc_ref, wa_ref, wb_ref, wc_ref, ga_ref, gb_ref,
                         gc_ref, o_ref, ya_ref):
    @pl.when(pl.program_id(1) == 0)
    def _():
        ng = ng_ref[...]
        for hd in range(NSA_HEADS):
            sl = slice(hd * HEAD_DIM, (hd + 1) * HEAD_DIM)
            ya = ng[:, 3 * hd:3 * hd + 1] * oc_ref[:, sl].astype(F32)
            ya += ng[:, 3 * hd + 1:3 * hd + 2] * os_ref[:, sl].astype(F32)
            ya += ng[:, 3 * hd + 2:3 * hd + 3] * ow_ref[:, sl].astype(F32)
            ya_ref[:, sl] = ya.astype(ya_ref.dtype)

    acc = ga_ref[...].astype(F32) * _dot(ya_ref[...], wa_ref[...])
    acc += gb_ref[...].astype(F32) * _dot(yb_ref[...], wb_ref[...])
    acc += gc_ref[...].astype(F32) * _dot(yc_ref[...], wc_ref[...])
    o_ref[...] = acc.astype(o_ref.dtype)


def branch_merge(o_cmp, o_slc, o_win, nsa_gate, yb, yc, wa, wb, wc, gates, *, tm, tn):
    m = yb.shape[0]
    d = wa.shape[1]
    nj = d // tn
    row = lambda width: pl.BlockSpec((tm, width), lambda i, j: (i, 0))
    wsp = lambda kdim: pl.BlockSpec((kdim, tn), lambda i, j: (0, j))
    gsp = lambda off: pl.BlockSpec((tm, tn), lambda i, j: (i, j + off * nj))
    wa_rows = wa.shape[0]
    return pl.pallas_call(
        _branch_merge_kernel,
        grid=(m // tm, nj),
        in_specs=[row(wa_rows), row(wa_rows), row(wa_rows), row(LANE), row(yb.shape[1]), row(yc.shape[1]),
                  wsp(wa_rows), wsp(wb.shape[0]), wsp(wc.shape[0]), gsp(0), gsp(1), gsp(2)],
        out_specs=pl.BlockSpec((tm, tn), lambda i, j: (i, j)),
        out_shape=jax.ShapeDtypeStruct((m, d), BF16),
        scratch_shapes=[pltpu.VMEM((tm, wa_rows), BF16)],
        compiler_params=_params("parallel", "arbitrary"),
    )(o_cmp, o_slc, o_win, nsa_gate, yb, yc, wa, wb, wc, gates, gates, gates)


def _compress_kernel(f_ref, pe_ref, w1_ref, w2_ref, o_ref):
    a = (f_ref[0] + pe_ref[0]).astype(BF16)
    hmid = _silu(_dot(a, w1_ref[0]))
    o_ref[0] = _dot(hmid.astype(BF16), w2_ref[0]).astype(o_ref.dtype)


def nsa_compress(flat, pe, w1, w2):
    two, r, kdim = flat.shape
    tm = min(r, 512)
    assert r % tm == 0
    return pl.pallas_call(
        _compress_kernel,
        grid=(two, r // tm),
        in_specs=[pl.BlockSpec((1, tm, kdim), lambda t, i: (t, i, 0)),
                  pl.BlockSpec((1, 1, kdim), lambda t, i: (t, 0, 0)),
                  pl.BlockSpec((1, kdim, HEAD_DIM), lambda t, i: (t, 0, 0)),
                  pl.BlockSpec((1, HEAD_DIM, HEAD_DIM), lambda t, i: (t, 0, 0))],
        out_specs=pl.BlockSpec((1, tm, HEAD_DIM), lambda t, i: (t, i, 0)),
        out_shape=jax.ShapeDtypeStruct((two, r, HEAD_DIM), BF16),
        compiler_params=_params("parallel", "parallel"),
    )(flat, pe, w1, w2)


def _cmp_select_kernel(q_ref, kc_ref, vct_ref, o_ref, sel_ref, *, tq, nc, ncp, ns, nsp, scale):
    qi = pl.program_id(2)
    kc = kc_ref[0, 0]
    vct = vct_ref[0, 0]
    spos = qi * tq + lax.broadcasted_iota(jnp.int32, (1, tq), 1)
    cidx = lax.broadcasted_iota(jnp.int32, (ncp, 1), 0)
    valid = jnp.logical_and(cidx * CMP_STRIDE + (CMP_BLOCK - 1) <= spos, cidx < nc)
    psum = jnp.zeros((ncp, tq), F32)
    for g in range(NSA_GROUP):
        qg = q_ref[0, :, g * HEAD_DIM:(g + 1) * HEAD_DIM]
        st = _dot_nt(kc, qg) * scale
        st = jnp.where(valid, st, MASKED)
        mx = jnp.max(st, axis=0, keepdims=True)
        e = jnp.where(valid, jnp.exp(st - mx), 0.0)
        den = jnp.sum(e, axis=0, keepdims=True)
        p = e * (1.0 / jnp.where(den > 0, den, 1.0))
        psum = psum + p
        og_t = _dot(vct, p.astype(BF16))
        o_ref[0, :, g * HEAD_DIM:(g + 1) * HEAD_DIM] = og_t.T.astype(o_ref.dtype)
    jrow = lax.broadcasted_iota(jnp.int32, (ns, ncp), 0)
    ccol = lax.broadcasted_iota(jnp.int32, (ns, ncp), 1)
    c0 = ccol * CMP_STRIDE
    j0 = jrow * SLC_BLOCK
    cover_t = jnp.logical_and(c0 < j0 + SLC_BLOCK, c0 + CMP_BLOCK > j0).astype(F32)
    hi = psum.astype(BF16)
    rem = psum - hi.astype(F32)
    mid = rem.astype(BF16)
    lo = (rem - mid.astype(F32)).astype(BF16)
    imp3 = _dot(cover_t.astype(BF16), jnp.concatenate([hi, mid, lo], axis=1))
    imp = imp3[:, :tq] + imp3[:, tq:2 * tq] + imp3[:, 2 * tq:]
    j = lax.broadcasted_iota(jnp.int32, (ns, 1), 0).astype(F32)
    cur = (spos // SLC_BLOCK).astype(F32)
    forced = jnp.logical_or(jnp.logical_or(j == 0, j == cur), j == cur - 1)
    score = jnp.where(j <= cur, imp + FORCE_BONUS * forced.astype(F32), -jnp.inf)
    sel = jnp.zeros((ns, tq), F32)
    for _ in range(SLC_TOPN):
        mx = jnp.max(score, axis=0, keepdims=True)
        first = jnp.min(jnp.where(score == mx, j, float(ns)), axis=0, keepdims=True)
        pick = j == first
        sel = jnp.where(pick, 1.0, sel)
        score = jnp.where(pick, -jnp.inf, score)
    if ns < nsp:
        sel = jnp.concatenate([sel, jnp.zeros((nsp - ns, tq), F32)], axis=0)
    sel_ref[0, 0] = sel.T.astype(sel_ref.dtype)


def cmp_select(q, kc, vct, *, nc, tq):
    b, s, _ = q.shape
    ncp = kc.shape[2]
    nsp = LANE
    assert s % tq == 0 and s // SLC_BLOCK <= nsp
    gw = NSA_GROUP * HEAD_DIM
    return pl.pallas_call(
        functools.partial(_cmp_select_kernel, tq=tq, nc=nc, ncp=ncp, ns=-(-(s // SLC_BLOCK) // 8) * 8, nsp=nsp,
                          scale=HEAD_DIM ** -0.5),
        grid=(b, NSA_KV_HEADS, s // tq),
        in_specs=[pl.BlockSpec((1, tq, gw), lambda bi, h, i: (bi, i, h)),
                  pl.BlockSpec((1, 1, ncp, HEAD_DIM), lambda bi, h, i: (bi, h, 0, 0)),
                  pl.BlockSpec((1, 1, HEAD_DIM, ncp), lambda bi, h, i: (bi, h, 0, 0))],
        out_specs=[pl.BlockSpec((1, tq, gw), lambda bi, h, i: (bi, i, h)),
                   pl.BlockSpec((1, 1, tq, nsp), lambda bi, h, i: (bi, h, i, 0))],
        out_shape=[jax.ShapeDtypeStruct((b, s, NSA_HEADS * HEAD_DIM), BF16),
                   jax.ShapeDtypeStruct((b, NSA_KV_HEADS, s, nsp), BF16)],
        compiler_params=_params("parallel", "parallel", "parallel"),
    )(q, kc, vct)


def _slc_kernel(qi_ref, ki_ref, q_ref, k_ref, v_ref, sel_ref, cos_ref, sin_ref, o_ref, qaug_ref, m_ref, acc_ref,
                *, tq, scale):
    step = pl.program_id(2)
    qi = qi_ref[step]
    ki = ki_ref[step]

    @pl.when(ki == 0)
    def _():
        cosf = cos_ref[0]
        sinf = sin_ref[0]
        pick = sel_ref[0, 0] * SLC_PICK_BIAS
        for g in range(NSA_GROUP):
            sl = slice(g * HEAD_DIM, (g + 1) * HEAD_DIM)
            qg = _rope(q_ref[0, :, sl].astype(F32), cosf, sinf) * scale
            qaug_ref[g] = jnp.concatenate([qg.astype(BF16), pick], axis=1)
        m_ref[...] = jnp.full_like(m_ref, MASKED)
        acc_ref[...] = jnp.zeros_like(acc_ref)

    def accumulate(causal):
        nsp = sel_ref.shape[3]
        block_of_key = (ki * tq + lax.broadcasted_iota(jnp.int32, (tq, nsp), 0)) // SLC_BLOCK
        onehot = (block_of_key == lax.broadcasted_iota(jnp.int32, (tq, nsp), 1)).astype(BF16)
        kaug = jnp.concatenate([k_ref[0], onehot], axis=1)
        vaug = jnp.concatenate([v_ref[0], jnp.ones((tq, LANE), BF16)], axis=1)
        if causal:
            qpos = lax.broadcasted_iota(jnp.int32, (tq, tq), 0)
            kpos = lax.broadcasted_iota(jnp.int32, (tq, tq), 1)
            visible = kpos <= qpos
        heads = range(NSA_GROUP)
        scores = [_dot_nt(qaug_ref[g], kaug) for g in heads]
        if causal:
            scores = [jnp.where(visible, s, MASKED) for s in scores]
        m_olds = [m_ref[g] for g in heads]
        m_news = [jnp.maximum(mo, jnp.max(s, axis=-1, keepdims=True)) for mo, s in zip(m_olds, scores)]
        alphas = [jnp.exp(mo - mn) for mo, mn in zip(m_olds, m_news)]
        probs = [jnp.exp((s - jnp.concatenate([mn] * (tq // LANE), axis=1)).astype(BF16))
                 for s, mn in zip(scores, m_news)]
        pvs = [_dot(p, vaug) for p in probs]
        for g in heads:
            acc_ref[g] = jnp.concatenate([alphas[g], alphas[g]], axis=1) * acc_ref[g] + pvs[g]
            m_ref[g] = m_news[g]

    @pl.when(ki < qi)
    def _():
        accumulate(False)

    @pl.when(ki == qi)
    def _():
        accumulate(True)
        for g in range(NSA_GROUP):
            sl = slice(g * HEAD_DIM, (g + 1) * HEAD_DIM)
            o_ref[0, :, sl] = (acc_ref[g, :, :HEAD_DIM] / acc_ref[g, :, HEAD_DIM:]).astype(o_ref.dtype)


def slc_attention(q, kv, sel, cosf, sinf, *, k_col, v_col, tq):
    b, s, _ = q.shape
    gw = NSA_GROUP * HEAD_DIM
    nq = s // tq
    nsp = sel.shape[3]
    pairs = [(i, j) for i in range(nq) for j in range(i + 1)]
    qi_tab = jnp.asarray([p[0] for p in pairs], jnp.int32)
    ki_tab = jnp.asarray([p[1] for p in pairs], jnp.int32)
    grid_spec = pltpu.PrefetchScalarGridSpec(
        num_scalar_prefetch=2,
        grid=(b, NSA_KV_HEADS, len(pairs)),
        in_specs=[pl.BlockSpec((1, tq, gw), lambda bi, h, t, qt, kt: (bi, qt[t], h)),
                  pl.BlockSpec((1, tq, HEAD_DIM), lambda bi, h, t, qt, kt: (bi, kt[t], k_col + h)),
                  pl.BlockSpec((1, tq, HEAD_DIM), lambda bi, h, t, qt, kt: (bi, kt[t], v_col + h)),
                  pl.BlockSpec((1, 1, tq, nsp), lambda bi, h, t, qt, kt: (bi, h, qt[t], 0)),
                  pl.BlockSpec((1, tq, LANE), lambda bi, h, t, qt, kt: (bi, qt[t], 0)),
                  pl.BlockSpec((1, tq, LANE), lambda bi, h, t, qt, kt: (bi, qt[t], 0))],
        out_specs=pl.BlockSpec((1, tq, gw), lambda bi, h, t, qt, kt: (bi, qt[t], h)),
        scratch_shapes=[pltpu.VMEM((NSA_GROUP, tq, 2 * HEAD_DIM), BF16),
                        pltpu.VMEM((NSA_GROUP, tq, LANE), F32),
                        pltpu.VMEM((NSA_GROUP, tq, 2 * HEAD_DIM), F32)],
    )
    return pl.pallas_call(
        functools.partial(_slc_kernel, tq=tq, scale=HEAD_DIM ** -0.5),
        grid_spec=grid_spec,
        out_shape=jax.ShapeDtypeStruct((b, s, NSA_HEADS * HEAD_DIM), BF16),
        compiler_params=_params("parallel", "parallel", "arbitrary"),
    )(qi_tab, ki_tab, q, kv, kv, sel, cosf, sinf)


def _band_kernel(*refs, nheads, kv_heads, nkv, tq, max_dist, rope_q, with_lse, scale, nsub):
    q_ref = refs[0]
    k_refs = refs[1:1 + nkv]
    v_refs = refs[1 + nkv:1 + 2 * nkv]
    pos = 1 + 2 * nkv
    if rope_q:
        cos_ref, sin_ref = refs[pos:pos + 2]
        pos += 2
    o_ref = refs[pos]
    lse_ref = refs[pos + 1] if with_lse else None
    qi = pl.program_id(2)
    qpos = qi * tq + lax.broadcasted_iota(jnp.int32, (tq, 1), 0)
    kpos = (qi - (nkv - 1)) * tq + lax.broadcasted_iota(jnp.int32, (1, nkv * tq), 1)
    diff = qpos - kpos
    mask = jnp.logical_and(jnp.logical_and(diff >= 0, diff <= max_dist), kpos >= 0)
    lane = lax.broadcasted_iota(jnp.int32, (tq, LANE), 1)
    jobs = [(u, g) for u in range(nsub) for g in range(nheads)]

    def rows(u):
        return slice(u * tq, (u + 1) * tq)

    def cols(g, kv=False):
        return slice(0, HEAD_DIM) if (kv and kv_heads == 1) else slice(g * HEAD_DIM, (g + 1) * HEAD_DIM)

    scores = []
    for u, g in jobs:
        q = q_ref[0, rows(u), cols(g)]
        if rope_q:
            q = _rope(q.astype(F32), cos_ref[0], sin_ref[0]).astype(BF16)
        kcat = jnp.concatenate([r[0, rows(u), cols(g, True)] for r in k_refs], axis=0)
        scores.append(jnp.where(mask, _dot_nt(q, kcat) * scale, MASKED))
    maxes = [jnp.max(s, axis=-1, keepdims=True) for s in scores]
    probs = [jnp.exp((s - mx).astype(BF16)) for s, mx in zip(scores, maxes)]
    pvs = []
    for (u, g), e in zip(jobs, probs):
        vcat = jnp.concatenate([r[0, rows(u), cols(g, True)] for r in v_refs], axis=0)
        pvs.append(_dot(e, jnp.concatenate([vcat, jnp.ones(vcat.shape, BF16)], axis=1)))
    lse_acc = [jnp.zeros((tq, LANE), F32) for _ in range(nsub)]
    for (u, g), pv, mx in zip(jobs, pvs, maxes):
        den = pv[:, HEAD_DIM:]
        o_ref[0, rows(u), cols(g)] = (pv[:, :HEAD_DIM] / den).astype(o_ref.dtype)
        if with_lse:
            lse_acc[u] = jnp.where(lane == g, mx + jnp.log(den), lse_acc[u])
    if with_lse:
        for u in range(nsub):
            lse_ref[0, rows(u)] = lse_acc[u]


def band_attention(q, k, v, *, nheads, kv_heads, q_col, k_col, v_col, o_cols, ncol, max_dist, tq,
                   rope=None, with_lse=False, out_dtype=BF16, seq_len=None, row_block=None, nsub=1):
    b = q.shape[0]
    rows = q.shape[1]
    seq_len = rows if seq_len is None else seq_len
    tq = min(tq, seq_len)
    assert seq_len % tq == 0 and ncol % nsub == 0
    nkv = -(-max_dist // tq) + 1
    qw = nheads * HEAD_DIM
    kw = kv_heads * HEAD_DIM
    bt = nsub * tq
    if row_block is None:
        assert nsub == 1
        row_block = lambda c, i: i
        o_col = lambda c: c
    else:
        assert rope is None
        o_col = lambda c: 0

    def kv_spec(col_fn, back):
        return pl.BlockSpec((1, bt, kw), lambda bi, c, i: (bi, row_block(c, jnp.maximum(i - back, 0)), col_fn(c)))

    in_specs = [pl.BlockSpec((1, bt, qw), lambda bi, c, i: (bi, row_block(c, i), q_col(c)))]
    in_specs += [kv_spec(k_col, nkv - 1 - t) for t in range(nkv)]
    in_specs += [kv_spec(v_col, nkv - 1 - t) for t in range(nkv)]
    args = [q] + [k] * nkv + [v] * nkv
    if rope is not None:
        in_specs += [pl.BlockSpec((1, tq, LANE), lambda bi, c, i: (bi, i, 0))] * 2
        args += list(rope)
    out_specs = [pl.BlockSpec((1, bt, qw), lambda bi, c, i: (bi, row_block(c, i), o_col(c)))]
    out_shape = [jax.ShapeDtypeStruct((b, rows, o_cols * qw), out_dtype)]
    if with_lse:
        out_specs.append(pl.BlockSpec((1, bt, LANE), lambda bi, c, i: (bi, row_block(c, i), o_col(c))))
        out_shape.append(jax.ShapeDtypeStruct((b, rows, o_cols * LANE), F32))
    res = pl.pallas_call(
        functools.partial(_band_kernel, nheads=nheads, kv_heads=kv_heads, nkv=nkv, tq=tq, max_dist=max_dist,
                          rope_q=rope is not None, with_lse=with_lse, scale=HEAD_DIM ** -0.5, nsub=nsub),
        grid=(b, ncol // nsub, seq_len // tq),
        in_specs=in_specs,
        out_specs=out_specs,
        out_shape=out_shape,
        compiler_params=_params("parallel", "parallel", "parallel"),
    )(*args)
    return res if with_lse else res[0]


def _bmm(x, y):
    return jnp.einsum("bij,bjk->bik", x.astype(BF16), y.astype(BF16), preferred_element_type=F32)


def _bmm_nt(x, y):
    return jnp.einsum("bik,bjk->bij", x.astype(BF16), y.astype(BF16), preferred_element_type=F32)


def _head_sums(x2, scale=1.0):
    n = x2.shape[1]
    blk = (lax.broadcasted_iota(jnp.int32, (n, n), 0) // RWKV_HEAD
           == lax.broadcasted_iota(jnp.int32, (n, n), 1) // RWKV_HEAD)
    ones = jnp.where(blk, scale, 0.0).astype(BF16)
    hi = x2.astype(BF16)
    lo = (x2 - hi.astype(F32)).astype(BF16)
    return _dot(hi, ones) + _dot(lo, ones)


def _wkv_chunk_kernel(*refs, nb, c, mix):
    if mix:
        zr_ref, zk_ref, zv_ref, wl_ref, al_ref, vu_ref, vf_ref, vec_ref = refs[:8]
    else:
        zr_ref, zk_ref, zv_ref, wl_ref, al_ref, vec_ref = refs[:6]
    q_ref, y0_ref, gt_ref, ht_ref, bonus_ref = refs[-5:]
    pair = 2 * RWKV_HEAD
    c2 = 2 * c
    rows = nb * c
    vec = vec_ref[...]
    w0, a0, k_k, k_a, r_k, v0 = [vec[i:i + 1] for i in range(6)]
    r2 = zr_ref[0]
    kraw = zk_ref[0]
    v2 = zv_ref[0]
    x = w0 + wl_ref[0]
    softplus_neg = jnp.maximum(-x, 0.0) + jnp.log(1.0 + jnp.exp(-jnp.abs(x)))
    lw2 = -jnp.exp(-softplus_neg - 0.5)
    a_gate = _sigmoid(a0 + al_ref[0])
    if mix:
        v2 = v2 + (vf_ref[0] - v2) * _sigmoid(v0 + vu_ref[0])
    kk = kraw * k_k
    kk = kk * lax.rsqrt(jnp.maximum(_head_sums(kk * kk), 1e-24))
    k2 = kraw * (1.0 + (a_gate - 1.0) * k_a)
    bonus_ref[0] = (_head_sums(r2 * k2 * r_k) * v2).astype(bonus_ref.dtype)

    def chunks(x2):
        return x2.reshape(nb, c, pair)

    r, lw, k, v, a, b = [chunks(t) for t in (r2, lw2, k2, v2, -kk, kk * a_gate)]
    row = lax.broadcasted_iota(jnp.int32, (c, c), 0)
    col = lax.broadcasted_iota(jnp.int32, (c, c), 1)
    tril = jnp.broadcast_to((row >= col).astype(BF16), (nb, c, c))
    hi = lw.astype(BF16)
    rem = lw - hi.astype(F32)
    mid = rem.astype(BF16)
    lo = (rem - mid.astype(F32)).astype(BF16)
    cum3 = _bmm(tril, jnp.concatenate([hi, mid, lo], axis=2))
    cum = cum3[:, :, :pair] + cum3[:, :, pair:2 * pair] + cum3[:, :, 2 * pair:]
    last = cum[:, c - 1:c, :]
    inv = jnp.exp(-cum)
    tail = jnp.exp(last - cum)
    lane = lax.broadcasted_iota(jnp.int32, (1, 1, pair), 2)
    first = lane < RWKV_HEAD

    def stack(x):
        return jnp.concatenate([jnp.where(first, x, 0.0), jnp.where(first, 0.0, x)], axis=1)

    a_s = stack(a * jnp.exp(cum - lw))
    r_s = stack(r * jnp.exp(cum))
    b_s = stack(b * inv)
    k_s = stack(k * inv)
    v_s = stack(v)
    bh_s = stack(b * tail)
    kh_s = stack(k * tail)
    ar = jnp.concatenate([a_s, r_s], axis=1)
    pbk = _bmm_nt(ar, jnp.concatenate([b_s, k_s], axis=1))
    pb = pbk[:, :, :c2]
    pk = pbk[:, :, c2:]
    row2 = lax.broadcasted_iota(jnp.int32, (c2, c2), 0) % c
    col2 = lax.broadcasted_iota(jnp.int32, (c2, c2), 1) % c
    strict = row2 > col2
    lower = row2 >= col2
    l_ab = jnp.where(strict, pb[:, :c2], 0.0)
    m_rb = jnp.where(lower, pb[:, c2:], 0.0)
    l_ak = jnp.where(strict, pk[:, :c2], 0.0)
    m_rk = jnp.where(lower, pk[:, c2:], 0.0)
    eye = (lax.broadcasted_iota(jnp.int32, (c2, c2), 0) == lax.broadcasted_iota(jnp.int32, (c2, c2), 1))
    eye = eye.astype(F32)
    tinv = eye + l_ab
    pw = _bmm(l_ab, l_ab)
    for _ in range(int(np.log2(c)) - 2):
        both = _bmm(jnp.concatenate([tinv, pw], axis=1), pw)
        tinv = tinv + both[:, :c2]
        pw = both[:, c2:]
    tinv = tinv + _bmm(tinv, pw)
    tu = _bmm(tinv, jnp.concatenate([_bmm(l_ak, v_s), a_s], axis=2))
    u0_s = tu[:, :, :pair]
    ta_s = tu[:, :, pair:]
    mu = _bmm(m_rb, jnp.concatenate([ta_s, u0_s], axis=2))
    q_s = r_s + mu[:, :, :pair]
    y0_s = _bmm(m_rk, v_s) + mu[:, :, pair:]
    q_ref[0] = (q_s[:, :c] + q_s[:, c:]).reshape(nb * c, pair).astype(q_ref.dtype)
    y0_ref[0] = (y0_s[:, :c] + y0_s[:, c:]).reshape(nb * c, pair).astype(y0_ref.dtype)
    eye_b = jnp.broadcast_to(eye.astype(BF16), (nb, c2, c2))
    tr = _bmm_nt(eye_b, jnp.concatenate([bh_s, kh_s], axis=1))
    bh_t = tr[:, :, :c2]
    kh_t = tr[:, :, c2:]
    gh = _bmm(bh_t, jnp.concatenate([ta_s, u0_s], axis=2))
    gt_ref[0, :, 0] = (eye * jnp.exp(last) + gh[:, :, :pair]).astype(gt_ref.dtype)
    ht_ref[0, :, 0] = (gh[:, :, pair:] + _bmm(kh_t, v_s)).astype(ht_ref.dtype)


def _wkv_scan_kernel(q_ref, y0_ref, gt_ref, ht_ref, bonus_ref, g_ref, lnx_ref, o_ref, state_ref, *, npair):
    pair = 2 * RWKV_HEAD
    bsz, c, _ = q_ref.shape

    @pl.when(pl.program_id(0) == 0)
    def _():
        state_ref[...] = jnp.zeros_like(state_ref)

    sls = [slice(p * pair, (p + 1) * pair) for p in range(npair)]
    idx = [(bi, p) for bi in range(bsz) for p in range(npair)]
    st = [state_ref[bi, p].astype(BF16) for bi, p in idx]
    ys = [_dot(q_ref[bi, :, sls[p]], s_) + y0_ref[bi, :, sls[p]].astype(F32) for (bi, p), s_ in zip(idx, st)]
    new = [_dot(gt_ref[bi, 0, p], s_) + ht_ref[bi, 0, p].astype(F32) for (bi, p), s_ in zip(idx, st)]
    for (bi, p), s_ in zip(idx, new):
        state_ref[bi, p] = s_
    y = jnp.concatenate(ys, axis=0)
    dev = y - _head_sums(y, 1.0 / RWKV_HEAD)
    yn = dev * lax.rsqrt(_head_sums(dev * dev, 1.0 / RWKV_HEAD) + LNX_EPS)
    for n, (bi, p) in enumerate(idx):
        ln = yn[n * c:(n + 1) * c] * lnx_ref[0:1, sls[p]] + lnx_ref[1:2, sls[p]]
        out = (ln + bonus_ref[bi, :, sls[p]].astype(F32)) * g_ref[bi, :, sls[p]].astype(F32)
        o_ref[bi, :, sls[p]] = out.astype(o_ref.dtype)


def wkv7(zs, zs_first, wl, al, vu, vec, g, lnx):
    bsz, s, _ = zs.shape
    width = wl.shape[2]
    pair = 2 * RWKV_HEAD
    npair = width // pair
    c = min(RWKV_CHUNK, s)
    nb = min(RWKV_CHUNKS_PER_STEP, s // c)
    nch = s // c
    mix = vu is not None
    assert s % (nb * c) == 0 and width % pair == 0

    def col(off):
        return pl.BlockSpec((1, nb * c, pair), lambda bi, p, j: (bi, j, off + p))

    vec_spec = pl.BlockSpec((8, pair), lambda bi, p, j: (0, p))
    mat = pl.BlockSpec((1, nb, 1, pair, pair), lambda bi, p, j: (bi, j, p, 0, 0))
    in_specs = [col(0), col(npair), col(2 * npair), col(0), col(0)]
    args = [zs, zs, zs, wl, al]
    if mix:
        in_specs += [col(0), col(2 * npair)]
        args += [vu, zs_first]
    q, y0, gt, ht, bonus = pl.pallas_call(
        functools.partial(_wkv_chunk_kernel, nb=nb, c=c, mix=mix),
        grid=(bsz, npair, nch // nb),
        in_specs=in_specs + [vec_spec],
        out_specs=[col(0), col(0), mat, mat, col(0)],
        out_shape=[jax.ShapeDtypeStruct((bsz, s, width), BF16), jax.ShapeDtypeStruct((bsz, s, width), BF16),
                   jax.ShapeDtypeStruct((bsz, nch, npair, pair, pair), BF16),
                   jax.ShapeDtypeStruct((bsz, nch, npair, pair, pair), BF16),
                   jax.ShapeDtypeStruct((bsz, s, width), BF16)],
        compiler_params=_params("parallel", "parallel", "parallel"),
    )(*args, vec)
    row = pl.BlockSpec((bsz, c, width), lambda j: (0, j, 0))
    mats = pl.BlockSpec((bsz, 1, npair, pair, pair), lambda j: (0, j, 0, 0, 0))
    return pl.pallas_call(
        functools.partial(_wkv_scan_kernel, npair=npair),
        grid=(nch,),
        in_specs=[row, row, mats, mats, row, row, pl.BlockSpec((8, width), lambda j: (0, 0))],
        out_specs=row,
        out_shape=jax.ShapeDtypeStruct((bsz, s, width), BF16),
        scratch_shapes=[pltpu.VMEM((bsz, npair, pair, pair), F32)],
        compiler_params=_params("arbitrary"),
    )(q, y0, gt, ht, bonus, g, lnx)


def _moe_up_kernel(te_ref, tv_ref, a_ref, wg_ref, wu_ref, o_ref, wg_bf, wu_bf):
    i = pl.program_id(1)
    changed = jnp.logical_or(i == 0, te_ref[i] != te_ref[jnp.maximum(i - 1, 0)])

    @pl.when(changed)
    def _():
        wg_bf[...] = wg_ref[...].astype(BF16)
        wu_bf[...] = wu_ref[...].astype(BF16)

    @pl.when(tv_ref[i] > 0)
    def _():
        a = a_ref[...]
        g = _dot(a, wg_bf[...])
        u = _dot(a, wu_bf[...])
        o_ref[...] = (_silu(g) * u).astype(o_ref.dtype)

    @pl.when(tv_ref[i] == 0)
    def _():
        o_ref[...] = jnp.zeros_like(o_ref)


def moe_up(tile_expert, tile_valid, xs, w_gu, layer, *, tm, tn):
    r, k = xs.shape
    f = w_gu.shape[3] // 2
    nj = f // tn
    grid_spec = pltpu.PrefetchScalarGridSpec(
        num_scalar_prefetch=2,
        grid=(nj, r // tm),
        in_specs=[pl.BlockSpec((tm, k), lambda j, i, te, tv: (i, 0)),
                  pl.BlockSpec((None, None, k, tn), lambda j, i, te, tv: (layer, te[i], 0, j)),
                  pl.BlockSpec((None, None, k, tn), lambda j, i, te, tv: (layer, te[i], 0, j + nj))],
        out_specs=pl.BlockSpec((tm, tn), lambda j, i, te, tv: (i, j)),
        scratch_shapes=[pltpu.VMEM((k, tn), BF16), pltpu.VMEM((k, tn), BF16)],
    )
    return pl.pallas_call(
        _moe_up_kernel,
        grid_spec=grid_spec,
        out_shape=jax.ShapeDtypeStruct((r, f), BF16),
        compiler_params=_params("parallel", "arbitrary"),
    )(tile_expert, tile_valid, xs, w_gu, w_gu)


def _moe_down_kernel(te_ref, tv_ref, a_ref, w_ref, o_ref, w_bf):
    i = pl.program_id(1)
    changed = jnp.logical_or(i == 0, te_ref[i] != te_ref[jnp.maximum(i - 1, 0)])

    @pl.when(changed)
    def _():
        w_bf[...] = w_ref[...].astype(BF16)

    @pl.when(tv_ref[i] > 0)
    def _():
        o_ref[...] = _dot(a_ref[...], w_bf[...]).astype(o_ref.dtype)

    @pl.when(tv_ref[i] == 0)
    def _():
        o_ref[...] = jnp.zeros_like(o_ref)


def moe_down(tile_expert, tile_valid, act, w_down, layer, *, tm, tn):
    r, f = act.shape
    d = w_down.shape[3]
    grid_spec = pltpu.PrefetchScalarGridSpec(
        num_scalar_prefetch=2,
        grid=(d // tn, r // tm),
        in_specs=[pl.BlockSpec((tm, f), lambda j, i, te, tv: (i, 0)),
                  pl.BlockSpec((None, None, f, tn), lambda j, i, te, tv: (layer, te[i], 0, j))],
        out_specs=pl.BlockSpec((tm, tn), lambda j, i, te, tv: (i, j)),
        scratch_shapes=[pltpu.VMEM((f, tn), BF16)],
    )
    return pl.pallas_call(
        _moe_down_kernel,
        grid_spec=grid_spec,
        out_shape=jax.ShapeDtypeStruct((r, d), BF16),
        compiler_params=_params("parallel", "arbitrary"),
    )(tile_expert, tile_valid, act, w_down)


def _rope_tables(positions):
    inv_freq = ROPE_THETA ** (-jnp.arange(ROPE_HALF, dtype=F32) / ROPE_HALF)
    ang = positions.astype(F32)[:, :, None] * inv_freq
    cos = jnp.cos(ang)
    sin = jnp.sin(ang)
    b, s = positions.shape
    pad1 = jnp.ones((b, s, HEAD_DIM - ROPE_DIM), F32)
    pad0 = jnp.zeros((b, s, HEAD_DIM - ROPE_DIM), F32)
    return jnp.concatenate([cos, cos, pad1], axis=-1), jnp.concatenate([-sin, sin, pad0], axis=-1)


def _pad_cols(w, n):
    return w if w.shape[-1] == n else jnp.pad(w, ((0, 0), (0, n - w.shape[-1])))


def nsa_branch(q_a, kv_a, cosf, sinf, cmp_pe, cmp_w1, cmp_w2):
    b, s, _ = q_a.shape
    nc = s // CMP_STRIDE - 1
    ncp = -(-nc // LANE) * LANE
    kvw = NSA_KV_HEADS * HEAD_DIM

    def blocks(t):
        ch = t.reshape(b, s // CMP_STRIDE, CMP_STRIDE, NSA_KV_HEADS, HEAD_DIM)
        blk = jnp.concatenate([ch[:, :-1], ch[:, 1:]], axis=2)
        return blk.transpose(0, 1, 3, 2, 4).reshape(b * nc * NSA_KV_HEADS, CMP_BLOCK * HEAD_DIM)

    rows = b * nc * NSA_KV_HEADS
    rows_p = -(-rows // 512) * 512 if rows > 512 else -(-rows // 8) * 8
    flat = jnp.stack([blocks(kv_a[..., :kvw]), blocks(kv_a[..., kvw:2 * kvw])]).astype(F32)
    flat = jnp.pad(flat, ((0, 0), (0, rows_p - rows), (0, 0)))
    comp = nsa_compress(flat, cmp_pe.reshape(2, 1, CMP_BLOCK * HEAD_DIM), cmp_w1.astype(BF16), cmp_w2.astype(BF16))
    comp = comp[:, :rows].reshape(2, b, nc, NSA_KV_HEADS, HEAD_DIM).transpose(0, 1, 3, 2, 4)
    comp = jnp.pad(comp, ((0, 0), (0, 0), (0, 0), (0, ncp - nc), (0, 0)))
    kc = comp[0]
    vct = comp[1].transpose(0, 1, 3, 2)
    tq = min(256, s)
    o_cmp, sel = cmp_select(q_a, kc, vct, nc=nc, tq=tq)
    o_slc = slc_attention(q_a, kv_a, sel, cosf, sinf, k_col=4, v_col=6, tq=min(512, s))
    o_win = band_attention(q_a, kv_a, kv_a, nheads=NSA_GROUP, kv_heads=1, q_col=lambda c: c,
                           k_col=lambda c: 8 + c, v_col=lambda c: 10 + c, o_cols=NSA_KV_HEADS,
                           ncol=NSA_KV_HEADS, max_dist=WIN_SIZE - 1, tq=tq, rope=(cosf, sinf))
    return o_cmp, o_slc, o_win


def rwkv_branch(zs, zs_first, vec, w_up, a_up, g_up, v_res):
    b, s, _ = zs.shape
    t = b * s
    w_ = RWKV_WIDTH
    o = 3 * w_
    zw, za = zs[..., o:o + LORA_W], zs[..., o + LORA_W:o + LORA_W + LORA_A]
    zg = zs[..., o + LORA_W + LORA_A:o + LORA_W + LORA_A + LORA_G]
    w0, a0, k_k, k_a, r_k, lnx_g, lnx_b = [vec[i] for i in range(7)]

    def lora(xin, wmat, out_dtype=F32):
        return matmul(xin.reshape(t, -1).astype(BF16), wmat.astype(BF16), out_dtype=out_dtype, tm=1024,
                      tn=wmat.shape[1]).reshape(b, s, -1)

    wl = lora(jnp.tanh(zw), w_up)
    al = lora(za, a_up)
    g = lora(jax.nn.sigmoid(zg), g_up, BF16)
    zero = jnp.zeros_like(w0)
    if v_res is None:
        vu, v0 = None, zero
    else:
        v0, v_down, v_up = v_res
        vd = matmul(zs.reshape(t, -1), _pad_cols(v_down, LANE).astype(BF16), out_dtype=BF16, tm=1024, tn=LANE,
                    a_col=2)
        vu = matmul(vd, jnp.pad(v_up, ((0, LANE - v_up.shape[0]), (0, 0))).astype(BF16), out_dtype=F32, tm=1024,
                    tn=512).reshape(b, s, w_)
    vecs = jnp.stack([w0, a0, k_k, k_a, r_k, v0, zero, zero])
    lnx = jnp.stack([lnx_g, lnx_b] + [zero] * 6)
    return wkv7(zs, zs_first, wl, al, vu, vecs, g, lnx)


DIL_ROW_TILE = 1024


def residue_perm(dil, transpose=False):
    i = np.arange(DIL_ROW_TILE)
    per = DIL_ROW_TILE // dil
    p = np.zeros((DIL_ROW_TILE, DIL_ROW_TILE), np.float32)
    p[i, (i % per) * dil + i // per] = 1.0
    return jnp.asarray(p.T if transpose else p, BF16)


def _dilated_merge_kernel(*refs, ngroups, nperm):
    o_refs = refs[:ngroups]
    l_refs = refs[ngroups:2 * ngroups]
    p_refs = refs[2 * ngroups:2 * ngroups + nperm]
    y_ref = refs[-1]
    outs, lses = [], []
    for g in range(ngroups):
        if g < ngroups - nperm:
            outs.append(o_refs[g][...].astype(F32))
            lses.append(l_refs[g][...])
            continue
        pt = p_refs[g - (ngroups - nperm)][...]
        outs.append(_dot(pt, o_refs[g][...]))
        lse = l_refs[g][...]
        hi = lse.astype(BF16)
        lo = (lse - hi.astype(F32)).astype(BF16)
        both = _dot(pt, jnp.concatenate([hi, lo], axis=1))
        lses.append(both[:, :LANE] + both[:, LANE:])
    mx = functools.reduce(jnp.maximum, lses)
    es = [jnp.exp(lse - mx) for lse in lses]
    inv = 1.0 / functools.reduce(jnp.add, es)
    for hd in range(DIL_HPG):
        sl = slice(hd * HEAD_DIM, (hd + 1) * HEAD_DIM)
        acc = functools.reduce(jnp.add, [e[:, hd:hd + 1] * o[:, sl] for e, o in zip(es, outs)])
        y_ref[:, sl] = (acc * inv[:, hd:hd + 1]).astype(y_ref.dtype)


def dilated_branch(qkvs):
    b, s, width = qkvs[0].shape
    t = b * s
    assert s % DIL_ROW_TILE == 0
    outs, lses, perms = [], [], []
    for gi, (win, dil) in enumerate(DIL_PATTERNS):
        common = dict(nheads=DIL_HPG, kv_heads=DIL_HPG, q_col=lambda c: 0, k_col=lambda c: 1, v_col=lambda c: 2,
                      o_cols=1, max_dist=win // dil, with_lse=True, out_dtype=BF16)
        if dil == 1:
            assert not perms
            o, lse = band_attention(qkvs[gi], qkvs[gi], qkvs[gi], ncol=1, tq=256, **common)
        else:
            tq = DIL_ROW_TILE // dil
            nsub = max(1, 256 // tq)
            o, lse = band_attention(qkvs[gi], qkvs[gi], qkvs[gi], ncol=dil, tq=tq, seq_len=s // dil, nsub=nsub,
                                    row_block=lambda c, i, per=dil // nsub: i * per + c, **common)
            perms.append(residue_perm(dil, transpose=True))
        outs.append(o.reshape(t, DIL_OUT))
        lses.append(lse.reshape(t, LANE))
    ng = len(DIL_PATTERNS)
    row = lambda width_: pl.BlockSpec((DIL_ROW_TILE, width_), lambda i: (i, 0))
    const = pl.BlockSpec((DIL_ROW_TILE, DIL_ROW_TILE), lambda i: (0, 0))
    return pl.pallas_call(
        functools.partial(_dilated_merge_kernel, ngroups=ng, nperm=len(perms)),
        grid=(t // DIL_ROW_TILE,),
        in_specs=[row(DIL_OUT)] * ng + [row(LANE)] * ng + [const] * len(perms),
        out_specs=row(DIL_OUT),
        out_shape=jax.ShapeDtypeStruct((t, DIL_OUT), BF16),
        compiler_params=_params("parallel"),
    )(*outs, *lses, *perms)


def moe_ffn(h, router_w, router_b, w_gu, w_down, layer, *, tm):
    t, d = h.shape
    logits = matmul(h, _pad_cols(router_w, LANE).astype(BF16), out_dtype=F32, tm=1024, tn=LANE)[:, :N_EXPERTS]
    logits = logits + router_b
    top_v, top_i = lax.top_k(logits, TOP_K)
    wts = jax.nn.softmax(top_v, axis=-1)
    flat_e = top_i.reshape(-1)
    onehot = (flat_e[:, None] == jnp.arange(N_EXPERTS)[None, :]).astype(jnp.int32)
    rank = jnp.take_along_axis(jnp.cumsum(onehot, axis=0), flat_e[:, None], axis=1)[:, 0] - 1
    counts = jnp.sum(onehot, axis=0)
    tiles_per = (counts + tm - 1) // tm
    tile_end = jnp.cumsum(tiles_per)
    group_start = (tile_end - tiles_per) * tm
    dest = group_start[flat_e] + rank
    ntiles = (TOP_K * t) // tm + N_EXPERTS
    rows = ntiles * tm
    row_token = jnp.zeros((rows,), jnp.int32).at[dest].set(jnp.arange(TOP_K * t, dtype=jnp.int32) // TOP_K)
    tile_ids = jnp.arange(ntiles, dtype=jnp.int32)
    tile_valid = (tile_ids < tile_end[-1]).astype(jnp.int32)
    tile_expert = jnp.minimum(jnp.searchsorted(tile_end, tile_ids, side="right"), N_EXPERTS - 1).astype(jnp.int32)
    xs = jnp.take(h, row_token, axis=0, mode="clip")
    act = moe_up(tile_expert, tile_valid, xs, w_gu, layer, tm=tm, tn=min(1024, w_gu.shape[3] // 2))
    out = moe_down(tile_expert, tile_valid, act, w_down, layer, tm=tm, tn=min(1024, d))
    slot_major = dest.reshape(t, TOP_K).T.reshape(-1)
    yab = jnp.take(out, slot_major, axis=0, mode="clip")
    return yab, jnp.pad(wts, ((0, 0), (0, LANE - TOP_K)))


def kernel(x, c, positions, ada_w, ada_b, norm_g, w_in, cmp_pe, cmp_w1, cmp_w2, rwkv_mu, rwkv_vec, w_up, a_up, g_up, v_res0, v_res_down, v_res_up, w_br_a, w_br_b, w_br_c, w_out, ffn_gu, ffn_down, router_w, router_b, moe_gu, moe_down):
    b, s, d = x.shape
    depth = ada_w.shape[0]
    t = b * s
    tm_row = min(512, s)
    cosf, sinf = _rope_tables(positions)
    cos_t = cosf.reshape(t, LANE)
    sin_t = sinf.reshape(t, LANE)

    cond = jnp.pad(jax.nn.silu(c), ((0, 8 - b % 8 if b % 8 else 0), (0, 0))).astype(BF16)
    mods = []
    for l in range(depth):
        mod = matmul(cond, ada_w, out_dtype=F32, tm=cond.shape[0], tn=d, layer=l)[:b] + ada_b[l]
        mods.append(mod.reshape(b, 6, 1, d))

    def mod_of(l, i):
        return mods[l][:, i]

    q_cols = NSA_HEADS * HEAD_DIM
    kv_cols = 6 * NSA_KV_HEADS * HEAD_DIM
    gate_cols = 3 * NSA_HEADS
    rwkv_cols = 3 * RWKV_WIDTH + LORA_W + LORA_A + LORA_G
    dil_cols = 3 * DIL_HEADS * HEAD_DIM
    offs = np.cumsum([0, q_cols, kv_cols, gate_cols, rwkv_cols, dil_cols, 3 * d]).tolist()
    rwkv_pad = -(-rwkv_cols // 512) * 512

    xf = x.reshape(t, d)
    h = norm_mod(xf, norm_g[0, 0][None], mod_of(0, 1), mod_of(0, 0), seq=s, tm=tm_row)
    zs_first = None
    for l in range(depth):
        wl = w_in[l]
        seg = lambda i: wl[:, offs[i]:offs[i + 1]]
        q_a = matmul(h, seg(0).astype(BF16), out_dtype=BF16, tm=1024, tn=q_cols)
        kv_a = matmul(h, seg(1).astype(BF16), out_dtype=BF16, tm=1024, tn=kv_cols, rope=(cos_t, sin_t, (4, 5, 8, 9)))
        gate_a = matmul(h, _pad_cols(seg(2), LANE).astype(BF16), out_dtype=F32, tm=1024, tn=LANE, act="sigmoid")
        zs = matmul_token_shift(h, _pad_cols(seg(3), rwkv_pad).astype(BF16), _pad_cols(rwkv_mu[l][None], rwkv_pad),
                                seq=s, tm=1024, tn=rwkv_pad // 4).reshape(b, s, -1)
        zs_first = zs if l == 0 else zs_first
        wc = seg(4).reshape(d, 3, len(DIL_PATTERNS), DIL_OUT)
        qkv_c = [matmul(h, wc[:, :, gi].reshape(d, 3 * DIL_OUT).astype(BF16), out_dtype=BF16, tm=DIL_ROW_TILE,
                        tn=3 * DIL_OUT, rope=(cos_t, sin_t, tuple(range(2 * DIL_HPG))),
                        row_perm=None if dil == 1 else residue_perm(dil)).reshape(b, s, -1)
                 for gi, (_, dil) in enumerate(DIL_PATTERNS)]
        mg = matmul(h, seg(5).astype(BF16), out_dtype=BF16, tm=1024, tn=1024, act="sigmoid")

        o_cmp, o_slc, o_win = nsa_branch(q_a.reshape(b, s, -1), kv_a.reshape(b, s, -1), cosf, sinf,
                                         cmp_pe[l], cmp_w1[l], cmp_w2[l])
        v_res = None if l == 0 else (v_res0[l - 1], v_res_down[l - 1], v_res_up[l - 1])
        y_b = rwkv_branch(zs, zs_first, rwkv_vec[l], w_up[l], a_up[l], g_up[l], v_res)
        y_c = dilated_branch(qkv_c)
        merged = branch_merge(o_cmp.reshape(t, -1), o_slc.reshape(t, -1), o_win.reshape(t, -1), gate_a,
                              y_b.reshape(t, -1), y_c.reshape(t, -1), w_br_a[l].astype(BF16),
                              w_br_b[l].astype(BF16), w_br_c[l].astype(BF16), mg, tm=1024, tn=512)
        xf, h = matmul_close(merged, w_out[l].astype(BF16), xf, norm_g[l, 1][None], mod_of(l, 2),
                             (norm_g[l, 2][None], mod_of(l, 4), mod_of(l, 3)), seq=s, tm=tm_row, tk=d)

        nxt = None if l == depth - 1 else (norm_g[l + 1, 0][None], mod_of(l + 1, 1), mod_of(l + 1, 0))
        if l % 2 == 0:
            act = swiglu_up(h, ffn_gu[l // 2].astype(BF16), tm=1024, tn=512)
            xf, h = matmul_close(act, ffn_down[l // 2].astype(BF16), xf, norm_g[l, 3][None], mod_of(l, 5), nxt,
                                 seq=s, tm=tm_row, tk=act.shape[1] // 4)
        else:
            yab, wts = moe_ffn(h, router_w[l // 2], router_b[l // 2], moe_gu, moe_down, l // 2, tm=512)
            xf, h = close_sublayer(yab, wts, xf, norm_g[l, 3][None], mod_of(l, 5), nxt, seq=s, tm=tm_row)
    return xf.reshape(b, s, d)
```

```python
import functools

import numpy as np
import jax
import jax.numpy as jnp
from jax import lax
from jax.experimental import pallas as pl
from jax.experimental.pallas import tpu as pltpu

F32 = jnp.float32
BF16 = jnp.bfloat16

HEAD_DIM = 128
ROPE_DIM = HEAD_DIM // 4
ROPE_HALF = ROPE_DIM // 2
ROPE_THETA = 500000.0
NORM_EPS = 1e-6

NSA_HEADS = 8
NSA_KV_HEADS = 2
NSA_GROUP = NSA_HEADS // NSA_KV_HEADS
CMP_BLOCK = 32
CMP_STRIDE = 16
SLC_BLOCK = 64
SLC_TOPN = 16
WIN_SIZE = 512
FORCE_BONUS = 1e4
SLC_PICK_BIAS = 8192.0

RWKV_HEADS = 16
RWKV_HEAD = 64
RWKV_WIDTH = RWKV_HEADS * RWKV_HEAD
LORA_W = 96
LORA_A = 96
LORA_G = 256
LNX_EPS = 64e-5
RWKV_CHUNK = 64
RWKV_CHUNKS_PER_STEP = 8

DIL_PATTERNS = ((128, 1), (512, 4), (2048, 16))
DIL_HPG = 4
DIL_HEADS = DIL_HPG * len(DIL_PATTERNS)
DIL_OUT = DIL_HPG * HEAD_DIM

N_EXPERTS = 8
TOP_K = 2

LANE = 128
VMEM_LIMIT_BYTES = 56 * 1024 * 1024
MASKED = -1e30


def _params(*sem):
    return pltpu.CompilerParams(dimension_semantics=sem, vmem_limit_bytes=VMEM_LIMIT_BYTES)


def _sigmoid(x):
    return 0.5 * jnp.tanh(0.5 * x) + 0.5


def _silu(x):
    return x * _sigmoid(x)


def _dot(a, b):
    return jnp.dot(a, b, preferred_element_type=F32)


def _dot_nt(a, b):
    return lax.dot_general(a, b, (((1,), (1,)), ((), ())), preferred_element_type=F32)


def _dot_tn(a, b):
    return lax.dot_general(a, b, (((0,), (0,)), ((), ())), preferred_element_type=F32)


def _rope(t, cosf, sinf):
    lane = lax.broadcasted_iota(jnp.int32, t.shape, 1)
    swapped = jnp.where(lane < ROPE_HALF, pltpu.roll(t, LANE - ROPE_HALF, 1), pltpu.roll(t, ROPE_HALF, 1))
    return t * cosf + swapped * sinf


def _rms(y):
    return y * lax.rsqrt(jnp.mean(y * y, axis=-1, keepdims=True) + NORM_EPS)


def _matmul_kernel(*refs, act, rope_chunks, ntiles, permute=False):
    if permute:
        a_ref, w_ref, cos_ref, sin_ref, perm_ref, o_ref = refs
    elif rope_chunks:
        a_ref, w_ref, cos_ref, sin_ref, o_ref = refs
    else:
        a_ref, w_ref, o_ref = refs
    acc = _dot(a_ref[...].astype(BF16), w_ref[...].astype(BF16))
    if act == "sigmoid":
        acc = _sigmoid(acc)
    elif act == "tanh":
        acc = jnp.tanh(acc)
    if not rope_chunks:
        o_ref[...] = acc.astype(o_ref.dtype)
        return
    per_tile = acc.shape[1] // LANE

    def store(tile):
        for c in range(per_tile):
            sl = slice(c * LANE, (c + 1) * LANE)
            if tile is not None and tile * per_tile + c in rope_chunks:
                o_ref[:, sl] = _rope(acc[:, sl], cos_ref[...], sin_ref[...]).astype(o_ref.dtype)
            else:
                o_ref[:, sl] = acc[:, sl].astype(o_ref.dtype)

    if ntiles == 1:
        store(0)
        if permute:
            o_ref[...] = _dot(perm_ref[...], o_ref[...]).astype(o_ref.dtype)
        return
    assert not permute
    j = pl.program_id(1)
    tiles = sorted({c // per_tile for c in rope_chunks})
    for t in tiles:
        pl.when(j == t)(functools.partial(store, t))
    pl.when(functools.reduce(jnp.logical_and, [j != t for t in tiles]))(functools.partial(store, None))


def _matmul_shift_kernel(a_ref, ap_ref, w_ref, mu_ref, o_ref, *, seq, tm):
    w = w_ref[...]
    z = _dot(a_ref[...], w)
    zp = _dot(ap_ref[...], w)
    at_start = (pl.program_id(0) * tm) % seq == 0
    last = zp.shape[0] - 1
    prev_row = jnp.where(at_start, 0.0, zp[last:last + 1, :])
    rowid = lax.broadcasted_iota(jnp.int32, z.shape, 0)
    shifted = jnp.where(rowid == 0, prev_row, pltpu.roll(z, 1, 0))
    o_ref[...] = (z + (shifted - z) * mu_ref[...]).astype(o_ref.dtype)


BF16_SUBLANES = 16


def matmul_token_shift(a, w, mu, *, seq, tm, tn):
    m, k = a.shape
    n = w.shape[1]
    assert m % tm == 0 and n % tn == 0 and seq % tm == 0
    per = tm // BF16_SUBLANES
    return pl.pallas_call(
        functools.partial(_matmul_shift_kernel, seq=seq, tm=tm),
        grid=(m // tm, n // tn),
        in_specs=[pl.BlockSpec((tm, k), lambda i, j: (i, 0)),
                  pl.BlockSpec((BF16_SUBLANES, k), lambda i, j: (jnp.maximum(i * per - 1, 0), 0)),
                  pl.BlockSpec((k, tn), lambda i, j: (0, j)),
                  pl.BlockSpec((1, tn), lambda i, j: (0, j))],
        out_specs=pl.BlockSpec((tm, tn), lambda i, j: (i, j)),
        out_shape=jax.ShapeDtypeStruct((m, n), F32),
        compiler_params=_params("parallel", "parallel"),
    )(a, a, w, mu)


def matmul(a, w, *, out_dtype, tm, tn, act=None, rope=None, layer=None, a_col=0, row_perm=None):
    m = a.shape[0]
    k = w.shape[-2]
    n = w.shape[-1]
    tm = min(tm, m)
    assert m % tm == 0 and n % tn == 0, (m, n, tm, tn)
    if layer is None:
        w_spec = pl.BlockSpec((k, tn), lambda i, j: (0, j))
    else:
        w_spec = pl.BlockSpec((None, k, tn), lambda i, j: (layer, 0, j))
    in_specs = [pl.BlockSpec((tm, k), lambda i, j: (i, a_col)), w_spec]
    args = [a, w]
    tiles = ()
    if rope is not None:
        cosf, sinf, tiles = rope
        in_specs += [pl.BlockSpec((tm, LANE), lambda i, j: (i, 0))] * 2
        args += [cosf, sinf]
    if row_perm is not None:
        assert row_perm.shape == (tm, tm) and rope is not None
        in_specs.append(pl.BlockSpec((tm, tm), lambda i, j: (0, 0)))
        args.append(row_perm)
    return pl.pallas_call(
        functools.partial(_matmul_kernel, act=act, rope_chunks=frozenset(tiles), ntiles=n // tn,
                          permute=row_perm is not None),
        grid=(m // tm, n // tn),
        in_specs=in_specs,
        out_specs=pl.BlockSpec((tm, tn), lambda i, j: (i, j)),
        out_shape=jax.ShapeDtypeStruct((m, n), out_dtype),
        compiler_params=_params("parallel", "parallel"),
    )(*args)


def _swiglu_up_kernel(a_ref, wg_ref, wu_ref, o_ref):
    a = a_ref[...]
    g = _dot(a, wg_ref[...])
    u = _dot(a, wu_ref[...])
    o_ref[...] = (_silu(g) * u).astype(o_ref.dtype)


def swiglu_up(a, w_gu, *, tm, tn):
    m, k = a.shape
    f = w_gu.shape[1] // 2
    assert m % tm == 0 and f % tn == 0
    nj = f // tn
    return pl.pallas_call(
        _swiglu_up_kernel,
        grid=(m // tm, nj),
        in_specs=[pl.BlockSpec((tm, k), lambda i, j: (i, 0)),
                  pl.BlockSpec((k, tn), lambda i, j: (0, j)),
                  pl.BlockSpec((k, tn), lambda i, j: (0, j + nj))],
        out_specs=pl.BlockSpec((tm, tn), lambda i, j: (i, j)),
        out_shape=jax.ShapeDtypeStruct((m, f), BF16),
        compiler_params=_params("parallel", "parallel"),
    )(a, w_gu, w_gu)


def _close_sublayer(x, y, gpost, gt, nxt):
    xn = x + gt * (_rms(y) * gpost)
    if nxt is None:
        return xn, None
    gnext, sc, sh = nxt
    return xn, (_rms(xn) * gnext) * (1.0 + sc) + sh


def _matmul_close_kernel(*refs, with_next):
    if with_next:
        a_ref, w_ref, x_ref, gp_ref, gt_ref, gn_ref, sc_ref, sh_ref, xo_ref, ho_ref, acc_ref = refs
    else:
        a_ref, w_ref, x_ref, gp_ref, gt_ref, xo_ref, acc_ref = refs
    kk = pl.program_id(1)

    @pl.when(kk == 0)
    def _():
        acc_ref[...] = jnp.zeros_like(acc_ref)

    acc_ref[...] += _dot(a_ref[...], w_ref[...])

    @pl.when(kk == pl.num_programs(1) - 1)
    def _():
        nxt = (gn_ref[...], sc_ref[0], sh_ref[0]) if with_next else None
        xn, h = _close_sublayer(x_ref[...], acc_ref[...], gp_ref[...], gt_ref[0], nxt)
        xo_ref[...] = xn
        if with_next:
            ho_ref[...] = h.astype(ho_ref.dtype)


def _close_specs(tm, d, seq, with_next, nidx):
    def row(i, *_):
        return (i, 0)

    def const(*_):
        return (0, 0)

    def batch(i, *_):
        return ((i * tm) // seq, 0, 0)

    specs = [pl.BlockSpec((tm, d), row), pl.BlockSpec((1, d), const), pl.BlockSpec((1, 1, d), batch)]
    if with_next:
        specs += [pl.BlockSpec((1, d), const), pl.BlockSpec((1, 1, d), batch), pl.BlockSpec((1, 1, d), batch)]
    return specs


def matmul_close(a, w, x, gpost, gt, nxt, *, seq, tm, tk):
    m, k = a.shape
    d = w.shape[1]
    assert m % tm == 0 and k % tk == 0 and seq % tm == 0
    with_next = nxt is not None
    args = [a, w, x, gpost, gt] + (list(nxt) if with_next else [])
    in_specs = [pl.BlockSpec((tm, tk), lambda i, kk: (i, kk)), pl.BlockSpec((tk, d), lambda i, kk: (kk, 0))]
    in_specs += _close_specs(tm, d, seq, with_next, 2)
    out_shape = [jax.ShapeDtypeStruct((m, d), F32)]
    out_specs = [pl.BlockSpec((tm, d), lambda i, kk: (i, 0))]
    if with_next:
        out_shape.append(jax.ShapeDtypeStruct((m, d), BF16))
        out_specs.append(pl.BlockSpec((tm, d), lambda i, kk: (i, 0)))
    res = pl.pallas_call(
        functools.partial(_matmul_close_kernel, with_next=with_next),
        grid=(m // tm, k // tk),
        in_specs=in_specs,
        out_specs=out_specs,
        out_shape=out_shape,
        scratch_shapes=[pltpu.VMEM((tm, d), F32)],
        compiler_params=_params("parallel", "arbitrary"),
    )(*args)
    return (res[0], res[1]) if with_next else (res[0], None)


def _close_kernel(*refs, with_next):
    if with_next:
        ya_ref, yb_ref, wt_ref, x_ref, gp_ref, gt_ref, gn_ref, sc_ref, sh_ref, xo_ref, ho_ref = refs
    else:
        ya_ref, yb_ref, wt_ref, x_ref, gp_ref, gt_ref, xo_ref = refs
    nxt = (gn_ref[...], sc_ref[0], sh_ref[0]) if with_next else None
    wt = wt_ref[...]
    y = wt[:, 0:1] * ya_ref[...].astype(F32) + wt[:, 1:2] * yb_ref[...].astype(F32)
    xn, h = _close_sublayer(x_ref[...], y, gp_ref[...], gt_ref[0], nxt)
    xo_ref[...] = xn
    if with_next:
        ho_ref[...] = h.astype(ho_ref.dtype)


def close_sublayer(yab, wt, x, gpost, gt, nxt, *, seq, tm):
    m, d = x.shape
    with_next = nxt is not None
    args = [yab, yab, wt, x, gpost, gt] + (list(nxt) if with_next else [])
    in_specs = [pl.BlockSpec((tm, d), lambda i: (i, 0)), pl.BlockSpec((tm, d), lambda i: (i + m // tm, 0)),
                pl.BlockSpec((tm, LANE), lambda i: (i, 0))] + _close_specs(tm, d, seq, with_next, 1)
    out_shape = [jax.ShapeDtypeStruct((m, d), F32)]
    out_specs = [pl.BlockSpec((tm, d), lambda i: (i, 0))]
    if with_next:
        out_shape.append(jax.ShapeDtypeStruct((m, d), BF16))
        out_specs.append(pl.BlockSpec((tm, d), lambda i: (i, 0)))
    res = pl.pallas_call(
        functools.partial(_close_kernel, with_next=with_next),
        grid=(m // tm,),
        in_specs=in_specs,
        out_specs=out_specs,
        out_shape=out_shape,
        compiler_params=_params("parallel"),
    )(*args)
    return (res[0], res[1]) if with_next else (res[0], None)


def _norm_mod_kernel(x_ref, g_ref, sc_ref, sh_ref, o_ref):
    o_ref[...] = ((_rms(x_ref[...]) * g_ref[...]) * (1.0 + sc_ref[0]) + sh_ref[0]).astype(o_ref.dtype)


def norm_mod(x, g, sc, sh, *, seq, tm):
    m, d = x.shape
    batch = lambda i: ((i * tm) // seq, 0, 0)
    return pl.pallas_call(
        _norm_mod_kernel,
        grid=(m // tm,),
        in_specs=[pl.BlockSpec((tm, d), lambda i: (i, 0)), pl.BlockSpec((1, d), lambda i: (0, 0)),
                  pl.BlockSpec((1, 1, d), batch), pl.BlockSpec((1, 1, d), batch)],
        out_specs=pl.BlockSpec((tm, d), lambda i: (i, 0)),
        out_shape=jax.ShapeDtypeStruct((m, d), BF16),
        compiler_params=_params("parallel"),
    )(x, g, sc, sh)


def _branch_merge_kernel(oc_ref, os_ref, ow_ref, ng_ref, yb_ref, yc_ref, wa_ref, wb_ref, wc_ref, ga_ref, gb_ref,
                         gc_ref, o_ref, ya_ref):
    @pl.when(pl.program_id(1) == 0)
    def _():
        ng = ng_ref[...]
        for hd in range(NSA_HEADS):
            sl = slice(hd * HEAD_DIM, (hd + 1) * HEAD_DIM)
            ya = ng[:, 3 * hd:3 * hd + 1] * oc_ref[:, sl].astype(F32)
            ya += ng[:, 3 * hd + 1:3 * hd + 2] * os_ref[:, sl].astype(F32)
            ya += ng[:, 3 * hd + 2:3 * hd + 3] * ow_ref[:, sl].astype(F32)
            ya_ref[:, sl] = ya.astype(ya_ref.dtype)

    acc = ga_ref[...].astype(F32) * _dot(ya_ref[...], wa_ref[...])
    acc += gb_ref[...].astype(F32) * _dot(yb_ref[...], wb_ref[...])
    acc += gc_ref[...].astype(F32) * _dot(yc_ref[...], wc_ref[...])
    o_ref[...] = acc.astype(o_ref.dtype)


def branch_merge(o_cmp, o_slc, o_win, nsa_gate, yb, yc, wa, wb, wc, gates, *, tm, tn):
    m = yb.shape[0]
    d = wa.shape[1]
    nj = d // tn
    row = lambda width: pl.BlockSpec((tm, width), lambda i, j: (i, 0))
    wsp = lambda kdim: pl.BlockSpec((kdim, tn), lambda i, j: (0, j))
    gsp = lambda off: pl.BlockSpec((tm, tn), lambda i, j: (i, j + off * nj))
    wa_rows = wa.shape[0]
    return pl.pallas_call(
        _branch_merge_kernel,
        grid=(m // tm, nj),
        in_specs=[row(wa_rows), row(wa_rows), row(wa_rows), row(LANE), row(yb.shape[1]), row(yc.shape[1]),
                  wsp(wa_rows), wsp(wb.shape[0]), wsp(wc.shape[0]), gsp(0), gsp(1), gsp(2)],
        out_specs=pl.BlockSpec((tm, tn), lambda i, j: (i, j)),
        out_shape=jax.ShapeDtypeStruct((m, d), BF16),
        scratch_shapes=[pltpu.VMEM((tm, wa_rows), BF16)],
        compiler_params=_params("parallel", "arbitrary"),
    )(o_cmp, o_slc, o_win, nsa_gate, yb, yc, wa, wb, wc, gates, gates, gates)


def _compress_out_kernel(h_ref, w2_ref, o_ref):
    o_ref[0] = _dot(_silu(h_ref[0]).astype(BF16), w2_ref[0]).astype(o_ref.dtype)


def nsa_compress(kv_cmp, pe, w1, w2):
    b, s, width = kv_cmp.shape
    groups = s // CMP_STRIDE
    heads = NSA_KV_HEADS
    x2 = kv_cmp.reshape(b * groups, CMP_STRIDE * width)
    w1r = w1.reshape(2, 2, CMP_STRIDE, HEAD_DIM, HEAD_DIM)
    eye = jnp.eye(heads, dtype=F32)
    wcat = jnp.einsum("pk,qh,kardn->rpqdkhan", jnp.eye(2, dtype=F32), eye, w1r)
    wcat = wcat.reshape(CMP_STRIDE * width, 2 * heads * 2 * HEAD_DIM).astype(BF16)
    halves = matmul(x2, wcat, out_dtype=F32, tm=min(512, b * groups), tn=2 * HEAD_DIM)
    halves = halves.reshape(b, groups, 2, heads, 2, HEAD_DIM)
    bias = jnp.einsum("kf,kfn->kn", pe.reshape(2, CMP_BLOCK * HEAD_DIM), w1, precision=lax.Precision.HIGHEST)
    pre = halves[:, :-1, :, :, 0] + halves[:, 1:, :, :, 1] + bias[None, None, :, None, :]
    pre = jnp.pad(pre, ((0, 0), (0, 1), (0, 0), (0, 0), (0, 0))).transpose(2, 0, 3, 1, 4)
    rows = b * heads * groups
    tm = min(rows, 512)
    out = pl.pallas_call(
        _compress_out_kernel,
        grid=(2, rows // tm),
        in_specs=[pl.BlockSpec((1, tm, HEAD_DIM), lambda t, i: (t, i, 0)),
                  pl.BlockSpec((1, HEAD_DIM, HEAD_DIM), lambda t, i: (t, 0, 0))],
        out_specs=pl.BlockSpec((1, tm, HEAD_DIM), lambda t, i: (t, i, 0)),
        out_shape=jax.ShapeDtypeStruct((2, rows, HEAD_DIM), BF16),
        compiler_params=_params("parallel", "parallel"),
    )(pre.reshape(2, rows, HEAD_DIM), w2.astype(BF16))
    return out.reshape(2, b, heads, groups, HEAD_DIM)


def _cmp_select_kernel(q_ref, kc_ref, vct_ref, o_ref, sel_ref, *, tq, nc, ncp, ns, nsp, scale):
    qi = pl.program_id(2)
    kc = kc_ref[0, 0]
    vct = vct_ref[0, 0]
    spos = qi * tq + lax.broadcasted_iota(jnp.int32, (1, tq), 1)
    cidx = lax.broadcasted_iota(jnp.int32, (ncp, 1), 0)
    valid = jnp.logical_and(cidx * CMP_STRIDE + (CMP_BLOCK - 1) <= spos, cidx < nc)
    psum = jnp.zeros((ncp, tq), F32)
    for g in range(NSA_GROUP):
        qg = q_ref[0, :, g * HEAD_DIM:(g + 1) * HEAD_DIM]
        st = _dot_nt(kc, qg) * scale
        st = jnp.where(valid, st, MASKED)
        mx = jnp.max(st, axis=0, keepdims=True)
        e = jnp.where(valid, jnp.exp(st - mx), 0.0)
        den = jnp.sum(e, axis=0, keepdims=True)
        p = e * (1.0 / jnp.where(den > 0, den, 1.0))
        psum = psum + p
        og_t = _dot(vct, p.astype(BF16))
        o_ref[0, :, g * HEAD_DIM:(g + 1) * HEAD_DIM] = og_t.T.astype(o_ref.dtype)
    jrow = lax.broadcasted_iota(jnp.int32, (ns, ncp), 0)
    ccol = lax.broadcasted_iota(jnp.int32, (ns, ncp), 1)
    c0 = ccol * CMP_STRIDE
    j0 = jrow * SLC_BLOCK
    cover_t = jnp.logical_and(c0 < j0 + SLC_BLOCK, c0 + CMP_BLOCK > j0).astype(F32)
    hi = psum.astype(BF16)
    rem = psum - hi.astype(F32)
    mid = rem.astype(BF16)
    lo = (rem - mid.astype(F32)).astype(BF16)
    imp3 = _dot(cover_t.astype(BF16), jnp.concatenate([hi, mid, lo], axis=1))
    imp = imp3[:, :tq] + imp3[:, tq:2 * tq] + imp3[:, 2 * tq:]
    j = lax.broadcasted_iota(jnp.int32, (ns, 1), 0).astype(F32)
    cur = (spos // SLC_BLOCK).astype(F32)
    forced = jnp.logical_or(jnp.logical_or(j == 0, j == cur), j == cur - 1)
    score = jnp.where(j <= cur, imp + FORCE_BONUS * forced.astype(F32), -jnp.inf)
    sel = jnp.zeros((ns, tq), F32)
    for _ in range(SLC_TOPN):
        mx = jnp.max(score, axis=0, keepdims=True)
        first = jnp.min(jnp.where(score == mx, j, float(ns)), axis=0, keepdims=True)
        pick = j == first
        sel = jnp.where(pick, 1.0, sel)
        score = jnp.where(pick, -jnp.inf, score)
    if ns < nsp:
        sel = jnp.concatenate([sel, jnp.zeros((nsp - ns, tq), F32)], axis=0)
    sel_ref[0, 0] = sel.T.astype(sel_ref.dtype)


def cmp_select(q, kc, vct, *, nc, tq):
    b, s, _ = q.shape
    ncp = kc.shape[2]
    nsp = LANE
    assert s % tq == 0 and s // SLC_BLOCK <= nsp
    gw = NSA_GROUP * HEAD_DIM
    return pl.pallas_call(
        functools.partial(_cmp_select_kernel, tq=tq, nc=nc, ncp=ncp, ns=-(-(s // SLC_BLOCK) // 8) * 8, nsp=nsp,
                          scale=HEAD_DIM ** -0.5),
        grid=(b, NSA_KV_HEADS, s // tq),
        in_specs=[pl.BlockSpec((1, tq, gw), lambda bi, h, i: (bi, i, h)),
                  pl.BlockSpec((1, 1, ncp, HEAD_DIM), lambda bi, h, i: (bi, h, 0, 0)),
                  pl.BlockSpec((1, 1, HEAD_DIM, ncp), lambda bi, h, i: (bi, h, 0, 0))],
        out_specs=[pl.BlockSpec((1, tq, gw), lambda bi, h, i: (bi, i, h)),
                   pl.BlockSpec((1, 1, tq, nsp), lambda bi, h, i: (bi, h, i, 0))],
        out_shape=[jax.ShapeDtypeStruct((b, s, NSA_HEADS * HEAD_DIM), BF16),
                   jax.ShapeDtypeStruct((b, NSA_KV_HEADS, s, nsp), BF16)],
        compiler_params=_params("parallel", "parallel", "parallel"),
    )(q, kc, vct)


def _slc_kernel(qi_ref, ki_ref, q_ref, k_ref, v_ref, sel_ref, cos_ref, sin_ref, o_ref, qaug_ref, m_ref, acc_ref,
                *, tq, scale):
    step = pl.program_id(2)
    qi = qi_ref[step]
    ki = ki_ref[step]

    @pl.when(ki == 0)
    def _():
        cosf = cos_ref[0]
        sinf = sin_ref[0]
        pick = sel_ref[0, 0] * SLC_PICK_BIAS
        for g in range(NSA_GROUP):
            sl = slice(g * HEAD_DIM, (g + 1) * HEAD_DIM)
            qg = _rope(q_ref[0, :, sl].astype(F32), cosf, sinf) * scale
            qaug_ref[g] = jnp.concatenate([qg.astype(BF16), pick], axis=1)
        m_ref[...] = jnp.full_like(m_ref, MASKED)
        acc_ref[...] = jnp.zeros_like(acc_ref)

    def accumulate(causal):
        nsp = sel_ref.shape[3]
        block_of_key = (ki * tq + lax.broadcasted_iota(jnp.int32, (tq, nsp), 0)) // SLC_BLOCK
        onehot = (block_of_key == lax.broadcasted_iota(jnp.int32, (tq, nsp), 1)).astype(BF16)
        kaug = jnp.concatenate([k_ref[0], onehot], axis=1)
        vaug = jnp.concatenate([v_ref[0], jnp.ones((tq, LANE), BF16)], axis=1)
        if causal:
            qpos = lax.broadcasted_iota(jnp.int32, (tq, tq), 0)
            kpos = lax.broadcasted_iota(jnp.int32, (tq, tq), 1)
            visible = kpos <= qpos
        heads = range(NSA_GROUP)
        scores = [_dot_nt(qaug_ref[g], kaug) for g in heads]
        if causal:
            scores = [jnp.where(visible, s, MASKED) for s in scores]
        m_olds = [m_ref[g] for g in heads]
        m_news = [jnp.maximum(mo, jnp.max(s, axis=-1, keepdims=True)) for mo, s in zip(m_olds, scores)]
        alphas = [jnp.exp(mo - mn) for mo, mn in zip(m_olds, m_news)]
        probs = [jnp.exp((s - jnp.concatenate([mn] * (tq // LANE), axis=1)).astype(BF16))
                 for s, mn in zip(scores, m_news)]
        pvs = [_dot(p, vaug) for p in probs]
        for g in heads:
            acc_ref[g] = jnp.concatenate([alphas[g], alphas[g]], axis=1) * acc_ref[g] + pvs[g]
            m_ref[g] = m_news[g]

    @pl.when(ki < qi)
    def _():
        accumulate(False)

    @pl.when(ki == qi)
    def _():
        accumulate(True)
        for g in range(NSA_GROUP):
            sl = slice(g * HEAD_DIM, (g + 1) * HEAD_DIM)
            o_ref[0, :, sl] = (acc_ref[g, :, :HEAD_DIM] / acc_ref[g, :, HEAD_DIM:]).astype(o_ref.dtype)


def slc_attention(q, kv, sel, cosf, sinf, *, k_col, v_col, tq):
    b, s, _ = q.shape
    gw = NSA_GROUP * HEAD_DIM
    nq = s // tq
    nsp = sel.shape[3]
    pairs = [(i, j) for i in range(nq) for j in range(i + 1)]
    qi_tab = jnp.asarray([p[0] for p in pairs], jnp.int32)
    ki_tab = jnp.asarray([p[1] for p in pairs], jnp.int32)
    grid_spec = pltpu.PrefetchScalarGridSpec(
        num_scalar_prefetch=2,
        grid=(b, NSA_KV_HEADS, len(pairs)),
        in_specs=[pl.BlockSpec((1, tq, gw), lambda bi, h, t, qt, kt: (bi, qt[t], h)),
                  pl.BlockSpec((1, tq, HEAD_DIM), lambda bi, h, t, qt, kt: (bi, kt[t], k_col + h)),
                  pl.BlockSpec((1, tq, HEAD_DIM), lambda bi, h, t, qt, kt: (bi, kt[t], v_col + h)),
                  pl.BlockSpec((1, 1, tq, nsp), lambda bi, h, t, qt, kt: (bi, h, qt[t], 0)),
                  pl.BlockSpec((1, tq, LANE), lambda bi, h, t, qt, kt: (bi, qt[t], 0)),
                  pl.BlockSpec((1, tq, LANE), lambda bi, h, t, qt, kt: (bi, qt[t], 0))],
        out_specs=pl.BlockSpec((1, tq, gw), lambda bi, h, t, qt, kt: (bi, qt[t], h)),
        scratch_shapes=[pltpu.VMEM((NSA_GROUP, tq, 2 * HEAD_DIM), BF16),
                        pltpu.VMEM((NSA_GROUP, tq, LANE), F32),
                        pltpu.VMEM((NSA_GROUP, tq, 2 * HEAD_DIM), F32)],
    )
    return pl.pallas_call(
        functools.partial(_slc_kernel, tq=tq, scale=HEAD_DIM ** -0.5),
        grid_spec=grid_spec,
        out_shape=jax.ShapeDtypeStruct((b, s, NSA_HEADS * HEAD_DIM), BF16),
        compiler_params=_params("parallel", "parallel", "arbitrary"),
    )(qi_tab, ki_tab, q, kv, kv, sel, cosf, sinf)


def _band_kernel(*refs, nheads, kv_heads, nkv, tq, max_dist, rope_q, with_lse, scale, nsub):
    q_ref = refs[0]
    k_refs = refs[1:1 + nkv]
    v_refs = refs[1 + nkv:1 + 2 * nkv]
    pos = 1 + 2 * nkv
    if rope_q:
        cos_ref, sin_ref = refs[pos:pos + 2]
        pos += 2
    o_ref = refs[pos]
    lse_ref = refs[pos + 1] if with_lse else None
    qi = pl.program_id(2)
    qpos = qi * tq + lax.broadcasted_iota(jnp.int32, (tq, 1), 0)
    kpos = (qi - (nkv - 1)) * tq + lax.broadcasted_iota(jnp.int32, (1, nkv * tq), 1)
    diff = qpos - kpos
    mask = jnp.logical_and(jnp.logical_and(diff >= 0, diff <= max_dist), kpos >= 0)
    lane = lax.broadcasted_iota(jnp.int32, (tq, LANE), 1)
    jobs = [(u, g) for u in range(nsub) for g in range(nheads)]

    def rows(u):
        return slice(u * tq, (u + 1) * tq)

    def cols(g, kv=False):
        return slice(0, HEAD_DIM) if (kv and kv_heads == 1) else slice(g * HEAD_DIM, (g + 1) * HEAD_DIM)

    scores = []
    for u, g in jobs:
        q = q_ref[0, rows(u), cols(g)]
        if rope_q:
            q = _rope(q.astype(F32), cos_ref[0], sin_ref[0]).astype(BF16)
        kcat = jnp.concatenate([r[0, rows(u), cols(g, True)] for r in k_refs], axis=0)
        scores.append(jnp.where(mask, _dot_nt(q, kcat) * scale, MASKED))
    maxes = [jnp.max(s, axis=-1, keepdims=True) for s in scores]
    probs = [jnp.exp((s - mx).astype(BF16)) for s, mx in zip(scores, maxes)]
    pvs = []
    for (u, g), e in zip(jobs, probs):
        vcat = jnp.concatenate([r[0, rows(u), cols(g, True)] for r in v_refs], axis=0)
        pvs.append(_dot(e, jnp.concatenate([vcat, jnp.ones(vcat.shape, BF16)], axis=1)))
    lse_acc = [jnp.zeros((tq, LANE), F32) for _ in range(nsub)]
    for (u, g), pv, mx in zip(jobs, pvs, maxes):
        den = pv[:, HEAD_DIM:]
        o_ref[0, rows(u), cols(g)] = (pv[:, :HEAD_DIM] / den).astype(o_ref.dtype)
        if with_lse:
            lse_acc[u] = jnp.where(lane == g, mx + jnp.log(den), lse_acc[u])
    if with_lse:
        for u in range(nsub):
            lse_ref[0, rows(u)] = lse_acc[u]


def band_attention(q, k, v, *, nheads, kv_heads, q_col, k_col, v_col, o_cols, ncol, max_dist, tq,
                   rope=None, with_lse=False, out_dtype=BF16, seq_len=None, row_block=None, nsub=1):
    b = q.shape[0]
    rows = q.shape[1]
    seq_len = rows if seq_len is None else seq_len
    tq = min(tq, seq_len)
    assert seq_len % tq == 0 and ncol % nsub == 0
    nkv = -(-max_dist // tq) + 1
    qw = nheads * HEAD_DIM
    kw = kv_heads * HEAD_DIM
    bt = nsub * tq
    if row_block is None:
        assert nsub == 1
        row_block = lambda c, i: i
        o_col = lambda c: c
    else:
        assert rope is None
        o_col = lambda c: 0

    def kv_spec(col_fn, back):
        return pl.BlockSpec((1, bt, kw), lambda bi, c, i: (bi, row_block(c, jnp.maximum(i - back, 0)), col_fn(c)))

    in_specs = [pl.BlockSpec((1, bt, qw), lambda bi, c, i: (bi, row_block(c, i), q_col(c)))]
    in_specs += [kv_spec(k_col, nkv - 1 - t) for t in range(nkv)]
    in_specs += [kv_spec(v_col, nkv - 1 - t) for t in range(nkv)]
    args = [q] + [k] * nkv + [v] * nkv
    if rope is not None:
        in_specs += [pl.BlockSpec((1, tq, LANE), lambda bi, c, i: (bi, i, 0))] * 2
        args += list(rope)
    out_specs = [pl.BlockSpec((1, bt, qw), lambda bi, c, i: (bi, row_block(c, i), o_col(c)))]
    out_shape = [jax.ShapeDtypeStruct((b, rows, o_cols * qw), out_dtype)]
    if with_lse:
        out_specs.append(pl.BlockSpec((1, bt, LANE), lambda bi, c, i: (bi, row_block(c, i), o_col(c))))
        out_shape.append(jax.ShapeDtypeStruct((b, rows, o_cols * LANE), F32))
    res = pl.pallas_call(
        functools.partial(_band_kernel, nheads=nheads, kv_heads=kv_heads, nkv=nkv, tq=tq, max_dist=max_dist,
                          rope_q=rope is not None, with_lse=with_lse, scale=HEAD_DIM ** -0.5, nsub=nsub),
        grid=(b, ncol // nsub, seq_len // tq),
        in_specs=in_specs,
        out_specs=out_specs,
        out_shape=out_shape,
        compiler_params=_params("parallel", "parallel", "parallel"),
    )(*args)
    return res if with_lse else res[0]


def _bmm(x, y):
    return jnp.einsum("bij,bjk->bik", x.astype(BF16), y.astype(BF16), preferred_element_type=F32)


def _bmm_nt(x, y):
    return jnp.einsum("bik,bjk->bij", x.astype(BF16), y.astype(BF16), preferred_element_type=F32)


def _head_sums(x2, scale=1.0):
    n = x2.shape[1]
    blk = (lax.broadcasted_iota(jnp.int32, (n, n), 0) // RWKV_HEAD
           == lax.broadcasted_iota(jnp.int32, (n, n), 1) // RWKV_HEAD)
    ones = jnp.where(blk, scale, 0.0).astype(BF16)
    hi = x2.astype(BF16)
    lo = (x2 - hi.astype(F32)).astype(BF16)
    return _dot(hi, ones) + _dot(lo, ones)


def _wkv_chunk_kernel(*refs, nb, c, mix):
    if mix:
        zr_ref, zk_ref, zv_ref, wl_ref, al_ref, vu_ref, vf_ref, vec_ref = refs[:8]
    else:
        zr_ref, zk_ref, zv_ref, wl_ref, al_ref, vec_ref = refs[:6]
    q_ref, y0_ref, gt_ref, ht_ref, bonus_ref = refs[-5:]
    pair = 2 * RWKV_HEAD
    c2 = 2 * c
    rows = nb * c
    vec = vec_ref[...]
    w0, a0, k_k, k_a, r_k, v0 = [vec[i:i + 1] for i in range(6)]
    r2 = zr_ref[0]
    kraw = zk_ref[0]
    v2 = zv_ref[0]
    x = w0 + wl_ref[0]
    softplus_neg = jnp.maximum(-x, 0.0) + jnp.log(1.0 + jnp.exp(-jnp.abs(x)))
    lw2 = -jnp.exp(-softplus_neg - 0.5)
    a_gate = _sigmoid(a0 + al_ref[0])
    if mix:
        v2 = v2 + (vf_ref[0] - v2) * _sigmoid(v0 + vu_ref[0])
    kk = kraw * k_k
    kk = kk * lax.rsqrt(jnp.maximum(_head_sums(kk * kk), 1e-24))
    k2 = kraw * (1.0 + (a_gate - 1.0) * k_a)
    bonus_ref[0] = (_head_sums(r2 * k2 * r_k) * v2).astype(bonus_ref.dtype)

    def chunks(x2):
        return x2.reshape(nb, c, pair)

    r, lw, k, v, a, b = [chunks(t) for t in (r2, lw2, k2, v2, -kk, kk * a_gate)]
    row = lax.broadcasted_iota(jnp.int32, (c, c), 0)
    col = lax.broadcasted_iota(jnp.int32, (c, c), 1)
    tril = jnp.broadcast_to((row >= col).astype(BF16), (nb, c, c))
    hi = lw.astype(BF16)
    rem = lw - hi.astype(F32)
    mid = rem.astype(BF16)
    lo = (rem - mid.astype(F32)).astype(BF16)
    cum3 = _bmm(tril, jnp.concatenate([hi, mid, lo], axis=2))
    cum = cum3[:, :, :pair] + cum3[:, :, pair:2 * pair] + cum3[:, :, 2 * pair:]
    last = cum[:, c - 1:c, :]
    inv = jnp.exp(-cum)
    tail = jnp.exp(last - cum)
    lane = lax.broadcasted_iota(jnp.int32, (1, 1, pair), 2)
    first = lane < RWKV_HEAD

    def stack(x):
        return jnp.concatenate([jnp.where(first, x, 0.0), jnp.where(first, 0.0, x)], axis=1)

    a_s = stack(a * jnp.exp(cum - lw))
    r_s = stack(r * jnp.exp(cum))
    b_s = stack(b * inv)
    k_s = stack(k * inv)
    v_s = stack(v)
    bh_s = stack(b * tail)
    kh_s = stack(k * tail)
    ar = jnp.concatenate([a_s, r_s], axis=1)
    pbk = _bmm_nt(ar, jnp.concatenate([b_s, k_s], axis=1))
    pb = pbk[:, :, :c2]
    pk = pbk[:, :, c2:]
    row2 = lax.broadcasted_iota(jnp.int32, (c2, c2), 0) % c
    col2 = lax.broadcasted_iota(jnp.int32, (c2, c2), 1) % c
    strict = row2 > col2
    lower = row2 >= col2
    l_ab = jnp.where(strict, pb[:, :c2], 0.0)
    m_rb = jnp.where(lower, pb[:, c2:], 0.0)
    l_ak = jnp.where(strict, pk[:, :c2], 0.0)
    m_rk = jnp.where(lower, pk[:, c2:], 0.0)
    eye = (lax.broadcasted_iota(jnp.int32, (c2, c2), 0) == lax.broadcasted_iota(jnp.int32, (c2, c2), 1))
    eye = eye.astype(F32)
    tinv = eye + l_ab
    pw = _bmm(l_ab, l_ab)
    for _ in range(int(np.log2(c)) - 2):
        both = _bmm(jnp.concatenate([tinv, pw], axis=1), pw)
        tinv = tinv + both[:, :c2]
        pw = both[:, c2:]
    tinv = tinv + _bmm(tinv, pw)
    tu = _bmm(tinv, jnp.concatenate([_bmm(l_ak, v_s), a_s], axis=2))
    u0_s = tu[:, :, :pair]
    ta_s = tu[:, :, pair:]
    mu = _bmm(m_rb, jnp.concatenate([ta_s, u0_s], axis=2))
    q_s = r_s + mu[:, :, :pair]
    y0_s = _bmm(m_rk, v_s) + mu[:, :, pair:]
    q_ref[0] = (q_s[:, :c] + q_s[:, c:]).reshape(nb * c, pair).astype(q_ref.dtype)
    y0_ref[0] = (y0_s[:, :c] + y0_s[:, c:]).reshape(nb * c, pair).astype(y0_ref.dtype)
    bh_t = jnp.swapaxes(bh_s, 1, 2)
    kh_t = jnp.swapaxes(kh_s, 1, 2)
    gh = _bmm(bh_t, jnp.concatenate([ta_s, u0_s], axis=2))
    gt_ref[0, :, 0] = (eye * jnp.exp(last) + gh[:, :, :pair]).astype(gt_ref.dtype)
    ht_ref[0, :, 0] = (gh[:, :, pair:] + _bmm(kh_t, v_s)).astype(ht_ref.dtype)


def _wkv_scan_kernel(q_ref, y0_ref, gt_ref, ht_ref, bonus_ref, g_ref, lnx_ref, o_ref, state_ref, *, npair):
    pair = 2 * RWKV_HEAD
    bsz, c, _ = q_ref.shape

    @pl.when(pl.program_id(0) == 0)
    def _():
        state_ref[...] = jnp.zeros_like(state_ref)

    sls = [slice(p * pair, (p + 1) * pair) for p in range(npair)]
    idx = [(bi, p) for bi in range(bsz) for p in range(npair)]
    st = [state_ref[bi, p].astype(BF16) for bi, p in idx]
    ys = [_dot(q_ref[bi, :, sls[p]], s_) + y0_ref[bi, :, sls[p]].astype(F32) for (bi, p), s_ in zip(idx, st)]
    new = [_dot(gt_ref[bi, 0, p], s_) + ht_ref[bi, 0, p].astype(F32) for (bi, p), s_ in zip(idx, st)]
    for (bi, p), s_ in zip(idx, new):
        state_ref[bi, p] = s_
    y = jnp.concatenate(ys, axis=0)
    dev = y - _head_sums(y, 1.0 / RWKV_HEAD)
    yn = dev * lax.rsqrt(_head_sums(dev * dev, 1.0 / RWKV_HEAD) + LNX_EPS)
    for n, (bi, p) in enumerate(idx):
        ln = yn[n * c:(n + 1) * c] * lnx_ref[0:1, sls[p]] + lnx_ref[1:2, sls[p]]
        out = (ln + bonus_ref[bi, :, sls[p]].astype(F32)) * g_ref[bi, :, sls[p]].astype(F32)
        o_ref[bi, :, sls[p]] = out.astype(o_ref.dtype)


def wkv7(zs, zs_first, wl, al, vu, vec, g, lnx):
    bsz, s, _ = zs.shape
    width = wl.shape[2]
    pair = 2 * RWKV_HEAD
    npair = width // pair
    c = min(RWKV_CHUNK, s)
    nb = min(RWKV_CHUNKS_PER_STEP, s // c)
    nch = s // c
    mix = vu is not None
    assert s % (nb * c) == 0 and width % pair == 0

    def col(off):
        return pl.BlockSpec((1, nb * c, pair), lambda bi, p, j: (bi, j, off + p))

    vec_spec = pl.BlockSpec((8, pair), lambda bi, p, j: (0, p))
    mat = pl.BlockSpec((1, nb, 1, pair, pair), lambda bi, p, j: (bi, j, p, 0, 0))
    in_specs = [col(0), col(npair), col(2 * npair), col(0), col(0)]
    args = [zs, zs, zs, wl, al]
    if mix:
        in_specs += [col(0), col(2 * npair)]
        args += [vu, zs_first]
    q, y0, gt, ht, bonus = pl.pallas_call(
        functools.partial(_wkv_chunk_kernel, nb=nb, c=c, mix=mix),
        grid=(bsz, npair, nch // nb),
        in_specs=in_specs + [vec_spec],
        out_specs=[col(0), col(0), mat, mat, col(0)],
        out_shape=[jax.ShapeDtypeStruct((bsz, s, width), BF16), jax.ShapeDtypeStruct((bsz, s, width), BF16),
                   jax.ShapeDtypeStruct((bsz, nch, npair, pair, pair), BF16),
                   jax.ShapeDtypeStruct((bsz, nch, npair, pair, pair), BF16),
                   jax.ShapeDtypeStruct((bsz, s, width), BF16)],
        compiler_params=_params("parallel", "parallel", "parallel"),
    )(*args, vec)
    row = pl.BlockSpec((bsz, c, width), lambda j: (0, j, 0))
    mats = pl.BlockSpec((bsz, 1, npair, pair, pair), lambda j: (0, j, 0, 0, 0))
    return pl.pallas_call(
        functools.partial(_wkv_scan_kernel, npair=npair),
        grid=(nch,),
        in_specs=[row, row, mats, mats, row, row, pl.BlockSpec((8, width), lambda j: (0, 0))],
        out_specs=row,
        out_shape=jax.ShapeDtypeStruct((bsz, s, width), BF16),
        scratch_shapes=[pltpu.VMEM((bsz, npair, pair, pair), F32)],
        compiler_params=_params("arbitrary"),
    )(q, y0, gt, ht, bonus, g, lnx)


def _moe_up_kernel(te_ref, tv_ref, a_ref, wg_ref, wu_ref, o_ref, wg_bf, wu_bf):
    i = pl.program_id(1)
    changed = jnp.logical_or(i == 0, te_ref[i] != te_ref[jnp.maximum(i - 1, 0)])

    @pl.when(changed)
    def _():
        wg_bf[...] = wg_ref[...].astype(BF16)
        wu_bf[...] = wu_ref[...].astype(BF16)

    @pl.when(tv_ref[i] > 0)
    def _():
        a = a_ref[...]
        g = _dot(a, wg_bf[...])
        u = _dot(a, wu_bf[...])
        o_ref[...] = (_silu(g) * u).astype(o_ref.dtype)

    @pl.when(tv_ref[i] == 0)
    def _():
        o_ref[...] = jnp.zeros_like(o_ref)


def moe_up(tile_expert, tile_valid, xs, w_gu, layer, *, tm, tn):
    r, k = xs.shape
    f = w_gu.shape[3] // 2
    nj = f // tn
    grid_spec = pltpu.PrefetchScalarGridSpec(
        num_scalar_prefetch=2,
        grid=(nj, r // tm),
        in_specs=[pl.BlockSpec((tm, k), lambda j, i, te, tv: (i, 0)),
                  pl.BlockSpec((None, None, k, tn), lambda j, i, te, tv: (layer, te[i], 0, j)),
                  pl.BlockSpec((None, None, k, tn), lambda j, i, te, tv: (layer, te[i], 0, j + nj))],
        out_specs=pl.BlockSpec((tm, tn), lambda j, i, te, tv: (i, j)),
        scratch_shapes=[pltpu.VMEM((k, tn), BF16), pltpu.VMEM((k, tn), BF16)],
    )
    return pl.pallas_call(
        _moe_up_kernel,
        grid_spec=grid_spec,
        out_shape=jax.ShapeDtypeStruct((r, f), BF16),
        compiler_params=_params("parallel", "arbitrary"),
    )(tile_expert, tile_valid, xs, w_gu, w_gu)


def _moe_down_kernel(te_ref, tv_ref, a_ref, w_ref, o_ref, w_bf):
    i = pl.program_id(1)
    changed = jnp.logical_or(i == 0, te_ref[i] != te_ref[jnp.maximum(i - 1, 0)])

    @pl.when(changed)
    def _():
        w_bf[...] = w_ref[...].astype(BF16)

    @pl.when(tv_ref[i] > 0)
    def _():
        o_ref[...] = _dot(a_ref[...], w_bf[...]).astype(o_ref.dtype)

    @pl.when(tv_ref[i] == 0)
    def _():
        o_ref[...] = jnp.zeros_like(o_ref)


def moe_down(tile_expert, tile_valid, act, w_down, layer, *, tm, tn):
    r, f = act.shape
    d = w_down.shape[3]
    grid_spec = pltpu.PrefetchScalarGridSpec(
        num_scalar_prefetch=2,
        grid=(d // tn, r // tm),
        in_specs=[pl.BlockSpec((tm, f), lambda j, i, te, tv: (i, 0)),
                  pl.BlockSpec((None, None, f, tn), lambda j, i, te, tv: (layer, te[i], 0, j))],
        out_specs=pl.BlockSpec((tm, tn), lambda j, i, te, tv: (i, j)),
        scratch_shapes=[pltpu.VMEM((f, tn), BF16)],
    )
    return pl.pallas_call(
        _moe_down_kernel,
        grid_spec=grid_spec,
        out_shape=jax.ShapeDtypeStruct((r, d), BF16),
        compiler_params=_params("parallel", "arbitrary"),
    )(tile_expert, tile_valid, act, w_down)


def _rope_tables(positions):
    inv_freq = ROPE_THETA ** (-jnp.arange(ROPE_HALF, dtype=F32) / ROPE_HALF)
    ang = positions.astype(F32)[:, :, None] * inv_freq
    cos = jnp.cos(ang)
    sin = jnp.sin(ang)
    b, s = positions.shape
    pad1 = jnp.ones((b, s, HEAD_DIM - ROPE_DIM), F32)
    pad0 = jnp.zeros((b, s, HEAD_DIM - ROPE_DIM), F32)
    return jnp.concatenate([cos, cos, pad1], axis=-1), jnp.concatenate([-sin, sin, pad0], axis=-1)


def _pad_cols(w, n):
    return w if w.shape[-1] == n else jnp.pad(w, ((0, 0), (0, n - w.shape[-1])))


def nsa_branch(q_a, kv_a, cosf, sinf, cmp_pe, cmp_w1, cmp_w2):
    b, s, _ = q_a.shape
    nc = s // CMP_STRIDE - 1
    ncp = -(-nc // LANE) * LANE
    kvw = NSA_KV_HEADS * HEAD_DIM
    comp = nsa_compress(kv_a[..., :2 * kvw], cmp_pe, cmp_w1, cmp_w2)
    comp = jnp.pad(comp, ((0, 0), (0, 0), (0, 0), (0, ncp - comp.shape[3]), (0, 0)))
    kc = comp[0]
    vct = comp[1].transpose(0, 1, 3, 2)
    tq = min(256, s)
    o_cmp, sel = cmp_select(q_a, kc, vct, nc=nc, tq=tq)
    o_slc = slc_attention(q_a, kv_a, sel, cosf, sinf, k_col=4, v_col=6, tq=min(512, s))
    o_win = band_attention(q_a, kv_a, kv_a, nheads=NSA_GROUP, kv_heads=1, q_col=lambda c: c,
                           k_col=lambda c: 8 + c, v_col=lambda c: 10 + c, o_cols=NSA_KV_HEADS,
                           ncol=NSA_KV_HEADS, max_dist=WIN_SIZE - 1, tq=tq, rope=(cosf, sinf))
    return o_cmp, o_slc, o_win


def rwkv_branch(zs, zs_first, vec, w_up, a_up, g_up, v_res):
    b, s, _ = zs.shape
    t = b * s
    w_ = RWKV_WIDTH
    o = 3 * w_
    zw, za = zs[..., o:o + LORA_W], zs[..., o + LORA_W:o + LORA_W + LORA_A]
    zg = zs[..., o + LORA_W + LORA_A:o + LORA_W + LORA_A + LORA_G]
    w0, a0, k_k, k_a, r_k, lnx_g, lnx_b = [vec[i] for i in range(7)]

    def lora(xin, wmat, out_dtype=F32):
        return matmul(xin.reshape(t, -1).astype(BF16), wmat.astype(BF16), out_dtype=out_dtype, tm=1024,
                      tn=wmat.shape[1]).reshape(b, s, -1)

    wl = lora(jnp.tanh(zw), w_up)
    al = lora(za, a_up)
    g = lora(jax.nn.sigmoid(zg), g_up, BF16)
    zero = jnp.zeros_like(w0)
    if v_res is None:
        vu, v0 = None, zero
    else:
        v0, v_down, v_up = v_res
        vd = matmul(zs.reshape(t, -1), _pad_cols(v_down, LANE).astype(BF16), out_dtype=BF16, tm=1024, tn=LANE,
                    a_col=2)
        vu = matmul(vd, jnp.pad(v_up, ((0, LANE - v_up.shape[0]), (0, 0))).astype(BF16), out_dtype=F32, tm=1024,
                    tn=512).reshape(b, s, w_)
    vecs = jnp.stack([w0, a0, k_k, k_a, r_k, v0, zero, zero])
    lnx = jnp.stack([lnx_g, lnx_b] + [zero] * 6)
    return wkv7(zs, zs_first, wl, al, vu, vecs, g, lnx)


DIL_ROW_TILE = 1024


def residue_perm(dil, transpose=False):
    i = np.arange(DIL_ROW_TILE)
    per = DIL_ROW_TILE // dil
    p = np.zeros((DIL_ROW_TILE, DIL_ROW_TILE), np.float32)
    p[i, (i % per) * dil + i // per] = 1.0
    return jnp.asarray(p.T if transpose else p, BF16)


def _dilated_merge_kernel(*refs, ngroups, nperm):
    o_refs = refs[:ngroups]
    l_refs = refs[ngroups:2 * ngroups]
    p_refs = refs[2 * ngroups:2 * ngroups + nperm]
    y_ref = refs[-1]
    outs, lses = [], []
    for g in range(ngroups):
        if g < ngroups - nperm:
            outs.append(o_refs[g][...].astype(F32))
            lses.append(l_refs[g][...])
            continue
        pt = p_refs[g - (ngroups - nperm)][...]
        outs.append(_dot(pt, o_refs[g][...]))
        lse = l_refs[g][...]
        hi = lse.astype(BF16)
        lo = (lse - hi.astype(F32)).astype(BF16)
        both = _dot(pt, jnp.concatenate([hi, lo], axis=1))
        lses.append(both[:, :LANE] + both[:, LANE:])
    mx = functools.reduce(jnp.maximum, lses)
    es = [jnp.exp(lse - mx) for lse in lses]
    inv = 1.0 / functools.reduce(jnp.add, es)
    for hd in range(DIL_HPG):
        sl = slice(hd * HEAD_DIM, (hd + 1) * HEAD_DIM)
        acc = functools.reduce(jnp.add, [e[:, hd:hd + 1] * o[:, sl] for e, o in zip(es, outs)])
        y_ref[:, sl] = (acc * inv[:, hd:hd + 1]).astype(y_ref.dtype)


def dilated_branch(qkvs):
    b, s, width = qkvs[0].shape
    t = b * s
    assert s % DIL_ROW_TILE == 0
    outs, lses, perms = [], [], []
    for gi, (win, dil) in enumerate(DIL_PATTERNS):
        common = dict(nheads=DIL_HPG, kv_heads=DIL_HPG, q_col=lambda c: 0, k_col=lambda c: 1, v_col=lambda c: 2,
                      o_cols=1, max_dist=win // dil, with_lse=True, out_dtype=BF16)
        if dil == 1:
            assert not perms
            o, lse = band_attention(qkvs[gi], qkvs[gi], qkvs[gi], ncol=1, tq=256, **common)
        else:
            tq = DIL_ROW_TILE // dil
            nsub = max(1, 256 // tq)
            o, lse = band_attention(qkvs[gi], qkvs[gi], qkvs[gi], ncol=dil, tq=tq, seq_len=s // dil, nsub=nsub,
                                    row_block=lambda c, i, per=dil // nsub: i * per + c, **common)
            perms.append(residue_perm(dil, transpose=True))
        outs.append(o.reshape(t, DIL_OUT))
        lses.append(lse.reshape(t, LANE))
    ng = len(DIL_PATTERNS)
    row = lambda width_: pl.BlockSpec((DIL_ROW_TILE, width_), lambda i: (i, 0))
    const = pl.BlockSpec((DIL_ROW_TILE, DIL_ROW_TILE), lambda i: (0, 0))
    return pl.pallas_call(
        functools.partial(_dilated_merge_kernel, ngroups=ng, nperm=len(perms)),
        grid=(t // DIL_ROW_TILE,),
        in_specs=[row(DIL_OUT)] * ng + [row(LANE)] * ng + [const] * len(perms),
        out_specs=row(DIL_OUT),
        out_shape=jax.ShapeDtypeStruct((t, DIL_OUT), BF16),
        compiler_params=_params("parallel"),
    )(*outs, *lses, *perms)


def moe_ffn(h, router_w, router_b, w_gu, w_down, layer, *, tm):
    t, d = h.shape
    logits = matmul(h, _pad_cols(router_w, LANE).astype(BF16), out_dtype=F32, tm=1024, tn=LANE)[:, :N_EXPERTS]
    logits = logits + router_b
    top_v, top_i = lax.top_k(logits, TOP_K)
    wts = jax.nn.softmax(top_v, axis=-1)
    flat_e = top_i.reshape(-1)
    onehot = (flat_e[:, None] == jnp.arange(N_EXPERTS)[None, :]).astype(jnp.int32)
    rank = jnp.take_along_axis(jnp.cumsum(onehot, axis=0), flat_e[:, None], axis=1)[:, 0] - 1
    counts = jnp.sum(onehot, axis=0)
    tiles_per = (counts + tm - 1) // tm
    tile_end = jnp.cumsum(tiles_per)
    group_start = (tile_end - tiles_per) * tm
    dest = group_start[flat_e] + rank
    ntiles = (TOP_K * t) // tm + N_EXPERTS
    rows = ntiles * tm
    row_token = jnp.zeros((rows,), jnp.int32).at[dest].set(jnp.arange(TOP_K * t, dtype=jnp.int32) // TOP_K)
    tile_ids = jnp.arange(ntiles, dtype=jnp.int32)
    tile_valid = (tile_ids < tile_end[-1]).astype(jnp.int32)
    tile_expert = jnp.minimum(jnp.searchsorted(tile_end, tile_ids, side="right"), N_EXPERTS - 1).astype(jnp.int32)
    xs = jnp.take(h, row_token, axis=0, mode="clip")
    act = moe_up(tile_expert, tile_valid, xs, w_gu, layer, tm=tm, tn=min(1024, w_gu.shape[3] // 2))
    out = moe_down(tile_expert, tile_valid, act, w_down, layer, tm=tm, tn=min(1024, d))
    slot_major = dest.reshape(t, TOP_K).T.reshape(-1)
    yab = jnp.take(out, slot_major, axis=0, mode="clip")
    return yab, jnp.pad(wts, ((0, 0), (0, LANE - TOP_K)))


def kernel(x, c, positions, ada_w, ada_b, norm_g, w_in, cmp_pe, cmp_w1, cmp_w2, rwkv_mu, rwkv_vec, w_up, a_up, g_up, v_res0, v_res_down, v_res_up, w_br_a, w_br_b, w_br_c, w_out, ffn_gu, ffn_down, router_w, router_b, moe_gu, moe_down):
    b, s, d = x.shape
    depth = ada_w.shape[0]
    t = b * s
    tm_row = min(512, s)
    cosf, sinf = _rope_tables(positions)
    cos_t = cosf.reshape(t, LANE)
    sin_t = sinf.reshape(t, LANE)

    cond = jnp.pad(jax.nn.silu(c), ((0, 8 - b % 8 if b % 8 else 0), (0, 0))).astype(BF16)
    mods = []
    for l in range(depth):
        mod = matmul(cond, ada_w, out_dtype=F32, tm=cond.shape[0], tn=d, layer=l)[:b] + ada_b[l]
        mods.append(mod.reshape(b, 6, 1, d))

    def mod_of(l, i):
        return mods[l][:, i]

    q_cols = NSA_HEADS * HEAD_DIM
    kv_cols = 6 * NSA_KV_HEADS * HEAD_DIM
    gate_cols = 3 * NSA_HEADS
    rwkv_cols = 3 * RWKV_WIDTH + LORA_W + LORA_A + LORA_G
    dil_cols = 3 * DIL_HEADS * HEAD_DIM
    offs = np.cumsum([0, q_cols, kv_cols, gate_cols, rwkv_cols, dil_cols, 3 * d]).tolist()
    rwkv_pad = -(-rwkv_cols // 512) * 512

    xf = x.reshape(t, d)
    h = norm_mod(xf, norm_g[0, 0][None], mod_of(0, 1), mod_of(0, 0), seq=s, tm=tm_row)
    zs_first = None
    for l in range(depth):
        wl = w_in[l]
        seg = lambda i: wl[:, offs[i]:offs[i + 1]]
        q_a = matmul(h, seg(0).astype(BF16), out_dtype=BF16, tm=1024, tn=q_cols)
        kv_a = matmul(h, seg(1).astype(BF16), out_dtype=BF16, tm=1024, tn=kv_cols, rope=(cos_t, sin_t, (4, 5, 8, 9)))
        gate_a = matmul(h, _pad_cols(seg(2), LANE).astype(BF16), out_dtype=F32, tm=1024, tn=LANE, act="sigmoid")
        zs = matmul_token_shift(h, _pad_cols(seg(3), rwkv_pad).astype(BF16), _pad_cols(rwkv_mu[l][None], rwkv_pad),
                                seq=s, tm=1024, tn=rwkv_pad // 4).reshape(b, s, -1)
        zs_first = zs if l == 0 else zs_first
        wc = seg(4).reshape(d, 3, len(DIL_PATTERNS), DIL_OUT)
        qkv_c = [matmul(h, wc[:, :, gi].reshape(d, 3 * DIL_OUT).astype(BF16), out_dtype=BF16, tm=DIL_ROW_TILE,
                        tn=3 * DIL_OUT, rope=(cos_t, sin_t, tuple(range(2 * DIL_HPG))),
                        row_perm=None if dil == 1 else residue_perm(dil)).reshape(b, s, -1)
                 for gi, (_, dil) in enumerate(DIL_PATTERNS)]
        mg = matmul(h, seg(5).astype(BF16), out_dtype=BF16, tm=1024, tn=1024, act="sigmoid")

        o_cmp, o_slc, o_win = nsa_branch(q_a.reshape(b, s, -1), kv_a.reshape(b, s, -1), cosf, sinf,
                                         cmp_pe[l], cmp_w1[l], cmp_w2[l])
        v_res = None if l == 0 else (v_res0[l - 1], v_res_down[l - 1], v_res_up[l - 1])
        y_b = rwkv_branch(zs, zs_first, rwkv_vec[l], w_up[l], a_up[l], g_up[l], v_res)
        y_c = dilated_branch(qkv_c)
        merged = branch_merge(o_cmp.reshape(t, -1), o_slc.reshape(t, -1), o_win.reshape(t, -1), gate_a,
                              y_b.reshape(t, -1), y_c.reshape(t, -1), w_br_a[l].astype(BF16),
                              w_br_b[l].astype(BF16), w_br_c[l].astype(BF16), mg, tm=1024, tn=512)
        xf, h = matmul_close(merged, w_out[l].astype(BF16), xf, norm_g[l, 1][None], mod_of(l, 2),
                             (norm_g[l, 2][None], mod_of(l, 4), mod_of(l, 3)), seq=s, tm=tm_row, tk=d)

        nxt = None if l == depth - 1 else (norm_g[l + 1, 0][None], mod_of(l + 1, 1), mod_of(l + 1, 0))
        if l % 2 == 0:
            act = swiglu_up(h, ffn_gu[l // 2].astype(BF16), tm=1024, tn=512)
            xf, h = matmul_close(act, ffn_down[l // 2].astype(BF16), xf, norm_g[l, 3][None], mod_of(l, 5), nxt,
                                 seq=s, tm=tm_row, tk=act.shape[1] // 4)
        else:
            yab, wts = moe_ffn(h, router_w[l // 2], router_b[l // 2], moe_gu, moe_down, l // 2, tm=512)
            xf, h = close_sublayer(yab, wts, xf, norm_g[l, 3][None], mod_of(l, 5), nxt, seq=s, tm=tm_row)
    return xf.reshape(b, s, d)
```

```python
import functools

import numpy as np
import jax
import jax.numpy as jnp
from jax import lax
from jax.experimental import pallas as pl
from jax.experimental.pallas import tpu as pltpu

F32 = jnp.float32
BF16 = jnp.bfloat16

HEAD_DIM = 128
ROPE_DIM = HEAD_DIM // 4
ROPE_HALF = ROPE_DIM // 2
ROPE_THETA = 500000.0
NORM_EPS = 1e-6

NSA_HEADS = 8
NSA_KV_HEADS = 2
NSA_GROUP = NSA_HEADS // NSA_KV_HEADS
CMP_BLOCK = 32
CMP_STRIDE = 16
SLC_BLOCK = 64
SLC_TOPN = 16
WIN_SIZE = 512
FORCE_BONUS = 1e4
SLC_PICK_BIAS = 8192.0

RWKV_HEADS = 16
RWKV_HEAD = 64
RWKV_WIDTH = RWKV_HEADS * RWKV_HEAD
LORA_W = 96
LORA_A = 96
LORA_G = 256
LNX_EPS = 64e-5
RWKV_CHUNK = 64
RWKV_CHUNKS_PER_STEP = 8

DIL_PATTERNS = ((128, 1), (512, 4), (2048, 16))
DIL_HPG = 4
DIL_HEADS = DIL_HPG * len(DIL_PATTERNS)
DIL_OUT = DIL_HPG * HEAD_DIM

N_EXPERTS = 8
TOP_K = 2

LANE = 128
VMEM_LIMIT_BYTES = 56 * 1024 * 1024
MASKED = -1e30


def _params(*sem):
    return pltpu.CompilerParams(dimension_semantics=sem, vmem_limit_bytes=VMEM_LIMIT_BYTES)


def _sigmoid(x):
    return 0.5 * jnp.tanh(0.5 * x) + 0.5


def _silu(x):
    return x * _sigmoid(x)


def _dot(a, b):
    return jnp.dot(a, b, preferred_element_type=F32)


def _dot_nt(a, b):
    return lax.dot_general(a, b, (((1,), (1,)), ((), ())), preferred_element_type=F32)


def _dot_tn(a, b):
    return lax.dot_general(a, b, (((0,), (0,)), ((), ())), preferred_element_type=F32)


def _rope(t, cosf, sinf):
    lane = lax.broadcasted_iota(jnp.int32, t.shape, 1)
    swapped = jnp.where(lane < ROPE_HALF, pltpu.roll(t, LANE - ROPE_HALF, 1), pltpu.roll(t, ROPE_HALF, 1))
    return t * cosf + swapped * sinf


def _rms(y):
    return y * lax.rsqrt(jnp.mean(y * y, axis=-1, keepdims=True) + NORM_EPS)


def _matmul_kernel(*refs, act, rope_chunks, ntiles, permute=False):
    if permute:
        a_ref, w_ref, cos_ref, sin_ref, perm_ref, o_ref = refs
    elif rope_chunks:
        a_ref, w_ref, cos_ref, sin_ref, o_ref = refs
    else:
        a_ref, w_ref, o_ref = refs
    acc = _dot(a_ref[...].astype(BF16), w_ref[...].astype(BF16))
    if act == "sigmoid":
        acc = _sigmoid(acc)
    elif act == "tanh":
        acc = jnp.tanh(acc)
    if not rope_chunks:
        o_ref[...] = acc.astype(o_ref.dtype)
        return
    per_tile = acc.shape[1] // LANE

    def store(tile):
        for c in range(per_tile):
            sl = slice(c * LANE, (c + 1) * LANE)
            if tile is not None and tile * per_tile + c in rope_chunks:
                o_ref[:, sl] = _rope(acc[:, sl], cos_ref[...], sin_ref[...]).astype(o_ref.dtype)
            else:
                o_ref[:, sl] = acc[:, sl].astype(o_ref.dtype)

    if ntiles == 1:
        store(0)
        if permute:
            o_ref[...] = _dot(perm_ref[...], o_ref[...]).astype(o_ref.dtype)
        return
    assert not permute
    j = pl.program_id(1)
    tiles = sorted({c // per_tile for c in rope_chunks})
    for t in tiles:
        pl.when(j == t)(functools.partial(store, t))
    pl.when(functools.reduce(jnp.logical_and, [j != t for t in tiles]))(functools.partial(store, None))


def _matmul_shift_kernel(a_ref, ap_ref, w_ref, mu_ref, o_ref, *, seq, tm):
    w = w_ref[...]
    z = _dot(a_ref[...], w)
    zp = _dot(ap_ref[...], w)
    at_start = (pl.program_id(0) * tm) % seq == 0
    last = zp.shape[0] - 1
    prev_row = jnp.where(at_start, 0.0, zp[last:last + 1, :])
    rowid = lax.broadcasted_iota(jnp.int32, z.shape, 0)
    shifted = jnp.where(rowid == 0, prev_row, pltpu.roll(z, 1, 0))
    o_ref[...] = (z + (shifted - z) * mu_ref[...]).astype(o_ref.dtype)


BF16_SUBLANES = 16


def matmul_token_shift(a, w, mu, *, seq, tm, tn):
    m, k = a.shape
    n = w.shape[1]
    assert m % tm == 0 and n % tn == 0 and seq % tm == 0
    per = tm // BF16_SUBLANES
    return pl.pallas_call(
        functools.partial(_matmul_shift_kernel, seq=seq, tm=tm),
        grid=(m // tm, n // tn),
        in_specs=[pl.BlockSpec((tm, k), lambda i, j: (i, 0)),
                  pl.BlockSpec((BF16_SUBLANES, k), lambda i, j: (jnp.maximum(i * per - 1, 0), 0)),
                  pl.BlockSpec((k, tn), lambda i, j: (0, j)),
                  pl.BlockSpec((1, tn), lambda i, j: (0, j))],
        out_specs=pl.BlockSpec((tm, tn), lambda i, j: (i, j)),
        out_shape=jax.ShapeDtypeStruct((m, n), F32),
        compiler_params=_params("parallel", "parallel"),
    )(a, a, w, mu)


def matmul(a, w, *, out_dtype, tm, tn, act=None, rope=None, layer=None, a_col=0, row_perm=None):
    m = a.shape[0]
    k = w.shape[-2]
    n = w.shape[-1]
    tm = min(tm, m)
    assert m % tm == 0 and n % tn == 0, (m, n, tm, tn)
    if layer is None:
        w_spec = pl.BlockSpec((k, tn), lambda i, j: (0, j))
    else:
        w_spec = pl.BlockSpec((None, k, tn), lambda i, j: (layer, 0, j))
    in_specs = [pl.BlockSpec((tm, k), lambda i, j: (i, a_col)), w_spec]
    args = [a, w]
    tiles = ()
    if rope is not None:
        cosf, sinf, tiles = rope
        in_specs += [pl.BlockSpec((tm, LANE), lambda i, j: (i, 0))] * 2
        args += [cosf, sinf]
    if row_perm is not None:
        assert row_perm.shape == (tm, tm) and rope is not None
        in_specs.append(pl.BlockSpec((tm, tm), lambda i, j: (0, 0)))
        args.append(row_perm)
    return pl.pallas_call(
        functools.partial(_matmul_kernel, act=act, rope_chunks=frozenset(tiles), ntiles=n // tn,
                          permute=row_perm is not None),
        grid=(m // tm, n // tn),
        in_specs=in_specs,
        out_specs=pl.BlockSpec((tm, tn), lambda i, j: (i, j)),
        out_shape=jax.ShapeDtypeStruct((m, n), out_dtype),
        compiler_params=_params("parallel", "parallel"),
    )(*args)


def _swiglu_up_kernel(a_ref, wg_ref, wu_ref, o_ref):
    a = a_ref[...]
    g = _dot(a, wg_ref[...])
    u = _dot(a, wu_ref[...])
    o_ref[...] = (_silu(g) * u).astype(o_ref.dtype)


def swiglu_up(a, w_gu, *, tm, tn):
    m, k = a.shape
    f = w_gu.shape[1] // 2
    assert m % tm == 0 and f % tn == 0
    nj = f // tn
    return pl.pallas_call(
        _swiglu_up_kernel,
        grid=(m // tm, nj),
        in_specs=[pl.BlockSpec((tm, k), lambda i, j: (i, 0)),
                  pl.BlockSpec((k, tn), lambda i, j: (0, j)),
                  pl.BlockSpec((k, tn), lambda i, j: (0, j + nj))],
        out_specs=pl.BlockSpec((tm, tn), lambda i, j: (i, j)),
        out_shape=jax.ShapeDtypeStruct((m, f), BF16),
        compiler_params=_params("parallel", "parallel"),
    )(a, w_gu, w_gu)


def _close_sublayer(x, y, gpost, gt, nxt):
    xn = x + gt * (_rms(y) * gpost)
    if nxt is None:
        return xn, None
    gnext, sc, sh = nxt
    return xn, (_rms(xn) * gnext) * (1.0 + sc) + sh


def _matmul_close_kernel(*refs, with_next):
    if with_next:
        a_ref, w_ref, x_ref, gp_ref, gt_ref, gn_ref, sc_ref, sh_ref, xo_ref, ho_ref, acc_ref = refs
    else:
        a_ref, w_ref, x_ref, gp_ref, gt_ref, xo_ref, acc_ref = refs
    kk = pl.program_id(1)

    @pl.when(kk == 0)
    def _():
        acc_ref[...] = jnp.zeros_like(acc_ref)

    acc_ref[...] += _dot(a_ref[...], w_ref[...])

    @pl.when(kk == pl.num_programs(1) - 1)
    def _():
        nxt = (gn_ref[...], sc_ref[0], sh_ref[0]) if with_next else None
        xn, h = _close_sublayer(x_ref[...], acc_ref[...], gp_ref[...], gt_ref[0], nxt)
        xo_ref[...] = xn
        if with_next:
            ho_ref[...] = h.astype(ho_ref.dtype)


def _close_specs(tm, d, seq, with_next, nidx):
    def row(i, *_):
        return (i, 0)

    def const(*_):
        return (0, 0)

    def batch(i, *_):
        return ((i * tm) // seq, 0, 0)

    specs = [pl.BlockSpec((tm, d), row), pl.BlockSpec((1, d), const), pl.BlockSpec((1, 1, d), batch)]
    if with_next:
        specs += [pl.BlockSpec((1, d), const), pl.BlockSpec((1, 1, d), batch), pl.BlockSpec((1, 1, d), batch)]
    return specs


def matmul_close(a, w, x, gpost, gt, nxt, *, seq, tm, tk):
    m, k = a.shape
    d = w.shape[1]
    assert m % tm == 0 and k % tk == 0 and seq % tm == 0
    with_next = nxt is not None
    args = [a, w, x, gpost, gt] + (list(nxt) if with_next else [])
    in_specs = [pl.BlockSpec((tm, tk), lambda i, kk: (i, kk)), pl.BlockSpec((tk, d), lambda i, kk: (kk, 0))]
    in_specs += _close_specs(tm, d, seq, with_next, 2)
    out_shape = [jax.ShapeDtypeStruct((m, d), F32)]
    out_specs = [pl.BlockSpec((tm, d), lambda i, kk: (i, 0))]
    if with_next:
        out_shape.append(jax.ShapeDtypeStruct((m, d), BF16))
        out_specs.append(pl.BlockSpec((tm, d), lambda i, kk: (i, 0)))
    res = pl.pallas_call(
        functools.partial(_matmul_close_kernel, with_next=with_next),
        grid=(m // tm, k // tk),
        in_specs=in_specs,
        out_specs=out_specs,
        out_shape=out_shape,
        scratch_shapes=[pltpu.VMEM((tm, d), F32)],
        compiler_params=_params("parallel", "arbitrary"),
    )(*args)
    return (res[0], res[1]) if with_next else (res[0], None)


def _close_kernel(*refs, with_next):
    if with_next:
        ya_ref, yb_ref, wt_ref, x_ref, gp_ref, gt_ref, gn_ref, sc_ref, sh_ref, xo_ref, ho_ref = refs
    else:
        ya_ref, yb_ref, wt_ref, x_ref, gp_ref, gt_ref, xo_ref = refs
    nxt = (gn_ref[...], sc_ref[0], sh_ref[0]) if with_next else None
    wt = wt_ref[...]
    y = wt[:, 0:1] * ya_ref[...].astype(F32) + wt[:, 1:2] * yb_ref[...].astype(F32)
    xn, h = _close_sublayer(x_ref[...], y, gp_ref[...], gt_ref[0], nxt)
    xo_ref[...] = xn
    if with_next:
        ho_ref[...] = h.astype(ho_ref.dtype)


def close_sublayer(yab, wt, x, gpost, gt, nxt, *, seq, tm):
    m, d = x.shape
    with_next = nxt is not None
    args = [yab, yab, wt, x, gpost, gt] + (list(nxt) if with_next else [])
    in_specs = [pl.BlockSpec((tm, d), lambda i: (i, 0)), pl.BlockSpec((tm, d), lambda i: (i + m // tm, 0)),
                pl.BlockSpec((tm, LANE), lambda i: (i, 0))] + _close_specs(tm, d, seq, with_next, 1)
    out_shape = [jax.ShapeDtypeStruct((m, d), F32)]
    out_specs = [pl.BlockSpec((tm, d), lambda i: (i, 0))]
    if with_next:
        out_shape.append(jax.ShapeDtypeStruct((m, d), BF16))
        out_specs.append(pl.BlockSpec((tm, d), lambda i: (i, 0)))
    res = pl.pallas_call(
        functools.partial(_close_kernel, with_next=with_next),
        grid=(m // tm,),
        in_specs=in_specs,
        out_specs=out_specs,
        out_shape=out_shape,
        compiler_params=_params("parallel"),
    )(*args)
    return (res[0], res[1]) if with_next else (res[0], None)


def _norm_mod_kernel(x_ref, g_ref, sc_ref, sh_ref, o_ref):
    o_ref[...] = ((_rms(x_ref[...]) * g_ref[...]) * (1.0 + sc_ref[0]) + sh_ref[0]).astype(o_ref.dtype)


def norm_mod(x, g, sc, sh, *, seq, tm):
    m, d = x.shape
    batch = lambda i: ((i * tm) // seq, 0, 0)
    return pl.pallas_call(
        _norm_mod_kernel,
        grid=(m // tm,),
        in_specs=[pl.BlockSpec((tm, d), lambda i: (i, 0)), pl.BlockSpec((1, d), lambda i: (0, 0)),
                  pl.BlockSpec((1, 1, d), batch), pl.BlockSpec((1, 1, d), batch)],
        out_specs=pl.BlockSpec((tm, d), lambda i: (i, 0)),
        out_shape=jax.ShapeDtypeStruct((m, d), BF16),
        compiler_params=_params("parallel"),
    )(x, g, sc, sh)


def _branch_merge_kernel(oc_ref, os_ref, ow_ref, ng_ref, yb_ref, yc_ref, wa_ref, wb_ref, wc_ref, ga_ref, gb_ref,
                         gc_ref, o_ref, ya_ref):
    @pl.when(pl.program_id(1) == 0)
    def _():
        ng = ng_ref[...]
        for hd in range(NSA_HEADS):
            sl = slice(hd * HEAD_DIM, (hd + 1) * HEAD_DIM)
            ya = ng[:, 3 * hd:3 * hd + 1] * oc_ref[:, sl].astype(F32)
            ya += ng[:, 3 * hd + 1:3 * hd + 2] * os_ref[:, sl].astype(F32)
            ya += ng[:, 3 * hd + 2:3 * hd + 3] * ow_ref[:, sl].astype(F32)
            ya_ref[:, sl] = ya.astype(ya_ref.dtype)

    acc = ga_ref[...].astype(F32) * _dot(ya_ref[...], wa_ref[...])
    acc += gb_ref[...].astype(F32) * _dot(yb_ref[...], wb_ref[...])
    acc += gc_ref[...].astype(F32) * _dot(yc_ref[...], wc_ref[...])
    o_ref[...] = acc.astype(o_ref.dtype)


def branch_merge(o_cmp, o_slc, o_win, nsa_gate, yb, yc, wa, wb, wc, gates, *, tm, tn):
    m = yb.shape[0]
    d = wa.shape[1]
    nj = d // tn
    row = lambda width: pl.BlockSpec((tm, width), lambda i, j: (i, 0))
    wsp = lambda kdim: pl.BlockSpec((kdim, tn), lambda i, j: (0, j))
    gsp = lambda off: pl.BlockSpec((tm, tn), lambda i, j: (i, j + off * nj))
    wa_rows = wa.shape[0]
    return pl.pallas_call(
        _branch_merge_kernel,
        grid=(m // tm, nj),
        in_specs=[row(wa_rows), row(wa_rows), row(wa_rows), row(LANE), row(yb.shape[1]), row(yc.shape[1]),
                  wsp(wa_rows), wsp(wb.shape[0]), wsp(wc.shape[0]), gsp(0), gsp(1), gsp(2)],
        out_specs=pl.BlockSpec((tm, tn), lambda i, j: (i, j)),
        out_shape=jax.ShapeDtypeStruct((m, d), BF16),
        scratch_shapes=[pltpu.VMEM((tm, wa_rows), BF16)],
        compiler_params=_params("parallel", "arbitrary"),
    )(o_cmp, o_slc, o_win, nsa_gate, yb, yc, wa, wb, wc, gates, gates, gates)


def _compress_out_kernel(h_ref, w2_ref, o_ref):
    o_ref[0] = _dot(_silu(h_ref[0]).astype(BF16), w2_ref[0]).astype(o_ref.dtype)


def nsa_compress(kv_cmp, pe, w1, w2):
    b, s, width = kv_cmp.shape
    groups = s // CMP_STRIDE
    heads = NSA_KV_HEADS
    x2 = kv_cmp.reshape(b * groups, CMP_STRIDE * width)
    w1r = w1.reshape(2, 2, CMP_STRIDE, HEAD_DIM, HEAD_DIM)
    eye = jnp.eye(heads, dtype=F32)
    wcat = jnp.einsum("pk,qh,kardn->rpqdkhan", jnp.eye(2, dtype=F32), eye, w1r)
    wcat = wcat.reshape(CMP_STRIDE * width, 2 * heads * 2 * HEAD_DIM).astype(BF16)
    halves = matmul(x2, wcat, out_dtype=F32, tm=min(512, b * groups), tn=2 * HEAD_DIM)
    halves = halves.reshape(b, groups, 2, heads, 2, HEAD_DIM)
    bias = jnp.einsum("kf,kfn->kn", pe.reshape(2, CMP_BLOCK * HEAD_DIM), w1, precision=lax.Precision.HIGHEST)
    pre = halves[:, :-1, :, :, 0] + halves[:, 1:, :, :, 1] + bias[None, None, :, None, :]
    pre = jnp.pad(pre, ((0, 0), (0, 1), (0, 0), (0, 0), (0, 0))).transpose(2, 0, 3, 1, 4)
    rows = b * heads * groups
    tm = min(rows, 512)
    out = pl.pallas_call(
        _compress_out_kernel,
        grid=(2, rows // tm),
        in_specs=[pl.BlockSpec((1, tm, HEAD_DIM), lambda t, i: (t, i, 0)),
                  pl.BlockSpec((1, HEAD_DIM, HEAD_DIM), lambda t, i: (t, 0, 0))],
        out_specs=pl.BlockSpec((1, tm, HEAD_DIM), lambda t, i: (t, i, 0)),
        out_shape=jax.ShapeDtypeStruct((2, rows, HEAD_DIM), BF16),
        compiler_params=_params("parallel", "parallel"),
    )(pre.reshape(2, rows, HEAD_DIM), w2.astype(BF16))
    return out.reshape(2, b, heads, groups, HEAD_DIM)


def _cmp_select_kernel(q_ref, kc_ref, vct_ref, o_ref, sel_ref, *, tq, nc, ncp, ns, nsp, scale):
    qi = pl.program_id(2)
    kc = kc_ref[0, 0]
    vct = vct_ref[0, 0]
    spos = qi * tq + lax.broadcasted_iota(jnp.int32, (1, tq), 1)
    cidx = lax.broadcasted_iota(jnp.int32, (ncp, 1), 0)
    valid = jnp.logical_and(cidx * CMP_STRIDE + (CMP_BLOCK - 1) <= spos, cidx < nc)
    heads = range(NSA_GROUP)
    sls = [slice(g * HEAD_DIM, (g + 1) * HEAD_DIM) for g in heads]
    scores = [jnp.where(valid, _dot_nt(kc, q_ref[0, :, sls[g]]) * scale, MASKED) for g in heads]
    maxes = [jnp.max(st, axis=0, keepdims=True) for st in scores]
    es = [jnp.where(valid, jnp.exp(st - mx), 0.0) for st, mx in zip(scores, maxes)]
    dens = [jnp.sum(e, axis=0, keepdims=True) for e in es]
    probs = [e * (1.0 / jnp.where(den > 0, den, 1.0)) for e, den in zip(es, dens)]
    psum = functools.reduce(jnp.add, probs)
    outs = [_dot(vct, p.astype(BF16)) for p in probs]
    for g in heads:
        o_ref[0, :, sls[g]] = outs[g].T.astype(o_ref.dtype)
    jrow = lax.broadcasted_iota(jnp.int32, (ns, ncp), 0)
    ccol = lax.broadcasted_iota(jnp.int32, (ns, ncp), 1)
    c0 = ccol * CMP_STRIDE
    j0 = jrow * SLC_BLOCK
    cover_t = jnp.logical_and(c0 < j0 + SLC_BLOCK, c0 + CMP_BLOCK > j0).astype(F32)
    hi = psum.astype(BF16)
    rem = psum - hi.astype(F32)
    mid = rem.astype(BF16)
    lo = (rem - mid.astype(F32)).astype(BF16)
    imp3 = _dot(cover_t.astype(BF16), jnp.concatenate([hi, mid, lo], axis=1))
    imp = imp3[:, :tq] + imp3[:, tq:2 * tq] + imp3[:, 2 * tq:]
    j = lax.broadcasted_iota(jnp.int32, (ns, 1), 0).astype(F32)
    cur = (spos // SLC_BLOCK).astype(F32)
    forced = jnp.logical_or(jnp.logical_or(j == 0, j == cur), j == cur - 1)
    score = jnp.where(j <= cur, imp + FORCE_BONUS * forced.astype(F32), -jnp.inf)
    sel = jnp.zeros((ns, tq), F32)
    for _ in range(SLC_TOPN):
        mx = jnp.max(score, axis=0, keepdims=True)
        first = jnp.min(jnp.where(score == mx, j, float(ns)), axis=0, keepdims=True)
        pick = j == first
        sel = jnp.where(pick, 1.0, sel)
        score = jnp.where(pick, -jnp.inf, score)
    if ns < nsp:
        sel = jnp.concatenate([sel, jnp.zeros((nsp - ns, tq), F32)], axis=0)
    sel_ref[0, 0] = sel.T.astype(sel_ref.dtype)


def cmp_select(q, kc, vct, *, nc, tq):
    b, s, _ = q.shape
    ncp = kc.shape[2]
    nsp = LANE
    assert s % tq == 0 and s // SLC_BLOCK <= nsp
    gw = NSA_GROUP * HEAD_DIM
    return pl.pallas_call(
        functools.partial(_cmp_select_kernel, tq=tq, nc=nc, ncp=ncp, ns=-(-(s // SLC_BLOCK) // 8) * 8, nsp=nsp,
                          scale=HEAD_DIM ** -0.5),
        grid=(b, NSA_KV_HEADS, s // tq),
        in_specs=[pl.BlockSpec((1, tq, gw), lambda bi, h, i: (bi, i, h)),
                  pl.BlockSpec((1, 1, ncp, HEAD_DIM), lambda bi, h, i: (bi, h, 0, 0)),
                  pl.BlockSpec((1, 1, HEAD_DIM, ncp), lambda bi, h, i: (bi, h, 0, 0))],
        out_specs=[pl.BlockSpec((1, tq, gw), lambda bi, h, i: (bi, i, h)),
                   pl.BlockSpec((1, 1, tq, nsp), lambda bi, h, i: (bi, h, i, 0))],
        out_shape=[jax.ShapeDtypeStruct((b, s, NSA_HEADS * HEAD_DIM), BF16),
                   jax.ShapeDtypeStruct((b, NSA_KV_HEADS, s, nsp), BF16)],
        compiler_params=_params("parallel", "parallel", "parallel"),
    )(q, kc, vct)


def _slc_kernel(qi_ref, ki_ref, q_ref, k_ref, v_ref, sel_ref, cos_ref, sin_ref, o_ref, qaug_ref, m_ref, acc_ref,
                *, tq, scale):
    step = pl.program_id(2)
    qi = qi_ref[step]
    ki = ki_ref[step]

    @pl.when(ki == 0)
    def _():
        cosf = cos_ref[0]
        sinf = sin_ref[0]
        pick = sel_ref[0, 0] * SLC_PICK_BIAS
        for g in range(NSA_GROUP):
            sl = slice(g * HEAD_DIM, (g + 1) * HEAD_DIM)
            qg = _rope(q_ref[0, :, sl].astype(F32), cosf, sinf) * scale
            qaug_ref[g] = jnp.concatenate([qg.astype(BF16), pick], axis=1)
        m_ref[...] = jnp.full_like(m_ref, MASKED)
        acc_ref[...] = jnp.zeros_like(acc_ref)

    def accumulate(causal):
        nsp = sel_ref.shape[3]
        block_of_key = (ki * tq + lax.broadcasted_iota(jnp.int32, (tq, nsp), 0)) // SLC_BLOCK
        onehot = (block_of_key == lax.broadcasted_iota(jnp.int32, (tq, nsp), 1)).astype(BF16)
        kaug = jnp.concatenate([k_ref[0], onehot], axis=1)
        vaug = jnp.concatenate([v_ref[0], jnp.ones((tq, LANE), BF16)], axis=1)
        if causal:
            qpos = lax.broadcasted_iota(jnp.int32, (tq, tq), 0)
            kpos = lax.broadcasted_iota(jnp.int32, (tq, tq), 1)
            visible = kpos <= qpos
        heads = range(NSA_GROUP)
        scores = [_dot_nt(qaug_ref[g], kaug) for g in heads]
        if causal:
            scores = [jnp.where(visible, s, MASKED) for s in scores]
        m_olds = [m_ref[g] for g in heads]
        m_news = [jnp.maximum(mo, jnp.max(s, axis=-1, keepdims=True)) for mo, s in zip(m_olds, scores)]
        alphas = [jnp.exp(mo - mn) for mo, mn in zip(m_olds, m_news)]
        probs = [jnp.exp((s - jnp.concatenate([mn] * (tq // LANE), axis=1)).astype(BF16))
                 for s, mn in zip(scores, m_news)]
        pvs = [_dot(p, vaug) for p in probs]
        for g in heads:
            acc_ref[g] = jnp.concatenate([alphas[g], alphas[g]], axis=1) * acc_ref[g] + pvs[g]
            m_ref[g] = m_news[g]

    @pl.when(ki < qi)
    def _():
        accumulate(False)

    @pl.when(ki == qi)
    def _():
        accumulate(True)
        for g in range(NSA_GROUP):
            sl = slice(g * HEAD_DIM, (g + 1) * HEAD_DIM)
            o_ref[0, :, sl] = (acc_ref[g, :, :HEAD_DIM] / acc_ref[g, :, HEAD_DIM:]).astype(o_ref.dtype)


def slc_attention(q, kv, sel, cosf, sinf, *, k_col, v_col, tq):
    b, s, _ = q.shape
    gw = NSA_GROUP * HEAD_DIM
    nq = s // tq
    nsp = sel.shape[3]
    pairs = [(i, j) for i in range(nq) for j in range(i + 1)]
    qi_tab = jnp.asarray([p[0] for p in pairs], jnp.int32)
    ki_tab = jnp.asarray([p[1] for p in pairs], jnp.int32)
    grid_spec = pltpu.PrefetchScalarGridSpec(
        num_scalar_prefetch=2,
        grid=(b, NSA_KV_HEADS, len(pairs)),
        in_specs=[pl.BlockSpec((1, tq, gw), lambda bi, h, t, qt, kt: (bi, qt[t], h)),
                  pl.BlockSpec((1, tq, HEAD_DIM), lambda bi, h, t, qt, kt: (bi, kt[t], k_col + h)),
                  pl.BlockSpec((1, tq, HEAD_DIM), lambda bi, h, t, qt, kt: (bi, kt[t], v_col + h)),
                  pl.BlockSpec((1, 1, tq, nsp), lambda bi, h, t, qt, kt: (bi, h, qt[t], 0)),
                  pl.BlockSpec((1, tq, LANE), lambda bi, h, t, qt, kt: (bi, qt[t], 0)),
                  pl.BlockSpec((1, tq, LANE), lambda bi, h, t, qt, kt: (bi, qt[t], 0))],
        out_specs=pl.BlockSpec((1, tq, gw), lambda bi, h, t, qt, kt: (bi, qt[t], h)),
        scratch_shapes=[pltpu.VMEM((NSA_GROUP, tq, 2 * HEAD_DIM), BF16),
                        pltpu.VMEM((NSA_GROUP, tq, LANE), F32),
                        pltpu.VMEM((NSA_GROUP, tq, 2 * HEAD_DIM), F32)],
    )
    return pl.pallas_call(
        functools.partial(_slc_kernel, tq=tq, scale=HEAD_DIM ** -0.5),
        grid_spec=grid_spec,
        out_shape=jax.ShapeDtypeStruct((b, s, NSA_HEADS * HEAD_DIM), BF16),
        compiler_params=_params("parallel", "parallel", "arbitrary"),
    )(qi_tab, ki_tab, q, kv, kv, sel, cosf, sinf)


def _band_kernel(*refs, nheads, kv_heads, nkv, tq, max_dist, rope_q, with_lse, scale, nsub):
    q_ref = refs[0]
    k_refs = refs[1:1 + nkv]
    v_refs = refs[1 + nkv:1 + 2 * nkv]
    pos = 1 + 2 * nkv
    if rope_q:
        cos_ref, sin_ref = refs[pos:pos + 2]
        pos += 2
    o_ref = refs[pos]
    lse_ref = refs[pos + 1] if with_lse else None
    qi = pl.program_id(2)
    qpos = qi * tq + lax.broadcasted_iota(jnp.int32, (tq, 1), 0)
    kpos = (qi - (nkv - 1)) * tq + lax.broadcasted_iota(jnp.int32, (1, nkv * tq), 1)
    diff = qpos - kpos
    mask = jnp.logical_and(jnp.logical_and(diff >= 0, diff <= max_dist), kpos >= 0)
    lane = lax.broadcasted_iota(jnp.int32, (tq, LANE), 1)
    jobs = [(u, g) for u in range(nsub) for g in range(nheads)]

    def rows(u):
        return slice(u * tq, (u + 1) * tq)

    def cols(g, kv=False):
        return slice(0, HEAD_DIM) if (kv and kv_heads == 1) else slice(g * HEAD_DIM, (g + 1) * HEAD_DIM)

    scores = []
    for u, g in jobs:
        q = q_ref[0, rows(u), cols(g)]
        if rope_q:
            q = _rope(q.astype(F32), cos_ref[0], sin_ref[0]).astype(BF16)
        kcat = jnp.concatenate([r[0, rows(u), cols(g, True)] for r in k_refs], axis=0)
        scores.append(jnp.where(mask, _dot_nt(q, kcat) * scale, MASKED))
    maxes = [jnp.max(s, axis=-1, keepdims=True) for s in scores]
    probs = [jnp.exp((s - mx).astype(BF16)) for s, mx in zip(scores, maxes)]
    pvs = []
    for (u, g), e in zip(jobs, probs):
        vcat = jnp.concatenate([r[0, rows(u), cols(g, True)] for r in v_refs], axis=0)
        pvs.append(_dot(e, jnp.concatenate([vcat, jnp.ones(vcat.shape, BF16)], axis=1)))
    lse_acc = [jnp.zeros((tq, LANE), F32) for _ in range(nsub)]
    for (u, g), pv, mx in zip(jobs, pvs, maxes):
        den = pv[:, HEAD_DIM:]
        o_ref[0, rows(u), cols(g)] = (pv[:, :HEAD_DIM] / den).astype(o_ref.dtype)
        if with_lse:
            lse_acc[u] = jnp.where(lane == g, mx + jnp.log(den), lse_acc[u])
    if with_lse:
        for u in range(nsub):
            lse_ref[0, rows(u)] = lse_acc[u]


def band_attention(q, k, v, *, nheads, kv_heads, q_col, k_col, v_col, o_cols, ncol, max_dist, tq,
                   rope=None, with_lse=False, out_dtype=BF16, seq_len=None, row_block=None, nsub=1):
    b = q.shape[0]
    rows = q.shape[1]
    seq_len = rows if seq_len is None else seq_len
    tq = min(tq, seq_len)
    assert seq_len % tq == 0 and ncol % nsub == 0
    nkv = -(-max_dist // tq) + 1
    qw = nheads * HEAD_DIM
    kw = kv_heads * HEAD_DIM
    bt = nsub * tq
    if row_block is None:
        assert nsub == 1
        row_block = lambda c, i: i
        o_col = lambda c: c
    else:
        assert rope is None
        o_col = lambda c: 0

    def kv_spec(col_fn, back):
        return pl.BlockSpec((1, bt, kw), lambda bi, c, i: (bi, row_block(c, jnp.maximum(i - back, 0)), col_fn(c)))

    in_specs = [pl.BlockSpec((1, bt, qw), lambda bi, c, i: (bi, row_block(c, i), q_col(c)))]
    in_specs += [kv_spec(k_col, nkv - 1 - t) for t in range(nkv)]
    in_specs += [kv_spec(v_col, nkv - 1 - t) for t in range(nkv)]
    args = [q] + [k] * nkv + [v] * nkv
    if rope is not None:
        in_specs += [pl.BlockSpec((1, tq, LANE), lambda bi, c, i: (bi, i, 0))] * 2
        args += list(rope)
    out_specs = [pl.BlockSpec((1, bt, qw), lambda bi, c, i: (bi, row_block(c, i), o_col(c)))]
    out_shape = [jax.ShapeDtypeStruct((b, rows, o_cols * qw), out_dtype)]
    if with_lse:
        out_specs.append(pl.BlockSpec((1, bt, LANE), lambda bi, c, i: (bi, row_block(c, i), o_col(c))))
        out_shape.append(jax.ShapeDtypeStruct((b, rows, o_cols * LANE), F32))
    res = pl.pallas_call(
        functools.partial(_band_kernel, nheads=nheads, kv_heads=kv_heads, nkv=nkv, tq=tq, max_dist=max_dist,
                          rope_q=rope is not None, with_lse=with_lse, scale=HEAD_DIM ** -0.5, nsub=nsub),
        grid=(b, ncol // nsub, seq_len // tq),
        in_specs=in_specs,
        out_specs=out_specs,
        out_shape=out_shape,
        compiler_params=_params("parallel", "parallel", "parallel"),
    )(*args)
    return res if with_lse else res[0]


def _bmm(x, y):
    return jnp.einsum("bij,bjk->bik", x.astype(BF16), y.astype(BF16), preferred_element_type=F32)


def _bmm_nt(x, y):
    return jnp.einsum("bik,bjk->bij", x.astype(BF16), y.astype(BF16), preferred_element_type=F32)


def _head_sums(x2, scale=1.0):
    n = x2.shape[1]
    blk = (lax.broadcasted_iota(jnp.int32, (n, n), 0) // RWKV_HEAD
           == lax.broadcasted_iota(jnp.int32, (n, n), 1) // RWKV_HEAD)
    ones = jnp.where(blk, scale, 0.0).astype(BF16)
    hi = x2.astype(BF16)
    lo = (x2 - hi.astype(F32)).astype(BF16)
    return _dot(hi, ones) + _dot(lo, ones)


def _wkv_chunk_kernel(*refs, nb, c, mix):
    if mix:
        zr_ref, zk_ref, zv_ref, wl_ref, al_ref, vu_ref, vf_ref, vec_ref = refs[:8]
    else:
        zr_ref, zk_ref, zv_ref, wl_ref, al_ref, vec_ref = refs[:6]
    q_ref, y0_ref, gt_ref, ht_ref, bonus_ref = refs[-5:]
    pair = 2 * RWKV_HEAD
    c2 = 2 * c
    rows = nb * c
    vec = vec_ref[...]
    w0, a0, k_k, k_a, r_k, v0 = [vec[i:i + 1] for i in range(6)]
    r2 = zr_ref[0]
    kraw = zk_ref[0]
    v2 = zv_ref[0]
    x = w0 + wl_ref[0]
    softplus_neg = jnp.maximum(-x, 0.0) + jnp.log(1.0 + jnp.exp(-jnp.abs(x)))
    lw2 = -jnp.exp(-softplus_neg - 0.5)
    a_gate = _sigmoid(a0 + al_ref[0])
    if mix:
        v2 = v2 + (vf_ref[0] - v2) * _sigmoid(v0 + vu_ref[0])
    kk = kraw * k_k
    kk = kk * lax.rsqrt(jnp.maximum(_head_sums(kk * kk), 1e-24))
    k2 = kraw * (1.0 + (a_gate - 1.0) * k_a)
    bonus_ref[0] = (_head_sums(r2 * k2 * r_k) * v2).astype(bonus_ref.dtype)

    def chunks(x2):
        return x2.reshape(nb, c, pair)

    r, lw, k, v, a, b = [chunks(t) for t in (r2, lw2, k2, v2, -kk, kk * a_gate)]
    row = lax.broadcasted_iota(jnp.int32, (c, c), 0)
    col = lax.broadcasted_iota(jnp.int32, (c, c), 1)
    tril = jnp.broadcast_to((row >= col).astype(BF16), (nb, c, c))
    hi = lw.astype(BF16)
    rem = lw - hi.astype(F32)
    mid = rem.astype(BF16)
    lo = (rem - mid.astype(F32)).astype(BF16)
    cum3 = _bmm(tril, jnp.concatenate([hi, mid, lo], axis=2))
    cum = cum3[:, :, :pair] + cum3[:, :, pair:2 * pair] + cum3[:, :, 2 * pair:]
    last = cum[:, c - 1:c, :]
    inv = jnp.exp(-cum)
    tail = jnp.exp(last - cum)
    lane = lax.broadcasted_iota(jnp.int32, (1, 1, pair), 2)
    first = lane < RWKV_HEAD

    def stack(x):
        return jnp.concatenate([jnp.where(first, x, 0.0), jnp.where(first, 0.0, x)], axis=1)

    a_s = stack(a * jnp.exp(cum - lw))
    r_s = stack(r * jnp.exp(cum))
    b_s = stack(b * inv)
    k_s = stack(k * inv)
    v_s = stack(v)
    bh_s = stack(b * tail)
    kh_s = stack(k * tail)
    ar = jnp.concatenate([a_s, r_s], axis=1)
    pbk = _bmm_nt(ar, jnp.concatenate([b_s, k_s], axis=1))
    pb = pbk[:, :, :c2]
    pk = pbk[:, :, c2:]
    row2 = lax.broadcasted_iota(jnp.int32, (c2, c2), 0) % c
    col2 = lax.broadcasted_iota(jnp.int32, (c2, c2), 1) % c
    strict = row2 > col2
    lower = row2 >= col2
    l_ab = jnp.where(strict, pb[:, :c2], 0.0)
    m_rb = jnp.where(lower, pb[:, c2:], 0.0)
    l_ak = jnp.where(strict, pk[:, :c2], 0.0)
    m_rk = jnp.where(lower, pk[:, c2:], 0.0)
    eye = (lax.broadcasted_iota(jnp.int32, (c2, c2), 0) == lax.broadcasted_iota(jnp.int32, (c2, c2), 1))
    eye = eye.astype(F32)
    tinv = eye + l_ab
    pw = _bmm(l_ab, l_ab)
    for _ in range(int(np.log2(c)) - 2):
        both = _bmm(jnp.concatenate([tinv, pw], axis=1), pw)
        tinv = tinv + both[:, :c2]
        pw = both[:, c2:]
    tinv = tinv + _bmm(tinv, pw)
    tu = _bmm(tinv, jnp.concatenate([_bmm(l_ak, v_s), a_s], axis=2))
    u0_s = tu[:, :, :pair]
    ta_s = tu[:, :, pair:]
    mu = _bmm(m_rb, jnp.concatenate([ta_s, u0_s], axis=2))
    q_s = r_s + mu[:, :, :pair]
    y0_s = _bmm(m_rk, v_s) + mu[:, :, pair:]
    q_ref[0] = (q_s[:, :c] + q_s[:, c:]).reshape(nb * c, pair).astype(q_ref.dtype)
    y0_ref[0] = (y0_s[:, :c] + y0_s[:, c:]).reshape(nb * c, pair).astype(y0_ref.dtype)
    bh_t = jnp.swapaxes(bh_s, 1, 2)
    kh_t = jnp.swapaxes(kh_s, 1, 2)
    gh = _bmm(bh_t, jnp.concatenate([ta_s, u0_s], axis=2))
    gt_ref[0, :, 0] = (eye * jnp.exp(last) + gh[:, :, :pair]).astype(gt_ref.dtype)
    ht_ref[0, :, 0] = (gh[:, :, pair:] + _bmm(kh_t, v_s)).astype(ht_ref.dtype)


def _wkv_scan_kernel(q_ref, y0_ref, gt_ref, ht_ref, bonus_ref, g_ref, lnx_ref, o_ref, state_ref, *, npair):
    pair = 2 * RWKV_HEAD
    bsz, c, _ = q_ref.shape

    @pl.when(pl.program_id(0) == 0)
    def _():
        state_ref[...] = jnp.zeros_like(state_ref)

    sls = [slice(p * pair, (p + 1) * pair) for p in range(npair)]
    idx = [(bi, p) for bi in range(bsz) for p in range(npair)]
    st = [state_ref[bi, p].astype(BF16) for bi, p in idx]
    ys = [_dot(q_ref[bi, :, sls[p]], s_) + y0_ref[bi, :, sls[p]].astype(F32) for (bi, p), s_ in zip(idx, st)]
    new = [_dot(gt_ref[bi, 0, p], s_) + ht_ref[bi, 0, p].astype(F32) for (bi, p), s_ in zip(idx, st)]
    for (bi, p), s_ in zip(idx, new):
        state_ref[bi, p] = s_
    y = jnp.concatenate(ys, axis=0)
    dev = y - _head_sums(y, 1.0 / RWKV_HEAD)
    yn = dev * lax.rsqrt(_head_sums(dev * dev, 1.0 / RWKV_HEAD) + LNX_EPS)
    for n, (bi, p) in enumerate(idx):
        ln = yn[n * c:(n + 1) * c] * lnx_ref[0:1, sls[p]] + lnx_ref[1:2, sls[p]]
        out = (ln + bonus_ref[bi, :, sls[p]].astype(F32)) * g_ref[bi, :, sls[p]].astype(F32)
        o_ref[bi, :, sls[p]] = out.astype(o_ref.dtype)


def wkv7(zs, zs_first, wl, al, vu, vec, g, lnx):
    bsz, s, _ = zs.shape
    width = wl.shape[2]
    pair = 2 * RWKV_HEAD
    npair = width // pair
    c = min(RWKV_CHUNK, s)
    nb = min(RWKV_CHUNKS_PER_STEP, s // c)
    nch = s // c
    mix = vu is not None
    assert s % (nb * c) == 0 and width % pair == 0

    def col(off):
        return pl.BlockSpec((1, nb * c, pair), lambda bi, p, j: (bi, j, off + p))

    vec_spec = pl.BlockSpec((8, pair), lambda bi, p, j: (0, p))
    mat = pl.BlockSpec((1, nb, 1, pair, pair), lambda bi, p, j: (bi, j, p, 0, 0))
    in_specs = [col(0), col(npair), col(2 * npair), col(0), col(0)]
    args = [zs, zs, zs, wl, al]
    if mix:
        in_specs += [col(0), col(2 * npair)]
        args += [vu, zs_first]
    q, y0, gt, ht, bonus = pl.pallas_call(
        functools.partial(_wkv_chunk_kernel, nb=nb, c=c, mix=mix),
        grid=(bsz, npair, nch // nb),
        in_specs=in_specs + [vec_spec],
        out_specs=[col(0), col(0), mat, mat, col(0)],
        out_shape=[jax.ShapeDtypeStruct((bsz, s, width), BF16), jax.ShapeDtypeStruct((bsz, s, width), BF16),
                   jax.ShapeDtypeStruct((bsz, nch, npair, pair, pair), BF16),
                   jax.ShapeDtypeStruct((bsz, nch, npair, pair, pair), BF16),
                   jax.ShapeDtypeStruct((bsz, s, width), BF16)],
        compiler_params=_params("parallel", "parallel", "parallel"),
    )(*args, vec)
    row = pl.BlockSpec((bsz, c, width), lambda j: (0, j, 0))
    mats = pl.BlockSpec((bsz, 1, npair, pair, pair), lambda j: (0, j, 0, 0, 0))
    return pl.pallas_call(
        functools.partial(_wkv_scan_kernel, npair=npair),
        grid=(nch,),
        in_specs=[row, row, mats, mats, row, row, pl.BlockSpec((8, width), lambda j: (0, 0))],
        out_specs=row,
        out_shape=jax.ShapeDtypeStruct((bsz, s, width), BF16),
        scratch_shapes=[pltpu.VMEM((bsz, npair, pair, pair), F32)],
        compiler_params=_params("arbitrary"),
    )(q, y0, gt, ht, bonus, g, lnx)


def _moe_up_kernel(te_ref, tv_ref, a_ref, wg_ref, wu_ref, o_ref, wg_bf, wu_bf):
    i = pl.program_id(1)
    changed = jnp.logical_or(i == 0, te_ref[i] != te_ref[jnp.maximum(i - 1, 0)])

    @pl.when(changed)
    def _():
        wg_bf[...] = wg_ref[...].astype(BF16)
        wu_bf[...] = wu_ref[...].astype(BF16)

    @pl.when(tv_ref[i] > 0)
    def _():
        a = a_ref[...]
        g = _dot(a, wg_bf[...])
        u = _dot(a, wu_bf[...])
        o_ref[...] = (_silu(g) * u).astype(o_ref.dtype)

    @pl.when(tv_ref[i] == 0)
    def _():
        o_ref[...] = jnp.zeros_like(o_ref)


def moe_up(tile_expert, tile_valid, xs, w_gu, layer, *, tm, tn):
    r, k = xs.shape
    f = w_gu.shape[3] // 2
    nj = f // tn
    grid_spec = pltpu.PrefetchScalarGridSpec(
        num_scalar_prefetch=2,
        grid=(nj, r // tm),
        in_specs=[pl.BlockSpec((tm, k), lambda j, i, te, tv: (i, 0)),
                  pl.BlockSpec((None, None, k, tn), lambda j, i, te, tv: (layer, te[i], 0, j)),
                  pl.BlockSpec((None, None, k, tn), lambda j, i, te, tv: (layer, te[i], 0, j + nj))],
        out_specs=pl.BlockSpec((tm, tn), lambda j, i, te, tv: (i, j)),
        scratch_shapes=[pltpu.VMEM((k, tn), BF16), pltpu.VMEM((k, tn), BF16)],
    )
    return pl.pallas_call(
        _moe_up_kernel,
        grid_spec=grid_spec,
        out_shape=jax.ShapeDtypeStruct((r, f), BF16),
        compiler_params=_params("parallel", "arbitrary"),
    )(tile_expert, tile_valid, xs, w_gu, w_gu)


def _moe_down_kernel(te_ref, tv_ref, a_ref, w_ref, o_ref, w_bf):
    i = pl.program_id(1)
    changed = jnp.logical_or(i == 0, te_ref[i] != te_ref[jnp.maximum(i - 1, 0)])

    @pl.when(changed)
    def _():
        w_bf[...] = w_ref[...].astype(BF16)

    @pl.when(tv_ref[i] > 0)
    def _():
        o_ref[...] = _dot(a_ref[...], w_bf[...]).astype(o_ref.dtype)

    @pl.when(tv_ref[i] == 0)
    def _():
        o_ref[...] = jnp.zeros_like(o_ref)


def moe_down(tile_expert, tile_valid, act, w_down, layer, *, tm, tn):
    r, f = act.shape
    d = w_down.shape[3]
    grid_spec = pltpu.PrefetchScalarGridSpec(
        num_scalar_prefetch=2,
        grid=(d // tn, r // tm),
        in_specs=[pl.BlockSpec((tm, f), lambda j, i, te, tv: (i, 0)),
                  pl.BlockSpec((None, None, f, tn), lambda j, i, te, tv: (layer, te[i], 0, j))],
        out_specs=pl.BlockSpec((tm, tn), lambda j, i, te, tv: (i, j)),
        scratch_shapes=[pltpu.VMEM((f, tn), BF16)],
    )
    return pl.pallas_call(
        _moe_down_kernel,
        grid_spec=grid_spec,
        out_shape=jax.ShapeDtypeStruct((r, d), BF16),
        compiler_params=_params("parallel", "arbitrary"),
    )(tile_expert, tile_valid, act, w_down)


def _rope_tables(positions):
    inv_freq = ROPE_THETA ** (-jnp.arange(ROPE_HALF, dtype=F32) / ROPE_HALF)
    ang = positions.astype(F32)[:, :, None] * inv_freq
    cos = jnp.cos(ang)
    sin = jnp.sin(ang)
    b, s = positions.shape
    pad1 = jnp.ones((b, s, HEAD_DIM - ROPE_DIM), F32)
    pad0 = jnp.zeros((b, s, HEAD_DIM - ROPE_DIM), F32)
    return jnp.concatenate([cos, cos, pad1], axis=-1), jnp.concatenate([-sin, sin, pad0], axis=-1)


def _pad_cols(w, n):
    return w if w.shape[-1] == n else jnp.pad(w, ((0, 0), (0, n - w.shape[-1])))


def nsa_branch(q_a, kv_a, cosf, sinf, cmp_pe, cmp_w1, cmp_w2):
    b, s, _ = q_a.shape
    nc = s // CMP_STRIDE - 1
    ncp = -(-nc // LANE) * LANE
    kvw = NSA_KV_HEADS * HEAD_DIM
    comp = nsa_compress(kv_a[..., :2 * kvw], cmp_pe, cmp_w1, cmp_w2)
    comp = jnp.pad(comp, ((0, 0), (0, 0), (0, 0), (0, ncp - comp.shape[3]), (0, 0)))
    kc = comp[0]
    vct = comp[1].transpose(0, 1, 3, 2)
    tq = min(256, s)
    o_cmp, sel = cmp_select(q_a, kc, vct, nc=nc, tq=tq)
    o_slc = slc_attention(q_a, kv_a, sel, cosf, sinf, k_col=4, v_col=6, tq=min(512, s))
    o_win = band_attention(q_a, kv_a, kv_a, nheads=NSA_GROUP, kv_heads=1, q_col=lambda c: c,
                           k_col=lambda c: 8 + c, v_col=lambda c: 10 + c, o_cols=NSA_KV_HEADS,
                           ncol=NSA_KV_HEADS, max_dist=WIN_SIZE - 1, tq=tq, rope=(cosf, sinf))
    return o_cmp, o_slc, o_win


def rwkv_branch(zs, zs_first, vec, w_up, a_up, g_up, v_res):
    b, s, _ = zs.shape
    t = b * s
    w_ = RWKV_WIDTH
    o = 3 * w_
    zw, za = zs[..., o:o + LORA_W], zs[..., o + LORA_W:o + LORA_W + LORA_A]
    zg = zs[..., o + LORA_W + LORA_A:o + LORA_W + LORA_A + LORA_G]
    w0, a0, k_k, k_a, r_k, lnx_g, lnx_b = [vec[i] for i in range(7)]

    def lora(xin, wmat, out_dtype=F32):
        return matmul(xin.reshape(t, -1).astype(BF16), wmat.astype(BF16), out_dtype=out_dtype, tm=1024,
                      tn=wmat.shape[1]).reshape(b, s, -1)

    wl = lora(jnp.tanh(zw), w_up)
    al = lora(za, a_up)
    g = lora(jax.nn.sigmoid(zg), g_up, BF16)
    zero = jnp.zeros_like(w0)
    if v_res is None:
        vu, v0 = None, zero
    else:
        v0, v_down, v_up = v_res
        vd = matmul(zs.reshape(t, -1), _pad_cols(v_down, LANE).astype(BF16), out_dtype=BF16, tm=1024, tn=LANE,
                    a_col=2)
        vu = matmul(vd, jnp.pad(v_up, ((0, LANE - v_up.shape[0]), (0, 0))).astype(BF16), out_dtype=F32, tm=1024,
                    tn=512).reshape(b, s, w_)
    vecs = jnp.stack([w0, a0, k_k, k_a, r_k, v0, zero, zero])
    lnx = jnp.stack([lnx_g, lnx_b] + [zero] * 6)
    return wkv7(zs, zs_first, wl, al, vu, vecs, g, lnx)


DIL_ROW_TILE = 1024


def residue_perm(dil, transpose=False):
    i = np.arange(DIL_ROW_TILE)
    per = DIL_ROW_TILE // dil
    p = np.zeros((DIL_ROW_TILE, DIL_ROW_TILE), np.float32)
    p[i, (i % per) * dil + i // per] = 1.0
    return jnp.asarray(p.T if transpose else p, BF16)


def _dilated_merge_kernel(*refs, ngroups, nperm):
    o_refs = refs[:ngroups]
    l_refs = refs[ngroups:2 * ngroups]
    p_refs = refs[2 * ngroups:2 * ngroups + nperm]
    y_ref = refs[-1]
    outs, lses = [], []
    for g in range(ngroups):
        if g < ngroups - nperm:
            outs.append(o_refs[g][...].astype(F32))
            lses.append(l_refs[g][...])
            continue
        pt = p_refs[g - (ngroups - nperm)][...]
        outs.append(_dot(pt, o_refs[g][...]))
        lse = l_refs[g][...]
        hi = lse.astype(BF16)
        lo = (lse - hi.astype(F32)).astype(BF16)
        both = _dot(pt, jnp.concatenate([hi, lo], axis=1))
        lses.append(both[:, :LANE] + both[:, LANE:])
    mx = functools.reduce(jnp.maximum, lses)
    es = [jnp.exp(lse - mx) for lse in lses]
    inv = 1.0 / functools.reduce(jnp.add, es)
    for hd in range(DIL_HPG):
        sl = slice(hd * HEAD_DIM, (hd + 1) * HEAD_DIM)
        acc = functools.reduce(jnp.add, [e[:, hd:hd + 1] * o[:, sl] for e, o in zip(es, outs)])
        y_ref[:, sl] = (acc * inv[:, hd:hd + 1]).astype(y_ref.dtype)


def dilated_branch(qkvs):
    b, s, width = qkvs[0].shape
    t = b * s
    assert s % DIL_ROW_TILE == 0
    outs, lses, perms = [], [], []
    for gi, (win, dil) in enumerate(DIL_PATTERNS):
        common = dict(nheads=DIL_HPG, kv_heads=DIL_HPG, q_col=lambda c: 0, k_col=lambda c: 1, v_col=lambda c: 2,
                      o_cols=1, max_dist=win // dil, with_lse=True, out_dtype=BF16)
        if dil == 1:
            assert not perms
            o, lse = band_attention(qkvs[gi], qkvs[gi], qkvs[gi], ncol=1, tq=256, **common)
        else:
            tq = DIL_ROW_TILE // dil
            nsub = max(1, 256 // tq)
            o, lse = band_attention(qkvs[gi], qkvs[gi], qkvs[gi], ncol=dil, tq=tq, seq_len=s // dil, nsub=nsub,
                                    row_block=lambda c, i, per=dil // nsub: i * per + c, **common)
            perms.append(residue_perm(dil, transpose=True))
        outs.append(o.reshape(t, DIL_OUT))
        lses.append(lse.reshape(t, LANE))
    ng = len(DIL_PATTERNS)
    row = lambda width_: pl.BlockSpec((DIL_ROW_TILE, width_), lambda i: (i, 0))
    const = pl.BlockSpec((DIL_ROW_TILE, DIL_ROW_TILE), lambda i: (0, 0))
    return pl.pallas_call(
        functools.partial(_dilated_merge_kernel, ngroups=ng, nperm=len(perms)),
        grid=(t // DIL_ROW_TILE,),
        in_specs=[row(DIL_OUT)] * ng + [row(LANE)] * ng + [const] * len(perms),
        out_specs=row(DIL_OUT),
        out_shape=jax.ShapeDtypeStruct((t, DIL_OUT), BF16),
        compiler_params=_params("parallel"),
    )(*outs, *lses, *perms)


def moe_ffn(h, router_w, router_b, w_gu, w_down, layer, *, tm):
    t, d = h.shape
    logits = matmul(h, _pad_cols(router_w, LANE).astype(BF16), out_dtype=F32, tm=1024, tn=LANE)[:, :N_EXPERTS]
    logits = logits + router_b
    top_v, top_i = lax.top_k(logits, TOP_K)
    wts = jax.nn.softmax(top_v, axis=-1)
    flat_e = top_i.reshape(-1)
    onehot = (flat_e[:, None] == jnp.arange(N_EXPERTS)[None, :]).astype(jnp.int32)
    rank = jnp.take_along_axis(jnp.cumsum(onehot, axis=0), flat_e[:, None], axis=1)[:, 0] - 1
    counts = jnp.sum(onehot, axis=0)
    tiles_per = (counts + tm - 1) // tm
    tile_end = jnp.cumsum(tiles_per)
    group_start = (tile_end - tiles_per) * tm
    dest = group_start[flat_e] + rank
    ntiles = (TOP_K * t) // tm + N_EXPERTS
    rows = ntiles * tm
    row_token = jnp.zeros((rows,), jnp.int32).at[dest].set(jnp.arange(TOP_K * t, dtype=jnp.int32) // TOP_K)
    tile_ids = jnp.arange(ntiles, dtype=jnp.int32)
    tile_valid = (tile_ids < tile_end[-1]).astype(jnp.int32)
    tile_expert = jnp.minimum(jnp.searchsorted(tile_end, tile_ids, side="right"), N_EXPERTS - 1).astype(jnp.int32)
    xs = jnp.take(h, row_token, axis=0, mode="clip")
    act = moe_up(tile_expert, tile_valid, xs, w_gu, layer, tm=tm, tn=min(1024, w_gu.shape[3] // 2))
    out = moe_down(tile_expert, tile_valid, act, w_down, layer, tm=tm, tn=min(1024, d))
    slot_major = dest.reshape(t, TOP_K).T.reshape(-1)
    yab = jnp.take(out, slot_major, axis=0, mode="clip")
    return yab, jnp.pad(wts, ((0, 0), (0, LANE - TOP_K)))


def kernel(x, c, positions, ada_w, ada_b, norm_g, w_in, cmp_pe, cmp_w1, cmp_w2, rwkv_mu, rwkv_vec, w_up, a_up, g_up, v_res0, v_res_down, v_res_up, w_br_a, w_br_b, w_br_c, w_out, ffn_gu, ffn_down, router_w, router_b, moe_gu, moe_down):
    b, s, d = x.shape
    depth = ada_w.shape[0]
    t = b * s
    tm_row = min(512, s)
    cosf, sinf = _rope_tables(positions)
    cos_t = cosf.reshape(t, LANE)
    sin_t = sinf.reshape(t, LANE)

    cond = jnp.pad(jax.nn.silu(c), ((0, 8 - b % 8 if b % 8 else 0), (0, 0))).astype(BF16)
    mods = []
    for l in range(depth):
        mod = matmul(cond, ada_w, out_dtype=F32, tm=cond.shape[0], tn=d, layer=l)[:b] + ada_b[l]
        mods.append(mod.reshape(b, 6, 1, d))

    def mod_of(l, i):
        return mods[l][:, i]

    q_cols = NSA_HEADS * HEAD_DIM
    kv_cols = 6 * NSA_KV_HEADS * HEAD_DIM
    gate_cols = 3 * NSA_HEADS
    rwkv_cols = 3 * RWKV_WIDTH + LORA_W + LORA_A + LORA_G
    dil_cols = 3 * DIL_HEADS * HEAD_DIM
    offs = np.cumsum([0, q_cols, kv_cols, gate_cols, rwkv_cols, dil_cols, 3 * d]).tolist()
    rwkv_pad = -(-rwkv_cols // 512) * 512

    xf = x.reshape(t, d)
    h = norm_mod(xf, norm_g[0, 0][None], mod_of(0, 1), mod_of(0, 0), seq=s, tm=tm_row)
    zs_first = None
    for l in range(depth):
        wl = w_in[l]
        seg = lambda i: wl[:, offs[i]:offs[i + 1]]
        q_a = matmul(h, seg(0).astype(BF16), out_dtype=BF16, tm=1024, tn=q_cols)
        kv_a = matmul(h, seg(1).astype(BF16), out_dtype=BF16, tm=1024, tn=kv_cols, rope=(cos_t, sin_t, (4, 5, 8, 9)))
        gate_a = matmul(h, _pad_cols(seg(2), LANE).astype(BF16), out_dtype=F32, tm=1024, tn=LANE, act="sigmoid")
        zs = matmul_token_shift(h, _pad_cols(seg(3), rwkv_pad).astype(BF16), _pad_cols(rwkv_mu[l][None], rwkv_pad),
                                seq=s, tm=1024, tn=rwkv_pad // 4).reshape(b, s, -1)
        zs_first = zs if l == 0 else zs_first
        wc = seg(4).reshape(d, 3, len(DIL_PATTERNS), DIL_OUT)
        qkv_c = [matmul(h, wc[:, :, gi].reshape(d, 3 * DIL_OUT).astype(BF16), out_dtype=BF16, tm=DIL_ROW_TILE,
                        tn=3 * DIL_OUT, rope=(cos_t, sin_t, tuple(range(2 * DIL_HPG))),
                        row_perm=None if dil == 1 else residue_perm(dil)).reshape(b, s, -1)
                 for gi, (_, dil) in enumerate(DIL_PATTERNS)]
        mg = matmul(h, seg(5).astype(BF16), out_dtype=BF16, tm=1024, tn=1024, act="sigmoid")

        o_cmp, o_slc, o_win = nsa_branch(q_a.reshape(b, s, -1), kv_a.reshape(b, s, -1), cosf, sinf,
                                         cmp_pe[l], cmp_w1[l], cmp_w2[l])
        v_res = None if l == 0 else (v_res0[l - 1], v_res_down[l - 1], v_res_up[l - 1])
        y_b = rwkv_branch(zs, zs_first, rwkv_vec[l], w_up[l], a_up[l], g_up[l], v_res)
        y_c = dilated_branch(qkv_c)
        merged = branch_merge(o_cmp.reshape(t, -1), o_slc.reshape(t, -1), o_win.reshape(t, -1), gate_a,
                              y_b.reshape(t, -1), y_c.reshape(t, -1), w_br_a[l].astype(BF16),
                              w_br_b[l].astype(BF16), w_br_c[l].astype(BF16), mg, tm=1024, tn=512)
        xf, h = matmul_close(merged, w_out[l].astype(BF16), xf, norm_g[l, 1][None], mod_of(l, 2),
                             (norm_g[l, 2][None], mod_of(l, 4), mod_of(l, 3)), seq=s, tm=tm_row, tk=d)

        nxt = None if l == depth - 1 else (norm_g[l + 1, 0][None], mod_of(l + 1, 1), mod_of(l + 1, 0))
        if l % 2 == 0:
            act = swiglu_up(h, ffn_gu[l // 2].astype(BF16), tm=1024, tn=512)
            xf, h = matmul_close(act, ffn_down[l // 2].astype(BF16), xf, norm_g[l, 3][None], mod_of(l, 5), nxt,
                                 seq=s, tm=tm_row, tk=act.shape[1] // 4)
        else:
            yab, wts = moe_ffn(h, router_w[l // 2], router_b[l // 2], moe_gu, moe_down, l // 2, tm=512)
            xf, h = close_sublayer(yab, wts, xf, norm_g[l, 3][None], mod_of(l, 5), nxt, seq=s, tm=tm_row)
    return xf.reshape(b, s, d)
```

```python
import functools

import numpy as np
import jax
import jax.numpy as jnp
from jax import lax
from jax.experimental import pallas as pl
from jax.experimental.pallas import tpu as pltpu

F32 = jnp.float32
BF16 = jnp.bfloat16

HEAD_DIM = 128
ROPE_DIM = HEAD_DIM // 4
ROPE_HALF = ROPE_DIM // 2
ROPE_THETA = 500000.0
NORM_EPS = 1e-6

NSA_HEADS = 8
NSA_KV_HEADS = 2
NSA_GROUP = NSA_HEADS // NSA_KV_HEADS
CMP_BLOCK = 32
CMP_STRIDE = 16
SLC_BLOCK = 64
SLC_TOPN = 16
WIN_SIZE = 512
FORCE_BONUS = 1e4
SLC_PICK_BIAS = 8192.0

RWKV_HEADS = 16
RWKV_HEAD = 64
RWKV_WIDTH = RWKV_HEADS * RWKV_HEAD
LORA_W = 96
LORA_A = 96
LORA_G = 256
LNX_EPS = 64e-5
RWKV_CHUNK = 64
RWKV_CHUNKS_PER_STEP = 8

DIL_PATTERNS = ((128, 1), (512, 4), (2048, 16))
DIL_HPG = 4
DIL_HEADS = DIL_HPG * len(DIL_PATTERNS)
DIL_OUT = DIL_HPG * HEAD_DIM

N_EXPERTS = 8
TOP_K = 2

ROW_TILE = 1024
CLOSE_ROW_TILE = 512
MOE_ROW_TILE = 512

KV_K_CMP, KV_V_CMP, KV_K_SLC, KV_V_SLC, KV_K_WIN, KV_V_WIN = (NSA_KV_HEADS * i for i in range(6))

LANE = 128
VMEM_LIMIT_BYTES = 56 * 1024 * 1024
MASKED = -1e30


def _params(*sem):
    return pltpu.CompilerParams(dimension_semantics=sem, vmem_limit_bytes=VMEM_LIMIT_BYTES)


def _sigmoid(x):
    return 0.5 * jnp.tanh(0.5 * x) + 0.5


def _silu(x):
    return x * _sigmoid(x)


def _dot(a, b):
    return jnp.dot(a, b, preferred_element_type=F32)


def _dot_nt(a, b):
    return lax.dot_general(a, b, (((1,), (1,)), ((), ())), preferred_element_type=F32)


def _rope(t, cosf, sinf):
    lane = lax.broadcasted_iota(jnp.int32, t.shape, 1)
    swapped = jnp.where(lane < ROPE_HALF, pltpu.roll(t, LANE - ROPE_HALF, 1), pltpu.roll(t, ROPE_HALF, 1))
    return t * cosf + swapped * sinf


def _rms(y):
    return y * lax.rsqrt(jnp.mean(y * y, axis=-1, keepdims=True) + NORM_EPS)


def _matmul_kernel(*refs, act, rope_chunks, ntiles, permute=False):
    if permute:
        a_ref, w_ref, cos_ref, sin_ref, perm_ref, o_ref = refs
    elif rope_chunks:
        a_ref, w_ref, cos_ref, sin_ref, o_ref = refs
    else:
        a_ref, w_ref, o_ref = refs
    acc = _dot(a_ref[...].astype(BF16), w_ref[...].astype(BF16))
    if act == "sigmoid":
        acc = _sigmoid(acc)
    elif act == "tanh":
        acc = jnp.tanh(acc)
    if not rope_chunks:
        o_ref[...] = acc.astype(o_ref.dtype)
        return
    per_tile = acc.shape[1] // LANE

    def store(tile):
        for c in range(per_tile):
            sl = slice(c * LANE, (c + 1) * LANE)
            if tile is not None and tile * per_tile + c in rope_chunks:
                o_ref[:, sl] = _rope(acc[:, sl], cos_ref[...], sin_ref[...]).astype(o_ref.dtype)
            else:
                o_ref[:, sl] = acc[:, sl].astype(o_ref.dtype)

    if ntiles == 1:
        store(0)
        if permute:
            o_ref[...] = _dot(perm_ref[...], o_ref[...]).astype(o_ref.dtype)
        return
    assert not permute
    j = pl.program_id(1)
    tiles = sorted({c // per_tile for c in rope_chunks})
    for t in tiles:
        pl.when(j == t)(functools.partial(store, t))
    pl.when(functools.reduce(jnp.logical_and, [j != t for t in tiles]))(functools.partial(store, None))


def _matmul_shift_kernel(a_ref, ap_ref, w_ref, mu_ref, o_ref, *, seq, tm):
    w = w_ref[...]
    z = _dot(a_ref[...], w)
    zp = _dot(ap_ref[...], w)
    at_start = (pl.program_id(0) * tm) % seq == 0
    last = zp.shape[0] - 1
    prev_row = jnp.where(at_start, 0.0, zp[last:last + 1, :])
    rowid = lax.broadcasted_iota(jnp.int32, z.shape, 0)
    shifted = jnp.where(rowid == 0, prev_row, pltpu.roll(z, 1, 0))
    o_ref[...] = (z + (shifted - z) * mu_ref[...]).astype(o_ref.dtype)


BF16_SUBLANES = 16


def matmul_token_shift(a, w, mu, *, seq, tm, tn):
    m, k = a.shape
    n = w.shape[1]
    assert m % tm == 0 and n % tn == 0 and seq % tm == 0
    per = tm // BF16_SUBLANES
    return pl.pallas_call(
        functools.partial(_matmul_shift_kernel, seq=seq, tm=tm),
        grid=(m // tm, n // tn),
        in_specs=[pl.BlockSpec((tm, k), lambda i, j: (i, 0)),
                  pl.BlockSpec((BF16_SUBLANES, k), lambda i, j: (jnp.maximum(i * per - 1, 0), 0)),
                  pl.BlockSpec((k, tn), lambda i, j: (0, j)),
                  pl.BlockSpec((1, tn), lambda i, j: (0, j))],
        out_specs=pl.BlockSpec((tm, tn), lambda i, j: (i, j)),
        out_shape=jax.ShapeDtypeStruct((m, n), F32),
        compiler_params=_params("parallel", "parallel"),
    )(a, a, w, mu)


def matmul(a, w, *, out_dtype, tm, tn, act=None, rope=None, layer=None, a_col=0, row_perm=None):
    m = a.shape[0]
    k = w.shape[-2]
    n = w.shape[-1]
    tm = min(tm, m)
    assert m % tm == 0 and n % tn == 0, (m, n, tm, tn)
    if layer is None:
        w_spec = pl.BlockSpec((k, tn), lambda i, j: (0, j))
    else:
        w_spec = pl.BlockSpec((None, k, tn), lambda i, j: (layer, 0, j))
    in_specs = [pl.BlockSpec((tm, k), lambda i, j: (i, a_col)), w_spec]
    args = [a, w]
    tiles = ()
    if rope is not None:
        cosf, sinf, tiles = rope
        in_specs += [pl.BlockSpec((tm, LANE), lambda i, j: (i, 0))] * 2
        args += [cosf, sinf]
    if row_perm is not None:
        assert row_perm.shape == (tm, tm) and rope is not None
        in_specs.append(pl.BlockSpec((tm, tm), lambda i, j: (0, 0)))
        args.append(row_perm)
    return pl.pallas_call(
        functools.partial(_matmul_kernel, act=act, rope_chunks=frozenset(tiles), ntiles=n // tn,
                          permute=row_perm is not None),
        grid=(m // tm, n // tn),
        in_specs=in_specs,
        out_specs=pl.BlockSpec((tm, tn), lambda i, j: (i, j)),
        out_shape=jax.ShapeDtypeStruct((m, n), out_dtype),
        compiler_params=_params("parallel", "parallel"),
    )(*args)


def _swiglu_up_kernel(a_ref, wg_ref, wu_ref, o_ref):
    a = a_ref[...]
    g = _dot(a, wg_ref[...])
    u = _dot(a, wu_ref[...])
    o_ref[...] = (_silu(g) * u).astype(o_ref.dtype)


def swiglu_up(a, w_gu, *, tm, tn):
    m, k = a.shape
    f = w_gu.shape[1] // 2
    assert m % tm == 0 and f % tn == 0
    nj = f // tn
    return pl.pallas_call(
        _swiglu_up_kernel,
        grid=(m // tm, nj),
        in_specs=[pl.BlockSpec((tm, k), lambda i, j: (i, 0)),
                  pl.BlockSpec((k, tn), lambda i, j: (0, j)),
                  pl.BlockSpec((k, tn), lambda i, j: (0, j + nj))],
        out_specs=pl.BlockSpec((tm, tn), lambda i, j: (i, j)),
        out_shape=jax.ShapeDtypeStruct((m, f), BF16),
        compiler_params=_params("parallel", "parallel"),
    )(a, w_gu, w_gu)


def _close_sublayer(x, y, gpost, gt, nxt):
    xn = x + gt * (_rms(y) * gpost)
    if nxt is None:
        return xn, None
    gnext, sc, sh = nxt
    return xn, (_rms(xn) * gnext) * (1.0 + sc) + sh


def _matmul_close_kernel(*refs, with_next):
    if with_next:
        a_ref, w_ref, x_ref, gp_ref, gt_ref, gn_ref, sc_ref, sh_ref, xo_ref, ho_ref, acc_ref = refs
    else:
        a_ref, w_ref, x_ref, gp_ref, gt_ref, xo_ref, acc_ref = refs
    kk = pl.program_id(1)

    @pl.when(kk == 0)
    def _():
        acc_ref[...] = jnp.zeros_like(acc_ref)

    acc_ref[...] += _dot(a_ref[...], w_ref[...])

    @pl.when(kk == pl.num_programs(1) - 1)
    def _():
        nxt = (gn_ref[...], sc_ref[0], sh_ref[0]) if with_next else None
        xn, h = _close_sublayer(x_ref[...], acc_ref[...], gp_ref[...], gt_ref[0], nxt)
        xo_ref[...] = xn
        if with_next:
            ho_ref[...] = h.astype(ho_ref.dtype)


def _close_specs(tm, d, seq, with_next):
    def row(i, *_):
        return (i, 0)

    def const(*_):
        return (0, 0)

    def batch(i, *_):
        return ((i * tm) // seq, 0, 0)

    specs = [pl.BlockSpec((tm, d), row), pl.BlockSpec((1, d), const), pl.BlockSpec((1, 1, d), batch)]
    if with_next:
        specs += [pl.BlockSpec((1, d), const), pl.BlockSpec((1, 1, d), batch), pl.BlockSpec((1, 1, d), batch)]
    return specs


def matmul_close(a, w, x, gpost, gt, nxt, *, seq, tm, tk):
    m, k = a.shape
    d = w.shape[1]
    assert m % tm == 0 and k % tk == 0 and seq % tm == 0
    with_next = nxt is not None
    args = [a, w, x, gpost, gt] + (list(nxt) if with_next else [])
    in_specs = [pl.BlockSpec((tm, tk), lambda i, kk: (i, kk)), pl.BlockSpec((tk, d), lambda i, kk: (kk, 0))]
    in_specs += _close_specs(tm, d, seq, with_next)
    out_shape = [jax.ShapeDtypeStruct((m, d), F32)]
    out_specs = [pl.BlockSpec((tm, d), lambda i, kk: (i, 0))]
    if with_next:
        out_shape.append(jax.ShapeDtypeStruct((m, d), BF16))
        out_specs.append(pl.BlockSpec((tm, d), lambda i, kk: (i, 0)))
    res = pl.pallas_call(
        functools.partial(_matmul_close_kernel, with_next=with_next),
        grid=(m // tm, k // tk),
        in_specs=in_specs,
        out_specs=out_specs,
        out_shape=out_shape,
        scratch_shapes=[pltpu.VMEM((tm, d), F32)],
        compiler_params=_params("parallel", "arbitrary"),
    )(*args)
    return (res[0], res[1]) if with_next else (res[0], None)


def _close_kernel(*refs, with_next):
    if with_next:
        ya_ref, yb_ref, wt_ref, x_ref, gp_ref, gt_ref, gn_ref, sc_ref, sh_ref, xo_ref, ho_ref = refs
    else:
        ya_ref, yb_ref, wt_ref, x_ref, gp_ref, gt_ref, xo_ref = refs
    nxt = (gn_ref[...], sc_ref[0], sh_ref[0]) if with_next else None
    wt = wt_ref[...]
    y = wt[:, 0:1] * ya_ref[...].astype(F32) + wt[:, 1:2] * yb_ref[...].astype(F32)
    xn, h = _close_sublayer(x_ref[...], y, gp_ref[...], gt_ref[0], nxt)
    xo_ref[...] = xn
    if with_next:
        ho_ref[...] = h.astype(ho_ref.dtype)


def close_sublayer(yab, wt, x, gpost, gt, nxt, *, seq, tm):
    m, d = x.shape
    with_next = nxt is not None
    args = [yab, yab, wt, x, gpost, gt] + (list(nxt) if with_next else [])
    in_specs = [pl.BlockSpec((tm, d), lambda i: (i, 0)), pl.BlockSpec((tm, d), lambda i: (i + m // tm, 0)),
                pl.BlockSpec((tm, LANE), lambda i: (i, 0))] + _close_specs(tm, d, seq, with_next)
    out_shape = [jax.ShapeDtypeStruct((m, d), F32)]
    out_specs = [pl.BlockSpec((tm, d), lambda i: (i, 0))]
    if with_next:
        out_shape.append(jax.ShapeDtypeStruct((m, d), BF16))
        out_specs.append(pl.BlockSpec((tm, d), lambda i: (i, 0)))
    res = pl.pallas_call(
        functools.partial(_close_kernel, with_next=with_next),
        grid=(m // tm,),
        in_specs=in_specs,
        out_specs=out_specs,
        out_shape=out_shape,
        compiler_params=_params("parallel"),
    )(*args)
    return (res[0], res[1]) if with_next else (res[0], None)


def _norm_mod_kernel(x_ref, g_ref, sc_ref, sh_ref, o_ref):
    o_ref[...] = ((_rms(x_ref[...]) * g_ref[...]) * (1.0 + sc_ref[0]) + sh_ref[0]).astype(o_ref.dtype)


def norm_mod(x, g, sc, sh, *, seq, tm):
    m, d = x.shape
    batch = lambda i: ((i * tm) // seq, 0, 0)
    return pl.pallas_call(
        _norm_mod_kernel,
        grid=(m // tm,),
        in_specs=[pl.BlockSpec((tm, d), lambda i: (i, 0)), pl.BlockSpec((1, d), lambda i: (0, 0)),
                  pl.BlockSpec((1, 1, d), batch), pl.BlockSpec((1, 1, d), batch)],
        out_specs=pl.BlockSpec((tm, d), lambda i: (i, 0)),
        out_shape=jax.ShapeDtypeStruct((m, d), BF16),
        compiler_params=_params("parallel"),
    )(x, g, sc, sh)


def _branch_merge_kernel(oc_ref, os_ref, ow_ref, ng_ref, yb_ref, yc_ref, wa_ref, wb_ref, wc_ref, ga_ref, gb_ref,
                         gc_ref, o_ref, ya_ref):
    @pl.when(pl.program_id(1) == 0)
    def _():
        ng = ng_ref[...]
        for hd in range(NSA_HEADS):
            sl = slice(hd * HEAD_DIM, (hd + 1) * HEAD_DIM)
            ya = ng[:, 3 * hd:3 * hd + 1] * oc_ref[:, sl].astype(F32)
            ya += ng[:, 3 * hd + 1:3 * hd + 2] * os_ref[:, sl].astype(F32)
            ya += ng[:, 3 * hd + 2:3 * hd + 3] * ow_ref[:, sl].astype(F32)
            ya_ref[:, sl] = ya.astype(ya_ref.dtype)

    acc = ga_ref[...].astype(F32) * _dot(ya_ref[...], wa_ref[...])
    acc += gb_ref[...].astype(F32) * _dot(yb_ref[...], wb_ref[...])
    acc += gc_ref[...].astype(F32) * _dot(yc_ref[...], wc_ref[...])
    o_ref[...] = acc.astype(o_ref.dtype)


def branch_merge(o_cmp, o_slc, o_win, nsa_gate, yb, yc, wa, wb, wc, gates, *, tm, tn):
    m = yb.shape[0]
    d = wa.shape[1]
    nj = d // tn
    row = lambda width: pl.BlockSpec((tm, width), lambda i, j: (i, 0))
    wsp = lambda kdim: pl.BlockSpec((kdim, tn), lambda i, j: (0, j))
    gsp = lambda off: pl.BlockSpec((tm, tn), lambda i, j: (i, j + off * nj))
    wa_rows = wa.shape[0]
    return pl.pallas_call(
        _branch_merge_kernel,
        grid=(m // tm, nj),
        in_specs=[row(wa_rows), row(wa_rows), row(wa_rows), row(LANE), row(yb.shape[1]), row(yc.shape[1]),
                  wsp(wa_rows), wsp(wb.shape[0]), wsp(wc.shape[0]), gsp(0), gsp(1), gsp(2)],
        out_specs=pl.BlockSpec((tm, tn), lambda i, j: (i, j)),
        out_shape=jax.ShapeDtypeStruct((m, d), BF16),
        scratch_shapes=[pltpu.VMEM((tm, wa_rows), BF16)],
        compiler_params=_params("parallel", "arbitrary"),
    )(o_cmp, o_slc, o_win, nsa_gate, yb, yc, wa, wb, wc, gates, gates, gates)


def _compress_out_kernel(h_ref, w2_ref, o_ref):
    o_ref[0] = _dot(_silu(h_ref[0]).astype(BF16), w2_ref[0]).astype(o_ref.dtype)


def nsa_compress(kv_cmp, pe, w1, w2):
    b, s, width = kv_cmp.shape
    groups = s // CMP_STRIDE
    heads = NSA_KV_HEADS
    x2 = kv_cmp.reshape(b * groups, CMP_STRIDE * width)
    w1r = w1.reshape(2, 2, CMP_STRIDE, HEAD_DIM, HEAD_DIM)
    eye = jnp.eye(heads, dtype=F32)
    wcat = jnp.einsum("pk,qh,kardn->rpqdkhan", jnp.eye(2, dtype=F32), eye, w1r)
    wcat = wcat.reshape(CMP_STRIDE * width, 2 * heads * 2 * HEAD_DIM).astype(BF16)
    halves = matmul(x2, wcat, out_dtype=F32, tm=min(512, b * groups), tn=2 * HEAD_DIM)
    halves = halves.reshape(b, groups, 2, heads, 2, HEAD_DIM)
    bias = jnp.einsum("kf,kfn->kn", pe.reshape(2, CMP_BLOCK * HEAD_DIM), w1, precision=lax.Precision.HIGHEST)
    pre = halves[:, :-1, :, :, 0] + halves[:, 1:, :, :, 1] + bias[None, None, :, None, :]
    pre = jnp.pad(pre, ((0, 0), (0, 1), (0, 0), (0, 0), (0, 0))).transpose(2, 0, 3, 1, 4)
    rows = b * heads * groups
    tm = min(rows, 512)
    out = pl.pallas_call(
        _compress_out_kernel,
        grid=(2, rows // tm),
        in_specs=[pl.BlockSpec((1, tm, HEAD_DIM), lambda t, i: (t, i, 0)),
                  pl.BlockSpec((1, HEAD_DIM, HEAD_DIM), lambda t, i: (t, 0, 0))],
        out_specs=pl.BlockSpec((1, tm, HEAD_DIM), lambda t, i: (t, i, 0)),
        out_shape=jax.ShapeDtypeStruct((2, rows, HEAD_DIM), BF16),
        compiler_params=_params("parallel", "parallel"),
    )(pre.reshape(2, rows, HEAD_DIM), w2.astype(BF16))
    return out.reshape(2, b, heads, groups, HEAD_DIM)


def _cmp_select_kernel(q_ref, kc_ref, vct_ref, o_ref, sel_ref, *, tq, nc, ncp, ns, nsp, scale):
    qi = pl.program_id(2)
    kc = kc_ref[0, 0]
    vct = vct_ref[0, 0]
    spos = qi * tq + lax.broadcasted_iota(jnp.int32, (1, tq), 1)
    cidx = lax.broadcasted_iota(jnp.int32, (ncp, 1), 0)
    valid = jnp.logical_and(cidx * CMP_STRIDE + (CMP_BLOCK - 1) <= spos, cidx < nc)
    heads = range(NSA_GROUP)
    sls = [slice(g * HEAD_DIM, (g + 1) * HEAD_DIM) for g in heads]
    scores = [jnp.where(valid, _dot_nt(kc, q_ref[0, :, sls[g]]) * scale, MASKED) for g in heads]
    maxes = [jnp.max(st, axis=0, keepdims=True) for st in scores]
    es = [jnp.where(valid, jnp.exp(st - mx), 0.0) for st, mx in zip(scores, maxes)]
    dens = [jnp.sum(e, axis=0, keepdims=True) for e in es]
    probs = [e * (1.0 / jnp.where(den > 0, den, 1.0)) for e, den in zip(es, dens)]
    psum = functools.reduce(jnp.add, probs)
    outs = [_dot(vct, p.astype(BF16)) for p in probs]
    for g in heads:
        o_ref[0, :, sls[g]] = outs[g].T.astype(o_ref.dtype)
    jrow = lax.broadcasted_iota(jnp.int32, (ns, ncp), 0)
    ccol = lax.broadcasted_iota(jnp.int32, (ns, ncp), 1)
    c0 = ccol * CMP_STRIDE
    j0 = jrow * SLC_BLOCK
    cover_t = jnp.logical_and(c0 < j0 + SLC_BLOCK, c0 + CMP_BLOCK > j0).astype(F32)
    hi = psum.astype(BF16)
    rem = psum - hi.astype(F32)
    mid = rem.astype(BF16)
    lo = (rem - mid.astype(F32)).astype(BF16)
    imp3 = _dot(cover_t.astype(BF16), jnp.concatenate([hi, mid, lo], axis=1))
    imp = imp3[:, :tq] + imp3[:, tq:2 * tq] + imp3[:, 2 * tq:]
    j = lax.broadcasted_iota(jnp.int32, (ns, 1), 0).astype(F32)
    cur = (spos // SLC_BLOCK).astype(F32)
    forced = jnp.logical_or(jnp.logical_or(j == 0, j == cur), j == cur - 1)
    score = jnp.where(j <= cur, imp + FORCE_BONUS * forced.astype(F32), -jnp.inf)
    sel = jnp.zeros((ns, tq), F32)
    for _ in range(SLC_TOPN):
        mx = jnp.max(score, axis=0, keepdims=True)
        first = jnp.min(jnp.where(score == mx, j, float(ns)), axis=0, keepdims=True)
        pick = j == first
        sel = jnp.where(pick, 1.0, sel)
        score = jnp.where(pick, -jnp.inf, score)
    if ns < nsp:
        sel = jnp.concatenate([sel, jnp.zeros((nsp - ns, tq), F32)], axis=0)
    sel_ref[0, 0] = sel.T.astype(sel_ref.dtype)


def cmp_select(q, kc, vct, *, nc, tq):
    b, s, _ = q.shape
    ncp = kc.shape[2]
    nsp = LANE
    assert s % tq == 0 and s // SLC_BLOCK <= nsp
    gw = NSA_GROUP * HEAD_DIM
    return pl.pallas_call(
        functools.partial(_cmp_select_kernel, tq=tq, nc=nc, ncp=ncp, ns=-(-(s // SLC_BLOCK) // 8) * 8, nsp=nsp,
                          scale=HEAD_DIM ** -0.5),
        grid=(b, NSA_KV_HEADS, s // tq),
        in_specs=[pl.BlockSpec((1, tq, gw), lambda bi, h, i: (bi, i, h)),
                  pl.BlockSpec((1, 1, ncp, HEAD_DIM), lambda bi, h, i: (bi, h, 0, 0)),
                  pl.BlockSpec((1, 1, HEAD_DIM, ncp), lambda bi, h, i: (bi, h, 0, 0))],
        out_specs=[pl.BlockSpec((1, tq, gw), lambda bi, h, i: (bi, i, h)),
                   pl.BlockSpec((1, 1, tq, nsp), lambda bi, h, i: (bi, h, i, 0))],
        out_shape=[jax.ShapeDtypeStruct((b, s, NSA_HEADS * HEAD_DIM), BF16),
                   jax.ShapeDtypeStruct((b, NSA_KV_HEADS, s, nsp), BF16)],
        compiler_params=_params("parallel", "parallel", "parallel"),
    )(q, kc, vct)


def _slc_kernel(qi_ref, ki_ref, q_ref, k_ref, v_ref, sel_ref, cos_ref, sin_ref, o_ref, qaug_ref, m_ref, acc_ref,
                *, tq, scale):
    step = pl.program_id(2)
    qi = qi_ref[step]
    ki = ki_ref[step]

    @pl.when(ki == 0)
    def _():
        cosf = cos_ref[0]
        sinf = sin_ref[0]
        pick = sel_ref[0, 0] * SLC_PICK_BIAS
        for g in range(NSA_GROUP):
            sl = slice(g * HEAD_DIM, (g + 1) * HEAD_DIM)
            qg = _rope(q_ref[0, :, sl].astype(F32), cosf, sinf) * scale
            qaug_ref[g] = jnp.concatenate([qg.astype(BF16), pick], axis=1)
        m_ref[...] = jnp.full_like(m_ref, MASKED)
        acc_ref[...] = jnp.zeros_like(acc_ref)

    def accumulate(causal):
        nsp = sel_ref.shape[3]
        block_of_key = (ki * tq + lax.broadcasted_iota(jnp.int32, (tq, nsp), 0)) // SLC_BLOCK
        onehot = (block_of_key == lax.broadcasted_iota(jnp.int32, (tq, nsp), 1)).astype(BF16)
        kaug = jnp.concatenate([k_ref[0], onehot], axis=1)
        vaug = jnp.concatenate([v_ref[0], jnp.ones((tq, LANE), BF16)], axis=1)
        if causal:
            qpos = lax.broadcasted_iota(jnp.int32, (tq, tq), 0)
            kpos = lax.broadcasted_iota(jnp.int32, (tq, tq), 1)
            visible = kpos <= qpos
        heads = range(NSA_GROUP)
        scores = [_dot_nt(qaug_ref[g], kaug) for g in heads]
        if causal:
            scores = [jnp.where(visible, s, MASKED) for s in scores]
        m_olds = [m_ref[g] for g in heads]
        m_news = [jnp.maximum(mo, jnp.max(s, axis=-1, keepdims=True)) for mo, s in zip(m_olds, scores)]
        alphas = [jnp.exp(mo - mn) for mo, mn in zip(m_olds, m_news)]
        probs = [jnp.exp((s - jnp.concatenate([mn] * (tq // LANE), axis=1)).astype(BF16))
                 for s, mn in zip(scores, m_news)]
        pvs = [_dot(p, vaug) for p in probs]
        for g in heads:
            acc_ref[g] = jnp.concatenate([alphas[g], alphas[g]], axis=1) * acc_ref[g] + pvs[g]
            m_ref[g] = m_news[g]

    @pl.when(ki < qi)
    def _():
        accumulate(False)

    @pl.when(ki == qi)
    def _():
        accumulate(True)
        for g in range(NSA_GROUP):
            sl = slice(g * HEAD_DIM, (g + 1) * HEAD_DIM)
            o_ref[0, :, sl] = (acc_ref[g, :, :HEAD_DIM] / acc_ref[g, :, HEAD_DIM:]).astype(o_ref.dtype)


def slc_attention(q, kv, sel, cosf, sinf, *, k_col, v_col, tq):
    b, s, _ = q.shape
    gw = NSA_GROUP * HEAD_DIM
    nq = s // tq
    nsp = sel.shape[3]
    pairs = [(i, j) for i in range(nq) for j in range(i + 1)]
    qi_tab = jnp.asarray([p[0] for p in pairs], jnp.int32)
    ki_tab = jnp.asarray([p[1] for p in pairs], jnp.int32)
    grid_spec = pltpu.PrefetchScalarGridSpec(
        num_scalar_prefetch=2,
        grid=(b, NSA_KV_HEADS, len(pairs)),
        in_specs=[pl.BlockSpec((1, tq, gw), lambda bi, h, t, qt, kt: (bi, qt[t], h)),
                  pl.BlockSpec((1, tq, HEAD_DIM), lambda bi, h, t, qt, kt: (bi, kt[t], k_col + h)),
                  pl.BlockSpec((1, tq, HEAD_DIM), lambda bi, h, t, qt, kt: (bi, kt[t], v_col + h)),
                  pl.BlockSpec((1, 1, tq, nsp), lambda bi, h, t, qt, kt: (bi, h, qt[t], 0)),
                  pl.BlockSpec((1, tq, LANE), lambda bi, h, t, qt, kt: (bi, qt[t], 0)),
                  pl.BlockSpec((1, tq, LANE), lambda bi, h, t, qt, kt: (bi, qt[t], 0))],
        out_specs=pl.BlockSpec((1, tq, gw), lambda bi, h, t, qt, kt: (bi, qt[t], h)),
        scratch_shapes=[pltpu.VMEM((NSA_GROUP, tq, 2 * HEAD_DIM), BF16),
                        pltpu.VMEM((NSA_GROUP, tq, LANE), F32),
                        pltpu.VMEM((NSA_GROUP, tq, 2 * HEAD_DIM), F32)],
    )
    return pl.pallas_call(
        functools.partial(_slc_kernel, tq=tq, scale=HEAD_DIM ** -0.5),
        grid_spec=grid_spec,
        out_shape=jax.ShapeDtypeStruct((b, s, NSA_HEADS * HEAD_DIM), BF16),
        compiler_params=_params("parallel", "parallel", "arbitrary"),
    )(qi_tab, ki_tab, q, kv, kv, sel, cosf, sinf)


def _band_kernel(*refs, nheads, kv_heads, nkv, tq, max_dist, rope_q, with_lse, scale, nsub):
    q_ref = refs[0]
    k_refs = refs[1:1 + nkv]
    v_refs = refs[1 + nkv:1 + 2 * nkv]
    pos = 1 + 2 * nkv
    if rope_q:
        cos_ref, sin_ref = refs[pos:pos + 2]
        pos += 2
    o_ref = refs[pos]
    lse_ref = refs[pos + 1] if with_lse else None
    qi = pl.program_id(2)
    qpos = qi * tq + lax.broadcasted_iota(jnp.int32, (tq, 1), 0)
    kpos = (qi - (nkv - 1)) * tq + lax.broadcasted_iota(jnp.int32, (1, nkv * tq), 1)
    diff = qpos - kpos
    mask = jnp.logical_and(jnp.logical_and(diff >= 0, diff <= max_dist), kpos >= 0)
    lane = lax.broadcasted_iota(jnp.int32, (tq, LANE), 1)
    jobs = [(u, g) for u in range(nsub) for g in range(nheads)]

    def rows(u):
        return slice(u * tq, (u + 1) * tq)

    def cols(g, kv=False):
        return slice(0, HEAD_DIM) if (kv and kv_heads == 1) else slice(g * HEAD_DIM, (g + 1) * HEAD_DIM)

    scores = []
    for u, g in jobs:
        q = q_ref[0, rows(u), cols(g)]
        if rope_q:
            q = _rope(q.astype(F32), cos_ref[0], sin_ref[0]).astype(BF16)
        kcat = jnp.concatenate([r[0, rows(u), cols(g, True)] for r in k_refs], axis=0)
        scores.append(jnp.where(mask, _dot_nt(q, kcat) * scale, MASKED))
    maxes = [jnp.max(s, axis=-1, keepdims=True) for s in scores]
    probs = [jnp.exp((s - mx).astype(BF16)) for s, mx in zip(scores, maxes)]
    pvs = []
    for (u, g), e in zip(jobs, probs):
        vcat = jnp.concatenate([r[0, rows(u), cols(g, True)] for r in v_refs], axis=0)
        pvs.append(_dot(e, jnp.concatenate([vcat, jnp.ones(vcat.shape, BF16)], axis=1)))
    lse_acc = [jnp.zeros((tq, LANE), F32) for _ in range(nsub)]
    for (u, g), pv, mx in zip(jobs, pvs, maxes):
        den = pv[:, HEAD_DIM:]
        o_ref[0, rows(u), cols(g)] = (pv[:, :HEAD_DIM] / den).astype(o_ref.dtype)
        if with_lse:
            lse_acc[u] = jnp.where(lane == g, mx + jnp.log(den), lse_acc[u])
    if with_lse:
        for u in range(nsub):
            lse_ref[0, rows(u)] = lse_acc[u]


def band_attention(q, k, v, *, nheads, kv_heads, q_col, k_col, v_col, o_cols, ncol, max_dist, tq,
                   rope=None, with_lse=False, out_dtype=BF16, seq_len=None, row_block=None, nsub=1):
    b = q.shape[0]
    rows = q.shape[1]
    seq_len = rows if seq_len is None else seq_len
    tq = min(tq, seq_len)
    assert seq_len % tq == 0 and ncol % nsub == 0
    nkv = -(-max_dist // tq) + 1
    qw = nheads * HEAD_DIM
    kw = kv_heads * HEAD_DIM
    bt = nsub * tq
    if row_block is None:
        assert nsub == 1
        row_block = lambda c, i: i
        o_col = lambda c: c
    else:
        assert rope is None
        o_col = lambda c: 0

    def kv_spec(col_fn, back):
        return pl.BlockSpec((1, bt, kw), lambda bi, c, i: (bi, row_block(c, jnp.maximum(i - back, 0)), col_fn(c)))

    in_specs = [pl.BlockSpec((1, bt, qw), lambda bi, c, i: (bi, row_block(c, i), q_col(c)))]
    in_specs += [kv_spec(k_col, nkv - 1 - t) for t in range(nkv)]
    in_specs += [kv_spec(v_col, nkv - 1 - t) for t in range(nkv)]
    args = [q] + [k] * nkv + [v] * nkv
    if rope is not None:
        in_specs += [pl.BlockSpec((1, tq, LANE), lambda bi, c, i: (bi, i, 0))] * 2
        args += list(rope)
    out_specs = [pl.BlockSpec((1, bt, qw), lambda bi, c, i: (bi, row_block(c, i), o_col(c)))]
    out_shape = [jax.ShapeDtypeStruct((b, rows, o_cols * qw), out_dtype)]
    if with_lse:
        out_specs.append(pl.BlockSpec((1, bt, LANE), lambda bi, c, i: (bi, row_block(c, i), o_col(c))))
        out_shape.append(jax.ShapeDtypeStruct((b, rows, o_cols * LANE), F32))
    res = pl.pallas_call(
        functools.partial(_band_kernel, nheads=nheads, kv_heads=kv_heads, nkv=nkv, tq=tq, max_dist=max_dist,
                          rope_q=rope is not None, with_lse=with_lse, scale=HEAD_DIM ** -0.5, nsub=nsub),
        grid=(b, ncol // nsub, seq_len // tq),
        in_specs=in_specs,
        out_specs=out_specs,
        out_shape=out_shape,
        compiler_params=_params("parallel", "parallel", "parallel"),
    )(*args)
    return res if with_lse else res[0]


def _bmm(x, y):
    return jnp.einsum("bij,bjk->bik", x.astype(BF16), y.astype(BF16), preferred_element_type=F32)


def _bmm_nt(x, y):
    return jnp.einsum("bik,bjk->bij", x.astype(BF16), y.astype(BF16), preferred_element_type=F32)


def _head_sums(x2, scale=1.0):
    n = x2.shape[1]
    blk = (lax.broadcasted_iota(jnp.int32, (n, n), 0) // RWKV_HEAD
           == lax.broadcasted_iota(jnp.int32, (n, n), 1) // RWKV_HEAD)
    ones = jnp.where(blk, scale, 0.0).astype(BF16)
    hi = x2.astype(BF16)
    lo = (x2 - hi.astype(F32)).astype(BF16)
    return _dot(hi, ones) + _dot(lo, ones)


def _wkv_chunk_kernel(*refs, nb, c, mix):
    if mix:
        zr_ref, zk_ref, zv_ref, wl_ref, al_ref, vu_ref, vf_ref, vec_ref = refs[:8]
    else:
        zr_ref, zk_ref, zv_ref, wl_ref, al_ref, vec_ref = refs[:6]
    q_ref, y0_ref, gt_ref, ht_ref, bonus_ref = refs[-5:]
    pair = 2 * RWKV_HEAD
    c2 = 2 * c
    vec = vec_ref[...]
    w0, a0, k_k, k_a, r_k, v0 = [vec[i:i + 1] for i in range(6)]
    r2 = zr_ref[0]
    kraw = zk_ref[0]
    v2 = zv_ref[0]
    x = w0 + wl_ref[0]
    softplus_neg = jnp.maximum(-x, 0.0) + jnp.log(1.0 + jnp.exp(-jnp.abs(x)))
    lw2 = -jnp.exp(-softplus_neg - 0.5)
    a_gate = _sigmoid(a0 + al_ref[0])
    if mix:
        v2 = v2 + (vf_ref[0] - v2) * _sigmoid(v0 + vu_ref[0])
    kk = kraw * k_k
    kk = kk * lax.rsqrt(jnp.maximum(_head_sums(kk * kk), 1e-24))
    k2 = kraw * (1.0 + (a_gate - 1.0) * k_a)
    bonus_ref[0] = (_head_sums(r2 * k2 * r_k) * v2).astype(bonus_ref.dtype)

    def chunks(x2):
        return x2.reshape(nb, c, pair)

    r, lw, k, v, a, b = [chunks(t) for t in (r2, lw2, k2, v2, -kk, kk * a_gate)]
    row = lax.broadcasted_iota(jnp.int32, (c, c), 0)
    col = lax.broadcasted_iota(jnp.int32, (c, c), 1)
    tril = jnp.broadcast_to((row >= col).astype(BF16), (nb, c, c))
    hi = lw.astype(BF16)
    rem = lw - hi.astype(F32)
    mid = rem.astype(BF16)
    lo = (rem - mid.astype(F32)).astype(BF16)
    cum3 = _bmm(tril, jnp.concatenate([hi, mid, lo], axis=2))
    cum = cum3[:, :, :pair] + cum3[:, :, pair:2 * pair] + cum3[:, :, 2 * pair:]
    last = cum[:, c - 1:c, :]
    inv = jnp.exp(-cum)
    tail = jnp.exp(last - cum)
    lane = lax.broadcasted_iota(jnp.int32, (1, 1, pair), 2)
    first = lane < RWKV_HEAD

    def stack(x):
        return jnp.concatenate([jnp.where(first, x, 0.0), jnp.where(first, 0.0, x)], axis=1)

    a_s = stack(a * jnp.exp(cum - lw))
    r_s = stack(r * jnp.exp(cum))
    b_s = stack(b * inv)
    k_s = stack(k * inv)
    v_s = stack(v)
    bh_s = stack(b * tail)
    kh_s = stack(k * tail)
    ar = jnp.concatenate([a_s, r_s], axis=1)
    pbk = _bmm_nt(ar, jnp.concatenate([b_s, k_s], axis=1))
    pb = pbk[:, :, :c2]
    pk = pbk[:, :, c2:]
    row2 = lax.broadcasted_iota(jnp.int32, (c2, c2), 0) % c
    col2 = lax.broadcasted_iota(jnp.int32, (c2, c2), 1) % c
    strict = row2 > col2
    lower = row2 >= col2
    l_ab = jnp.where(strict, pb[:, :c2], 0.0)
    m_rb = jnp.where(lower, pb[:, c2:], 0.0)
    l_ak = jnp.where(strict, pk[:, :c2], 0.0)
    m_rk = jnp.where(lower, pk[:, c2:], 0.0)
    eye = (lax.broadcasted_iota(jnp.int32, (c2, c2), 0) == lax.broadcasted_iota(jnp.int32, (c2, c2), 1))
    eye = eye.astype(F32)
    tinv = eye + l_ab
    pw = _bmm(l_ab, l_ab)
    for _ in range(int(np.log2(c)) - 2):
        both = _bmm(jnp.concatenate([tinv, pw], axis=1), pw)
        tinv = tinv + both[:, :c2]
        pw = both[:, c2:]
    tinv = tinv + _bmm(tinv, pw)
    tu = _bmm(tinv, jnp.concatenate([_bmm(l_ak, v_s), a_s], axis=2))
    u0_s = tu[:, :, :pair]
    ta_s = tu[:, :, pair:]
    mu = _bmm(m_rb, jnp.concatenate([ta_s, u0_s], axis=2))
    q_s = r_s + mu[:, :, :pair]
    y0_s = _bmm(m_rk, v_s) + mu[:, :, pair:]
    q_ref[0] = (q_s[:, :c] + q_s[:, c:]).reshape(nb * c, pair).astype(q_ref.dtype)
    y0_ref[0] = (y0_s[:, :c] + y0_s[:, c:]).reshape(nb * c, pair).astype(y0_ref.dtype)
    bh_t = jnp.swapaxes(bh_s, 1, 2)
    kh_t = jnp.swapaxes(kh_s, 1, 2)
    gh = _bmm(bh_t, jnp.concatenate([ta_s, u0_s], axis=2))
    gt_ref[0, :, 0] = (eye * jnp.exp(last) + gh[:, :, :pair]).astype(gt_ref.dtype)
    ht_ref[0, :, 0] = (gh[:, :, pair:] + _bmm(kh_t, v_s)).astype(ht_ref.dtype)


def _wkv_scan_kernel(q_ref, y0_ref, gt_ref, ht_ref, bonus_ref, g_ref, lnx_ref, o_ref, state_ref, *, npair):
    pair = 2 * RWKV_HEAD
    bsz, c, _ = q_ref.shape

    @pl.when(pl.program_id(0) == 0)
    def _():
        state_ref[...] = jnp.zeros_like(state_ref)

    sls = [slice(p * pair, (p + 1) * pair) for p in range(npair)]
    idx = [(bi, p) for bi in range(bsz) for p in range(npair)]
    st = [state_ref[bi, p].astype(BF16) for bi, p in idx]
    ys = [_dot(q_ref[bi, :, sls[p]], s_) + y0_ref[bi, :, sls[p]].astype(F32) for (bi, p), s_ in zip(idx, st)]
    new = [_dot(gt_ref[bi, 0, p], s_) + ht_ref[bi, 0, p].astype(F32) for (bi, p), s_ in zip(idx, st)]
    for (bi, p), s_ in zip(idx, new):
        state_ref[bi, p] = s_
    y = jnp.concatenate(ys, axis=0)
    dev = y - _head_sums(y, 1.0 / RWKV_HEAD)
    yn = dev * lax.rsqrt(_head_sums(dev * dev, 1.0 / RWKV_HEAD) + LNX_EPS)
    for n, (bi, p) in enumerate(idx):
        ln = yn[n * c:(n + 1) * c] * lnx_ref[0:1, sls[p]] + lnx_ref[1:2, sls[p]]
        out = (ln + bonus_ref[bi, :, sls[p]].astype(F32)) * g_ref[bi, :, sls[p]].astype(F32)
        o_ref[bi, :, sls[p]] = out.astype(o_ref.dtype)


def wkv7(zs, zs_first, wl, al, vu, vec, g, lnx):
    bsz, s, _ = zs.shape
    width = wl.shape[2]
    pair = 2 * RWKV_HEAD
    npair = width // pair
    c = min(RWKV_CHUNK, s)
    nb = min(RWKV_CHUNKS_PER_STEP, s // c)
    nch = s // c
    mix = vu is not None
    assert s % (nb * c) == 0 and width % pair == 0

    def col(off):
        return pl.BlockSpec((1, nb * c, pair), lambda bi, p, j: (bi, j, off + p))

    vec_spec = pl.BlockSpec((8, pair), lambda bi, p, j: (0, p))
    mat = pl.BlockSpec((1, nb, 1, pair, pair), lambda bi, p, j: (bi, j, p, 0, 0))
    in_specs = [col(0), col(npair), col(2 * npair), col(0), col(0)]
    args = [zs, zs, zs, wl, al]
    if mix:
        in_specs += [col(0), col(2 * npair)]
        args += [vu, zs_first]
    q, y0, gt, ht, bonus = pl.pallas_call(
        functools.partial(_wkv_chunk_kernel, nb=nb, c=c, mix=mix),
        grid=(bsz, npair, nch // nb),
        in_specs=in_specs + [vec_spec],
        out_specs=[col(0), col(0), mat, mat, col(0)],
        out_shape=[jax.ShapeDtypeStruct((bsz, s, width), BF16), jax.ShapeDtypeStruct((bsz, s, width), BF16),
                   jax.ShapeDtypeStruct((bsz, nch, npair, pair, pair), BF16),
                   jax.ShapeDtypeStruct((bsz, nch, npair, pair, pair), BF16),
                   jax.ShapeDtypeStruct((bsz, s, width), BF16)],
        compiler_params=_params("parallel", "parallel", "parallel"),
    )(*args, vec)
    row = pl.BlockSpec((bsz, c, width), lambda j: (0, j, 0))
    mats = pl.BlockSpec((bsz, 1, npair, pair, pair), lambda j: (0, j, 0, 0, 0))
    return pl.pallas_call(
        functools.partial(_wkv_scan_kernel, npair=npair),
        grid=(nch,),
        in_specs=[row, row, mats, mats, row, row, pl.BlockSpec((8, width), lambda j: (0, 0))],
        out_specs=row,
        out_shape=jax.ShapeDtypeStruct((bsz, s, width), BF16),
        scratch_shapes=[pltpu.VMEM((bsz, npair, pair, pair), F32)],
        compiler_params=_params("arbitrary"),
    )(q, y0, gt, ht, bonus, g, lnx)


def _moe_up_kernel(te_ref, tv_ref, a_ref, wg_ref, wu_ref, o_ref, wg_bf, wu_bf):
    i = pl.program_id(1)
    changed = jnp.logical_or(i == 0, te_ref[i] != te_ref[jnp.maximum(i - 1, 0)])

    @pl.when(changed)
    def _():
        wg_bf[...] = wg_ref[...].astype(BF16)
        wu_bf[...] = wu_ref[...].astype(BF16)

    @pl.when(tv_ref[i] > 0)
    def _():
        a = a_ref[...]
        g = _dot(a, wg_bf[...])
        u = _dot(a, wu_bf[...])
        o_ref[...] = (_silu(g) * u).astype(o_ref.dtype)

    @pl.when(tv_ref[i] == 0)
    def _():
        o_ref[...] = jnp.zeros_like(o_ref)


def moe_up(tile_expert, tile_valid, xs, w_gu, layer, *, tm, tn):
    r, k = xs.shape
    f = w_gu.shape[3] // 2
    nj = f // tn
    grid_spec = pltpu.PrefetchScalarGridSpec(
        num_scalar_prefetch=2,
        grid=(nj, r // tm),
        in_specs=[pl.BlockSpec((tm, k), lambda j, i, te, tv: (i, 0)),
                  pl.BlockSpec((None, None, k, tn), lambda j, i, te, tv: (layer, te[i], 0, j)),
                  pl.BlockSpec((None, None, k, tn), lambda j, i, te, tv: (layer, te[i], 0, j + nj))],
        out_specs=pl.BlockSpec((tm, tn), lambda j, i, te, tv: (i, j)),
        scratch_shapes=[pltpu.VMEM((k, tn), BF16), pltpu.VMEM((k, tn), BF16)],
    )
    return pl.pallas_call(
        _moe_up_kernel,
        grid_spec=grid_spec,
        out_shape=jax.ShapeDtypeStruct((r, f), BF16),
        compiler_params=_params("parallel", "arbitrary"),
    )(tile_expert, tile_valid, xs, w_gu, w_gu)


def _moe_down_kernel(te_ref, tv_ref, a_ref, w_ref, o_ref, w_bf):
    i = pl.program_id(1)
    changed = jnp.logical_or(i == 0, te_ref[i] != te_ref[jnp.maximum(i - 1, 0)])

    @pl.when(changed)
    def _():
        w_bf[...] = w_ref[...].astype(BF16)

    @pl.when(tv_ref[i] > 0)
    def _():
        o_ref[...] = _dot(a_ref[...], w_bf[...]).astype(o_ref.dtype)

    @pl.when(tv_ref[i] == 0)
    def _():
        o_ref[...] = jnp.zeros_like(o_ref)


def moe_down(tile_expert, tile_valid, act, w_down, layer, *, tm, tn):
    r, f = act.shape
    d = w_down.shape[3]
    grid_spec = pltpu.PrefetchScalarGridSpec(
        num_scalar_prefetch=2,
        grid=(d // tn, r // tm),
        in_specs=[pl.BlockSpec((tm, f), lambda j, i, te, tv: (i, 0)),
                  pl.BlockSpec((None, None, f, tn), lambda j, i, te, tv: (layer, te[i], 0, j))],
        out_specs=pl.BlockSpec((tm, tn), lambda j, i, te, tv: (i, j)),
        scratch_shapes=[pltpu.VMEM((f, tn), BF16)],
    )
    return pl.pallas_call(
        _moe_down_kernel,
        grid_spec=grid_spec,
        out_shape=jax.ShapeDtypeStruct((r, d), BF16),
        compiler_params=_params("parallel", "arbitrary"),
    )(tile_expert, tile_valid, act, w_down)


def _rope_tables(positions):
    inv_freq = ROPE_THETA ** (-jnp.arange(ROPE_HALF, dtype=F32) / ROPE_HALF)
    ang = positions.astype(F32)[:, :, None] * inv_freq
    cos = jnp.cos(ang)
    sin = jnp.sin(ang)
    b, s = positions.shape
    pad1 = jnp.ones((b, s, HEAD_DIM - ROPE_DIM), F32)
    pad0 = jnp.zeros((b, s, HEAD_DIM - ROPE_DIM), F32)
    return jnp.concatenate([cos, cos, pad1], axis=-1), jnp.concatenate([-sin, sin, pad0], axis=-1)


def _pad_cols(w, n):
    return w if w.shape[-1] == n else jnp.pad(w, ((0, 0), (0, n - w.shape[-1])))


def nsa_branch(q_a, kv_a, cosf, sinf, cmp_pe, cmp_w1, cmp_w2):
    b, s, _ = q_a.shape
    nc = s // CMP_STRIDE - 1
    ncp = -(-nc // LANE) * LANE
    kvw = NSA_KV_HEADS * HEAD_DIM
    comp = nsa_compress(kv_a[..., :2 * kvw], cmp_pe, cmp_w1, cmp_w2)
    comp = jnp.pad(comp, ((0, 0), (0, 0), (0, 0), (0, ncp - comp.shape[3]), (0, 0)))
    kc = comp[0]
    vct = comp[1].transpose(0, 1, 3, 2)
    tq = min(256, s)
    o_cmp, sel = cmp_select(q_a, kc, vct, nc=nc, tq=tq)
    o_slc = slc_attention(q_a, kv_a, sel, cosf, sinf, k_col=KV_K_SLC, v_col=KV_V_SLC, tq=min(512, s))
    o_win = band_attention(q_a, kv_a, kv_a, nheads=NSA_GROUP, kv_heads=1, q_col=lambda c: c,
                           k_col=lambda c: KV_K_WIN + c, v_col=lambda c: KV_V_WIN + c, o_cols=NSA_KV_HEADS,
                           ncol=NSA_KV_HEADS, max_dist=WIN_SIZE - 1, tq=tq, rope=(cosf, sinf))
    return o_cmp, o_slc, o_win


def rwkv_branch(zs, zs_first, vec, w_up, a_up, g_up, v_res):
    b, s, _ = zs.shape
    t = b * s
    w_ = RWKV_WIDTH
    o = 3 * w_
    zw, za = zs[..., o:o + LORA_W], zs[..., o + LORA_W:o + LORA_W + LORA_A]
    zg = zs[..., o + LORA_W + LORA_A:o + LORA_W + LORA_A + LORA_G]
    w0, a0, k_k, k_a, r_k, lnx_g, lnx_b = [vec[i] for i in range(7)]

    def lora(xin, wmat, out_dtype=F32):
        return matmul(xin.reshape(t, -1).astype(BF16), wmat.astype(BF16), out_dtype=out_dtype, tm=ROW_TILE,
                      tn=wmat.shape[1]).reshape(b, s, -1)

    wl = lora(jnp.tanh(zw), w_up)
    al = lora(za, a_up)
    g = lora(jax.nn.sigmoid(zg), g_up, BF16)
    zero = jnp.zeros_like(w0)
    if v_res is None:
        vu, v0 = None, zero
    else:
        v0, v_down, v_up = v_res
        vd = matmul(zs.reshape(t, -1), _pad_cols(v_down, LANE).astype(BF16), out_dtype=BF16, tm=ROW_TILE, tn=LANE,
                    a_col=2)
        vu = matmul(vd, jnp.pad(v_up, ((0, LANE - v_up.shape[0]), (0, 0))).astype(BF16), out_dtype=F32, tm=ROW_TILE,
                    tn=512).reshape(b, s, w_)
    vecs = jnp.stack([w0, a0, k_k, k_a, r_k, v0, zero, zero])
    lnx = jnp.stack([lnx_g, lnx_b] + [zero] * 6)
    return wkv7(zs, zs_first, wl, al, vu, vecs, g, lnx)


DIL_ROW_TILE = ROW_TILE


def residue_perm(dil, transpose=False):
    i = np.arange(DIL_ROW_TILE)
    per = DIL_ROW_TILE // dil
    p = np.zeros((DIL_ROW_TILE, DIL_ROW_TILE), np.float32)
    p[i, (i % per) * dil + i // per] = 1.0
    return jnp.asarray(p.T if transpose else p, BF16)


def _dilated_merge_kernel(*refs, ngroups, nperm):
    o_refs = refs[:ngroups]
    l_refs = refs[ngroups:2 * ngroups]
    p_refs = refs[2 * ngroups:2 * ngroups + nperm]
    y_ref = refs[-1]
    outs, lses = [], []
    for g in range(ngroups):
        if g < ngroups - nperm:
            outs.append(o_refs[g][...].astype(F32))
            lses.append(l_refs[g][...])
            continue
        pt = p_refs[g - (ngroups - nperm)][...]
        outs.append(_dot(pt, o_refs[g][...]))
        lse = l_refs[g][...]
        hi = lse.astype(BF16)
        lo = (lse - hi.astype(F32)).astype(BF16)
        both = _dot(pt, jnp.concatenate([hi, lo], axis=1))
        lses.append(both[:, :LANE] + both[:, LANE:])
    mx = functools.reduce(jnp.maximum, lses)
    es = [jnp.exp(lse - mx) for lse in lses]
    inv = 1.0 / functools.reduce(jnp.add, es)
    for hd in range(DIL_HPG):
        sl = slice(hd * HEAD_DIM, (hd + 1) * HEAD_DIM)
        acc = functools.reduce(jnp.add, [e[:, hd:hd + 1] * o[:, sl] for e, o in zip(es, outs)])
        y_ref[:, sl] = (acc * inv[:, hd:hd + 1]).astype(y_ref.dtype)


def dilated_branch(qkvs):
    b, s, width = qkvs[0].shape
    t = b * s
    assert s % DIL_ROW_TILE == 0
    outs, lses, perms = [], [], []
    for gi, (win, dil) in enumerate(DIL_PATTERNS):
        common = dict(nheads=DIL_HPG, kv_heads=DIL_HPG, q_col=lambda c: 0, k_col=lambda c: 1, v_col=lambda c: 2,
                      o_cols=1, max_dist=win // dil, with_lse=True, out_dtype=BF16)
        if dil == 1:
            assert not perms
            o, lse = band_attention(qkvs[gi], qkvs[gi], qkvs[gi], ncol=1, tq=256, **common)
        else:
            tq = DIL_ROW_TILE // dil
            nsub = max(1, 256 // tq)
            o, lse = band_attention(qkvs[gi], qkvs[gi], qkvs[gi], ncol=dil, tq=tq, seq_len=s // dil, nsub=nsub,
                                    row_block=lambda c, i, per=dil // nsub: i * per + c, **common)
            perms.append(residue_perm(dil, transpose=True))
        outs.append(o.reshape(t, DIL_OUT))
        lses.append(lse.reshape(t, LANE))
    ng = len(DIL_PATTERNS)
    row = lambda width_: pl.BlockSpec((DIL_ROW_TILE, width_), lambda i: (i, 0))
    const = pl.BlockSpec((DIL_ROW_TILE, DIL_ROW_TILE), lambda i: (0, 0))
    return pl.pallas_call(
        functools.partial(_dilated_merge_kernel, ngroups=ng, nperm=len(perms)),
        grid=(t // DIL_ROW_TILE,),
        in_specs=[row(DIL_OUT)] * ng + [row(LANE)] * ng + [const] * len(perms),
        out_specs=row(DIL_OUT),
        out_shape=jax.ShapeDtypeStruct((t, DIL_OUT), BF16),
        compiler_params=_params("parallel"),
    )(*outs, *lses, *perms)


def moe_ffn(h, router_w, router_b, w_gu, w_down, layer, *, tm):
    t, d = h.shape
    logits = matmul(h, _pad_cols(router_w, LANE).astype(BF16), out_dtype=F32, tm=ROW_TILE, tn=LANE)[:, :N_EXPERTS]
    logits = logits + router_b
    experts = jnp.arange(N_EXPERTS)[None, :]
    rest = logits
    top_v, top_i = [], []
    for _ in range(TOP_K):
        idx = jnp.argmax(rest, axis=-1)
        top_i.append(idx)
        top_v.append(jnp.max(rest, axis=-1))
        rest = jnp.where(experts == idx[:, None], -jnp.inf, rest)
    top_v = jnp.stack(top_v, axis=-1)
    top_i = jnp.stack(top_i, axis=-1).astype(jnp.int32)
    wts = jax.nn.softmax(top_v, axis=-1)
    flat_e = top_i.reshape(-1)
    onehot = (flat_e[:, None] == jnp.arange(N_EXPERTS)[None, :]).astype(jnp.int32)
    rank = jnp.take_along_axis(jnp.cumsum(onehot, axis=0), flat_e[:, None], axis=1)[:, 0] - 1
    counts = jnp.sum(onehot, axis=0)
    tiles_per = (counts + tm - 1) // tm
    tile_end = jnp.cumsum(tiles_per)
    group_start = (tile_end - tiles_per) * tm
    dest = group_start[flat_e] + rank
    ntiles = (TOP_K * t) // tm + N_EXPERTS
    rows = ntiles * tm
    row_token = jnp.zeros((rows,), jnp.int32).at[dest].set(jnp.arange(TOP_K * t, dtype=jnp.int32) // TOP_K)
    tile_ids = jnp.arange(ntiles, dtype=jnp.int32)
    tile_valid = (tile_ids < tile_end[-1]).astype(jnp.int32)
    tile_expert = jnp.sum((tile_ids[:, None] >= tile_end[None, :]).astype(jnp.int32), axis=1)
    tile_expert = jnp.minimum(tile_expert, N_EXPERTS - 1)
    xs = jnp.take(h, row_token, axis=0, mode="clip")
    act = moe_up(tile_expert, tile_valid, xs, w_gu, layer, tm=tm, tn=min(1024, w_gu.shape[3] // 2))
    out = moe_down(tile_expert, tile_valid, act, w_down, layer, tm=tm, tn=min(1024, d))
    slot_major = dest.reshape(t, TOP_K).T.reshape(-1)
    yab = jnp.take(out, slot_major, axis=0, mode="clip")
    return yab, jnp.pad(wts, ((0, 0), (0, LANE - TOP_K)))


def kernel(x, c, positions, ada_w, ada_b, norm_g, w_in, cmp_pe, cmp_w1, cmp_w2, rwkv_mu, rwkv_vec, w_up, a_up, g_up, v_res0, v_res_down, v_res_up, w_br_a, w_br_b, w_br_c, w_out, ffn_gu, ffn_down, router_w, router_b, moe_gu, moe_down):
    b, s, d = x.shape
    depth = ada_w.shape[0]
    t = b * s
    tm_row = min(CLOSE_ROW_TILE, s)
    cosf, sinf = _rope_tables(positions)
    cos_t = cosf.reshape(t, LANE)
    sin_t = sinf.reshape(t, LANE)

    cond = jnp.pad(jax.nn.silu(c), ((0, 8 - b % 8 if b % 8 else 0), (0, 0))).astype(BF16)
    mods = []
    for l in range(depth):
        mod = matmul(cond, ada_w, out_dtype=F32, tm=cond.shape[0], tn=d, layer=l)[:b] + ada_b[l]
        mods.append(mod.reshape(b, 6, 1, d))

    def mod_of(l, i):
        return mods[l][:, i]

    q_cols = NSA_HEADS * HEAD_DIM
    kv_cols = 6 * NSA_KV_HEADS * HEAD_DIM
    gate_cols = 3 * NSA_HEADS
    rwkv_cols = 3 * RWKV_WIDTH + LORA_W + LORA_A + LORA_G
    dil_cols = 3 * DIL_HEADS * HEAD_DIM
    offs = np.cumsum([0, q_cols, kv_cols, gate_cols, rwkv_cols, dil_cols, 3 * d]).tolist()
    rwkv_pad = -(-rwkv_cols // 512) * 512

    xf = x.reshape(t, d)
    h = norm_mod(xf, norm_g[0, 0][None], mod_of(0, 1), mod_of(0, 0), seq=s, tm=tm_row)
    zs_first = None
    for l in range(depth):
        wl = w_in[l]
        seg = lambda i: wl[:, offs[i]:offs[i + 1]]
        q_a = matmul(h, seg(0).astype(BF16), out_dtype=BF16, tm=ROW_TILE, tn=q_cols)
        roped = tuple(k0 + hd for k0 in (KV_K_SLC, KV_K_WIN) for hd in range(NSA_KV_HEADS))
        kv_a = matmul(h, seg(1).astype(BF16), out_dtype=BF16, tm=ROW_TILE, tn=kv_cols, rope=(cos_t, sin_t, roped))
        gate_a = matmul(h, _pad_cols(seg(2), LANE).astype(BF16), out_dtype=F32, tm=ROW_TILE, tn=LANE, act="sigmoid")
        zs = matmul_token_shift(h, _pad_cols(seg(3), rwkv_pad).astype(BF16), _pad_cols(rwkv_mu[l][None], rwkv_pad),
                                seq=s, tm=ROW_TILE, tn=rwkv_pad // 4).reshape(b, s, -1)
        zs_first = zs if l == 0 else zs_first
        wc = seg(4).reshape(d, 3, len(DIL_PATTERNS), DIL_OUT)
        qkv_c = [matmul(h, wc[:, :, gi].reshape(d, 3 * DIL_OUT).astype(BF16), out_dtype=BF16, tm=DIL_ROW_TILE,
                        tn=3 * DIL_OUT, rope=(cos_t, sin_t, tuple(range(2 * DIL_HPG))),
                        row_perm=None if dil == 1 else residue_perm(dil)).reshape(b, s, -1)
                 for gi, (_, dil) in enumerate(DIL_PATTERNS)]
        mg = matmul(h, seg(5).astype(BF16), out_dtype=BF16, tm=ROW_TILE, tn=1024, act="sigmoid")

        o_cmp, o_slc, o_win = nsa_branch(q_a.reshape(b, s, -1), kv_a.reshape(b, s, -1), cosf, sinf,
                                         cmp_pe[l], cmp_w1[l], cmp_w2[l])
        v_res = None if l == 0 else (v_res0[l - 1], v_res_down[l - 1], v_res_up[l - 1])
        y_b = rwkv_branch(zs, zs_first, rwkv_vec[l], w_up[l], a_up[l], g_up[l], v_res)
        y_c = dilated_branch(qkv_c)
        merged = branch_merge(o_cmp.reshape(t, -1), o_slc.reshape(t, -1), o_win.reshape(t, -1), gate_a,
                              y_b.reshape(t, -1), y_c.reshape(t, -1), w_br_a[l].astype(BF16),
                              w_br_b[l].astype(BF16), w_br_c[l].astype(BF16), mg, tm=ROW_TILE, tn=512)
        xf, h = matmul_close(merged, w_out[l].astype(BF16), xf, norm_g[l, 1][None], mod_of(l, 2),
                             (norm_g[l, 2][None], mod_of(l, 4), mod_of(l, 3)), seq=s, tm=tm_row, tk=d)

        nxt = None if l == depth - 1 else (norm_g[l + 1, 0][None], mod_of(l + 1, 1), mod_of(l + 1, 0))
        if l % 2 == 0:
            act = swiglu_up(h, ffn_gu[l // 2].astype(BF16), tm=ROW_TILE, tn=512)
            xf, h = matmul_close(act, ffn_down[l // 2].astype(BF16), xf, norm_g[l, 3][None], mod_of(l, 5), nxt,
                                 seq=s, tm=tm_row, tk=act.shape[1] // 4)
        else:
            yab, wts = moe_ffn(h, router_w[l // 2], router_b[l // 2], moe_gu, moe_down, l // 2, tm=MOE_ROW_TILE)
            xf, h = close_sublayer(yab, wts, xf, norm_g[l, 3][None], mod_of(l, 5), nxt, seq=s, tm=tm_row)
    return xf.reshape(b, s, d)
```
